```python
import jax, jax.numpy as jnp
from jax import lax
import numpy as np

D_MODEL = 1024
BATCH = 4
SEQ = 4096
DEPTH = 2

CHUNK = 64
N_MIXERS = 4
GROUP_W = D_MODEL // N_MIXERS
ATT_HEAD_DIM = 64
ATT_HEADS = GROUP_W // ATT_HEAD_DIM
ROPE_DIM = ATT_HEAD_DIM // 4
ROPE_THETA = 500000.0
IDX_HEADS = 4
IDX_DIM = 32
IDX_ROPE_DIM = IDX_DIM // 4
TOPK_MAX = 256
Q_BLOCK = 128
CONV_WIDTH = 31
CONV_GROUPS = 4
POOL_WINDOWS = (2, 4, 8, 16)
POOL_CH = GROUP_W // 4
SGU_CHUNK = 128
SGU_HEADS = 4
SGU_HEAD_W = GROUP_W // SGU_HEADS
N_EXP_GROUPS = 4
EXP_PER_GROUP = 8
N_EXPERTS = N_EXP_GROUPS * EXP_PER_GROUP
EXPERT_FF = 512
MOE_TOPK = 2
MOE_BLOCK = 128
EPS = 1e-6
IN_SPLITS = (GROUP_W, GROUP_W, GROUP_W, IDX_HEADS * IDX_DIM, IDX_DIM, IDX_HEADS, 2 * GROUP_W, GROUP_W, 2 * GROUP_W)
D_IN = 8 * GROUP_W + IDX_HEADS * IDX_DIM + IDX_DIM + IDX_HEADS

kernel_name = "hybrid_chunk_causal_parallel_heads_hmoe"


def _rms(x, g):
    xf = x.astype(jnp.float32)
    y = xf * lax.rsqrt(jnp.mean(xf * xf, axis=-1, keepdims=True) + EPS)
    return y.astype(x.dtype) * g


def _ln(x, g, b):
    xf = x.astype(jnp.float32)
    mu = jnp.mean(xf, axis=-1, keepdims=True)
    var = jnp.mean(jnp.square(xf - mu), axis=-1, keepdims=True)
    return ((xf - mu) * lax.rsqrt(var + EPS)).astype(x.dtype) * g + b


def _rope_tables(positions, rot_dim, dtype):
    half = rot_dim // 2
    inv = jnp.power(jnp.float32(ROPE_THETA), -2.0 * jnp.arange(half, dtype=jnp.float32) / rot_dim)
    ang = positions.astype(jnp.float32)[..., None] * inv
    return jnp.cos(ang).astype(dtype)[:, :, None, :], jnp.sin(ang).astype(dtype)[:, :, None, :]


def _partial_rope(x, cos, sin):
    half = cos.shape[-1]
    x1, x2, rest = x[..., :half], x[..., half:2 * half], x[..., 2 * half:]
    return jnp.concatenate([x1 * cos - x2 * sin, x2 * cos + x1 * sin, rest], axis=-1)


def _dsa_mixer(q, k, v, iq, ik, iw, cos_a, sin_a, cos_i, sin_i, q_norm, k_norm):
    B, S = q.shape[:2]
    q = _partial_rope(_rms(q.reshape(B, S, ATT_HEADS, ATT_HEAD_DIM), q_norm), cos_a, sin_a)
    k = _partial_rope(_rms(k.reshape(B, S, ATT_HEADS, ATT_HEAD_DIM), k_norm), cos_a, sin_a)
    v = v.reshape(B, S, ATT_HEADS, ATT_HEAD_DIM)
    iq = _partial_rope(iq.reshape(B, S, IDX_HEADS, IDX_DIM), cos_i, sin_i)
    ik = _partial_rope(ik[:, :, None, :], cos_i, sin_i)[:, :, 0, :]
    iw = iw * (IDX_HEADS ** -0.5)
    topk = min(TOPK_MAX, S // 4)
    n_blk = S // Q_BLOCK
    key_pos = jnp.arange(S)

    def to_blocks(a):
        return a.reshape(B, n_blk, Q_BLOCK, *a.shape[2:]).swapaxes(0, 1)

    def block(args):
        qb, iqb, iwb, bi = args
        q_pos = bi * Q_BLOCK + jnp.arange(Q_BLOCK)
        key_end = (q_pos // CHUNK + 1) * CHUNK
        admissible = key_pos[None, :] < key_end[:, None]
        dots = jnp.einsum('bqhd,bsd->bqhs', iqb, ik) * (IDX_DIM ** -0.5)
        score = jnp.einsum('bqhs,bqh->bqs', jax.nn.relu(dots), iwb).astype(jnp.float32)
        score = jnp.where(admissible[None], score, -jnp.inf)
        top_val, top_idx = lax.top_k(score, topk)
        valid = jnp.isfinite(top_val)
        k_sel = jax.vmap(lambda kb, ib: kb[ib])(k, top_idx)
        v_sel = jax.vmap(lambda vb, ib: vb[ib])(v, top_idx)
        logits = jnp.einsum('bqhd,bqkhd->bqhk', qb, k_sel).astype(jnp.float32) * (ATT_HEAD_DIM ** -0.5)
        logits = jnp.where(valid[:, :, None, :], logits, -jnp.inf)
        p = jax.nn.softmax(logits, axis=-1).astype(v.dtype)
        o = jnp.einsum('bqhk,bqkhd->bqhd', p, v_sel)
        return o.reshape(B, Q_BLOCK, GROUP_W)

    out = lax.map(block, (to_blocks(q), to_blocks(iq), to_blocks(iw), jnp.arange(n_blk)))
    return out.swapaxes(0, 1).reshape(B, S, GROUP_W)


def _conv_module(a, gate, conv_w, conv_b, ln_g, ln_b, pw_w, pw_b):
    B, S, C = a.shape
    y = a * jax.nn.sigmoid(gate)
    y = jnp.pad(y, ((0, 0), (CONV_WIDTH - 1, 0), (0, 0)))
    y = lax.conv_general_dilated(y, conv_w[:, None, :], window_strides=(1,), padding='VALID',
                                 dimension_numbers=('NWC', 'WIO', 'NWC'), feature_group_count=C) + conv_b
    y = _ln(y.reshape(B, S, CONV_GROUPS, C // CONV_GROUPS),
            ln_g.reshape(CONV_GROUPS, -1), ln_b.reshape(CONV_GROUPS, -1)).reshape(B, S, C)
    return jax.nn.silu(y) @ pw_w + pw_b


def _pool_mixer(p, pool_w, pool_b, pool_scale):
    B, S, C = p.shape
    pg = p.reshape(B, S, len(POOL_WINDOWS), POOL_CH)
    cs = lax.cumsum(pg.astype(jnp.float32), axis=1)
    count = jnp.arange(1, S + 1, dtype=jnp.float32)
    means = []
    for i, w in enumerate(POOL_WINDOWS):
        c_i = cs[:, :, i]
        prev = jnp.pad(c_i, ((0, 0), (w, 0), (0, 0)))[:, :S]
        means.append((c_i - prev) / jnp.minimum(count, w)[None, :, None])
    pooled = jnp.stack(means, axis=2).astype(p.dtype) - pg
    y = jnp.einsum('bsgc,gcd->bsgd', pooled, pool_w) + pool_b
    return y.reshape(B, S, C) * pool_scale


def _sgu_mixer(u, v, ln_g, ln_b, sgu_w, sgu_b):
    B, S, C = u.shape
    n = S // SGU_CHUNK
    vc = _ln(v, ln_g, ln_b).reshape(B, n, SGU_CHUNK, SGU_HEADS, SGU_HEAD_W)
    causal = jnp.tril(jnp.ones((SGU_CHUNK, SGU_CHUNK), dtype=bool))
    w = jnp.where(causal[None], sgu_w, 0)
    mixed = jnp.einsum('hts,bnshc->bnthc', w, vc) + sgu_b.T[None, None, :, :, None]
    return u * mixed.reshape(B, S, C)


def _hier_moe(h, w_rg, b_rg, w_re, b_re, w_gate, w_up, w_down):
    N, D = h.shape
    p_grp = jax.nn.softmax((h @ w_rg + b_rg).astype(jnp.float32), axis=-1)
    g_idx = jnp.argmax(p_grp, axis=-1).astype(jnp.int32)
    p_sel = jnp.take_along_axis(p_grp, g_idx[:, None], axis=1)
    e_logits = jnp.einsum('nd,dge->nge', h, w_re) + b_re
    e_logits = jnp.take_along_axis(e_logits, g_idx[:, None, None], axis=1)[:, 0].astype(jnp.float32)
    top_val, top_j = lax.top_k(e_logits, MOE_TOPK)
    gates = (p_sel * jax.nn.softmax(top_val, axis=-1)).astype(h.dtype)
    expert = g_idx[:, None] * EXP_PER_GROUP + top_j.astype(jnp.int32)
    M = N * MOE_TOPK
    e_flat = expert.reshape(-1)
    tok_flat = jnp.repeat(jnp.arange(N, dtype=jnp.int32), MOE_TOPK)
    g_flat = gates.reshape(-1)
    order = jnp.argsort(e_flat)
    e_s, tok_s, g_s = e_flat[order], tok_flat[order], g_flat[order]
    counts = jnp.zeros(N_EXPERTS, jnp.int32).at[e_flat].add(1)
    padded = (counts + MOE_BLOCK - 1) // MOE_BLOCK * MOE_BLOCK
    pad_end = jnp.cumsum(padded)
    pad_start = pad_end - padded
    start = jnp.cumsum(counts) - counts
    dest = pad_start[e_s] + jnp.arange(M, dtype=jnp.int32) - start[e_s]
    n_blocks = (M + N_EXPERTS * (MOE_BLOCK - 1) + MOE_BLOCK - 1) // MOE_BLOCK
    rows = n_blocks * MOE_BLOCK
    row_tok = jnp.zeros(rows, jnp.int32).at[dest].set(tok_s)
    row_gate = jnp.zeros(rows, h.dtype).at[dest].set(g_s)
    block_expert = jnp.minimum(
        jnp.searchsorted(pad_end, jnp.arange(n_blocks, dtype=jnp.int32) * MOE_BLOCK, side='right'),
        N_EXPERTS - 1)

    def expert_block(args):
        toks, gts, e = args
        xb = h[toks]
        hid = jax.nn.silu(xb @ w_gate[e]) * (xb @ w_up[e])
        return (hid @ w_down[e]) * gts[:, None]

    y_rows = lax.map(expert_block, (row_tok.reshape(n_blocks, MOE_BLOCK),
                                    row_gate.reshape(n_blocks, MOE_BLOCK), block_expert))
    return jnp.zeros_like(h).at[row_tok].add(y_rows.reshape(rows, D))


def setup_inputs(seed: int = 0) -> dict:
    key = jax.random.key(seed)
    it = iter(jax.random.split(key, 32))
    L, D, G, F = DEPTH, D_MODEL, GROUP_W, EXPERT_FF

    def nrm(shape, scale):
        return jax.random.normal(next(it), shape, jnp.float32) * scale

    def gain(shape):
        return 1.0 + 0.02 * jax.random.normal(next(it), shape, jnp.float32)

    x = nrm((BATCH, SEQ, D), 1.0)
    c = nrm((BATCH, D), 1.0)
    offset = jax.random.randint(next(it), (BATCH, 1), 0, 1024, dtype=jnp.int32)
    positions = (jnp.arange(SEQ, dtype=jnp.int32)[None, :] + offset).astype(jnp.int32)
    return {
        "x": x, "c": c, "positions": positions,
        "w_ada": nrm((L, D, 6 * D), D ** -0.5), "b_ada": nrm((L, 6 * D), 0.01),
        "norm1": gain((L, D)), "w_in": nrm((L, D, D_IN), D ** -0.5),
        "q_norm": gain((L, ATT_HEAD_DIM)), "k_norm": gain((L, ATT_HEAD_DIM)),
        "conv_w": nrm((L, CONV_WIDTH, G), CONV_WIDTH ** -0.5), "conv_b": nrm((L, G), 0.01),
        "conv_ln_g": gain((L, G)), "conv_ln_b": nrm((L, G), 0.01),
        "conv_pw_w": nrm((L, G, G), G ** -0.5), "conv_pw_b": nrm((L, G), 0.01),
        "pool_w": nrm((L, len(POOL_WINDOWS), POOL_CH, POOL_CH), POOL_CH ** -0.5),
        "pool_b": nrm((L, len(POOL_WINDOWS), POOL_CH), 0.01), "pool_scale": gain((L, G)),
        "sgu_ln_g": gain((L, G)), "sgu_ln_b": nrm((L, G), 0.01),
        "sgu_w": nrm((L, SGU_HEADS, SGU_CHUNK, SGU_CHUNK), SGU_CHUNK ** -0.5),
        "sgu_b": gain((L, SGU_HEADS, SGU_CHUNK)),
        "out_norm": gain((L, D)), "w_out": nrm((L, D, D), D ** -0.5),
        "norm2": gain((L, D)),
        "w_rg": nrm((L, D, N_EXP_GROUPS), D ** -0.5), "b_rg": nrm((L, N_EXP_GROUPS), 0.01),
        "w_re": nrm((L, D, N_EXP_GROUPS, EXP_PER_GROUP), D ** -0.5),
        "b_re": nrm((L, N_EXP_GROUPS, EXP_PER_GROUP), 0.01),
        "w_gate": nrm((L, N_EXPERTS, D, F), D ** -0.5), "w_up": nrm((L, N_EXPERTS, D, F), D ** -0.5),
        "w_down": nrm((L, N_EXPERTS, F, D), F ** -0.5),
    }


def reference(x, c, positions, w_ada, b_ada, norm1, w_in, q_norm, k_norm, conv_w, conv_b,
              conv_ln_g, conv_ln_b, conv_pw_w, conv_pw_b, pool_w, pool_b, pool_scale,
              sgu_ln_g, sgu_ln_b, sgu_w, sgu_b, out_norm, w_out, norm2, w_rg, b_rg, w_re, b_re,
              w_gate, w_up, w_down):
    B, S, D = x.shape
    split_points = np.cumsum(IN_SPLITS)[:-1].tolist()
    cos_a, sin_a = _rope_tables(positions, ROPE_DIM, x.dtype)
    cos_i, sin_i = _rope_tables(positions, IDX_ROPE_DIM, x.dtype)
    c_act = jax.nn.silu(c)
    for l in range(DEPTH):
        mod = c_act @ w_ada[l] + b_ada[l]
        shift1, scale1, gate1, shift2, scale2, gate2 = [m[:, None, :] for m in jnp.split(mod, 6, axis=-1)]
        h = _rms(x, norm1[l]) * (1 + scale1) + shift1
        proj = h @ w_in[l]
        qa, ka, va, iqa, ika, iwa, b_in, c_in, d_in = jnp.split(proj, split_points, axis=-1)
        b_a, b_g = jnp.split(b_in, 2, axis=-1)
        d_u, d_v = jnp.split(d_in, 2, axis=-1)
        o_a = _dsa_mixer(qa, ka, va, iqa, ika, iwa, cos_a, sin_a, cos_i, sin_i, q_norm[l], k_norm[l])
        o_b = _conv_module(b_a, b_g, conv_w[l], conv_b[l], conv_ln_g[l], conv_ln_b[l], conv_pw_w[l], conv_pw_b[l])
        o_c = _pool_mixer(c_in, pool_w[l], pool_b[l], pool_scale[l])
        o_d = _sgu_mixer(d_u, d_v, sgu_ln_g[l], sgu_ln_b[l], sgu_w[l], sgu_b[l])
        mix = jnp.concatenate([o_a, o_b, o_c, o_d], axis=-1).reshape(B, S, N_MIXERS, GROUP_W)
        mix = _rms(mix, out_norm[l].reshape(N_MIXERS, GROUP_W)).reshape(B, S, D)
        x = x + gate1 * (mix @ w_out[l])
        h2 = _rms(x, norm2[l]) * (1 + scale2) + shift2
        y = _hier_moe(h2.reshape(B * S, D), w_rg[l], b_rg[l], w_re[l], b_re[l], w_gate[l], w_up[l], w_down[l])
        x = x + gate2 * y.reshape(B, S, D)
    return x
```

```python
import functools

import numpy as np
import jax
import jax.numpy as jnp
from jax import lax
from jax.experimental import pallas as pl
from jax.experimental.pallas import tpu as pltpu

F32 = jnp.float32
BF16 = jnp.bfloat16
NEG_INF = float("-inf")

D_MODEL = 1024
DEPTH = 2
CHUNK = 64
N_MIXERS = 4
GROUP_W = D_MODEL // N_MIXERS
ATT_HEAD_DIM = 64
ATT_HEADS = GROUP_W // ATT_HEAD_DIM
ROPE_DIM = ATT_HEAD_DIM // 4
ROPE_THETA = 500000.0
IDX_HEADS = 4
IDX_DIM = 32
IDX_ROPE_DIM = IDX_DIM // 4
TOPK_MAX = 256
CONV_WIDTH = 31
CONV_GROUPS = 4
POOL_WINDOWS = (2, 4, 8, 16)
POOL_CH = GROUP_W // 4
SGU_CHUNK = 128
SGU_HEADS = 4
N_EXP_GROUPS = 4
EXP_PER_GROUP = 8
N_EXPERTS = N_EXP_GROUPS * EXP_PER_GROUP
EXPERT_FF = 512
MOE_TOPK = 2
EPS = 1e-6
IN_SPLITS = (GROUP_W, GROUP_W, GROUP_W, IDX_HEADS * IDX_DIM, IDX_DIM, IDX_HEADS, 2 * GROUP_W, GROUP_W, 2 * GROUP_W)

LANES = 128
HALO = 32
W_IN_COLS = 3 * GROUP_W + 3 * LANES + 5 * GROUP_W

TM_PROJ = 512
TQ_ATT = 128
TK_ATT = 512
TM_MIX = 256
MOE_ROWS = 256
TM_COMB = 256
BISECT_ITERS = 24
VMEM_LIMIT = 56 * 1024 * 1024


def _cparams(sem):
    return pltpu.CompilerParams(dimension_semantics=sem, vmem_limit_bytes=VMEM_LIMIT)


def _lane_iota(shape):
    return lax.broadcasted_iota(jnp.int32, shape, len(shape) - 1)


def _seg_mean(y, width):
    shift = int(np.log2(width))
    grp = _lane_iota(y.shape) >> shift
    out = jnp.zeros_like(y)
    for g in range(y.shape[-1] // width):
        msk = grp == g
        s = jnp.sum(jnp.where(msk, y, 0.0), axis=-1, keepdims=True)
        out = jnp.where(msk, s, out)
    return out * (1.0 / width)


def _rope(x, cos_f, sin_s, head_w, half):
    c = x.shape[-1]
    lane = _lane_iota(x.shape) & (head_w - 1)
    partner = jnp.where(lane < half, pltpu.roll(x, c - half, 1), pltpu.roll(x, half, 1))
    return x * cos_f + partner * sin_s


def _silu(x):
    return x * jax.nn.sigmoid(x)


def _ada_kernel(c_ref, w_ref, b_ref, o_ref):
    ca = _silu(c_ref[...])
    o_ref[0] = jnp.dot(ca.astype(BF16), w_ref[0].astype(BF16), preferred_element_type=F32) + b_ref[0]


def _ada_call(c_pad, w_ada, b_ada):
    L, D, D6 = w_ada.shape
    rows = c_pad.shape[0]
    tn = D
    return pl.pallas_call(
        _ada_kernel,
        grid=(L, D6 // tn),
        in_specs=[
            pl.BlockSpec((rows, D), lambda l, j: (0, 0)),
            pl.BlockSpec((1, D, tn), lambda l, j: (l, 0, j)),
            pl.BlockSpec((1, 1, tn), lambda l, j: (l, 0, j)),
        ],
        out_specs=pl.BlockSpec((1, rows, tn), lambda l, j: (l, 0, j)),
        out_shape=jax.ShapeDtypeStruct((L, rows, D6), F32),
        compiler_params=_cparams(("arbitrary", "arbitrary")),
        name="ada_mod",
    )(c_pad, w_ada, b_ada.reshape(L, 1, D6))


def _proj_kernel(x_ref, mod_ref, n1_ref, w_ref, qn_ref, kn_ref, cosa_ref, sina_ref, cosi_ref, sini_ref,
                 q_ref, k_ref, v_ref, iq_ref, ik_ref, iw_ref, pb_ref, pc_ref, pd_ref):
    x = x_ref[...]
    ms = jnp.mean(x * x, axis=-1, keepdims=True)
    h = (x * lax.rsqrt(ms + EPS)) * n1_ref[...]
    h = h * (1.0 + mod_ref[0, 1:2, :]) + mod_ref[0, 0:1, :]
    proj = jnp.dot(h.astype(BF16), w_ref[...], preferred_element_type=F32)
    G = GROUP_W
    cos_a, sin_a = cosa_ref[...], sina_ref[...]
    cos_i, sin_i = cosi_ref[...], sini_ref[...]

    def qk(t, g_ref):
        tn = (t * lax.rsqrt(_seg_mean(t * t, ATT_HEAD_DIM) + EPS)) * g_ref[...]
        return _rope(tn, cos_a, sin_a, ATT_HEAD_DIM, ROPE_DIM // 2)

    q_ref[...] = (qk(proj[:, 0:G], qn_ref) * (ATT_HEAD_DIM ** -0.5)).astype(BF16)
    k_ref[...] = qk(proj[:, G:2 * G], kn_ref).astype(BF16)
    v_ref[...] = proj[:, 2 * G:3 * G].astype(BF16)
    o = 3 * G
    iq_ref[...] = _rope(proj[:, o:o + LANES], cos_i, sin_i, IDX_DIM, IDX_ROPE_DIM // 2).astype(BF16)
    ik_ref[...] = _rope(proj[:, o + LANES:o + 2 * LANES], cos_i, sin_i, IDX_DIM, IDX_ROPE_DIM // 2).astype(BF16)
    iw_ref[...] = proj[:, o + 2 * LANES:o + 3 * LANES] * (IDX_HEADS ** -0.5)
    o += 3 * LANES
    pb_ref[...] = proj[:, o:o + 2 * G]
    pc_ref[...] = proj[:, o + 2 * G:o + 3 * G]
    pd_ref[...] = proj[:, o + 3 * G:o + 5 * G]


def _proj_call(x2, mod, n1, w_in_p, qn_t, kn_t, cos_a, sin_a, cos_i, sin_i, seq):
    N, D = x2.shape
    tm = TM_PROJ
    tpb = seq // tm
    G = GROUP_W
    row = lambda i: (i, 0)
    fixed = lambda i: (0, 0)
    outs = [(G, BF16), (G, BF16), (G, BF16), (LANES, BF16), (LANES, BF16), (LANES, F32),
            (2 * G, F32), (G, F32), (2 * G, F32)]
    return pl.pallas_call(
        _proj_kernel,
        grid=(N // tm,),
        in_specs=[
            pl.BlockSpec((tm, D), row),
            pl.BlockSpec((1, 6, D), lambda i: (i // tpb, 0, 0)),
            pl.BlockSpec((1, D), fixed),
            pl.BlockSpec((D, W_IN_COLS), fixed),
            pl.BlockSpec((1, G), fixed),
            pl.BlockSpec((1, G), fixed),
            pl.BlockSpec((tm, G), row),
            pl.BlockSpec((tm, G), row),
            pl.BlockSpec((tm, LANES), row),
            pl.BlockSpec((tm, LANES), row),
        ],
        out_specs=[pl.BlockSpec((tm, w), row) for w, _ in outs],
        out_shape=[jax.ShapeDtypeStruct((N, w), dt) for w, dt in outs],
        compiler_params=_cparams(("parallel",)),
        name="norm_in_proj",
    )(x2, mod, n1, w_in_p, qn_t, kn_t, cos_a, sin_a, cos_i, sin_i)


def _dsa_kernel(q_ref, k_ref, v_ref, iq_ref, ik_ref, iw_ref, o_ref, sc_ref, *, topk):
    tq, tk = TQ_ATT, TK_ATT
    i = pl.program_id(1)
    q0 = i * tq
    n_kv = (q0 + tq + tk - 1) // tk
    kf = float(topk)

    row = lax.broadcasted_iota(jnp.int32, (tq, 1), 0) + q0
    key_end = ((row >> 6) + 1) << 6
    lane_k = _lane_iota((tq, tk))
    lane128 = _lane_iota((tq, LANES))

    iq = iq_ref[...]
    iw = iw_ref[...]
    iq_h = [jnp.where((lane128 >> 5) == h, iq, jnp.zeros_like(iq)) for h in range(IDX_HEADS)]
    iw_h = [iw[:, h:h + 1] for h in range(IDX_HEADS)]

    def score_body(kc, carry):
        ikc = ik_ref[pl.ds(pl.multiple_of(kc * tk, tk), tk), :]
        s = jnp.zeros((tq, tk), F32)
        for h in range(IDX_HEADS):
            d = lax.dot_general(iq_h[h], ikc, (((1,), (1,)), ((), ())), preferred_element_type=F32)
            s = s + jnp.maximum(d * (IDX_DIM ** -0.5), 0.0) * iw_h[h]
        sc_ref[kc] = jnp.where(lane_k + kc * tk < key_end, s, NEG_INF)
        return carry

    lax.fori_loop(0, n_kv, score_body, 0)

    def reduce_chunks(fn, init, combine):
        def body(kc, part):
            blk = sc_ref[kc]
            for j in range(tk // LANES):
                part = combine(part, fn(blk[:, j * LANES:(j + 1) * LANES], kc * tk + j * LANES))
            return part
        return lax.fori_loop(0, n_kv, body, jnp.full((tq, LANES), init, F32))

    def count(ind):
        return jnp.sum(reduce_chunks(ind, 0.0, jnp.add), axis=1, keepdims=True)

    def row_maximum(val):
        return jnp.max(reduce_chunks(val, NEG_INF, jnp.maximum), axis=1, keepdims=True)

    def bcast(v):
        return jnp.broadcast_to(v, (tq, LANES))

    small = key_end <= topk

    @pl.when(q0 + tq > topk)
    def _select():
        row_max = row_maximum(lambda b, off: b)
        row_min = -row_maximum(lambda b, off: jnp.where(b == NEG_INF, NEG_INF, -b))

        def bis_body(_, c):
            lo, hi = c
            mid = jnp.where(hi == jnp.inf, row_max, lo + (hi - lo) * 0.5)
            midb = bcast(mid)
            ge = count(lambda b, off: jnp.where(b >= midb, 1.0, 0.0)) >= kf
            return jnp.where(ge, mid, lo), jnp.where(ge, hi, mid)

        lo, hi = lax.fori_loop(0, BISECT_ITERS, bis_body, (row_min, jnp.full((tq, 1), jnp.inf, F32)))

        def sd_cond(c):
            return c[0] > 0.0

        def sd_body(c):
            _, hi, thr, done = c
            hib = bcast(hi)
            cand = row_maximum(lambda b, off: jnp.where(b < hib, b, NEG_INF))
            candb = bcast(cand)
            ok = count(lambda b, off: jnp.where(b >= candb, 1.0, 0.0)) >= kf
            thr = jnp.where(done > 0.0, thr, cand)
            hi = jnp.where(done > 0.0, hi, cand)
            done = jnp.where(ok, 1.0, done)
            return jnp.sum(1.0 - done), hi, thr, done

        done0 = jnp.where(small, 1.0, 0.0)
        n0 = jnp.sum(1.0 - done0)
        _, _, thr, _ = lax.while_loop(sd_cond, sd_body, (n0, hi, jnp.full((tq, 1), NEG_INF, F32), done0))
        thr = jnp.where(small, NEG_INF, thr)
        thrb = bcast(thr)

        need = kf - count(lambda b, off: jnp.where(b > thrb, 1.0, 0.0))
        idx128 = lane128.astype(F32)

        def tie_body(_, c):
            mlo, mhi = c
            mid = jnp.floor((mlo + mhi) * 0.5)
            midb = bcast(mid)
            cnt = count(lambda b, off: jnp.where(
                b == thrb, jnp.where(idx128 + off.astype(F32) < midb, 1.0, 0.0), 0.0))
            ge = cnt >= need
            return jnp.where(ge, mlo, mid), jnp.where(ge, mid, mhi)

        n_bits = int(np.ceil(np.log2(sc_ref.shape[0] * tk))) + 1
        _, cut = lax.fori_loop(0, n_bits, tie_body,
                               (jnp.zeros((tq, 1), F32), jnp.full((tq, 1), float(sc_ref.shape[0] * tk), F32)))
        cutb = jnp.broadcast_to(cut, (tq, tk))
        thrk = jnp.broadcast_to(thr, (tq, tk))

        def bias_body(kc, carry):
            blk = sc_ref[kc]
            idx = (lane_k + kc * tk).astype(F32)
            tie = jnp.where(blk == thrk, jnp.where(idx < cutb, 0.0, NEG_INF), NEG_INF)
            bias = jnp.where(blk > thrk, 0.0, tie)
            sc_ref[kc] = jnp.where(blk == NEG_INF, NEG_INF, bias)
            return carry

        lax.fori_loop(0, n_kv, bias_body, 0)

    @pl.when(q0 + tq <= topk)
    def _all():
        def bias_body(kc, carry):
            sc_ref[kc] = jnp.where(sc_ref[kc] == NEG_INF, NEG_INF, 0.0)
            return carry
        lax.fori_loop(0, n_kv, bias_body, 0)

    q = q_ref[...]
    lane_q = _lane_iota((tq, GROUP_W)) >> 6
    out = jnp.zeros((tq, GROUP_W), F32)
    for h in range(ATT_HEADS):
        qh = jnp.where(lane_q == h, q, jnp.zeros_like(q))

        def att_body(kc, c, qh=qh):
            m, l, acc = c
            ks = pl.ds(pl.multiple_of(kc * tk, tk), tk)
            s = lax.dot_general(qh, k_ref[ks, :], (((1,), (1,)), ((), ())), preferred_element_type=F32)
            s = s + sc_ref[kc]
            m_new = jnp.maximum(m, jnp.max(s, axis=1, keepdims=True))
            m_safe = jnp.where(m_new == NEG_INF, 0.0, m_new)
            alpha = jnp.exp(m - m_safe)
            p = jnp.exp(s - m_safe)
            l = alpha * l + jnp.sum(p, axis=1, keepdims=True)
            acc = alpha * acc + jnp.dot(p.astype(BF16), v_ref[ks, :], preferred_element_type=F32)
            return m_new, l, acc

        m0 = jnp.full((tq, 1), NEG_INF, F32)
        _, l, acc = lax.fori_loop(0, n_kv, att_body, (m0, jnp.zeros((tq, 1), F32), jnp.zeros((tq, GROUP_W), F32)))
        out = jnp.where(lane_q == h, acc / l, out)
    o_ref[...] = out


def _dsa_call(q, k, v, iq, ik, iw, batch, seq):
    N = q.shape[0]
    tq = TQ_ATT
    nq = seq // tq
    G = GROUP_W
    topk = min(TOPK_MAX, seq // 4)
    qrow = lambda b, i: (b * nq + i, 0)
    brow = lambda b, i: (b, 0)
    return pl.pallas_call(
        functools.partial(_dsa_kernel, topk=topk),
        grid=(batch, nq),
        in_specs=[
            pl.BlockSpec((tq, G), qrow),
            pl.BlockSpec((seq, G), brow),
            pl.BlockSpec((seq, G), brow),
            pl.BlockSpec((tq, LANES), qrow),
            pl.BlockSpec((seq, LANES), brow),
            pl.BlockSpec((tq, LANES), qrow),
        ],
        out_specs=pl.BlockSpec((tq, G), qrow),
        out_shape=jax.ShapeDtypeStruct((N, G), F32),
        scratch_shapes=[pltpu.VMEM((seq // TK_ATT, tq, TK_ATT), F32)],
        compiler_params=_cparams(("parallel", "arbitrary")),
        name="dsa_attention",
    )(q, k, v, iq, ik, iw)


def _mix_kernel(x_ref, oa_ref, pb_ref, pbh_ref, pc_ref, pch_ref, pd_ref, mod_ref,
                cw_ref, cb_ref, clg_ref, clb_ref, cpw_ref, cpb_ref,
                pw_ref, pbias_ref, ps_ref, slg_ref, slb_ref, sw_ref, sb_ref,
                on_ref, wo_ref, n2_ref, wr_ref, br_ref,
                xo_ref, h2_ref, route_ref, cnt_ref,
                ypad_ref, ppad_ref, s2_ref, s4_ref, s8_ref, run_ref, *, tiles_per_batch):
    tm = TM_MIX
    G = GROUP_W
    i = pl.program_id(0)
    t_in_b = i % tiles_per_batch
    first = t_in_b == 0
    lane_g = _lane_iota((tm, G))

    def glu(pb):
        return pb[:, 0:G] * jax.nn.sigmoid(pb[:, G:2 * G])

    ypad_ref[0:HALO, :] = jnp.where(first, 0.0, glu(pbh_ref[...]))
    ypad_ref[HALO:HALO + tm, :] = glu(pb_ref[...])
    acc = jnp.zeros((tm, G), F32)
    for j in range(CONV_WIDTH):
        off = HALO - (CONV_WIDTH - 1) + j
        acc = acc + cw_ref[j:j + 1, :] * ypad_ref[off:off + tm, :]
    y = acc + cb_ref[...]
    gw = G // CONV_GROUPS
    mu = _seg_mean(y, gw)
    yc = y - mu
    var = _seg_mean(yc * yc, gw)
    y = (yc * lax.rsqrt(var + EPS)) * clg_ref[...] + clb_ref[...]
    o_b = jnp.dot(_silu(y).astype(BF16), cpw_ref[...], preferred_element_type=F32) + cpb_ref[...]

    p = pc_ref[...]
    ppad_ref[0:HALO, :] = jnp.where(first, 0.0, pch_ref[...])
    ppad_ref[HALO:HALO + tm, :] = p
    n8 = tm + HALO - 8
    s2_ref[8:8 + n8, :] = ppad_ref[8:8 + n8, :] + ppad_ref[7:7 + n8, :]
    n16 = tm + HALO - 16
    s4_ref[16:16 + n16, :] = s2_ref[16:16 + n16, :] + s2_ref[14:14 + n16, :]
    n24 = tm + HALO - 24
    s8_ref[24:24 + n24, :] = s4_ref[24:24 + n24, :] + s4_ref[20:20 + n24, :]
    s2 = s2_ref[HALO:HALO + tm, :]
    s4 = s4_ref[HALO:HALO + tm, :]
    s8 = s8_ref[HALO:HALO + tm, :]
    s16 = s8 + s8_ref[HALO - 8:HALO - 8 + tm, :]
    pgrp = lane_g >> 6
    wsum = jnp.where(pgrp == 0, s2, jnp.where(pgrp == 1, s4, jnp.where(pgrp == 2, s8, s16)))
    wlen = jnp.where(pgrp == 0, 2.0, jnp.where(pgrp == 1, 4.0, jnp.where(pgrp == 2, 8.0, 16.0)))
    tpos = (lax.broadcasted_iota(jnp.int32, (tm, G), 0) + t_in_b * tm + 1).astype(F32)
    pooled = wsum / jnp.minimum(tpos, wlen) - p
    o_c = (jnp.dot(pooled.astype(BF16), pw_ref[...], preferred_element_type=F32) + pbias_ref[...]) * ps_ref[...]

    pd = pd_ref[...]
    u, v = pd[:, 0:G], pd[:, G:2 * G]
    mu = jnp.mean(v, axis=-1, keepdims=True)
    vc = v - mu
    var = jnp.mean(vc * vc, axis=-1, keepdims=True)
    vn = ((vc * lax.rsqrt(var + EPS)) * slg_ref[...] + slb_ref[...]).astype(BF16)
    r_i = lax.broadcasted_iota(jnp.int32, (SGU_CHUNK, SGU_CHUNK), 0)
    c_i = lax.broadcasted_iota(jnp.int32, (SGU_CHUNK, SGU_CHUNK), 1)
    w_heads = [jnp.where(r_i >= c_i, sw_ref[h], 0.0).astype(BF16) for h in range(SGU_HEADS)]
    lane_c = _lane_iota((SGU_CHUNK, G)) >> 6
    mixed = []
    for n in range(tm // SGU_CHUNK):
        vch = vn[n * SGU_CHUNK:(n + 1) * SGU_CHUNK, :]
        mx = jnp.zeros((SGU_CHUNK, G), F32)
        for h in range(SGU_HEADS):
            mx = jnp.where(lane_c == h, jnp.dot(w_heads[h], vch, preferred_element_type=F32), mx)
        mixed.append(mx + sb_ref[...])
    o_d = u * jnp.concatenate(mixed, axis=0)

    proj = jnp.zeros((tm, D_MODEL), F32)
    for g, piece in enumerate((oa_ref[...], o_b, o_c, o_d)):
        ms = jnp.mean(piece * piece, axis=-1, keepdims=True)
        pn = (piece * lax.rsqrt(ms + EPS)) * on_ref[:, g * G:(g + 1) * G]
        proj = proj + jnp.dot(pn.astype(BF16), wo_ref[g * G:(g + 1) * G, :], preferred_element_type=F32)
    x_new = x_ref[...] + mod_ref[0, 2:3, :] * proj
    xo_ref[...] = x_new

    ms = jnp.mean(x_new * x_new, axis=-1, keepdims=True)
    h2 = (x_new * lax.rsqrt(ms + EPS)) * n2_ref[...]
    h2 = h2 * (1.0 + mod_ref[0, 4:5, :]) + mod_ref[0, 3:4, :]
    h2_ref[...] = h2
    logits = jnp.dot(h2.astype(BF16), wr_ref[...], preferred_element_type=F32) + br_ref[...]
    lane = _lane_iota((tm, LANES))
    lane_f = lane.astype(F32)
    big = float(LANES)
    glog = jnp.where(lane < N_EXP_GROUPS, logits, NEG_INF)
    gmax = jnp.max(glog, axis=-1, keepdims=True)
    p_sel = 1.0 / jnp.sum(jnp.exp(glog - gmax), axis=-1, keepdims=True)
    g_idx = jnp.min(jnp.where(glog == gmax, lane_f, big), axis=-1, keepdims=True)
    e_lane = lane - N_EXP_GROUPS
    elog = jnp.where((e_lane >> 3).astype(F32) == g_idx, logits, NEG_INF)
    top1 = jnp.max(elog, axis=-1, keepdims=True)
    j1 = jnp.min(jnp.where(elog == top1, lane_f, big), axis=-1, keepdims=True)
    elog2 = jnp.where(lane_f == j1, NEG_INF, elog)
    top2 = jnp.max(elog2, axis=-1, keepdims=True)
    j2 = jnp.min(jnp.where(elog2 == top2, lane_f, big), axis=-1, keepdims=True)
    e2w = jnp.exp(top2 - top1)
    gate1 = p_sel * (1.0 / (1.0 + e2w))
    gate2 = p_sel * (e2w / (1.0 + e2w))
    e1 = j1 - float(N_EXP_GROUPS)
    e2 = j2 - float(N_EXP_GROUPS)

    @pl.when(i == 0)
    def _init():
        run_ref[...] = jnp.zeros_like(run_ref)

    onehot = jnp.where(jnp.logical_or(lane_f == e1, lane_f == e2), 1.0, 0.0)
    rr = lax.broadcasted_iota(jnp.int32, (tm, tm), 0)
    cc = lax.broadcasted_iota(jnp.int32, (tm, tm), 1)
    before = jnp.where(rr > cc, 1.0, 0.0).astype(BF16)
    prior = jnp.dot(before, onehot.astype(BF16), preferred_element_type=F32) + run_ref[0:1, :]
    rank1 = jnp.sum(jnp.where(lane_f == e1, prior, 0.0), axis=-1, keepdims=True)
    rank2 = jnp.sum(jnp.where(lane_f == e2, prior, 0.0), axis=-1, keepdims=True)
    run_new = run_ref[0:1, :] + jnp.sum(onehot, axis=0, keepdims=True)
    run_ref[...] = jnp.broadcast_to(run_new, run_ref.shape)
    cnt_ref[...] = jnp.broadcast_to(run_new, cnt_ref.shape)
    route = jnp.where(lane == 0, e1, jnp.where(lane == 1, e2, jnp.where(lane == 2, rank1, jnp.where(
        lane == 3, rank2, jnp.where(lane == 4, gate1, jnp.where(lane == 5, gate2, 0.0))))))
    route_ref[...] = route


def _mix_call(x2, oa, pb, pc, pd, mod, lw, seq):
    N, D = x2.shape
    tm = TM_MIX
    tpb = seq // tm
    G = GROUP_W
    row = lambda i: (i, 0)
    halo = lambda i: (jnp.maximum(i * (tm // HALO) - 1, 0), 0)
    fixed2 = lambda i: (0, 0)
    fixed3 = lambda i: (0, 0, 0)
    params = [lw["conv_w"], lw["conv_b"], lw["conv_ln_g"], lw["conv_ln_b"], lw["conv_pw_w"], lw["conv_pw_b"],
              lw["pool_w"], lw["pool_b"], lw["pool_scale"], lw["sgu_ln_g"], lw["sgu_ln_b"], lw["sgu_w"], lw["sgu_b"],
              lw["out_norm"], lw["w_out"], lw["norm2"], lw["w_router"], lw["b_router"]]
    param_specs = [pl.BlockSpec(p.shape, fixed3 if p.ndim == 3 else fixed2) for p in params]
    return pl.pallas_call(
        functools.partial(_mix_kernel, tiles_per_batch=tpb),
        grid=(N // tm,),
        in_specs=[
            pl.BlockSpec((tm, D), row),
            pl.BlockSpec((tm, G), row),
            pl.BlockSpec((tm, 2 * G), row),
            pl.BlockSpec((HALO, 2 * G), halo),
            pl.BlockSpec((tm, G), row),
            pl.BlockSpec((HALO, G), halo),
            pl.BlockSpec((tm, 2 * G), row),
            pl.BlockSpec((1, 6, D), lambda i: (i // tpb, 0, 0)),
        ] + param_specs,
        out_specs=[
            pl.BlockSpec((tm, D), row),
            pl.BlockSpec((tm, D), row),
            pl.BlockSpec((tm, LANES), row),
            pl.BlockSpec((8, LANES), fixed2),
        ],
        out_shape=[
            jax.ShapeDtypeStruct((N, D), F32),
            jax.ShapeDtypeStruct((N, D), F32),
            jax.ShapeDtypeStruct((N, LANES), F32),
            jax.ShapeDtypeStruct((8, LANES), F32),
        ],
        scratch_shapes=[pltpu.VMEM((tm + HALO, G), F32) for _ in range(5)] + [pltpu.VMEM((8, LANES), F32)],
        compiler_params=_cparams(("arbitrary",)),
        name="mixers_out_router",
    )(x2, oa, pb, pb, pc, pc, pd, mod, *params)


def _row_copy(src_hbm, row, dst_ref, slot, r, sem):
    return pltpu.make_async_copy(src_hbm.at[pl.ds(row, 1), :], dst_ref.at[slot, pl.ds(r, 1), :], sem.at[slot])


def _start_rows(idx_ref, src_hbm, dst_ref, slot, sem, n):
    def body(r, carry):
        _row_copy(src_hbm, idx_ref[0, 0, r], dst_ref, slot, r, sem).start()
        return carry
    lax.fori_loop(0, n, body, 0)


def _wait_rows(src_hbm, dst_ref, slot, sem, n):
    def body(r, carry):
        _row_copy(src_hbm, 0, dst_ref, slot, r, sem).wait()
        return carry
    lax.fori_loop(0, n, body, 0)


def _expert_kernel(be_ref, nu_ref, tok_ref, tokn_ref, h_hbm, wg_ref, wu_ref, wd_ref, y_ref,
                   xbuf, sem, wg_bf, wu_bf, wd_bf):
    i = pl.program_id(0)
    n_used = nu_ref[0]
    slot = i % 2
    rows = MOE_ROWS

    @pl.when(jnp.logical_and(i == 0, n_used > 0))
    def _first():
        _start_rows(tok_ref, h_hbm, xbuf, 0, sem, rows)

    @pl.when(i + 1 < n_used)
    def _prefetch():
        _start_rows(tokn_ref, h_hbm, xbuf, 1 - slot, sem, rows)

    changed = jnp.logical_or(i == 0, be_ref[i] != be_ref[jnp.maximum(i - 1, 0)])

    @pl.when(jnp.logical_and(changed, i < n_used))
    def _cast():
        wg_bf[...] = wg_ref[0].astype(BF16)
        wu_bf[...] = wu_ref[0].astype(BF16)
        wd_bf[...] = wd_ref[0].astype(BF16)

    @pl.when(i < n_used)
    def _compute():
        _wait_rows(h_hbm, xbuf, slot, sem, rows)
        xb = xbuf[slot].astype(BF16)
        g = jnp.dot(xb, wg_bf[...], preferred_element_type=F32)
        u = jnp.dot(xb, wu_bf[...], preferred_element_type=F32)
        hid = (_silu(g) * u).astype(BF16)
        y_ref[...] = jnp.dot(hid, wd_bf[...], preferred_element_type=F32)

    @pl.when(i >= n_used)
    def _skip():
        y_ref[...] = jnp.zeros_like(y_ref)


def _expert_call(block_expert, n_used, row_tok2, h2, w_gate, w_up, w_down):
    n_blocks, _, rows = row_tok2.shape
    N, D = h2.shape
    FF = w_gate.shape[-1]
    grid_spec = pltpu.PrefetchScalarGridSpec(
        num_scalar_prefetch=2,
        grid=(n_blocks,),
        in_specs=[
            pl.BlockSpec((1, 1, rows), lambda i, be, nu: (i, 0, 0), memory_space=pltpu.SMEM),
            pl.BlockSpec((1, 1, rows), lambda i, be, nu: (jnp.minimum(i + 1, n_blocks - 1), 0, 0), memory_space=pltpu.SMEM),
            pl.BlockSpec(memory_space=pl.ANY),
            pl.BlockSpec((1, D, FF), lambda i, be, nu: (be[i], 0, 0)),
            pl.BlockSpec((1, D, FF), lambda i, be, nu: (be[i], 0, 0)),
            pl.BlockSpec((1, FF, D), lambda i, be, nu: (be[i], 0, 0)),
        ],
        out_specs=pl.BlockSpec((rows, D), lambda i, be, nu: (i, 0)),
        scratch_shapes=[
            pltpu.VMEM((2, rows, D), F32),
            pltpu.SemaphoreType.DMA((2,)),
            pltpu.VMEM((D, FF), BF16),
            pltpu.VMEM((D, FF), BF16),
            pltpu.VMEM((FF, D), BF16),
        ],
    )
    return pl.pallas_call(
        _expert_kernel,
        grid_spec=grid_spec,
        out_shape=jax.ShapeDtypeStruct((n_blocks * rows, D), F32),
        compiler_params=_cparams(("arbitrary",)),
        name="expert_mlp",
    )(block_expert, n_used, row_tok2, row_tok2, h2, w_gate, w_up, w_down)


def _combine_kernel(pos_ref, posn_ref, y_hbm, x_ref, route_ref, mod_ref, o_ref, ybuf, sem):
    i = pl.program_id(0)
    n = pl.num_programs(0)
    slot = i % 2
    tm = TM_COMB

    @pl.when(i == 0)
    def _first():
        _start_rows(pos_ref, y_hbm, ybuf, 0, sem, 2 * tm)

    @pl.when(i + 1 < n)
    def _prefetch():
        _start_rows(posn_ref, y_hbm, ybuf, 1 - slot, sem, 2 * tm)

    _wait_rows(y_hbm, ybuf, slot, sem, 2 * tm)
    route = route_ref[...]
    y = route[:, 4:5] * ybuf[slot, 0:tm, :] + route[:, 5:6] * ybuf[slot, tm:2 * tm, :]
    o_ref[...] = x_ref[...] + mod_ref[0, 5:6, :] * y


def _combine_call(pos2, y_rows, x2, route, mod, seq):
    N, D = x2.shape
    tm = TM_COMB
    tpb = seq // tm
    nt = N // tm
    row = lambda i: (i, 0)
    return pl.pallas_call(
        _combine_kernel,
        grid=(nt,),
        in_specs=[
            pl.BlockSpec((1, 1, 2 * tm), lambda i: (i, 0, 0), memory_space=pltpu.SMEM),
            pl.BlockSpec((1, 1, 2 * tm), lambda i: (jnp.minimum(i + 1, nt - 1), 0, 0), memory_space=pltpu.SMEM),
            pl.BlockSpec(memory_space=pl.ANY),
            pl.BlockSpec((tm, D), row),
            pl.BlockSpec((tm, LANES), row),
            pl.BlockSpec((1, 6, D), lambda i: (i // tpb, 0, 0)),
        ],
        out_specs=pl.BlockSpec((tm, D), row),
        out_shape=jax.ShapeDtypeStruct((N, D), F32),
        scratch_shapes=[pltpu.VMEM((2, 2 * tm, D), F32), pltpu.SemaphoreType.DMA((2,))],
        compiler_params=_cparams(("arbitrary",)),
        name="moe_combine",
    )(pos2, pos2, y_rows, x2, route, mod)


def _rope_lane_tables(positions, rot_dim, head_w, n_rep):
    half = rot_dim // 2
    inv = jnp.power(jnp.float32(ROPE_THETA), -2.0 * jnp.arange(half, dtype=jnp.float32) / rot_dim)
    ang = positions.astype(jnp.float32)[..., None] * inv
    cos, sin = jnp.cos(ang), jnp.sin(ang)
    rest = head_w - rot_dim
    cos_h = jnp.concatenate([cos, cos, jnp.ones(cos.shape[:-1] + (rest,), F32)], axis=-1)
    sin_h = jnp.concatenate([-sin, sin, jnp.zeros(sin.shape[:-1] + (rest,), F32)], axis=-1)
    n = positions.shape[0] * positions.shape[1]
    return (jnp.tile(cos_h, (1, 1, n_rep)).reshape(n, head_w * n_rep),
            jnp.tile(sin_h, (1, 1, n_rep)).reshape(n, head_w * n_rep))


def _layer_weights(l, w_in, q_norm, k_norm, conv_w, conv_b, conv_ln_g, conv_ln_b, conv_pw_w, conv_pw_b,
                   pool_w, pool_b, pool_scale, sgu_ln_g, sgu_ln_b, sgu_w, sgu_b, out_norm, w_out, norm2,
                   w_rg, b_rg, w_re, b_re):
    G = GROUP_W
    D = D_MODEL
    pts = np.cumsum(IN_SPLITS)[:-1].tolist()
    wq, wk, wv, wiq, wik, wiw, wb, wc, wd = jnp.split(w_in[l], pts, axis=-1)
    wiw_p = jnp.pad(wiw, ((0, 0), (0, LANES - IDX_HEADS)))
    w_in_p = jnp.concatenate([wq, wk, wv, wiq, jnp.tile(wik, (1, IDX_HEADS)), wiw_p, wb, wc, wd], axis=-1).astype(BF16)
    npool = len(POOL_WINDOWS)
    pool_bd = jnp.zeros((G, G), F32)
    for g in range(npool):
        pool_bd = lax.dynamic_update_slice(pool_bd, pool_w[l, g], (g * POOL_CH, g * POOL_CH))
    sgu_bias = jnp.repeat(sgu_b[l].T, G // SGU_HEADS, axis=1)
    w_router = jnp.concatenate([w_rg[l], w_re[l].reshape(D, N_EXPERTS),
                                jnp.zeros((D, LANES - N_EXP_GROUPS - N_EXPERTS), F32)], axis=-1).astype(BF16)
    b_router = jnp.concatenate([b_rg[l], b_re[l].reshape(N_EXPERTS),
                                jnp.zeros((LANES - N_EXP_GROUPS - N_EXPERTS,), F32)]).reshape(1, LANES)
    r1 = lambda a: a.reshape(1, -1)
    return dict(
        w_in=w_in_p,
        q_norm=jnp.tile(q_norm[l], ATT_HEADS).reshape(1, G), k_norm=jnp.tile(k_norm[l], ATT_HEADS).reshape(1, G),
        conv_w=conv_w[l], conv_b=r1(conv_b[l]), conv_ln_g=r1(conv_ln_g[l]), conv_ln_b=r1(conv_ln_b[l]),
        conv_pw_w=conv_pw_w[l].astype(BF16), conv_pw_b=r1(conv_pw_b[l]),
        pool_w=pool_bd.astype(BF16), pool_b=r1(pool_b[l]), pool_scale=r1(pool_scale[l]),
        sgu_ln_g=r1(sgu_ln_g[l]), sgu_ln_b=r1(sgu_ln_b[l]), sgu_w=sgu_w[l], sgu_b=sgu_bias,
        out_norm=r1(out_norm[l]), w_out=w_out[l].astype(BF16), norm2=r1(norm2[l]),
        w_router=w_router, b_router=b_router,
    )


def _dispatch_tables(route, cnt, n_tokens):
    rows_blk = MOE_ROWS
    e = route[:, 0:2].astype(jnp.int32)
    rank = route[:, 2:4].astype(jnp.int32)
    counts = cnt[0, :N_EXPERTS].astype(jnp.int32)
    padded = (counts + rows_blk - 1) // rows_blk * rows_blk
    pad_end = jnp.cumsum(padded)
    pad_start = pad_end - padded
    pos = pad_start[e] + rank
    m = n_tokens * MOE_TOPK
    n_blocks = (m + N_EXPERTS * (rows_blk - 1) + rows_blk - 1) // rows_blk
    n_used = (pad_end[-1] // rows_blk).astype(jnp.int32).reshape(1)
    block_expert = jnp.minimum(
        jnp.searchsorted(pad_end, jnp.arange(n_blocks, dtype=jnp.int32) * rows_blk, side="right"),
        N_EXPERTS - 1).astype(jnp.int32)
    tok = jnp.repeat(jnp.arange(n_tokens, dtype=jnp.int32), MOE_TOPK)
    row_tok = jnp.zeros((n_blocks * rows_blk,), jnp.int32).at[pos.reshape(-1)].set(tok, unique_indices=True)
    pos2 = pos.reshape(n_tokens // TM_COMB, TM_COMB, MOE_TOPK).transpose(0, 2, 1).reshape(-1, 1, MOE_TOPK * TM_COMB)
    return block_expert, n_used, row_tok.reshape(n_blocks, 1, rows_blk), pos2


def kernel(x, c, positions, w_ada, b_ada, norm1, w_in, q_norm, k_norm, conv_w, conv_b, conv_ln_g, conv_ln_b, conv_pw_w, conv_pw_b, pool_w, pool_b, pool_scale, sgu_ln_g, sgu_ln_b, sgu_w, sgu_b, out_norm, w_out, norm2, w_rg, b_rg, w_re, b_re, w_gate, w_up, w_down):
    B, S, D = x.shape
    N = B * S
    assert D == D_MODEL and S % TM_PROJ == 0 and S % TK_ATT == 0 and N % TM_COMB == 0
    depth = w_ada.shape[0]
    cos_a, sin_a = _rope_lane_tables(positions, ROPE_DIM, ATT_HEAD_DIM, ATT_HEADS)
    cos_i, sin_i = _rope_lane_tables(positions, IDX_ROPE_DIM, IDX_DIM, IDX_HEADS)
    c_pad = jnp.pad(c, ((0, (-B) % 8), (0, 0)))
    mod_all = _ada_call(c_pad, w_ada, b_ada)
    x2 = x.reshape(N, D)
    for l in range(depth):
        lw = _layer_weights(l, w_in, q_norm, k_norm, conv_w, conv_b, conv_ln_g, conv_ln_b, conv_pw_w, conv_pw_b,
                            pool_w, pool_b, pool_scale, sgu_ln_g, sgu_ln_b, sgu_w, sgu_b, out_norm, w_out, norm2,
                            w_rg, b_rg, w_re, b_re)
        mod = mod_all[l, :B].reshape(B, 6, D)
        q, k, v, iq, ik, iw, pb, pc, pd = _proj_call(
            x2, mod, norm1[l].reshape(1, D), lw["w_in"], lw["q_norm"], lw["k_norm"], cos_a, sin_a, cos_i, sin_i, S)
        oa = _dsa_call(q, k, v, iq, ik, iw, B, S)
        x_mid, h2, route, cnt = _mix_call(x2, oa, pb, pc, pd, mod, lw, S)
        block_expert, n_used, row_tok2, pos2 = _dispatch_tables(route, cnt, N)
        y_rows = _expert_call(block_expert, n_used, row_tok2, h2, w_gate[l], w_up[l], w_down[l])
        x2 = _combine_call(pos2, y_rows, x_mid, route, mod, S)
    return x2.reshape(B, S, D)
```

```python
import functools

import numpy as np
import jax
import jax.numpy as jnp
from jax import lax
from jax.experimental import pallas as pl
from jax.experimental.pallas import tpu as pltpu

F32 = jnp.float32
BF16 = jnp.bfloat16
NEG_INF = float("-inf")

D_MODEL = 1024
DEPTH = 2
CHUNK = 64
N_MIXERS = 4
GROUP_W = D_MODEL // N_MIXERS
ATT_HEAD_DIM = 64
ATT_HEADS = GROUP_W // ATT_HEAD_DIM
ROPE_DIM = ATT_HEAD_DIM // 4
ROPE_THETA = 500000.0
IDX_HEADS = 4
IDX_DIM = 32
IDX_ROPE_DIM = IDX_DIM // 4
TOPK_MAX = 256
CONV_WIDTH = 31
CONV_GROUPS = 4
POOL_WINDOWS = (2, 4, 8, 16)
POOL_CH = GROUP_W // 4
SGU_CHUNK = 128
SGU_HEADS = 4
N_EXP_GROUPS = 4
EXP_PER_GROUP = 8
N_EXPERTS = N_EXP_GROUPS * EXP_PER_GROUP
EXPERT_FF = 512
MOE_TOPK = 2
EPS = 1e-6
IN_SPLITS = (GROUP_W, GROUP_W, GROUP_W, IDX_HEADS * IDX_DIM, IDX_DIM, IDX_HEADS, 2 * GROUP_W, GROUP_W, 2 * GROUP_W)

LANES = 128
HALO = 32
W_IN_COLS = 3 * GROUP_W + 3 * LANES + 5 * GROUP_W

TM_PROJ = 512
TQ_ATT = 128
TK_ATT = 512
TM_MIX = 256
MOE_ROWS = 256
TM_COMB = 256
BISECT_ITERS = 24
VMEM_LIMIT = 56 * 1024 * 1024


def _cparams(sem):
    return pltpu.CompilerParams(dimension_semantics=sem, vmem_limit_bytes=VMEM_LIMIT)


def _lane_iota(shape):
    return lax.broadcasted_iota(jnp.int32, shape, len(shape) - 1)


def _seg_mean(y, width):
    shift = int(np.log2(width))
    grp = _lane_iota(y.shape) >> shift
    out = jnp.zeros_like(y)
    for g in range(y.shape[-1] // width):
        msk = grp == g
        s = jnp.sum(jnp.where(msk, y, 0.0), axis=-1, keepdims=True)
        out = jnp.where(msk, s, out)
    return out * (1.0 / width)


def _rope(x, cos_f, sin_s, head_w, half):
    c = x.shape[-1]
    lane = _lane_iota(x.shape) & (head_w - 1)
    partner = jnp.where(lane < half, pltpu.roll(x, c - half, 1), pltpu.roll(x, half, 1))
    return x * cos_f + partner * sin_s


def _silu(x):
    return x * jax.nn.sigmoid(x)


def _ada_kernel(c_ref, w_ref, b_ref, o_ref):
    ca = _silu(c_ref[...])
    o_ref[0] = jnp.dot(ca.astype(BF16), w_ref[0].astype(BF16), preferred_element_type=F32) + b_ref[0]


def _ada_call(c_pad, w_ada, b_ada):
    L, D, D6 = w_ada.shape
    rows = c_pad.shape[0]
    tn = D
    return pl.pallas_call(
        _ada_kernel,
        grid=(L, D6 // tn),
        in_specs=[
            pl.BlockSpec((rows, D), lambda l, j: (0, 0)),
            pl.BlockSpec((1, D, tn), lambda l, j: (l, 0, j)),
            pl.BlockSpec((1, 1, tn), lambda l, j: (l, 0, j)),
        ],
        out_specs=pl.BlockSpec((1, rows, tn), lambda l, j: (l, 0, j)),
        out_shape=jax.ShapeDtypeStruct((L, rows, D6), F32),
        compiler_params=_cparams(("arbitrary", "arbitrary")),
        name="ada_mod",
    )(c_pad, w_ada, b_ada.reshape(L, 1, D6))


def _proj_kernel(x_ref, mod_ref, n1_ref, w_ref, qn_ref, kn_ref, cosa_ref, sina_ref, cosi_ref, sini_ref,
                 q_ref, k_ref, v_ref, iq_ref, ik_ref, iw_ref, pb_ref, pc_ref, pd_ref):
    x = x_ref[...]
    ms = jnp.mean(x * x, axis=-1, keepdims=True)
    h = (x * lax.rsqrt(ms + EPS)) * n1_ref[...]
    h = h * (1.0 + mod_ref[0, 1:2, :]) + mod_ref[0, 0:1, :]
    proj = jnp.dot(h.astype(BF16), w_ref[...], preferred_element_type=F32)
    G = GROUP_W
    cos_a, sin_a = cosa_ref[...], sina_ref[...]
    cos_i, sin_i = cosi_ref[...], sini_ref[...]

    def qk(t, g_ref):
        tn = (t * lax.rsqrt(_seg_mean(t * t, ATT_HEAD_DIM) + EPS)) * g_ref[...]
        return _rope(tn, cos_a, sin_a, ATT_HEAD_DIM, ROPE_DIM // 2)

    q_ref[...] = (qk(proj[:, 0:G], qn_ref) * (ATT_HEAD_DIM ** -0.5)).T.astype(BF16)
    k_ref[...] = qk(proj[:, G:2 * G], kn_ref).astype(BF16)
    v_ref[0] = proj[:, 2 * G:3 * G].T.astype(BF16)
    o = 3 * G
    iq_ref[...] = _rope(proj[:, o:o + LANES], cos_i, sin_i, IDX_DIM, IDX_ROPE_DIM // 2).T.astype(BF16)
    ik_ref[...] = _rope(proj[:, o + LANES:o + 2 * LANES], cos_i, sin_i, IDX_DIM, IDX_ROPE_DIM // 2).astype(BF16)
    iw_ref[...] = (proj[:, o + 2 * LANES:o + 3 * LANES] * (IDX_HEADS ** -0.5)).T[0:8, :]
    o += 3 * LANES
    pb_ref[...] = proj[:, o:o + 2 * G]
    pc_ref[...] = proj[:, o + 2 * G:o + 3 * G]
    pd_ref[...] = proj[:, o + 3 * G:o + 5 * G]


def _proj_call(x2, mod, n1, w_in_p, qn_t, kn_t, cos_a, sin_a, cos_i, sin_i, seq):
    N, D = x2.shape
    tm = TM_PROJ
    tpb = seq // tm
    G = GROUP_W
    assert tm == TK_ATT
    row = lambda i: (i, 0)
    col = lambda i: (0, i)
    fixed = lambda i: (0, 0)
    sds = jax.ShapeDtypeStruct
    out_specs = [
        pl.BlockSpec((G, tm), col),
        pl.BlockSpec((tm, G), row),
        pl.BlockSpec((1, G, tm), lambda i: (i, 0, 0)),
        pl.BlockSpec((LANES, tm), col),
        pl.BlockSpec((tm, LANES), row),
        pl.BlockSpec((8, tm), col),
        pl.BlockSpec((tm, 2 * G), row),
        pl.BlockSpec((tm, G), row),
        pl.BlockSpec((tm, 2 * G), row),
    ]
    out_shape = [sds((G, N), BF16), sds((N, G), BF16), sds((N // tm, G, tm), BF16), sds((LANES, N), BF16),
                 sds((N, LANES), BF16), sds((8, N), F32), sds((N, 2 * G), F32), sds((N, G), F32), sds((N, 2 * G), F32)]
    return pl.pallas_call(
        _proj_kernel,
        grid=(N // tm,),
        in_specs=[
            pl.BlockSpec((tm, D), row),
            pl.BlockSpec((1, 6, D), lambda i: (i // tpb, 0, 0)),
            pl.BlockSpec((1, D), fixed),
            pl.BlockSpec((D, W_IN_COLS), fixed),
            pl.BlockSpec((1, G), fixed),
            pl.BlockSpec((1, G), fixed),
            pl.BlockSpec((tm, G), row),
            pl.BlockSpec((tm, G), row),
            pl.BlockSpec((tm, LANES), row),
            pl.BlockSpec((tm, LANES), row),
        ],
        out_specs=out_specs,
        out_shape=out_shape,
        compiler_params=_cparams(("parallel",)),
        name="norm_in_proj",
    )(x2, mod, n1, w_in_p, qn_t, kn_t, cos_a, sin_a, cos_i, sin_i)


def _pair_rhs(xt, head_rows, h0):
    head = lax.broadcasted_iota(jnp.int32, xt.shape, 0) >> int(np.log2(head_rows))
    zero = jnp.zeros_like(xt)
    return jnp.concatenate([jnp.where(head == h0, xt, zero), jnp.where(head == h0 + 1, xt, zero)], axis=1)


def _dsa_kernel(q_ref, k_ref, v_ref, iq_ref, ik_ref, iw_ref, o_ref, sc_ref, *, topk):
    tq, tk = TQ_ATT, TK_ATT
    i = pl.program_id(1)
    q0 = i * tq
    n_kv = (q0 + tq + tk - 1) // tk
    kf = float(topk)
    n_keys_max = sc_ref.shape[0] * tk

    q_pos = _lane_iota((1, tq)) + q0
    key_end = ((q_pos >> 6) + 1) << 6
    key_i = lax.broadcasted_iota(jnp.int32, (tk, tq), 0)

    def fold8(x, op):
        parts = [x[r * 8:(r + 1) * 8, :] for r in range(tk // 8)]
        while len(parts) > 1:
            parts = [op(parts[a], parts[a + 1]) for a in range(0, len(parts), 2)]
        return parts[0]

    iqt = iq_ref[...]
    iq_pairs = [_pair_rhs(iqt, IDX_DIM, h0) for h0 in range(0, IDX_HEADS, 2)]
    iw_h = [iw_ref[h:h + 1, :] for h in range(IDX_HEADS)]

    def score_body(kc, carry):
        ikc = ik_ref[pl.ds(pl.multiple_of(kc * tk, tk), tk), :]
        s = jnp.zeros((tk, tq), F32)
        for pi, rhs in enumerate(iq_pairs):
            d2 = jnp.dot(ikc, rhs, preferred_element_type=F32)
            for j in range(2):
                d = d2[:, j * tq:(j + 1) * tq]
                s = s + jnp.maximum(d * (IDX_DIM ** -0.5), 0.0) * iw_h[2 * pi + j]
        sc_ref[kc] = jnp.where(key_i + kc * tk < key_end, s, NEG_INF)
        return carry

    lax.fori_loop(0, n_kv, score_body, 0)

    def reduce_chunks(fn, init, combine, fold):
        def body(kc, part):
            return combine(part, fold8(fn(sc_ref[kc], kc * tk), combine))
        part = lax.fori_loop(0, n_kv, body, jnp.full((8, tq), init, F32))
        return fold(part, axis=0, keepdims=True)

    def count(ind):
        return reduce_chunks(ind, 0.0, jnp.add, jnp.sum)

    def col_maximum(val):
        return reduce_chunks(val, NEG_INF, jnp.maximum, jnp.max)

    small = key_end <= topk

    @pl.when(q0 + tq > topk)
    def _select():
        col_max = col_maximum(lambda b, off: b)
        col_min = -col_maximum(lambda b, off: jnp.where(b == NEG_INF, NEG_INF, -b))

        def bis_body(_, c):
            lo, hi = c
            mid = jnp.where(hi == jnp.inf, col_max, lo + (hi - lo) * 0.5)
            ge = count(lambda b, off: jnp.where(b >= mid, 1.0, 0.0)) >= kf
            return jnp.where(ge, mid, lo), jnp.where(ge, hi, mid)

        lo, hi = lax.fori_loop(0, BISECT_ITERS, bis_body, (col_min, jnp.full((1, tq), jnp.inf, F32)))

        def sd_cond(c):
            return c[0] > 0.0

        def sd_body(c):
            _, hi, thr, done = c
            cand = col_maximum(lambda b, off: jnp.where(b < hi, b, NEG_INF))
            ok = count(lambda b, off: jnp.where(b >= cand, 1.0, 0.0)) >= kf
            thr = jnp.where(done > 0.0, thr, cand)
            hi = jnp.where(done > 0.0, hi, cand)
            done = jnp.where(ok, 1.0, done)
            return jnp.sum(1.0 - done), hi, thr, done

        done0 = jnp.where(small, 1.0, 0.0)
        n0 = jnp.sum(1.0 - done0)
        _, _, thr, _ = lax.while_loop(sd_cond, sd_body, (n0, hi, jnp.full((1, tq), NEG_INF, F32), done0))
        thr = jnp.where(small, NEG_INF, thr)

        need = kf - count(lambda b, off: jnp.where(b > thr, 1.0, 0.0))
        n_tied = count(lambda b, off: jnp.where(b == thr, 1.0, 0.0))
        key_f = key_i.astype(F32)

        def tie_search():
            def tie_body(_, c):
                mlo, mhi = c
                mid = jnp.floor((mlo + mhi) * 0.5)
                cnt = count(lambda b, off: jnp.where(
                    b == thr, jnp.where(key_f + off.astype(F32) < mid, 1.0, 0.0), 0.0))
                ge = cnt >= need
                return jnp.where(ge, mlo, mid), jnp.where(ge, mid, mhi)

            n_bits = int(np.ceil(np.log2(n_keys_max))) + 1
            return lax.fori_loop(0, n_bits, tie_body,
                                 (jnp.zeros((1, tq), F32), jnp.full((1, tq), float(n_keys_max), F32)))[1]

        excess = jnp.sum(jnp.where(jnp.where(small, 0.0, n_tied) > need, 1.0, 0.0))
        cut = lax.cond(excess > 0.0, tie_search, lambda: jnp.full((1, tq), float(n_keys_max), F32))

        def bias_body(kc, carry):
            blk = sc_ref[kc]
            idx = key_f + (kc * tk).astype(F32)
            tie = jnp.where(blk == thr, jnp.where(idx < cut, 0.0, NEG_INF), NEG_INF)
            bias = jnp.where(blk > thr, 0.0, tie)
            sc_ref[kc] = jnp.where(blk == NEG_INF, NEG_INF, bias)
            return carry

        lax.fori_loop(0, n_kv, bias_body, 0)

    @pl.when(q0 + tq <= topk)
    def _all():
        def bias_body(kc, carry):
            sc_ref[kc] = jnp.where(sc_ref[kc] == NEG_INF, NEG_INF, 0.0)
            return carry
        lax.fori_loop(0, n_kv, bias_body, 0)

    qt = q_ref[...]
    q_pairs = [_pair_rhs(qt, ATT_HEAD_DIM, h0) for h0 in range(0, ATT_HEADS, 2)]
    dh = ATT_HEAD_DIM

    def att_body(kc, c):
        ms, ls, accs = c
        kblk = k_ref[pl.ds(pl.multiple_of(kc * tk, tk), tk), :]
        bias = sc_ref[kc]
        vt = v_ref[kc]
        ms_n, ls_n, accs_n = [], [], []
        for pi, rhs in enumerate(q_pairs):
            s2 = jnp.dot(kblk, rhs, preferred_element_type=F32)
            for j in range(2):
                h = 2 * pi + j
                s = s2[:, j * tq:(j + 1) * tq] + bias
                m_new = jnp.maximum(ms[h], jnp.max(fold8(s, jnp.maximum), axis=0, keepdims=True))
                m_safe = jnp.where(m_new == NEG_INF, 0.0, m_new)
                alpha = jnp.exp(ms[h] - m_safe)
                p = jnp.exp(s - m_safe)
                ls_n.append(alpha * ls[h] + jnp.sum(fold8(p, jnp.add), axis=0, keepdims=True))
                pv = jnp.dot(vt[h * dh:(h + 1) * dh, :], p.astype(BF16), preferred_element_type=F32)
                accs_n.append(alpha * accs[h] + pv)
                ms_n.append(m_new)
        return tuple(ms_n), tuple(ls_n), tuple(accs_n)

    init = (tuple(jnp.full((1, tq), NEG_INF, F32) for _ in range(ATT_HEADS)),
            tuple(jnp.zeros((1, tq), F32) for _ in range(ATT_HEADS)),
            tuple(jnp.zeros((dh, tq), F32) for _ in range(ATT_HEADS)))
    _, ls, accs = lax.fori_loop(0, n_kv, att_body, init)
    out_t = jnp.concatenate([accs[h] / ls[h] for h in range(ATT_HEADS)], axis=0)
    o_ref[...] = out_t.T


def _dsa_call(qt, k, vt, iqt, ik, iwt, batch, seq):
    N, G = k.shape
    tq = TQ_ATT
    nq = seq // tq
    nkc = seq // TK_ATT
    topk = min(TOPK_MAX, seq // 4)
    qcol = lambda b, i: (0, b * nq + i)
    brow = lambda b, i: (b, 0)
    return pl.pallas_call(
        functools.partial(_dsa_kernel, topk=topk),
        grid=(batch, nq),
        in_specs=[
            pl.BlockSpec((G, tq), qcol),
            pl.BlockSpec((seq, G), brow),
            pl.BlockSpec((nkc, G, TK_ATT), lambda b, i: (b, 0, 0)),
            pl.BlockSpec((LANES, tq), qcol),
            pl.BlockSpec((seq, LANES), brow),
            pl.BlockSpec((8, tq), qcol),
        ],
        out_specs=pl.BlockSpec((tq, G), lambda b, i: (b * nq + i, 0)),
        out_shape=jax.ShapeDtypeStruct((N, G), F32),
        scratch_shapes=[pltpu.VMEM((nkc, TK_ATT, tq), F32)],
        compiler_params=_cparams(("parallel", "arbitrary")),
        name="dsa_attention",
    )(qt, k, vt, iqt, ik, iwt)


def _mix_kernel(x_ref, oa_ref, pb_ref, pbh_ref, pc_ref, pch_ref, pd_ref, mod_ref,
                cw_ref, cb_ref, clg_ref, clb_ref, cpw_ref, cpb_ref,
                pw_ref, pbias_ref, ps_ref, slg_ref, slb_ref, sw_ref, sb_ref,
                on_ref, wo_ref, n2_ref, wr_ref, br_ref,
                xo_ref, h2_ref, route_ref, cnt_ref,
                ypad_ref, ppad_ref, s2_ref, s4_ref, s8_ref, run_ref, *, tiles_per_batch):
    tm = TM_MIX
    G = GROUP_W
    i = pl.program_id(0)
    t_in_b = i % tiles_per_batch
    first = t_in_b == 0
    lane_g = _lane_iota((tm, G))

    def glu(pb):
        return pb[:, 0:G] * jax.nn.sigmoid(pb[:, G:2 * G])

    ypad_ref[0:HALO, :] = jnp.where(first, 0.0, glu(pbh_ref[...]))
    ypad_ref[HALO:HALO + tm, :] = glu(pb_ref[...])
    acc = jnp.zeros((tm, G), F32)
    for j in range(CONV_WIDTH):
        off = HALO - (CONV_WIDTH - 1) + j
        acc = acc + cw_ref[j:j + 1, :] * ypad_ref[off:off + tm, :]
    y = acc + cb_ref[...]
    gw = G // CONV_GROUPS
    mu = _seg_mean(y, gw)
    yc = y - mu
    var = _seg_mean(yc * yc, gw)
    y = (yc * lax.rsqrt(var + EPS)) * clg_ref[...] + clb_ref[...]
    o_b = jnp.dot(_silu(y).astype(BF16), cpw_ref[...], preferred_element_type=F32) + cpb_ref[...]

    p = pc_ref[...]
    ppad_ref[0:HALO, :] = jnp.where(first, 0.0, pch_ref[...])
    ppad_ref[HALO:HALO + tm, :] = p
    n8 = tm + HALO - 8
    s2_ref[8:8 + n8, :] = ppad_ref[8:8 + n8, :] + ppad_ref[7:7 + n8, :]
    n16 = tm + HALO - 16
    s4_ref[16:16 + n16, :] = s2_ref[16:16 + n16, :] + s2_ref[14:14 + n16, :]
    n24 = tm + HALO - 24
    s8_ref[24:24 + n24, :] = s4_ref[24:24 + n24, :] + s4_ref[20:20 + n24, :]
    s2 = s2_ref[HALO:HALO + tm, :]
    s4 = s4_ref[HALO:HALO + tm, :]
    s8 = s8_ref[HALO:HALO + tm, :]
    s16 = s8 + s8_ref[HALO - 8:HALO - 8 + tm, :]
    pgrp = lane_g >> 6
    wsum = jnp.where(pgrp == 0, s2, jnp.where(pgrp == 1, s4, jnp.where(pgrp == 2, s8, s16)))
    wlen = jnp.where(pgrp == 0, 2.0, jnp.where(pgrp == 1, 4.0, jnp.where(pgrp == 2, 8.0, 16.0)))
    tpos = (lax.broadcasted_iota(jnp.int32, (tm, G), 0) + t_in_b * tm + 1).astype(F32)
    pooled = wsum / jnp.minimum(tpos, wlen) - p
    o_c = (jnp.dot(pooled.astype(BF16), pw_ref[...], preferred_element_type=F32) + pbias_ref[...]) * ps_ref[...]

    pd = pd_ref[...]
    u, v = pd[:, 0:G], pd[:, G:2 * G]
    mu = jnp.mean(v, axis=-1, keepdims=True)
    vc = v - mu
    var = jnp.mean(vc * vc, axis=-1, keepdims=True)
    vn = ((vc * lax.rsqrt(var + EPS)) * slg_ref[...] + slb_ref[...]).astype(BF16)
    r_i = lax.broadcasted_iota(jnp.int32, (SGU_CHUNK, SGU_CHUNK), 0)
    c_i = lax.broadcasted_iota(jnp.int32, (SGU_CHUNK, SGU_CHUNK), 1)
    w_heads = [jnp.where(r_i >= c_i, sw_ref[h], 0.0).astype(BF16) for h in range(SGU_HEADS)]
    lane_c = _lane_iota((SGU_CHUNK, G)) >> 6
    mixed = []
    for n in range(tm // SGU_CHUNK):
        vch = vn[n * SGU_CHUNK:(n + 1) * SGU_CHUNK, :]
        mx = jnp.zeros((SGU_CHUNK, G), F32)
        for h in range(SGU_HEADS):
            mx = jnp.where(lane_c == h, jnp.dot(w_heads[h], vch, preferred_element_type=F32), mx)
        mixed.append(mx + sb_ref[...])
    o_d = u * jnp.concatenate(mixed, axis=0)

    proj = jnp.zeros((tm, D_MODEL), F32)
    for g, piece in enumerate((oa_ref[...], o_b, o_c, o_d)):
        ms = jnp.mean(piece * piece, axis=-1, keepdims=True)
        pn = (piece * lax.rsqrt(ms + EPS)) * on_ref[:, g * G:(g + 1) * G]
        proj = proj + jnp.dot(pn.astype(BF16), wo_ref[g * G:(g + 1) * G, :], preferred_element_type=F32)
    x_new = x_ref[...] + mod_ref[0, 2:3, :] * proj
    xo_ref[...] = x_new

    ms = jnp.mean(x_new * x_new, axis=-1, keepdims=True)
    h2 = (x_new * lax.rsqrt(ms + EPS)) * n2_ref[...]
    h2 = h2 * (1.0 + mod_ref[0, 4:5, :]) + mod_ref[0, 3:4, :]
    h2_ref[...] = h2
    logits = jnp.dot(h2.astype(BF16), wr_ref[...], preferred_element_type=F32) + br_ref[...]
    lane = _lane_iota((tm, LANES))
    lane_f = lane.astype(F32)
    big = float(LANES)
    glog = jnp.where(lane < N_EXP_GROUPS, logits, NEG_INF)
    gmax = jnp.max(glog, axis=-1, keepdims=True)
    p_sel = 1.0 / jnp.sum(jnp.exp(glog - gmax), axis=-1, keepdims=True)
    g_idx = jnp.min(jnp.where(glog == gmax, lane_f, big), axis=-1, keepdims=True)
    e_lane = lane - N_EXP_GROUPS
    elog = jnp.where((e_lane >> 3).astype(F32) == g_idx, logits, NEG_INF)
    top1 = jnp.max(elog, axis=-1, keepdims=True)
    j1 = jnp.min(jnp.where(elog == top1, lane_f, big), axis=-1, keepdims=True)
    elog2 = jnp.where(lane_f == j1, NEG_INF, elog)
    top2 = jnp.max(elog2, axis=-1, keepdims=True)
    j2 = jnp.min(jnp.where(elog2 == top2, lane_f, big), axis=-1, keepdims=True)
    e2w = jnp.exp(top2 - top1)
    gate1 = p_sel * (1.0 / (1.0 + e2w))
    gate2 = p_sel * (e2w / (1.0 + e2w))
    e1 = j1 - float(N_EXP_GROUPS)
    e2 = j2 - float(N_EXP_GROUPS)

    @pl.when(i == 0)
    def _init():
        run_ref[...] = jnp.zeros_like(run_ref)

    onehot = jnp.where(jnp.logical_or(lane_f == e1, lane_f == e2), 1.0, 0.0)
    rr = lax.broadcasted_iota(jnp.int32, (tm, tm), 0)
    cc = lax.broadcasted_iota(jnp.int32, (tm, tm), 1)
    before = jnp.where(rr > cc, 1.0, 0.0).astype(BF16)
    prior = jnp.dot(before, onehot.astype(BF16), preferred_element_type=F32) + run_ref[0:1, :]
    rank1 = jnp.sum(jnp.where(lane_f == e1, prior, 0.0), axis=-1, keepdims=True)
    rank2 = jnp.sum(jnp.where(lane_f == e2, prior, 0.0), axis=-1, keepdims=True)
    run_new = run_ref[0:1, :] + jnp.sum(onehot, axis=0, keepdims=True)
    run_ref[...] = jnp.broadcast_to(run_new, run_ref.shape)
    cnt_ref[...] = jnp.broadcast_to(run_new, cnt_ref.shape)
    route = jnp.where(lane == 0, e1, jnp.where(lane == 1, e2, jnp.where(lane == 2, rank1, jnp.where(
        lane == 3, rank2, jnp.where(lane == 4, gate1, jnp.where(lane == 5, gate2, 0.0))))))
    route_ref[...] = route


def _mix_call(x2, oa, pb, pc, pd, mod, lw, seq):
    N, D = x2.shape
    tm = TM_MIX
    tpb = seq // tm
    G = GROUP_W
    row = lambda i: (i, 0)
    halo = lambda i: (jnp.maximum(i * (tm // HALO) - 1, 0), 0)
    fixed2 = lambda i: (0, 0)
    fixed3 = lambda i: (0, 0, 0)
    params = [lw["conv_w"], lw["conv_b"], lw["conv_ln_g"], lw["conv_ln_b"], lw["conv_pw_w"], lw["conv_pw_b"],
              lw["pool_w"], lw["pool_b"], lw["pool_scale"], lw["sgu_ln_g"], lw["sgu_ln_b"], lw["sgu_w"], lw["sgu_b"],
              lw["out_norm"], lw["w_out"], lw["norm2"], lw["w_router"], lw["b_router"]]
    param_specs = [pl.BlockSpec(p.shape, fixed3 if p.ndim == 3 else fixed2) for p in params]
    return pl.pallas_call(
        functools.partial(_mix_kernel, tiles_per_batch=tpb),
        grid=(N // tm,),
        in_specs=[
            pl.BlockSpec((tm, D), row),
            pl.BlockSpec((tm, G), row),
            pl.BlockSpec((tm, 2 * G), row),
            pl.BlockSpec((HALO, 2 * G), halo),
            pl.BlockSpec((tm, G), row),
            pl.BlockSpec((HALO, G), halo),
            pl.BlockSpec((tm, 2 * G), row),
            pl.BlockSpec((1, 6, D), lambda i: (i // tpb, 0, 0)),
        ] + param_specs,
        out_specs=[
            pl.BlockSpec((tm, D), row),
            pl.BlockSpec((tm, D), row),
            pl.BlockSpec((tm, LANES), row),
            pl.BlockSpec((8, LANES), fixed2),
        ],
        out_shape=[
            jax.ShapeDtypeStruct((N, D), F32),
            jax.ShapeDtypeStruct((N, D), F32),
            jax.ShapeDtypeStruct((N, LANES), F32),
            jax.ShapeDtypeStruct((8, LANES), F32),
        ],
        scratch_shapes=[pltpu.VMEM((tm + HALO, G), F32) for _ in range(5)] + [pltpu.VMEM((8, LANES), F32)],
        compiler_params=_cparams(("arbitrary",)),
        name="mixers_out_router",
    )(x2, oa, pb, pb, pc, pc, pd, mod, *params)


def _row_copy(src_hbm, row, dst_ref, slot, r, sem):
    return pltpu.make_async_copy(src_hbm.at[pl.ds(row, 1), :], dst_ref.at[slot, pl.ds(r, 1), :], sem.at[slot])


def _start_rows(idx_ref, src_hbm, dst_ref, slot, sem, n):
    def body(r, carry):
        _row_copy(src_hbm, idx_ref[0, 0, r], dst_ref, slot, r, sem).start()
        return carry
    lax.fori_loop(0, n, body, 0)


def _wait_rows(src_hbm, dst_ref, slot, sem, n):
    def body(r, carry):
        _row_copy(src_hbm, 0, dst_ref, slot, r, sem).wait()
        return carry
    lax.fori_loop(0, n, body, 0)


def _expert_kernel(be_ref, nu_ref, tok_ref, tokn_ref, h_hbm, wg_ref, wu_ref, wd_ref, y_ref,
                   xbuf, sem, wg_bf, wu_bf, wd_bf):
    i = pl.program_id(0)
    n_used = nu_ref[0]
    slot = i % 2
    rows = MOE_ROWS

    @pl.when(jnp.logical_and(i == 0, n_used > 0))
    def _first():
        _start_rows(tok_ref, h_hbm, xbuf, 0, sem, rows)

    @pl.when(i + 1 < n_used)
    def _prefetch():
        _start_rows(tokn_ref, h_hbm, xbuf, 1 - slot, sem, rows)

    changed = jnp.logical_or(i == 0, be_ref[i] != be_ref[jnp.maximum(i - 1, 0)])

    @pl.when(jnp.logical_and(changed, i < n_used))
    def _cast():
        wg_bf[...] = wg_ref[0].astype(BF16)
        wu_bf[...] = wu_ref[0].astype(BF16)
        wd_bf[...] = wd_ref[0].astype(BF16)

    @pl.when(i < n_used)
    def _compute():
        _wait_rows(h_hbm, xbuf, slot, sem, rows)
        xb = xbuf[slot].astype(BF16)
        g = jnp.dot(xb, wg_bf[...], preferred_element_type=F32)
        u = jnp.dot(xb, wu_bf[...], preferred_element_type=F32)
        hid = (_silu(g) * u).astype(BF16)
        y_ref[...] = jnp.dot(hid, wd_bf[...], preferred_element_type=F32)

    @pl.when(i >= n_used)
    def _skip():
        y_ref[...] = jnp.zeros_like(y_ref)


def _expert_call(block_expert, n_used, row_tok2, h2, w_gate, w_up, w_down):
    n_blocks, _, rows = row_tok2.shape
    N, D = h2.shape
    FF = w_gate.shape[-1]
    grid_spec = pltpu.PrefetchScalarGridSpec(
        num_scalar_prefetch=2,
        grid=(n_blocks,),
        in_specs=[
            pl.BlockSpec((1, 1, rows), lambda i, be, nu: (i, 0, 0), memory_space=pltpu.SMEM),
            pl.BlockSpec((1, 1, rows), lambda i, be, nu: (jnp.minimum(i + 1, n_blocks - 1), 0, 0), memory_space=pltpu.SMEM),
            pl.BlockSpec(memory_space=pl.ANY),
            pl.BlockSpec((1, D, FF), lambda i, be, nu: (be[i], 0, 0)),
            pl.BlockSpec((1, D, FF), lambda i, be, nu: (be[i], 0, 0)),
            pl.BlockSpec((1, FF, D), lambda i, be, nu: (be[i], 0, 0)),
        ],
        out_specs=pl.BlockSpec((rows, D), lambda i, be, nu: (i, 0)),
        scratch_shapes=[
            pltpu.VMEM((2, rows, D), F32),
            pltpu.SemaphoreType.DMA((2,)),
            pltpu.VMEM((D, FF), BF16),
            pltpu.VMEM((D, FF), BF16),
            pltpu.VMEM((FF, D), BF16),
        ],
    )
    return pl.pallas_call(
        _expert_kernel,
        grid_spec=grid_spec,
        out_shape=jax.ShapeDtypeStruct((n_blocks * rows, D), F32),
        compiler_params=_cparams(("arbitrary",)),
        name="expert_mlp",
    )(block_expert, n_used, row_tok2, row_tok2, h2, w_gate, w_up, w_down)


def _combine_kernel(pos_ref, posn_ref, y_hbm, x_ref, route_ref, mod_ref, o_ref, ybuf, sem):
    i = pl.program_id(0)
    n = pl.num_programs(0)
    slot = i % 2
    tm = TM_COMB

    @pl.when(i == 0)
    def _first():
        _start_rows(pos_ref, y_hbm, ybuf, 0, sem, 2 * tm)

    @pl.when(i + 1 < n)
    def _prefetch():
        _start_rows(posn_ref, y_hbm, ybuf, 1 - slot, sem, 2 * tm)

    _wait_rows(y_hbm, ybuf, slot, sem, 2 * tm)
    route = route_ref[...]
    y = route[:, 4:5] * ybuf[slot, 0:tm, :] + route[:, 5:6] * ybuf[slot, tm:2 * tm, :]
    o_ref[...] = x_ref[...] + mod_ref[0, 5:6, :] * y


def _combine_call(pos2, y_rows, x2, route, mod, seq):
    N, D = x2.shape
    tm = TM_COMB
    tpb = seq // tm
    nt = N // tm
    row = lambda i: (i, 0)
    return pl.pallas_call(
        _combine_kernel,
        grid=(nt,),
        in_specs=[
            pl.BlockSpec((1, 1, 2 * tm), lambda i: (i, 0, 0), memory_space=pltpu.SMEM),
            pl.BlockSpec((1, 1, 2 * tm), lambda i: (jnp.minimum(i + 1, nt - 1), 0, 0), memory_space=pltpu.SMEM),
            pl.BlockSpec(memory_space=pl.ANY),
            pl.BlockSpec((tm, D), row),
            pl.BlockSpec((tm, LANES), row),
            pl.BlockSpec((1, 6, D), lambda i: (i // tpb, 0, 0)),
        ],
        out_specs=pl.BlockSpec((tm, D), row),
        out_shape=jax.ShapeDtypeStruct((N, D), F32),
        scratch_shapes=[pltpu.VMEM((2, 2 * tm, D), F32), pltpu.SemaphoreType.DMA((2,))],
        compiler_params=_cparams(("arbitrary",)),
        name="moe_combine",
    )(pos2, pos2, y_rows, x2, route, mod)


def _rope_lane_tables(positions, rot_dim, head_w, n_rep):
    half = rot_dim // 2
    inv = jnp.power(jnp.float32(ROPE_THETA), -2.0 * jnp.arange(half, dtype=jnp.float32) / rot_dim)
    ang = positions.astype(jnp.float32)[..., None] * inv
    cos, sin = jnp.cos(ang), jnp.sin(ang)
    rest = head_w - rot_dim
    cos_h = jnp.concatenate([cos, cos, jnp.ones(cos.shape[:-1] + (rest,), F32)], axis=-1)
    sin_h = jnp.concatenate([-sin, sin, jnp.zeros(sin.shape[:-1] + (rest,), F32)], axis=-1)
    n = positions.shape[0] * positions.shape[1]
    return (jnp.tile(cos_h, (1, 1, n_rep)).reshape(n, head_w * n_rep),
            jnp.tile(sin_h, (1, 1, n_rep)).reshape(n, head_w * n_rep))


def _layer_weights(l, w_in, q_norm, k_norm, conv_w, conv_b, conv_ln_g, conv_ln_b, conv_pw_w, conv_pw_b,
                   pool_w, pool_b, pool_scale, sgu_ln_g, sgu_ln_b, sgu_w, sgu_b, out_norm, w_out, norm2,
                   w_rg, b_rg, w_re, b_re):
    G = GROUP_W
    D = D_MODEL
    pts = np.cumsum(IN_SPLITS)[:-1].tolist()
    wq, wk, wv, wiq, wik, wiw, wb, wc, wd = jnp.split(w_in[l], pts, axis=-1)
    wiw_p = jnp.pad(wiw, ((0, 0), (0, LANES - IDX_HEADS)))
    w_in_p = jnp.concatenate([wq, wk, wv, wiq, jnp.tile(wik, (1, IDX_HEADS)), wiw_p, wb, wc, wd], axis=-1).astype(BF16)
    npool = len(POOL_WINDOWS)
    pool_bd = jnp.zeros((G, G), F32)
    for g in range(npool):
        pool_bd = lax.dynamic_update_slice(pool_bd, pool_w[l, g], (g * POOL_CH, g * POOL_CH))
    sgu_bias = jnp.repeat(sgu_b[l].T, G // SGU_HEADS, axis=1)
    w_router = jnp.concatenate([w_rg[l], w_re[l].reshape(D, N_EXPERTS),
                                jnp.zeros((D, LANES - N_EXP_GROUPS - N_EXPERTS), F32)], axis=-1).astype(BF16)
    b_router = jnp.concatenate([b_rg[l], b_re[l].reshape(N_EXPERTS),
                                jnp.zeros((LANES - N_EXP_GROUPS - N_EXPERTS,), F32)]).reshape(1, LANES)
    r1 = lambda a: a.reshape(1, -1)
    return dict(
        w_in=w_in_p,
        q_norm=jnp.tile(q_norm[l], ATT_HEADS).reshape(1, G), k_norm=jnp.tile(k_norm[l], ATT_HEADS).reshape(1, G),
        conv_w=conv_w[l], conv_b=r1(conv_b[l]), conv_ln_g=r1(conv_ln_g[l]), conv_ln_b=r1(conv_ln_b[l]),
        conv_pw_w=conv_pw_w[l].astype(BF16), conv_pw_b=r1(conv_pw_b[l]),
        pool_w=pool_bd.astype(BF16), pool_b=r1(pool_b[l]), pool_scale=r1(pool_scale[l]),
        sgu_ln_g=r1(sgu_ln_g[l]), sgu_ln_b=r1(sgu_ln_b[l]), sgu_w=sgu_w[l], sgu_b=sgu_bias,
        out_norm=r1(out_norm[l]), w_out=w_out[l].astype(BF16), norm2=r1(norm2[l]),
        w_router=w_router, b_router=b_router,
    )


def _dispatch_tables(route, cnt, n_tokens):
    rows_blk = MOE_ROWS
    e = route[:, 0:2].astype(jnp.int32)
    rank = route[:, 2:4].astype(jnp.int32)
    counts = cnt[0, :N_EXPERTS].astype(jnp.int32)
    padded = (counts + rows_blk - 1) // rows_blk * rows_blk
    pad_end = jnp.cumsum(padded)
    pad_start = pad_end - padded
    pos = pad_start[e] + rank
    m = n_tokens * MOE_TOPK
    n_blocks = (m + N_EXPERTS * (rows_blk - 1) + rows_blk - 1) // rows_blk
    n_used = (pad_end[-1] // rows_blk).astype(jnp.int32).reshape(1)
    block_expert = jnp.minimum(
        jnp.searchsorted(pad_end, jnp.arange(n_blocks, dtype=jnp.int32) * rows_blk, side="right"),
        N_EXPERTS - 1).astype(jnp.int32)
    tok = jnp.repeat(jnp.arange(n_tokens, dtype=jnp.int32), MOE_TOPK)
    row_tok = jnp.zeros((n_blocks * rows_blk,), jnp.int32).at[pos.reshape(-1)].set(tok, unique_indices=True)
    pos2 = pos.reshape(n_tokens // TM_COMB, TM_COMB, MOE_TOPK).transpose(0, 2, 1).reshape(-1, 1, MOE_TOPK * TM_COMB)
    return block_expert, n_used, row_tok.reshape(n_blocks, 1, rows_blk), pos2


def kernel(x, c, positions, w_ada, b_ada, norm1, w_in, q_norm, k_norm, conv_w, conv_b, conv_ln_g, conv_ln_b, conv_pw_w, conv_pw_b, pool_w, pool_b, pool_scale, sgu_ln_g, sgu_ln_b, sgu_w, sgu_b, out_norm, w_out, norm2, w_rg, b_rg, w_re, b_re, w_gate, w_up, w_down):
    B, S, D = x.shape
    N = B * S
    assert D == D_MODEL and S % TM_PROJ == 0 and S % TK_ATT == 0 and N % TM_COMB == 0
    depth = w_ada.shape[0]
    cos_a, sin_a = _rope_lane_tables(positions, ROPE_DIM, ATT_HEAD_DIM, ATT_HEADS)
    cos_i, sin_i = _rope_lane_tables(positions, IDX_ROPE_DIM, IDX_DIM, IDX_HEADS)
    c_pad = jnp.pad(c, ((0, (-B) % 8), (0, 0)))
    mod_all = _ada_call(c_pad, w_ada, b_ada)
    x2 = x.reshape(N, D)
    for l in range(depth):
        lw = _layer_weights(l, w_in, q_norm, k_norm, conv_w, conv_b, conv_ln_g, conv_ln_b, conv_pw_w, conv_pw_b,
                            pool_w, pool_b, pool_scale, sgu_ln_g, sgu_ln_b, sgu_w, sgu_b, out_norm, w_out, norm2,
                            w_rg, b_rg, w_re, b_re)
        mod = mod_all[l, :B].reshape(B, 6, D)
        q, k, v, iq, ik, iw, pb, pc, pd = _proj_call(
            x2, mod, norm1[l].reshape(1, D), lw["w_in"], lw["q_norm"], lw["k_norm"], cos_a, sin_a, cos_i, sin_i, S)
        oa = _dsa_call(q, k, v, iq, ik, iw, B, S)
        x_mid, h2, route, cnt = _mix_call(x2, oa, pb, pc, pd, mod, lw, S)
        block_expert, n_used, row_tok2, pos2 = _dispatch_tables(route, cnt, N)
        y_rows = _expert_call(block_expert, n_used, row_tok2, h2, w_gate[l], w_up[l], w_down[l])
        x2 = _combine_call(pos2, y_rows, x_mid, route, mod, S)
    return x2.reshape(B, S, D)
```

```python
import functools

import numpy as np
import jax
import jax.numpy as jnp
from jax import lax
from jax.experimental import pallas as pl
from jax.experimental.pallas import tpu as pltpu

F32 = jnp.float32
BF16 = jnp.bfloat16
NEG_INF = float("-inf")

D_MODEL = 1024
DEPTH = 2
CHUNK = 64
N_MIXERS = 4
GROUP_W = D_MODEL // N_MIXERS
ATT_HEAD_DIM = 64
ATT_HEADS = GROUP_W // ATT_HEAD_DIM
ROPE_DIM = ATT_HEAD_DIM // 4
ROPE_THETA = 500000.0
IDX_HEADS = 4
IDX_DIM = 32
IDX_ROPE_DIM = IDX_DIM // 4
TOPK_MAX = 256
CONV_WIDTH = 31
CONV_GROUPS = 4
POOL_WINDOWS = (2, 4, 8, 16)
POOL_CH = GROUP_W // 4
SGU_CHUNK = 128
SGU_HEADS = 4
N_EXP_GROUPS = 4
EXP_PER_GROUP = 8
N_EXPERTS = N_EXP_GROUPS * EXP_PER_GROUP
EXPERT_FF = 512
MOE_TOPK = 2
EPS = 1e-6
IN_SPLITS = (GROUP_W, GROUP_W, GROUP_W, IDX_HEADS * IDX_DIM, IDX_DIM, IDX_HEADS, 2 * GROUP_W, GROUP_W, 2 * GROUP_W)

LANES = 128
HALO = 32
W_IN_COLS = 3 * GROUP_W + 3 * LANES + 5 * GROUP_W

TM_PROJ = 512
TQ_ATT = 128
TK_ATT = 512
TM_MIX = 256
MOE_ROWS = 256
TM_COMB = 256
BISECT_ITERS = 24
VMEM_LIMIT = 56 * 1024 * 1024


def _cparams(sem):
    return pltpu.CompilerParams(dimension_semantics=sem, vmem_limit_bytes=VMEM_LIMIT)


def _lane_iota(shape):
    return lax.broadcasted_iota(jnp.int32, shape, len(shape) - 1)


def _seg_mean(y, width):
    shift = int(np.log2(width))
    grp = _lane_iota(y.shape) >> shift
    out = jnp.zeros_like(y)
    for g in range(y.shape[-1] // width):
        msk = grp == g
        s = jnp.sum(jnp.where(msk, y, 0.0), axis=-1, keepdims=True)
        out = jnp.where(msk, s, out)
    return out * (1.0 / width)


def _rope(x, cos_f, sin_s, head_w, half):
    c = x.shape[-1]
    lane = _lane_iota(x.shape) & (head_w - 1)
    partner = jnp.where(lane < half, pltpu.roll(x, c - half, 1), pltpu.roll(x, half, 1))
    return x * cos_f + partner * sin_s


def _silu(x):
    return x * jax.nn.sigmoid(x)


def _ada_kernel(c_ref, w_ref, b_ref, o_ref):
    ca = _silu(c_ref[...])
    o_ref[0] = jnp.dot(ca.astype(BF16), w_ref[0].astype(BF16), preferred_element_type=F32) + b_ref[0]


def _ada_call(c_pad, w_ada, b_ada):
    L, D, D6 = w_ada.shape
    rows = c_pad.shape[0]
    tn = D
    return pl.pallas_call(
        _ada_kernel,
        grid=(L, D6 // tn),
        in_specs=[
            pl.BlockSpec((rows, D), lambda l, j: (0, 0)),
            pl.BlockSpec((1, D, tn), lambda l, j: (l, 0, j)),
            pl.BlockSpec((1, 1, tn), lambda l, j: (l, 0, j)),
        ],
        out_specs=pl.BlockSpec((1, rows, tn), lambda l, j: (l, 0, j)),
        out_shape=jax.ShapeDtypeStruct((L, rows, D6), F32),
        compiler_params=_cparams(("arbitrary", "arbitrary")),
        name="ada_mod",
    )(c_pad, w_ada, b_ada.reshape(L, 1, D6))


def _proj_kernel(x_ref, mod_ref, n1_ref, w_ref, qn_ref, kn_ref, cosa_ref, sina_ref, cosi_ref, sini_ref,
                 q_ref, k_ref, v_ref, iq_ref, ik_ref, iw_ref, pb_ref, pc_ref, pd_ref):
    x = x_ref[...]
    ms = jnp.mean(x * x, axis=-1, keepdims=True)
    h = (x * lax.rsqrt(ms + EPS)) * n1_ref[...]
    h = h * (1.0 + mod_ref[0, 1:2, :]) + mod_ref[0, 0:1, :]
    proj = jnp.dot(h.astype(BF16), w_ref[...], preferred_element_type=F32)
    G = GROUP_W
    cos_a, sin_a = cosa_ref[...], sina_ref[...]
    cos_i, sin_i = cosi_ref[...], sini_ref[...]

    def qk(t, g_ref):
        tn = (t * lax.rsqrt(_seg_mean(t * t, ATT_HEAD_DIM) + EPS)) * g_ref[...]
        return _rope(tn, cos_a, sin_a, ATT_HEAD_DIM, ROPE_DIM // 2)

    q_ref[...] = (qk(proj[:, 0:G], qn_ref) * (ATT_HEAD_DIM ** -0.5)).T.astype(BF16)
    k_ref[...] = qk(proj[:, G:2 * G], kn_ref).astype(BF16)
    v_ref[0] = proj[:, 2 * G:3 * G].T.astype(BF16)
    o = 3 * G
    iq_ref[...] = _rope(proj[:, o:o + LANES], cos_i, sin_i, IDX_DIM, IDX_ROPE_DIM // 2).T.astype(BF16)
    ik_ref[...] = _rope(proj[:, o + LANES:o + 2 * LANES], cos_i, sin_i, IDX_DIM, IDX_ROPE_DIM // 2).astype(BF16)
    iw_ref[...] = (proj[:, o + 2 * LANES:o + 3 * LANES] * (IDX_HEADS ** -0.5)).T[0:8, :]
    o += 3 * LANES
    pb_ref[...] = proj[:, o:o + 2 * G]
    pc_ref[...] = proj[:, o + 2 * G:o + 3 * G]
    pd_ref[...] = proj[:, o + 3 * G:o + 5 * G]


def _proj_call(x2, mod, n1, w_in_p, qn_t, kn_t, cos_a, sin_a, cos_i, sin_i, seq):
    N, D = x2.shape
    tm = TM_PROJ
    tpb = seq // tm
    G = GROUP_W
    assert tm == TK_ATT
    row = lambda i: (i, 0)
    col = lambda i: (0, i)
    fixed = lambda i: (0, 0)
    sds = jax.ShapeDtypeStruct
    out_specs = [
        pl.BlockSpec((G, tm), col),
        pl.BlockSpec((tm, G), row),
        pl.BlockSpec((1, G, tm), lambda i: (i, 0, 0)),
        pl.BlockSpec((LANES, tm), col),
        pl.BlockSpec((tm, LANES), row),
        pl.BlockSpec((8, tm), col),
        pl.BlockSpec((tm, 2 * G), row),
        pl.BlockSpec((tm, G), row),
        pl.BlockSpec((tm, 2 * G), row),
    ]
    out_shape = [sds((G, N), BF16), sds((N, G), BF16), sds((N // tm, G, tm), BF16), sds((LANES, N), BF16),
                 sds((N, LANES), BF16), sds((8, N), F32), sds((N, 2 * G), F32), sds((N, G), F32), sds((N, 2 * G), F32)]
    return pl.pallas_call(
        _proj_kernel,
        grid=(N // tm,),
        in_specs=[
            pl.BlockSpec((tm, D), row),
            pl.BlockSpec((1, 6, D), lambda i: (i // tpb, 0, 0)),
            pl.BlockSpec((1, D), fixed),
            pl.BlockSpec((D, W_IN_COLS), fixed),
            pl.BlockSpec((1, G), fixed),
            pl.BlockSpec((1, G), fixed),
            pl.BlockSpec((tm, G), row),
            pl.BlockSpec((tm, G), row),
            pl.BlockSpec((tm, LANES), row),
            pl.BlockSpec((tm, LANES), row),
        ],
        out_specs=out_specs,
        out_shape=out_shape,
        compiler_params=_cparams(("parallel",)),
        name="norm_in_proj",
    )(x2, mod, n1, w_in_p, qn_t, kn_t, cos_a, sin_a, cos_i, sin_i)


def _pair_rhs(xt, head_rows, h0):
    head = lax.broadcasted_iota(jnp.int32, xt.shape, 0) >> int(np.log2(head_rows))
    zero = jnp.zeros_like(xt)
    return jnp.concatenate([jnp.where(head == h0, xt, zero), jnp.where(head == h0 + 1, xt, zero)], axis=1)


def _dsa_kernel(q_ref, k_ref, v_ref, iq_ref, ik_ref, iw_ref, ltri_ref, o_ref, sc_ref, *, topk):
    tq, tk = TQ_ATT, TK_ATT
    i = pl.program_id(1)
    q0 = i * tq
    n_kv = (q0 + tq + tk - 1) // tk
    kf = float(topk)

    q_pos = _lane_iota((1, tq)) + q0
    key_end = ((q_pos >> 6) + 1) << 6
    key_i = lax.broadcasted_iota(jnp.int32, (tk, tq), 0)

    def fold8(x, op):
        parts = [x[r * 8:(r + 1) * 8, :] for r in range(tk // 8)]
        while len(parts) > 1:
            parts = [op(parts[a], parts[a + 1]) for a in range(0, len(parts), 2)]
        return parts[0]

    iqt = iq_ref[...]
    iq_pairs = [_pair_rhs(iqt, IDX_DIM, h0) for h0 in range(0, IDX_HEADS, 2)]
    iw_h = [iw_ref[h:h + 1, :] for h in range(IDX_HEADS)]

    def score_body(kc, carry):
        ikc = ik_ref[pl.ds(pl.multiple_of(kc * tk, tk), tk), :]
        s = jnp.zeros((tk, tq), F32)
        for pi, rhs in enumerate(iq_pairs):
            d2 = jnp.dot(ikc, rhs, preferred_element_type=F32)
            for j in range(2):
                d = d2[:, j * tq:(j + 1) * tq]
                s = s + jnp.maximum(d * (IDX_DIM ** -0.5), 0.0) * iw_h[2 * pi + j]
        sc_ref[kc] = jnp.where(key_i + kc * tk < key_end, s, NEG_INF)
        return carry

    lax.fori_loop(0, n_kv, score_body, 0)

    def reduce_chunks(fn, init, combine, fold):
        def body(kc, part):
            return combine(part, fold8(fn(sc_ref[kc], kc * tk), combine))
        part = lax.fori_loop(0, n_kv, body, jnp.full((8, tq), init, F32))
        return fold(part, axis=0, keepdims=True)

    def count(ind):
        return reduce_chunks(ind, 0.0, jnp.add, jnp.sum)

    def col_maximum(val):
        return reduce_chunks(val, NEG_INF, jnp.maximum, jnp.max)

    small = key_end <= topk

    @pl.when(q0 + tq > topk)
    def _select():
        col_max = col_maximum(lambda b, off: b)
        col_min = -col_maximum(lambda b, off: jnp.where(b == NEG_INF, NEG_INF, -b))

        def bis_body(_, c):
            lo, hi = c
            mid = jnp.where(hi == jnp.inf, col_max, lo + (hi - lo) * 0.5)
            ge = count(lambda b, off: jnp.where(b >= mid, 1.0, 0.0)) >= kf
            return jnp.where(ge, mid, lo), jnp.where(ge, hi, mid)

        lo, hi = lax.fori_loop(0, BISECT_ITERS, bis_body, (col_min, jnp.full((1, tq), jnp.inf, F32)))

        def sd_cond(c):
            return c[0] > 0.0

        def sd_body(c):
            _, hi, thr, done = c
            cand = col_maximum(lambda b, off: jnp.where(b < hi, b, NEG_INF))
            ok = count(lambda b, off: jnp.where(b >= cand, 1.0, 0.0)) >= kf
            thr = jnp.where(done > 0.0, thr, cand)
            hi = jnp.where(done > 0.0, hi, cand)
            done = jnp.where(ok, 1.0, done)
            return jnp.sum(1.0 - done), hi, thr, done

        done0 = jnp.where(small, 1.0, 0.0)
        n0 = jnp.sum(1.0 - done0)
        _, _, thr, _ = lax.while_loop(sd_cond, sd_body, (n0, hi, jnp.full((1, tq), NEG_INF, F32), done0))
        thr = jnp.where(small, NEG_INF, thr)

        need = kf - count(lambda b, off: jnp.where(b > thr, 1.0, 0.0))
        n_tied = count(lambda b, off: jnp.where(b == thr, 1.0, 0.0))
        excess = jnp.sum(jnp.where(jnp.where(small, 0.0, n_tied) > need, 1.0, 0.0))

        @pl.when(excess <= 0.0)
        def _keep_all_ties():
            def bias_body(kc, carry):
                blk = sc_ref[kc]
                sc_ref[kc] = jnp.where(blk == NEG_INF, NEG_INF, jnp.where(blk >= thr, 0.0, NEG_INF))
                return carry
            lax.fori_loop(0, n_kv, bias_body, 0)

        @pl.when(excess > 0.0)
        def _rank_ties():
            ltri = ltri_ref[...]

            def bias_body(kc, seen):
                blk = sc_ref[kc]
                tied = jnp.where(blk == thr, 1.0, 0.0)
                rank = jnp.dot(ltri, tied.astype(BF16), preferred_element_type=F32) + seen
                tie = jnp.where(blk == thr, jnp.where(rank <= need, 0.0, NEG_INF), NEG_INF)
                bias = jnp.where(blk > thr, 0.0, tie)
                sc_ref[kc] = jnp.where(blk == NEG_INF, NEG_INF, bias)
                return seen + jnp.sum(fold8(tied, jnp.add), axis=0, keepdims=True)

            lax.fori_loop(0, n_kv, bias_body, jnp.zeros((1, tq), F32))

    @pl.when(q0 + tq <= topk)
    def _all():
        def bias_body(kc, carry):
            sc_ref[kc] = jnp.where(sc_ref[kc] == NEG_INF, NEG_INF, 0.0)
            return carry
        lax.fori_loop(0, n_kv, bias_body, 0)

    qt = q_ref[...]
    q_pairs = [_pair_rhs(qt, ATT_HEAD_DIM, h0) for h0 in range(0, ATT_HEADS, 2)]
    dh = ATT_HEAD_DIM

    def att_body(kc, c):
        ms, ls, accs = c
        kblk = k_ref[pl.ds(pl.multiple_of(kc * tk, tk), tk), :]
        bias = sc_ref[kc]
        vt = v_ref[kc]
        ms_n, ls_n, accs_n = [], [], []
        for pi, rhs in enumerate(q_pairs):
            s2 = jnp.dot(kblk, rhs, preferred_element_type=F32)
            for j in range(2):
                h = 2 * pi + j
                s = s2[:, j * tq:(j + 1) * tq] + bias
                m_new = jnp.maximum(ms[h], jnp.max(fold8(s, jnp.maximum), axis=0, keepdims=True))
                m_safe = jnp.where(m_new == NEG_INF, 0.0, m_new)
                alpha = jnp.exp(ms[h] - m_safe)
                p = jnp.exp(s - m_safe)
                ls_n.append(alpha * ls[h] + jnp.sum(fold8(p, jnp.add), axis=0, keepdims=True))
                pv = jnp.dot(vt[h * dh:(h + 1) * dh, :], p.astype(BF16), preferred_element_type=F32)
                accs_n.append(alpha * accs[h] + pv)
                ms_n.append(m_new)
        return tuple(ms_n), tuple(ls_n), tuple(accs_n)

    init = (tuple(jnp.full((1, tq), NEG_INF, F32) for _ in range(ATT_HEADS)),
            tuple(jnp.zeros((1, tq), F32) for _ in range(ATT_HEADS)),
            tuple(jnp.zeros((dh, tq), F32) for _ in range(ATT_HEADS)))
    _, ls, accs = lax.fori_loop(0, n_kv, att_body, init)
    out_t = jnp.concatenate([accs[h] / ls[h] for h in range(ATT_HEADS)], axis=0)
    o_ref[...] = out_t.T


def _dsa_call(qt, k, vt, iqt, ik, iwt, batch, seq):
    N, G = k.shape
    tq = TQ_ATT
    nq = seq // tq
    nkc = seq // TK_ATT
    topk = min(TOPK_MAX, seq // 4)
    qcol = lambda b, i: (0, b * nq + i)
    brow = lambda b, i: (b, 0)
    return pl.pallas_call(
        functools.partial(_dsa_kernel, topk=topk),
        grid=(batch, nq),
        in_specs=[
            pl.BlockSpec((G, tq), qcol),
            pl.BlockSpec((seq, G), brow),
            pl.BlockSpec((nkc, G, TK_ATT), lambda b, i: (b, 0, 0)),
            pl.BlockSpec((LANES, tq), qcol),
            pl.BlockSpec((seq, LANES), brow),
            pl.BlockSpec((8, tq), qcol),
            pl.BlockSpec((TK_ATT, TK_ATT), lambda b, i: (0, 0)),
        ],
        out_specs=pl.BlockSpec((tq, G), lambda b, i: (b * nq + i, 0)),
        out_shape=jax.ShapeDtypeStruct((N, G), F32),
        scratch_shapes=[pltpu.VMEM((nkc, TK_ATT, tq), F32)],
        compiler_params=_cparams(("parallel", "arbitrary")),
        name="dsa_attention",
    )(qt, k, vt, iqt, ik, iwt, jnp.tril(jnp.ones((TK_ATT, TK_ATT), BF16)))


def _mix_kernel(x_ref, oa_ref, pb_ref, pbh_ref, pc_ref, pch_ref, pd_ref, mod_ref,
                cw_ref, cb_ref, clg_ref, clb_ref, cpw_ref, cpb_ref,
                pw_ref, pbias_ref, ps_ref, slg_ref, slb_ref, sw_ref, sb_ref,
                on_ref, wo_ref, n2_ref, wr_ref, br_ref,
                xo_ref, h2_ref, route_ref, cnt_ref,
                ypad_ref, ppad_ref, s2_ref, s4_ref, s8_ref, run_ref, *, tiles_per_batch):
    tm = TM_MIX
    G = GROUP_W
    i = pl.program_id(0)
    t_in_b = i % tiles_per_batch
    first = t_in_b == 0
    lane_g = _lane_iota((tm, G))

    def glu(pb):
        return pb[:, 0:G] * jax.nn.sigmoid(pb[:, G:2 * G])

    ypad_ref[0:HALO, :] = jnp.where(first, 0.0, glu(pbh_ref[...]))
    ypad_ref[HALO:HALO + tm, :] = glu(pb_ref[...])
    acc = jnp.zeros((tm, G), F32)
    for j in range(CONV_WIDTH):
        off = HALO - (CONV_WIDTH - 1) + j
        acc = acc + cw_ref[j:j + 1, :] * ypad_ref[off:off + tm, :]
    y = acc + cb_ref[...]
    gw = G // CONV_GROUPS
    mu = _seg_mean(y, gw)
    yc = y - mu
    var = _seg_mean(yc * yc, gw)
    y = (yc * lax.rsqrt(var + EPS)) * clg_ref[...] + clb_ref[...]
    o_b = jnp.dot(_silu(y).astype(BF16), cpw_ref[...], preferred_element_type=F32) + cpb_ref[...]

    p = pc_ref[...]
    ppad_ref[0:HALO, :] = jnp.where(first, 0.0, pch_ref[...])
    ppad_ref[HALO:HALO + tm, :] = p
    n8 = tm + HALO - 8
    s2_ref[8:8 + n8, :] = ppad_ref[8:8 + n8, :] + ppad_ref[7:7 + n8, :]
    n16 = tm + HALO - 16
    s4_ref[16:16 + n16, :] = s2_ref[16:16 + n16, :] + s2_ref[14:14 + n16, :]
    n24 = tm + HALO - 24
    s8_ref[24:24 + n24, :] = s4_ref[24:24 + n24, :] + s4_ref[20:20 + n24, :]
    s2 = s2_ref[HALO:HALO + tm, :]
    s4 = s4_ref[HALO:HALO + tm, :]
    s8 = s8_ref[HALO:HALO + tm, :]
    s16 = s8 + s8_ref[HALO - 8:HALO - 8 + tm, :]
    pgrp = lane_g >> 6
    wsum = jnp.where(pgrp == 0, s2, jnp.where(pgrp == 1, s4, jnp.where(pgrp == 2, s8, s16)))
    wlen = jnp.where(pgrp == 0, 2.0, jnp.where(pgrp == 1, 4.0, jnp.where(pgrp == 2, 8.0, 16.0)))
    tpos = (lax.broadcasted_iota(jnp.int32, (tm, G), 0) + t_in_b * tm + 1).astype(F32)
    pooled = wsum / jnp.minimum(tpos, wlen) - p
    o_c = (jnp.dot(pooled.astype(BF16), pw_ref[...], preferred_element_type=F32) + pbias_ref[...]) * ps_ref[...]

    pd = pd_ref[...]
    u, v = pd[:, 0:G], pd[:, G:2 * G]
    mu = jnp.mean(v, axis=-1, keepdims=True)
    vc = v - mu
    var = jnp.mean(vc * vc, axis=-1, keepdims=True)
    vn = ((vc * lax.rsqrt(var + EPS)) * slg_ref[...] + slb_ref[...]).astype(BF16)
    r_i = lax.broadcasted_iota(jnp.int32, (SGU_CHUNK, SGU_CHUNK), 0)
    c_i = lax.broadcasted_iota(jnp.int32, (SGU_CHUNK, SGU_CHUNK), 1)
    w_heads = [jnp.where(r_i >= c_i, sw_ref[h], 0.0).astype(BF16) for h in range(SGU_HEADS)]
    lane_c = _lane_iota((SGU_CHUNK, G)) >> 6
    mixed = []
    for n in range(tm // SGU_CHUNK):
        vch = vn[n * SGU_CHUNK:(n + 1) * SGU_CHUNK, :]
        mx = jnp.zeros((SGU_CHUNK, G), F32)
        for h in range(SGU_HEADS):
            mx = jnp.where(lane_c == h, jnp.dot(w_heads[h], vch, preferred_element_type=F32), mx)
        mixed.append(mx + sb_ref[...])
    o_d = u * jnp.concatenate(mixed, axis=0)

    proj = jnp.zeros((tm, D_MODEL), F32)
    for g, piece in enumerate((oa_ref[...], o_b, o_c, o_d)):
        ms = jnp.mean(piece * piece, axis=-1, keepdims=True)
        pn = (piece * lax.rsqrt(ms + EPS)) * on_ref[:, g * G:(g + 1) * G]
        proj = proj + jnp.dot(pn.astype(BF16), wo_ref[g * G:(g + 1) * G, :], preferred_element_type=F32)
    x_new = x_ref[...] + mod_ref[0, 2:3, :] * proj
    xo_ref[...] = x_new

    ms = jnp.mean(x_new * x_new, axis=-1, keepdims=True)
    h2 = (x_new * lax.rsqrt(ms + EPS)) * n2_ref[...]
    h2 = h2 * (1.0 + mod_ref[0, 4:5, :]) + mod_ref[0, 3:4, :]
    _to_token_major(h2_ref, h2)
    logits = jnp.dot(h2.astype(BF16), wr_ref[...], preferred_element_type=F32) + br_ref[...]
    lane = _lane_iota((tm, LANES))
    lane_f = lane.astype(F32)
    big = float(LANES)
    glog = jnp.where(lane < N_EXP_GROUPS, logits, NEG_INF)
    gmax = jnp.max(glog, axis=-1, keepdims=True)
    p_sel = 1.0 / jnp.sum(jnp.exp(glog - gmax), axis=-1, keepdims=True)
    g_idx = jnp.min(jnp.where(glog == gmax, lane_f, big), axis=-1, keepdims=True)
    e_lane = lane - N_EXP_GROUPS
    elog = jnp.where((e_lane >> 3).astype(F32) == g_idx, logits, NEG_INF)
    top1 = jnp.max(elog, axis=-1, keepdims=True)
    j1 = jnp.min(jnp.where(elog == top1, lane_f, big), axis=-1, keepdims=True)
    elog2 = jnp.where(lane_f == j1, NEG_INF, elog)
    top2 = jnp.max(elog2, axis=-1, keepdims=True)
    j2 = jnp.min(jnp.where(elog2 == top2, lane_f, big), axis=-1, keepdims=True)
    e2w = jnp.exp(top2 - top1)
    gate1 = p_sel * (1.0 / (1.0 + e2w))
    gate2 = p_sel * (e2w / (1.0 + e2w))
    e1 = j1 - float(N_EXP_GROUPS)
    e2 = j2 - float(N_EXP_GROUPS)

    @pl.when(i == 0)
    def _init():
        run_ref[...] = jnp.zeros_like(run_ref)

    onehot = jnp.where(jnp.logical_or(lane_f == e1, lane_f == e2), 1.0, 0.0)
    rr = lax.broadcasted_iota(jnp.int32, (tm, tm), 0)
    cc = lax.broadcasted_iota(jnp.int32, (tm, tm), 1)
    before = jnp.where(rr > cc, 1.0, 0.0).astype(BF16)
    prior = jnp.dot(before, onehot.astype(BF16), preferred_element_type=F32) + run_ref[0:1, :]
    rank1 = jnp.sum(jnp.where(lane_f == e1, prior, 0.0), axis=-1, keepdims=True)
    rank2 = jnp.sum(jnp.where(lane_f == e2, prior, 0.0), axis=-1, keepdims=True)
    run_new = run_ref[0:1, :] + jnp.sum(onehot, axis=0, keepdims=True)
    run_ref[...] = jnp.broadcast_to(run_new, run_ref.shape)
    cnt_ref[...] = jnp.broadcast_to(run_new, cnt_ref.shape)
    route = jnp.where(lane == 0, e1, jnp.where(lane == 1, e2, jnp.where(lane == 2, rank1, jnp.where(
        lane == 3, rank2, jnp.where(lane == 4, gate1, jnp.where(lane == 5, gate2, 0.0))))))
    route_ref[...] = route


def _mix_call(x2, oa, pb, pc, pd, mod, lw, seq):
    N, D = x2.shape
    tm = TM_MIX
    tpb = seq // tm
    G = GROUP_W
    row = lambda i: (i, 0)
    halo = lambda i: (jnp.maximum(i * (tm // HALO) - 1, 0), 0)
    fixed2 = lambda i: (0, 0)
    fixed3 = lambda i: (0, 0, 0)
    params = [lw["conv_w"], lw["conv_b"], lw["conv_ln_g"], lw["conv_ln_b"], lw["conv_pw_w"], lw["conv_pw_b"],
              lw["pool_w"], lw["pool_b"], lw["pool_scale"], lw["sgu_ln_g"], lw["sgu_ln_b"], lw["sgu_w"], lw["sgu_b"],
              lw["out_norm"], lw["w_out"], lw["norm2"], lw["w_router"], lw["b_router"]]
    param_specs = [pl.BlockSpec(p.shape, fixed3 if p.ndim == 3 else fixed2) for p in params]
    return pl.pallas_call(
        functools.partial(_mix_kernel, tiles_per_batch=tpb),
        grid=(N // tm,),
        in_specs=[
            pl.BlockSpec((tm, D), row),
            pl.BlockSpec((tm, G), row),
            pl.BlockSpec((tm, 2 * G), row),
            pl.BlockSpec((HALO, 2 * G), halo),
            pl.BlockSpec((tm, G), row),
            pl.BlockSpec((HALO, G), halo),
            pl.BlockSpec((tm, 2 * G), row),
            pl.BlockSpec((1, 6, D), lambda i: (i // tpb, 0, 0)),
        ] + param_specs,
        out_specs=[
            pl.BlockSpec((tm, D), row),
            pl.BlockSpec((tm * TOK_SUB, LANES), row),
            pl.BlockSpec((tm, LANES), row),
            pl.BlockSpec((8, LANES), fixed2),
        ],
        out_shape=[
            jax.ShapeDtypeStruct((N, D), F32),
            jax.ShapeDtypeStruct((N * TOK_SUB, LANES), F32),
            jax.ShapeDtypeStruct((N, LANES), F32),
            jax.ShapeDtypeStruct((8, LANES), F32),
        ],
        scratch_shapes=[pltpu.VMEM((tm + HALO, G), F32) for _ in range(5)] + [pltpu.VMEM((8, LANES), F32)],
        compiler_params=_cparams(("arbitrary",)),
        name="mixers_out_router",
    )(x2, oa, pb, pb, pc, pc, pd, mod, *params)


TOK_SUB = D_MODEL // LANES
ROW_DMA_UNROLL = 8


def _to_token_major(ref, x):
    n = x.shape[0]
    for j in range(TOK_SUB):
        ref[pl.ds(j, n, stride=TOK_SUB), :] = x[:, j * LANES:(j + 1) * LANES]


def _from_token_major(ref, tok0, n):
    return jnp.concatenate([ref[pl.ds(tok0 * TOK_SUB + j, n, stride=TOK_SUB), :] for j in range(TOK_SUB)], axis=1)


def _token_copy(src_hbm, row8, dst_ref, slot, r, sem):
    src = src_hbm.at[pl.ds(pl.multiple_of(row8, TOK_SUB), TOK_SUB), :]
    dst = dst_ref.at[slot, pl.ds(pl.multiple_of(r * TOK_SUB, TOK_SUB), TOK_SUB), :]
    return pltpu.make_async_copy(src, dst, sem.at[slot])


def _start_tokens(idx_ref, src_hbm, dst_ref, slot, sem, n):
    def body(g, carry):
        for j in range(ROW_DMA_UNROLL):
            r = g * ROW_DMA_UNROLL + j
            _token_copy(src_hbm, idx_ref[0, 0, r], dst_ref, slot, r, sem).start(priority=j % 2)
        return carry
    lax.fori_loop(0, n // ROW_DMA_UNROLL, body, 0)


def _wait_tokens(src_hbm, dst_ref, slot, sem, n):
    pltpu.make_async_copy(src_hbm.at[pl.ds(0, n * TOK_SUB), :], dst_ref.at[slot], sem.at[slot]).wait()


def _expert_kernel(be_ref, nu_ref, tok_ref, tokn_ref, h_hbm, wg_ref, wu_ref, wd_ref, y_ref,
                   xbuf, sem, wg_bf, wu_bf, wd_bf):
    i = pl.program_id(0)
    n_used = nu_ref[0]
    slot = i % 2
    rows = MOE_ROWS

    @pl.when(jnp.logical_and(i == 0, n_used > 0))
    def _first():
        _start_tokens(tok_ref, h_hbm, xbuf, 0, sem, rows)

    @pl.when(i + 1 < n_used)
    def _prefetch():
        _start_tokens(tokn_ref, h_hbm, xbuf, 1 - slot, sem, rows)

    changed = jnp.logical_or(i == 0, be_ref[i] != be_ref[jnp.maximum(i - 1, 0)])

    @pl.when(jnp.logical_and(changed, i < n_used))
    def _cast():
        wg_bf[...] = wg_ref[0, 0].astype(BF16)
        wu_bf[...] = wu_ref[0, 0].astype(BF16)
        wd_bf[...] = wd_ref[0, 0].astype(BF16)

    @pl.when(i < n_used)
    def _compute():
        _wait_tokens(h_hbm, xbuf, slot, sem, rows)
        xb = _from_token_major(xbuf.at[slot], 0, rows).astype(BF16)
        g = jnp.dot(xb, wg_bf[...], preferred_element_type=F32)
        u = jnp.dot(xb, wu_bf[...], preferred_element_type=F32)
        hid = (_silu(g) * u).astype(BF16)
        _to_token_major(y_ref, jnp.dot(hid, wd_bf[...], preferred_element_type=F32))

    @pl.when(i >= n_used)
    def _skip():
        y_ref[...] = jnp.zeros_like(y_ref)


def _expert_call(block_expert, n_used, row_tok2, h2, w_gate, w_up, w_down, layer):
    n_blocks, _, rows = row_tok2.shape
    D = D_MODEL
    FF = w_gate.shape[-1]
    grid_spec = pltpu.PrefetchScalarGridSpec(
        num_scalar_prefetch=2,
        grid=(n_blocks,),
        in_specs=[
            pl.BlockSpec((1, 1, rows), lambda i, be, nu: (i, 0, 0), memory_space=pltpu.SMEM),
            pl.BlockSpec((1, 1, rows), lambda i, be, nu: (jnp.minimum(i + 1, n_blocks - 1), 0, 0), memory_space=pltpu.SMEM),
            pl.BlockSpec(memory_space=pl.ANY),
            pl.BlockSpec((1, 1, D, FF), lambda i, be, nu: (layer, be[i], 0, 0)),
            pl.BlockSpec((1, 1, D, FF), lambda i, be, nu: (layer, be[i], 0, 0)),
            pl.BlockSpec((1, 1, FF, D), lambda i, be, nu: (layer, be[i], 0, 0)),
        ],
        out_specs=pl.BlockSpec((rows * TOK_SUB, LANES), lambda i, be, nu: (i, 0)),
        scratch_shapes=[
            pltpu.VMEM((2, rows * TOK_SUB, LANES), F32),
            pltpu.SemaphoreType.DMA((2,)),
            pltpu.VMEM((D, FF), BF16),
            pltpu.VMEM((D, FF), BF16),
            pltpu.VMEM((FF, D), BF16),
        ],
    )
    return pl.pallas_call(
        _expert_kernel,
        grid_spec=grid_spec,
        out_shape=jax.ShapeDtypeStruct((n_blocks * rows * TOK_SUB, LANES), F32),
        compiler_params=_cparams(("arbitrary",)),
        name="expert_mlp",
    )(block_expert, n_used, row_tok2, row_tok2, h2, w_gate, w_up, w_down)


def _combine_kernel(pos_ref, posn_ref, y_hbm, x_ref, route_ref, mod_ref, o_ref, ybuf, sem):
    i = pl.program_id(0)
    n = pl.num_programs(0)
    slot = i % 2
    tm = TM_COMB

    @pl.when(i == 0)
    def _first():
        _start_tokens(pos_ref, y_hbm, ybuf, 0, sem, 2 * tm)

    @pl.when(i + 1 < n)
    def _prefetch():
        _start_tokens(posn_ref, y_hbm, ybuf, 1 - slot, sem, 2 * tm)

    _wait_tokens(y_hbm, ybuf, slot, sem, 2 * tm)
    route = route_ref[...]
    yb = ybuf.at[slot]
    y = route[:, 4:5] * _from_token_major(yb, 0, tm) + route[:, 5:6] * _from_token_major(yb, tm, tm)
    o_ref[...] = x_ref[...] + mod_ref[0, 5:6, :] * y


def _combine_call(pos2, y_rows, x2, route, mod, seq):
    N, D = x2.shape
    tm = TM_COMB
    tpb = seq // tm
    nt = N // tm
    row = lambda i: (i, 0)
    return pl.pallas_call(
        _combine_kernel,
        grid=(nt,),
        in_specs=[
            pl.BlockSpec((1, 1, 2 * tm), lambda i: (i, 0, 0), memory_space=pltpu.SMEM),
            pl.BlockSpec((1, 1, 2 * tm), lambda i: (jnp.minimum(i + 1, nt - 1), 0, 0), memory_space=pltpu.SMEM),
            pl.BlockSpec(memory_space=pl.ANY),
            pl.BlockSpec((tm, D), row),
            pl.BlockSpec((tm, LANES), row),
            pl.BlockSpec((1, 6, D), lambda i: (i // tpb, 0, 0)),
        ],
        out_specs=pl.BlockSpec((tm, D), row),
        out_shape=jax.ShapeDtypeStruct((N, D), F32),
        scratch_shapes=[pltpu.VMEM((2, 2 * tm * TOK_SUB, LANES), F32), pltpu.SemaphoreType.DMA((2,))],
        compiler_params=_cparams(("arbitrary",)),
        name="moe_combine",
    )(pos2, pos2, y_rows, x2, route, mod)


def _rope_lane_tables(positions, rot_dim, head_w, n_rep):
    half = rot_dim // 2
    inv = jnp.power(jnp.float32(ROPE_THETA), -2.0 * jnp.arange(half, dtype=jnp.float32) / rot_dim)
    ang = positions.astype(jnp.float32)[..., None] * inv
    cos, sin = jnp.cos(ang), jnp.sin(ang)
    rest = head_w - rot_dim
    cos_h = jnp.concatenate([cos, cos, jnp.ones(cos.shape[:-1] + (rest,), F32)], axis=-1)
    sin_h = jnp.concatenate([-sin, sin, jnp.zeros(sin.shape[:-1] + (rest,), F32)], axis=-1)
    n = positions.shape[0] * positions.shape[1]
    return (jnp.tile(cos_h, (1, 1, n_rep)).reshape(n, head_w * n_rep),
            jnp.tile(sin_h, (1, 1, n_rep)).reshape(n, head_w * n_rep))


def _layer_weights(l, w_in, q_norm, k_norm, conv_w, conv_b, conv_ln_g, conv_ln_b, conv_pw_w, conv_pw_b,
                   pool_w, pool_b, pool_scale, sgu_ln_g, sgu_ln_b, sgu_w, sgu_b, out_norm, w_out, norm2,
                   w_rg, b_rg, w_re, b_re):
    G = GROUP_W
    D = D_MODEL
    pts = np.cumsum(IN_SPLITS)[:-1].tolist()
    wq, wk, wv, wiq, wik, wiw, wb, wc, wd = jnp.split(w_in[l], pts, axis=-1)
    wiw_p = jnp.pad(wiw, ((0, 0), (0, LANES - IDX_HEADS)))
    w_in_p = jnp.concatenate([wq, wk, wv, wiq, jnp.tile(wik, (1, IDX_HEADS)), wiw_p, wb, wc, wd], axis=-1).astype(BF16)
    npool = len(POOL_WINDOWS)
    pool_bd = jnp.zeros((G, G), F32)
    for g in range(npool):
        pool_bd = lax.dynamic_update_slice(pool_bd, pool_w[l, g], (g * POOL_CH, g * POOL_CH))
    sgu_bias = jnp.repeat(sgu_b[l].T, G // SGU_HEADS, axis=1)
    w_router = jnp.concatenate([w_rg[l], w_re[l].reshape(D, N_EXPERTS),
                                jnp.zeros((D, LANES - N_EXP_GROUPS - N_EXPERTS), F32)], axis=-1).astype(BF16)
    b_router = jnp.concatenate([b_rg[l], b_re[l].reshape(N_EXPERTS),
                                jnp.zeros((LANES - N_EXP_GROUPS - N_EXPERTS,), F32)]).reshape(1, LANES)
    r1 = lambda a: a.reshape(1, -1)
    return dict(
        w_in=w_in_p,
        q_norm=jnp.tile(q_norm[l], ATT_HEADS).reshape(1, G), k_norm=jnp.tile(k_norm[l], ATT_HEADS).reshape(1, G),
        conv_w=conv_w[l], conv_b=r1(conv_b[l]), conv_ln_g=r1(conv_ln_g[l]), conv_ln_b=r1(conv_ln_b[l]),
        conv_pw_w=conv_pw_w[l].astype(BF16), conv_pw_b=r1(conv_pw_b[l]),
        pool_w=pool_bd.astype(BF16), pool_b=r1(pool_b[l]), pool_scale=r1(pool_scale[l]),
        sgu_ln_g=r1(sgu_ln_g[l]), sgu_ln_b=r1(sgu_ln_b[l]), sgu_w=sgu_w[l], sgu_b=sgu_bias,
        out_norm=r1(out_norm[l]), w_out=w_out[l].astype(BF16), norm2=r1(norm2[l]),
        w_router=w_router, b_router=b_router,
    )


def _dispatch_tables(route, cnt, n_tokens):
    rows_blk = MOE_ROWS
    e = route[:, 0:2].astype(jnp.int32)
    rank = route[:, 2:4].astype(jnp.int32)
    counts = cnt[0, :N_EXPERTS].astype(jnp.int32)
    padded = (counts + rows_blk - 1) // rows_blk * rows_blk
    pad_end = jnp.cumsum(padded)
    pad_start = pad_end - padded
    pos = pad_start[e] + rank
    m = n_tokens * MOE_TOPK
    n_blocks = (m + N_EXPERTS * (rows_blk - 1) + rows_blk - 1) // rows_blk
    n_used = (pad_end[-1] // rows_blk).astype(jnp.int32).reshape(1)
    blk_row0 = jnp.arange(n_blocks, dtype=jnp.int32) * rows_blk
    block_expert = jnp.minimum(jnp.sum((pad_end[None, :] <= blk_row0[:, None]).astype(jnp.int32), axis=1),
                               N_EXPERTS - 1)
    tok = jnp.repeat(jnp.arange(n_tokens, dtype=jnp.int32), MOE_TOPK)
    row_tok = jnp.zeros((n_blocks * rows_blk,), jnp.int32).at[pos.reshape(-1)].set(tok * TOK_SUB, unique_indices=True)
    pos2 = (pos * TOK_SUB).reshape(n_tokens // TM_COMB, TM_COMB, MOE_TOPK).transpose(0, 2, 1).reshape(
        -1, 1, MOE_TOPK * TM_COMB)
    return block_expert, n_used, row_tok.reshape(n_blocks, 1, rows_blk), pos2


def kernel(x, c, positions, w_ada, b_ada, norm1, w_in, q_norm, k_norm, conv_w, conv_b, conv_ln_g, conv_ln_b, conv_pw_w, conv_pw_b, pool_w, pool_b, pool_scale, sgu_ln_g, sgu_ln_b, sgu_w, sgu_b, out_norm, w_out, norm2, w_rg, b_rg, w_re, b_re, w_gate, w_up, w_down):
    B, S, D = x.shape
    N = B * S
    assert D == D_MODEL and S % TM_PROJ == 0 and S % TK_ATT == 0 and N % TM_COMB == 0
    depth = w_ada.shape[0]
    cos_a, sin_a = _rope_lane_tables(positions, ROPE_DIM, ATT_HEAD_DIM, ATT_HEADS)
    cos_i, sin_i = _rope_lane_tables(positions, IDX_ROPE_DIM, IDX_DIM, IDX_HEADS)
    c_pad = jnp.pad(c, ((0, (-B) % 8), (0, 0)))
    mod_all = _ada_call(c_pad, w_ada, b_ada)
    x2 = x.reshape(N, D)
    for l in range(depth):
        lw = _layer_weights(l, w_in, q_norm, k_norm, conv_w, conv_b, conv_ln_g, conv_ln_b, conv_pw_w, conv_pw_b,
                            pool_w, pool_b, pool_scale, sgu_ln_g, sgu_ln_b, sgu_w, sgu_b, out_norm, w_out, norm2,
                            w_rg, b_rg, w_re, b_re)
        mod = mod_all[l, :B].reshape(B, 6, D)
        q, k, v, iq, ik, iw, pb, pc, pd = _proj_call(
            x2, mod, norm1[l].reshape(1, D), lw["w_in"], lw["q_norm"], lw["k_norm"], cos_a, sin_a, cos_i, sin_i, S)
        oa = _dsa_call(q, k, v, iq, ik, iw, B, S)
        x_mid, h2, route, cnt = _mix_call(x2, oa, pb, pc, pd, mod, lw, S)
        block_expert, n_used, row_tok2, pos2 = _dispatch_tables(route, cnt, N)
        y_rows = _expert_call(block_expert, n_used, row_tok2, h2, w_gate, w_up, w_down, l)
        x2 = _combine_call(pos2, y_rows, x_mid, route, mod, S)
    return x2.reshape(B, S, D)
```

```python
import functools

import numpy as np
import jax
import jax.numpy as jnp
from jax import lax
from jax.experimental import pallas as pl
from jax.experimental.pallas import tpu as pltpu

F32 = jnp.float32
BF16 = jnp.bfloat16
NEG_INF = float("-inf")

D_MODEL = 1024
DEPTH = 2
CHUNK = 64
N_MIXERS = 4
GROUP_W = D_MODEL // N_MIXERS
ATT_HEAD_DIM = 64
ATT_HEADS = GROUP_W // ATT_HEAD_DIM
ROPE_DIM = ATT_HEAD_DIM // 4
ROPE_THETA = 500000.0
IDX_HEADS = 4
IDX_DIM = 32
IDX_ROPE_DIM = IDX_DIM // 4
TOPK_MAX = 256
CONV_WIDTH = 31
CONV_GROUPS = 4
POOL_WINDOWS = (2, 4, 8, 16)
POOL_CH = GROUP_W // 4
SGU_CHUNK = 128
SGU_HEADS = 4
N_EXP_GROUPS = 4
EXP_PER_GROUP = 8
N_EXPERTS = N_EXP_GROUPS * EXP_PER_GROUP
EXPERT_FF = 512
MOE_TOPK = 2
EPS = 1e-6
IN_SPLITS = (GROUP_W, GROUP_W, GROUP_W, IDX_HEADS * IDX_DIM, IDX_DIM, IDX_HEADS, 2 * GROUP_W, GROUP_W, 2 * GROUP_W)

LANES = 128
HALO = 32
W_IN_COLS = 3 * GROUP_W + 3 * LANES + 5 * GROUP_W

TM_PROJ = 512
TQ_ATT = 128
TK_ATT = 512
TM_MIX = 256
MOE_ROWS = 256
TM_COMB = 256
BISECT_ITERS = 18
VMEM_LIMIT = 56 * 1024 * 1024


def _cparams(sem):
    return pltpu.CompilerParams(dimension_semantics=sem, vmem_limit_bytes=VMEM_LIMIT)


def _lane_iota(shape):
    return lax.broadcasted_iota(jnp.int32, shape, len(shape) - 1)


def _seg_mean(y, width):
    shift = int(np.log2(width))
    grp = _lane_iota(y.shape) >> shift
    out = jnp.zeros_like(y)
    for g in range(y.shape[-1] // width):
        msk = grp == g
        s = jnp.sum(jnp.where(msk, y, 0.0), axis=-1, keepdims=True)
        out = jnp.where(msk, s, out)
    return out * (1.0 / width)


def _rope(x, cos_f, sin_s, head_w, half):
    c = x.shape[-1]
    lane = _lane_iota(x.shape) & (head_w - 1)
    partner = jnp.where(lane < half, pltpu.roll(x, c - half, 1), pltpu.roll(x, half, 1))
    return x * cos_f + partner * sin_s


def _silu(x):
    return x * jax.nn.sigmoid(x)


def _ada_kernel(c_ref, w_ref, b_ref, o_ref):
    ca = _silu(c_ref[...])
    o_ref[0] = jnp.dot(ca.astype(BF16), w_ref[0].astype(BF16), preferred_element_type=F32) + b_ref[0]


def _ada_call(c_pad, w_ada, b_ada):
    L, D, D6 = w_ada.shape
    rows = c_pad.shape[0]
    tn = D
    return pl.pallas_call(
        _ada_kernel,
        grid=(L, D6 // tn),
        in_specs=[
            pl.BlockSpec((rows, D), lambda l, j: (0, 0)),
            pl.BlockSpec((1, D, tn), lambda l, j: (l, 0, j)),
            pl.BlockSpec((1, 1, tn), lambda l, j: (l, 0, j)),
        ],
        out_specs=pl.BlockSpec((1, rows, tn), lambda l, j: (l, 0, j)),
        out_shape=jax.ShapeDtypeStruct((L, rows, D6), F32),
        compiler_params=_cparams(("arbitrary", "arbitrary")),
        name="ada_mod",
    )(c_pad, w_ada, b_ada.reshape(L, 1, D6))


def _proj_kernel(x_ref, mod_ref, n1_ref, w_ref, qn_ref, kn_ref, cosa_ref, sina_ref, cosi_ref, sini_ref,
                 q_ref, k_ref, v_ref, iq_ref, ik_ref, iw_ref, pb_ref, pc_ref, pd_ref):
    x = x_ref[...]
    ms = jnp.mean(x * x, axis=-1, keepdims=True)
    h = (x * lax.rsqrt(ms + EPS)) * n1_ref[...]
    h = h * (1.0 + mod_ref[0, 1:2, :]) + mod_ref[0, 0:1, :]
    proj = jnp.dot(h.astype(BF16), w_ref[...], preferred_element_type=F32)
    G = GROUP_W
    cos_a, sin_a = cosa_ref[...], sina_ref[...]
    cos_i, sin_i = cosi_ref[...], sini_ref[...]

    def qk(t, g_ref):
        tn = (t * lax.rsqrt(_seg_mean(t * t, ATT_HEAD_DIM) + EPS)) * g_ref[...]
        return _rope(tn, cos_a, sin_a, ATT_HEAD_DIM, ROPE_DIM // 2)

    q_ref[...] = (qk(proj[:, 0:G], qn_ref) * (ATT_HEAD_DIM ** -0.5)).T.astype(BF16)
    k_ref[...] = qk(proj[:, G:2 * G], kn_ref).astype(BF16)
    v_ref[0] = proj[:, 2 * G:3 * G].T.astype(BF16)
    o = 3 * G
    iq_ref[...] = _rope(proj[:, o:o + LANES], cos_i, sin_i, IDX_DIM, IDX_ROPE_DIM // 2).T.astype(BF16)
    ik_ref[...] = _rope(proj[:, o + LANES:o + 2 * LANES], cos_i, sin_i, IDX_DIM, IDX_ROPE_DIM // 2).astype(BF16)
    iw_ref[...] = (proj[:, o + 2 * LANES:o + 3 * LANES] * (IDX_HEADS ** -0.5)).T[0:8, :]
    o += 3 * LANES
    pb_ref[...] = proj[:, o:o + 2 * G]
    pc_ref[...] = proj[:, o + 2 * G:o + 3 * G]
    pd_ref[...] = proj[:, o + 3 * G:o + 5 * G]


def _proj_call(x2, mod, n1, w_in_p, qn_t, kn_t, cos_a, sin_a, cos_i, sin_i, seq):
    N, D = x2.shape
    tm = TM_PROJ
    tpb = seq // tm
    G = GROUP_W
    assert tm == TK_ATT
    row = lambda i: (i, 0)
    col = lambda i: (0, i)
    fixed = lambda i: (0, 0)
    sds = jax.ShapeDtypeStruct
    out_specs = [
        pl.BlockSpec((G, tm), col),
        pl.BlockSpec((tm, G), row),
        pl.BlockSpec((1, G, tm), lambda i: (i, 0, 0)),
        pl.BlockSpec((LANES, tm), col),
        pl.BlockSpec((tm, LANES), row),
        pl.BlockSpec((8, tm), col),
        pl.BlockSpec((tm, 2 * G), row),
        pl.BlockSpec((tm, G), row),
        pl.BlockSpec((tm, 2 * G), row),
    ]
    out_shape = [sds((G, N), BF16), sds((N, G), BF16), sds((N // tm, G, tm), BF16), sds((LANES, N), BF16),
                 sds((N, LANES), BF16), sds((8, N), F32), sds((N, 2 * G), F32), sds((N, G), F32), sds((N, 2 * G), F32)]
    return pl.pallas_call(
        _proj_kernel,
        grid=(N // tm,),
        in_specs=[
            pl.BlockSpec((tm, D), row),
            pl.BlockSpec((1, 6, D), lambda i: (i // tpb, 0, 0)),
            pl.BlockSpec((1, D), fixed),
            pl.BlockSpec((D, W_IN_COLS), fixed),
            pl.BlockSpec((1, G), fixed),
            pl.BlockSpec((1, G), fixed),
            pl.BlockSpec((tm, G), row),
            pl.BlockSpec((tm, G), row),
            pl.BlockSpec((tm, LANES), row),
            pl.BlockSpec((tm, LANES), row),
        ],
        out_specs=out_specs,
        out_shape=out_shape,
        compiler_params=_cparams(("parallel",)),
        name="norm_in_proj",
    )(x2, mod, n1, w_in_p, qn_t, kn_t, cos_a, sin_a, cos_i, sin_i)


def _pair_rhs(xt, head_rows, h0):
    head = lax.broadcasted_iota(jnp.int32, xt.shape, 0) >> int(np.log2(head_rows))
    zero = jnp.zeros_like(xt)
    return jnp.concatenate([jnp.where(head == h0, xt, zero), jnp.where(head == h0 + 1, xt, zero)], axis=1)


def _dsa_kernel(q_ref, k_ref, v_ref, iq_ref, ik_ref, iw_ref, ltri_ref, o_ref, sc_ref, lg_ref, *, topk):
    tq, tk = TQ_ATT, TK_ATT
    i = pl.program_id(1)
    q0 = i * tq
    n_kv = (q0 + tq + tk - 1) // tk
    kf = float(topk)

    q_pos = _lane_iota((1, tq)) + q0
    key_end = ((q_pos >> 6) + 1) << 6
    key_i = lax.broadcasted_iota(jnp.int32, (tk, tq), 0)

    def fold8(x, op):
        parts = [x[r * 8:(r + 1) * 8, :] for r in range(tk // 8)]
        while len(parts) > 1:
            parts = [op(parts[a], parts[a + 1]) for a in range(0, len(parts), 2)]
        return parts[0]

    iqt = iq_ref[...]
    iq_pairs = [_pair_rhs(iqt, IDX_DIM, h0) for h0 in range(0, IDX_HEADS, 2)]
    iw_h = [iw_ref[h:h + 1, :] for h in range(IDX_HEADS)]

    def score_body(kc, carry):
        ikc = ik_ref[pl.ds(pl.multiple_of(kc * tk, tk), tk), :]
        s = jnp.zeros((tk, tq), F32)
        for pi, rhs in enumerate(iq_pairs):
            d2 = jnp.dot(ikc, rhs, preferred_element_type=F32)
            for j in range(2):
                d = d2[:, j * tq:(j + 1) * tq]
                s = s + jnp.maximum(d * (IDX_DIM ** -0.5), 0.0) * iw_h[2 * pi + j]
        sc_ref[kc] = jnp.where(key_i + kc * tk < key_end, s, NEG_INF)
        return carry

    lax.fori_loop(0, n_kv, score_body, 0)

    def reduce_chunks(fn, init, combine, fold):
        def body(kc, part):
            return combine(part, fold8(fn(sc_ref[kc], kc * tk), combine))
        part = lax.fori_loop(0, n_kv, body, jnp.full((8, tq), init, F32))
        return fold(part, axis=0, keepdims=True)

    def count(ind):
        return reduce_chunks(ind, 0.0, jnp.add, jnp.sum)

    def col_maximum(val):
        return reduce_chunks(val, NEG_INF, jnp.maximum, jnp.max)

    small = key_end <= topk

    @pl.when(q0 + tq > topk)
    def _select():
        col_max = col_maximum(lambda b, off: b)
        col_min = -col_maximum(lambda b, off: jnp.where(b == NEG_INF, NEG_INF, -b))

        def bis_body(_, c):
            lo, hi = c
            mid = jnp.where(hi == jnp.inf, col_max, lo + (hi - lo) * 0.5)
            ge = count(lambda b, off: jnp.where(b >= mid, 1.0, 0.0)) >= kf
            return jnp.where(ge, mid, lo), jnp.where(ge, hi, mid)

        lo, hi = lax.fori_loop(0, BISECT_ITERS, bis_body, (col_min, jnp.full((1, tq), jnp.inf, F32)))

        def sd_cond(c):
            return c[0] > 0.0

        def sd_body(c):
            _, hi, thr, done = c
            cand = col_maximum(lambda b, off: jnp.where(b < hi, b, NEG_INF))
            ok = count(lambda b, off: jnp.where(b >= cand, 1.0, 0.0)) >= kf
            thr = jnp.where(done > 0.0, thr, cand)
            hi = jnp.where(done > 0.0, hi, cand)
            done = jnp.where(ok, 1.0, done)
            return jnp.sum(1.0 - done), hi, thr, done

        done0 = jnp.where(small, 1.0, 0.0)
        n0 = jnp.sum(1.0 - done0)
        _, _, thr, _ = lax.while_loop(sd_cond, sd_body, (n0, hi, jnp.full((1, tq), NEG_INF, F32), done0))
        thr = jnp.where(small, NEG_INF, thr)

        need = kf - count(lambda b, off: jnp.where(b > thr, 1.0, 0.0))
        n_tied = count(lambda b, off: jnp.where(b == thr, 1.0, 0.0))
        excess = jnp.sum(jnp.where(jnp.where(small, 0.0, n_tied) > need, 1.0, 0.0))

        @pl.when(excess <= 0.0)
        def _keep_all_ties():
            def bias_body(kc, carry):
                blk = sc_ref[kc]
                sc_ref[kc] = jnp.where(blk == NEG_INF, NEG_INF, jnp.where(blk >= thr, 0.0, NEG_INF))
                return carry
            lax.fori_loop(0, n_kv, bias_body, 0)

        @pl.when(excess > 0.0)
        def _rank_ties():
            half = tk // 2
            ltri_top = ltri_ref[0:half, 0:half]
            ltri_bot = ltri_ref[half:tk, :]

            def bias_body(kc, seen):
                blk = sc_ref[kc]
                tied = jnp.where(blk == thr, 1.0, 0.0)
                tied16 = tied.astype(BF16)
                rank = jnp.concatenate([jnp.dot(ltri_top, tied16[0:half, :], preferred_element_type=F32),
                                        jnp.dot(ltri_bot, tied16, preferred_element_type=F32)], axis=0) + seen
                tie = jnp.where(blk == thr, jnp.where(rank <= need, 0.0, NEG_INF), NEG_INF)
                bias = jnp.where(blk > thr, 0.0, tie)
                sc_ref[kc] = jnp.where(blk == NEG_INF, NEG_INF, bias)
                return seen + jnp.sum(fold8(tied, jnp.add), axis=0, keepdims=True)

            lax.fori_loop(0, n_kv, bias_body, jnp.zeros((1, tq), F32))

    @pl.when(q0 + tq <= topk)
    def _all():
        def bias_body(kc, carry):
            sc_ref[kc] = jnp.where(sc_ref[kc] == NEG_INF, NEG_INF, 0.0)
            return carry
        lax.fori_loop(0, n_kv, bias_body, 0)

    qt = q_ref[...]
    q_pairs = [_pair_rhs(qt, ATT_HEAD_DIM, h0) for h0 in range(0, ATT_HEADS, 2)]
    dh = ATT_HEAD_DIM

    def logit_body(kc, ms):
        kblk = k_ref[pl.ds(pl.multiple_of(kc * tk, tk), tk), :]
        bias = sc_ref[kc]
        ms_n = []
        for pi, rhs in enumerate(q_pairs):
            s2 = jnp.dot(kblk, rhs, preferred_element_type=F32)
            for j in range(2):
                h = 2 * pi + j
                s = s2[:, j * tq:(j + 1) * tq] + bias
                lg_ref[kc, h] = s
                ms_n.append(jnp.maximum(ms[h], fold8(s, jnp.maximum)))
        return tuple(ms_n)

    ms = lax.fori_loop(0, n_kv, logit_body, tuple(jnp.full((8, tq), NEG_INF, F32) for _ in range(ATT_HEADS)))
    ms = [jnp.max(m, axis=0, keepdims=True) for m in ms]

    def pv_body(kc, c):
        ls, accs = c
        vt = v_ref[kc]
        ls_n, accs_n = [], []
        for h in range(ATT_HEADS):
            p = jnp.exp(lg_ref[kc, h] - ms[h])
            ls_n.append(ls[h] + fold8(p, jnp.add))
            accs_n.append(accs[h] + jnp.dot(vt[h * dh:(h + 1) * dh, :], p.astype(BF16), preferred_element_type=F32))
        return tuple(ls_n), tuple(accs_n)

    init = (tuple(jnp.zeros((8, tq), F32) for _ in range(ATT_HEADS)),
            tuple(jnp.zeros((dh, tq), F32) for _ in range(ATT_HEADS)))
    ls, accs = lax.fori_loop(0, n_kv, pv_body, init)
    out_t = jnp.concatenate([accs[h] / jnp.sum(ls[h], axis=0, keepdims=True) for h in range(ATT_HEADS)], axis=0)
    o_ref[...] = out_t.T


def _dsa_call(qt, k, vt, iqt, ik, iwt, batch, seq):
    N, G = k.shape
    tq = TQ_ATT
    nq = seq // tq
    nkc = seq // TK_ATT
    topk = min(TOPK_MAX, seq // 4)
    qcol = lambda b, i: (0, b * nq + i)
    brow = lambda b, i: (b, 0)
    return pl.pallas_call(
        functools.partial(_dsa_kernel, topk=topk),
        grid=(batch, nq),
        in_specs=[
            pl.BlockSpec((G, tq), qcol),
            pl.BlockSpec((seq, G), brow),
            pl.BlockSpec((nkc, G, TK_ATT), lambda b, i: (b, 0, 0)),
            pl.BlockSpec((LANES, tq), qcol),
            pl.BlockSpec((seq, LANES), brow),
            pl.BlockSpec((8, tq), qcol),
            pl.BlockSpec((TK_ATT, TK_ATT), lambda b, i: (0, 0)),
        ],
        out_specs=pl.BlockSpec((tq, G), lambda b, i: (b * nq + i, 0)),
        out_shape=jax.ShapeDtypeStruct((N, G), F32),
        scratch_shapes=[pltpu.VMEM((nkc, TK_ATT, tq), F32), pltpu.VMEM((nkc, ATT_HEADS, TK_ATT, tq), F32)],
        compiler_params=_cparams(("parallel", "arbitrary")),
        name="dsa_attention",
    )(qt, k, vt, iqt, ik, iwt, jnp.tril(jnp.ones((TK_ATT, TK_ATT), BF16)))


def _mix_kernel(x_ref, oa_ref, pb_ref, pbh_ref, pc_ref, pch_ref, pd_ref, mod_ref,
                cw_ref, cb_ref, clg_ref, clb_ref, cpw_ref, cpb_ref,
                pw_ref, pbias_ref, ps_ref, slg_ref, slb_ref, sw_ref, sb_ref,
                on_ref, wo_ref, n2_ref, wr_ref, br_ref,
                xo_ref, h2_ref, route_ref, cnt_ref,
                ypad_ref, ppad_ref, s2_ref, s4_ref, s8_ref, run_ref, *, tiles_per_batch):
    tm = TM_MIX
    G = GROUP_W
    i = pl.program_id(0)
    t_in_b = i % tiles_per_batch
    first = t_in_b == 0
    lane_g = _lane_iota((tm, G))

    def glu(pb):
        return pb[:, 0:G] * jax.nn.sigmoid(pb[:, G:2 * G])

    ypad_ref[0:HALO, :] = jnp.where(first, 0.0, glu(pbh_ref[...]))
    ypad_ref[HALO:HALO + tm, :] = glu(pb_ref[...])
    acc = jnp.zeros((tm, G), F32)
    first_off = HALO - (CONV_WIDTH - 1)
    for phase in range(8):
        offs = [o for o in range(first_off, HALO + 1) if o % 8 == phase]
        if not offs:
            continue
        slab = ypad_ref[offs[0]:offs[-1] + tm, :]
        for o in offs:
            acc = acc + cw_ref[o - first_off:o - first_off + 1, :] * slab[o - offs[0]:o - offs[0] + tm, :]
    y = acc + cb_ref[...]
    gw = G // CONV_GROUPS
    mu = _seg_mean(y, gw)
    yc = y - mu
    var = _seg_mean(yc * yc, gw)
    y = (yc * lax.rsqrt(var + EPS)) * clg_ref[...] + clb_ref[...]
    o_b = jnp.dot(_silu(y).astype(BF16), cpw_ref[...], preferred_element_type=F32) + cpb_ref[...]

    p = pc_ref[...]
    ppad_ref[0:HALO, :] = jnp.where(first, 0.0, pch_ref[...])
    ppad_ref[HALO:HALO + tm, :] = p
    n8 = tm + HALO - 8
    s2_ref[8:8 + n8, :] = ppad_ref[8:8 + n8, :] + ppad_ref[7:7 + n8, :]
    n16 = tm + HALO - 16
    s4_ref[16:16 + n16, :] = s2_ref[16:16 + n16, :] + s2_ref[14:14 + n16, :]
    n24 = tm + HALO - 24
    s8_ref[24:24 + n24, :] = s4_ref[24:24 + n24, :] + s4_ref[20:20 + n24, :]
    s2 = s2_ref[HALO:HALO + tm, :]
    s4 = s4_ref[HALO:HALO + tm, :]
    s8 = s8_ref[HALO:HALO + tm, :]
    s16 = s8 + s8_ref[HALO - 8:HALO - 8 + tm, :]
    pgrp = lane_g >> 6
    wsum = jnp.where(pgrp == 0, s2, jnp.where(pgrp == 1, s4, jnp.where(pgrp == 2, s8, s16)))
    wlen = jnp.where(pgrp == 0, 2.0, jnp.where(pgrp == 1, 4.0, jnp.where(pgrp == 2, 8.0, 16.0)))
    tpos = (lax.broadcasted_iota(jnp.int32, (tm, G), 0) + t_in_b * tm + 1).astype(F32)
    pooled = wsum / jnp.minimum(tpos, wlen) - p
    o_c = (jnp.dot(pooled.astype(BF16), pw_ref[...], preferred_element_type=F32) + pbias_ref[...]) * ps_ref[...]

    pd = pd_ref[...]
    u, v = pd[:, 0:G], pd[:, G:2 * G]
    mu = jnp.mean(v, axis=-1, keepdims=True)
    vc = v - mu
    var = jnp.mean(vc * vc, axis=-1, keepdims=True)
    vn = ((vc * lax.rsqrt(var + EPS)) * slg_ref[...] + slb_ref[...]).astype(BF16)
    r_i = lax.broadcasted_iota(jnp.int32, (SGU_CHUNK, SGU_CHUNK), 0)
    c_i = lax.broadcasted_iota(jnp.int32, (SGU_CHUNK, SGU_CHUNK), 1)
    w_heads = [jnp.where(r_i >= c_i, sw_ref[h], 0.0).astype(BF16) for h in range(SGU_HEADS)]
    lane_c = _lane_iota((SGU_CHUNK, G)) >> 6
    mixed = []
    for n in range(tm // SGU_CHUNK):
        vch = vn[n * SGU_CHUNK:(n + 1) * SGU_CHUNK, :]
        mx = jnp.zeros((SGU_CHUNK, G), F32)
        for h in range(SGU_HEADS):
            mx = jnp.where(lane_c == h, jnp.dot(w_heads[h], vch, preferred_element_type=F32), mx)
        mixed.append(mx + sb_ref[...])
    o_d = u * jnp.concatenate(mixed, axis=0)

    proj = jnp.zeros((tm, D_MODEL), F32)
    for g, piece in enumerate((oa_ref[...], o_b, o_c, o_d)):
        ms = jnp.mean(piece * piece, axis=-1, keepdims=True)
        pn = (piece * lax.rsqrt(ms + EPS)) * on_ref[:, g * G:(g + 1) * G]
        proj = proj + jnp.dot(pn.astype(BF16), wo_ref[g * G:(g + 1) * G, :], preferred_element_type=F32)
    x_new = x_ref[...] + mod_ref[0, 2:3, :] * proj
    xo_ref[...] = x_new

    ms = jnp.mean(x_new * x_new, axis=-1, keepdims=True)
    h2 = (x_new * lax.rsqrt(ms + EPS)) * n2_ref[...]
    h2 = h2 * (1.0 + mod_ref[0, 4:5, :]) + mod_ref[0, 3:4, :]
    _to_token_major(h2_ref, h2)
    logits = jnp.dot(h2.astype(BF16), wr_ref[...], preferred_element_type=F32) + br_ref[...]
    lane = _lane_iota((tm, LANES))
    lane_f = lane.astype(F32)
    big = float(LANES)
    glog = jnp.where(lane < N_EXP_GROUPS, logits, NEG_INF)
    gmax = jnp.max(glog, axis=-1, keepdims=True)
    p_sel = 1.0 / jnp.sum(jnp.exp(glog - gmax), axis=-1, keepdims=True)
    g_idx = jnp.min(jnp.where(glog == gmax, lane_f, big), axis=-1, keepdims=True)
    e_lane = lane - N_EXP_GROUPS
    elog = jnp.where((e_lane >> 3).astype(F32) == g_idx, logits, NEG_INF)
    top1 = jnp.max(elog, axis=-1, keepdims=True)
    j1 = jnp.min(jnp.where(elog == top1, lane_f, big), axis=-1, keepdims=True)
    elog2 = jnp.where(lane_f == j1, NEG_INF, elog)
    top2 = jnp.max(elog2, axis=-1, keepdims=True)
    j2 = jnp.min(jnp.where(elog2 == top2, lane_f, big), axis=-1, keepdims=True)
    e2w = jnp.exp(top2 - top1)
    gate1 = p_sel * (1.0 / (1.0 + e2w))
    gate2 = p_sel * (e2w / (1.0 + e2w))
    e1 = j1 - float(N_EXP_GROUPS)
    e2 = j2 - float(N_EXP_GROUPS)

    @pl.when(i == 0)
    def _init():
        run_ref[...] = jnp.zeros_like(run_ref)

    onehot = jnp.where(jnp.logical_or(lane_f == e1, lane_f == e2), 1.0, 0.0)
    rr = lax.broadcasted_iota(jnp.int32, (tm, tm), 0)
    cc = lax.broadcasted_iota(jnp.int32, (tm, tm), 1)
    before = jnp.where(rr > cc, 1.0, 0.0).astype(BF16)
    prior = jnp.dot(before, onehot.astype(BF16), preferred_element_type=F32) + run_ref[0:1, :]
    rank1 = jnp.sum(jnp.where(lane_f == e1, prior, 0.0), axis=-1, keepdims=True)
    rank2 = jnp.sum(jnp.where(lane_f == e2, prior, 0.0), axis=-1, keepdims=True)
    run_new = run_ref[0:1, :] + jnp.sum(onehot, axis=0, keepdims=True)
    run_ref[...] = jnp.broadcast_to(run_new, run_ref.shape)
    cnt_ref[...] = jnp.broadcast_to(run_new, cnt_ref.shape)
    route = jnp.where(lane == 0, e1, jnp.where(lane == 1, e2, jnp.where(lane == 2, rank1, jnp.where(
        lane == 3, rank2, jnp.where(lane == 4, gate1, jnp.where(lane == 5, gate2, 0.0))))))
    route_ref[...] = route


def _mix_call(x2, oa, pb, pc, pd, mod, lw, seq):
    N, D = x2.shape
    tm = TM_MIX
    tpb = seq // tm
    G = GROUP_W
    row = lambda i: (i, 0)
    halo = lambda i: (jnp.maximum(i * (tm // HALO) - 1, 0), 0)
    fixed2 = lambda i: (0, 0)
    fixed3 = lambda i: (0, 0, 0)
    params = [lw["conv_w"], lw["conv_b"], lw["conv_ln_g"], lw["conv_ln_b"], lw["conv_pw_w"], lw["conv_pw_b"],
              lw["pool_w"], lw["pool_b"], lw["pool_scale"], lw["sgu_ln_g"], lw["sgu_ln_b"], lw["sgu_w"], lw["sgu_b"],
              lw["out_norm"], lw["w_out"], lw["norm2"], lw["w_router"], lw["b_router"]]
    param_specs = [pl.BlockSpec(p.shape, fixed3 if p.ndim == 3 else fixed2) for p in params]
    return pl.pallas_call(
        functools.partial(_mix_kernel, tiles_per_batch=tpb),
        grid=(N // tm,),
        in_specs=[
            pl.BlockSpec((tm, D), row),
            pl.BlockSpec((tm, G), row),
            pl.BlockSpec((tm, 2 * G), row),
            pl.BlockSpec((HALO, 2 * G), halo),
            pl.BlockSpec((tm, G), row),
            pl.BlockSpec((HALO, G), halo),
            pl.BlockSpec((tm, 2 * G), row),
            pl.BlockSpec((1, 6, D), lambda i: (i // tpb, 0, 0)),
        ] + param_specs,
        out_specs=[
            pl.BlockSpec((tm, D), row),
            pl.BlockSpec((tm * TOK_SUB, LANES), row),
            pl.BlockSpec((tm, LANES), row),
            pl.BlockSpec((8, LANES), fixed2),
        ],
        out_shape=[
            jax.ShapeDtypeStruct((N, D), F32),
            jax.ShapeDtypeStruct((N * TOK_SUB, LANES), F32),
            jax.ShapeDtypeStruct((N, LANES), F32),
            jax.ShapeDtypeStruct((8, LANES), F32),
        ],
        scratch_shapes=[pltpu.VMEM((tm + HALO, G), F32) for _ in range(5)] + [pltpu.VMEM((8, LANES), F32)],
        compiler_params=_cparams(("arbitrary",)),
        name="mixers_out_router",
    )(x2, oa, pb, pb, pc, pc, pd, mod, *params)


TOK_SUB = D_MODEL // LANES
ROW_DMA_UNROLL = 8


def _to_token_major(ref, x):
    n = x.shape[0]
    for j in range(TOK_SUB):
        ref[pl.ds(j, n, stride=TOK_SUB), :] = x[:, j * LANES:(j + 1) * LANES]


def _from_token_major(ref, tok0, n):
    return jnp.concatenate([ref[pl.ds(tok0 * TOK_SUB + j, n, stride=TOK_SUB), :] for j in range(TOK_SUB)], axis=1)


def _token_copy(src_hbm, row8, dst_ref, slot, r, sem):
    src = src_hbm.at[pl.ds(pl.multiple_of(row8, TOK_SUB), TOK_SUB), :]
    dst = dst_ref.at[slot, pl.ds(pl.multiple_of(r * TOK_SUB, TOK_SUB), TOK_SUB), :]
    return pltpu.make_async_copy(src, dst, sem.at[slot])


def _start_tokens(idx_ref, src_hbm, dst_ref, slot, sem, n):
    def body(g, carry):
        for j in range(ROW_DMA_UNROLL):
            r = g * ROW_DMA_UNROLL + j
            _token_copy(src_hbm, idx_ref[0, 0, r], dst_ref, slot, r, sem).start(priority=j % 2)
        return carry
    lax.fori_loop(0, n // ROW_DMA_UNROLL, body, 0)


def _wait_tokens(src_hbm, dst_ref, slot, sem, n):
    pltpu.make_async_copy(src_hbm.at[pl.ds(0, n * TOK_SUB), :], dst_ref.at[slot], sem.at[slot]).wait()


def _expert_kernel(be_ref, nu_ref, tok_ref, tokn_ref, h_hbm, wg_ref, wu_ref, wd_ref, y_ref,
                   xbuf, sem, wg_bf, wu_bf, wd_bf):
    i = pl.program_id(0)
    n_used = nu_ref[0]
    slot = i % 2
    rows = MOE_ROWS

    @pl.when(jnp.logical_and(i == 0, n_used > 0))
    def _first():
        _start_tokens(tok_ref, h_hbm, xbuf, 0, sem, rows)

    @pl.when(i + 1 < n_used)
    def _prefetch():
        _start_tokens(tokn_ref, h_hbm, xbuf, 1 - slot, sem, rows)

    changed = jnp.logical_or(i == 0, be_ref[i] != be_ref[jnp.maximum(i - 1, 0)])

    @pl.when(jnp.logical_and(changed, i < n_used))
    def _cast():
        wg_bf[...] = wg_ref[0, 0].astype(BF16)
        wu_bf[...] = wu_ref[0, 0].astype(BF16)
        wd_bf[...] = wd_ref[0, 0].astype(BF16)

    @pl.when(i < n_used)
    def _compute():
        _wait_tokens(h_hbm, xbuf, slot, sem, rows)
        xb = _from_token_major(xbuf.at[slot], 0, rows).astype(BF16)
        g = jnp.dot(xb, wg_bf[...], preferred_element_type=F32)
        u = jnp.dot(xb, wu_bf[...], preferred_element_type=F32)
        hid = (_silu(g) * u).astype(BF16)
        _to_token_major(y_ref, jnp.dot(hid, wd_bf[...], preferred_element_type=F32))

    @pl.when(i >= n_used)
    def _skip():
        y_ref[...] = jnp.zeros_like(y_ref)


def _expert_call(block_expert, n_used, row_tok2, h2, w_gate, w_up, w_down, layer):
    n_blocks, _, rows = row_tok2.shape
    D = D_MODEL
    FF = w_gate.shape[-1]
    grid_spec = pltpu.PrefetchScalarGridSpec(
        num_scalar_prefetch=2,
        grid=(n_blocks,),
        in_specs=[
            pl.BlockSpec((1, 1, rows), lambda i, be, nu: (i, 0, 0), memory_space=pltpu.SMEM),
            pl.BlockSpec((1, 1, rows), lambda i, be, nu: (jnp.minimum(i + 1, n_blocks - 1), 0, 0), memory_space=pltpu.SMEM),
            pl.BlockSpec(memory_space=pl.ANY),
            pl.BlockSpec((1, 1, D, FF), lambda i, be, nu: (layer, be[i], 0, 0)),
            pl.BlockSpec((1, 1, D, FF), lambda i, be, nu: (layer, be[i], 0, 0)),
            pl.BlockSpec((1, 1, FF, D), lambda i, be, nu: (layer, be[i], 0, 0)),
        ],
        out_specs=pl.BlockSpec((rows * TOK_SUB, LANES), lambda i, be, nu: (i, 0)),
        scratch_shapes=[
            pltpu.VMEM((2, rows * TOK_SUB, LANES), F32),
            pltpu.SemaphoreType.DMA((2,)),
            pltpu.VMEM((D, FF), BF16),
            pltpu.VMEM((D, FF), BF16),
            pltpu.VMEM((FF, D), BF16),
        ],
    )
    return pl.pallas_call(
        _expert_kernel,
        grid_spec=grid_spec,
        out_shape=jax.ShapeDtypeStruct((n_blocks * rows * TOK_SUB, LANES), F32),
        compiler_params=_cparams(("arbitrary",)),
        name="expert_mlp",
    )(block_expert, n_used, row_tok2, row_tok2, h2, w_gate, w_up, w_down)


def _combine_kernel(pos_ref, posn_ref, y_hbm, x_ref, route_ref, mod_ref, o_ref, ybuf, sem):
    i = pl.program_id(0)
    n = pl.num_programs(0)
    slot = i % 2
    tm = TM_COMB

    @pl.when(i == 0)
    def _first():
        _start_tokens(pos_ref, y_hbm, ybuf, 0, sem, 2 * tm)

    @pl.when(i + 1 < n)
    def _prefetch():
        _start_tokens(posn_ref, y_hbm, ybuf, 1 - slot, sem, 2 * tm)

    _wait_tokens(y_hbm, ybuf, slot, sem, 2 * tm)
    route = route_ref[...]
    yb = ybuf.at[slot]
    y = route[:, 4:5] * _from_token_major(yb, 0, tm) + route[:, 5:6] * _from_token_major(yb, tm, tm)
    o_ref[...] = x_ref[...] + mod_ref[0, 5:6, :] * y


def _combine_call(pos2, y_rows, x2, route, mod, seq):
    N, D = x2.shape
    tm = TM_COMB
    tpb = seq // tm
    nt = N // tm
    row = lambda i: (i, 0)
    return pl.pallas_call(
        _combine_kernel,
        grid=(nt,),
        in_specs=[
            pl.BlockSpec((1, 1, 2 * tm), lambda i: (i, 0, 0), memory_space=pltpu.SMEM),
            pl.BlockSpec((1, 1, 2 * tm), lambda i: (jnp.minimum(i + 1, nt - 1), 0, 0), memory_space=pltpu.SMEM),
            pl.BlockSpec(memory_space=pl.ANY),
            pl.BlockSpec((tm, D), row),
            pl.BlockSpec((tm, LANES), row),
            pl.BlockSpec((1, 6, D), lambda i: (i // tpb, 0, 0)),
        ],
        out_specs=pl.BlockSpec((tm, D), row),
        out_shape=jax.ShapeDtypeStruct((N, D), F32),
        scratch_shapes=[pltpu.VMEM((2, 2 * tm * TOK_SUB, LANES), F32), pltpu.SemaphoreType.DMA((2,))],
        compiler_params=_cparams(("arbitrary",)),
        name="moe_combine",
    )(pos2, pos2, y_rows, x2, route, mod)


def _rope_lane_tables(positions, rot_dim, head_w, n_rep):
    half = rot_dim // 2
    inv = jnp.power(jnp.float32(ROPE_THETA), -2.0 * jnp.arange(half, dtype=jnp.float32) / rot_dim)
    ang = positions.astype(jnp.float32)[..., None] * inv
    cos, sin = jnp.cos(ang), jnp.sin(ang)
    rest = head_w - rot_dim
    cos_h = jnp.concatenate([cos, cos, jnp.ones(cos.shape[:-1] + (rest,), F32)], axis=-1)
    sin_h = jnp.concatenate([-sin, sin, jnp.zeros(sin.shape[:-1] + (rest,), F32)], axis=-1)
    n = positions.shape[0] * positions.shape[1]
    return (jnp.tile(cos_h, (1, 1, n_rep)).reshape(n, head_w * n_rep),
            jnp.tile(sin_h, (1, 1, n_rep)).reshape(n, head_w * n_rep))


def _layer_weights(l, w_in, q_norm, k_norm, conv_w, conv_b, conv_ln_g, conv_ln_b, conv_pw_w, conv_pw_b,
                   pool_w, pool_b, pool_scale, sgu_ln_g, sgu_ln_b, sgu_w, sgu_b, out_norm, w_out, norm2,
                   w_rg, b_rg, w_re, b_re):
    G = GROUP_W
    D = D_MODEL
    pts = np.cumsum(IN_SPLITS)[:-1].tolist()
    wq, wk, wv, wiq, wik, wiw, wb, wc, wd = jnp.split(w_in[l], pts, axis=-1)
    wiw_p = jnp.pad(wiw, ((0, 0), (0, LANES - IDX_HEADS)))
    w_in_p = jnp.concatenate([wq, wk, wv, wiq, jnp.tile(wik, (1, IDX_HEADS)), wiw_p, wb, wc, wd], axis=-1).astype(BF16)
    npool = len(POOL_WINDOWS)
    pool_bd = jnp.zeros((G, G), F32)
    for g in range(npool):
        pool_bd = lax.dynamic_update_slice(pool_bd, pool_w[l, g], (g * POOL_CH, g * POOL_CH))
    sgu_bias = jnp.repeat(sgu_b[l].T, G // SGU_HEADS, axis=1)
    w_router = jnp.concatenate([w_rg[l], w_re[l].reshape(D, N_EXPERTS),
                                jnp.zeros((D, LANES - N_EXP_GROUPS - N_EXPERTS), F32)], axis=-1).astype(BF16)
    b_router = jnp.concatenate([b_rg[l], b_re[l].reshape(N_EXPERTS),
                                jnp.zeros((LANES - N_EXP_GROUPS - N_EXPERTS,), F32)]).reshape(1, LANES)
    r1 = lambda a: a.reshape(1, -1)
    return dict(
        w_in=w_in_p,
        q_norm=jnp.tile(q_norm[l], ATT_HEADS).reshape(1, G), k_norm=jnp.tile(k_norm[l], ATT_HEADS).reshape(1, G),
        conv_w=conv_w[l], conv_b=r1(conv_b[l]), conv_ln_g=r1(conv_ln_g[l]), conv_ln_b=r1(conv_ln_b[l]),
        conv_pw_w=conv_pw_w[l].astype(BF16), conv_pw_b=r1(conv_pw_b[l]),
        pool_w=pool_bd.astype(BF16), pool_b=r1(pool_b[l]), pool_scale=r1(pool_scale[l]),
        sgu_ln_g=r1(sgu_ln_g[l]), sgu_ln_b=r1(sgu_ln_b[l]), sgu_w=sgu_w[l], sgu_b=sgu_bias,
        out_norm=r1(out_norm[l]), w_out=w_out[l].astype(BF16), norm2=r1(norm2[l]),
        w_router=w_router, b_router=b_router,
    )


def _dispatch_tables(route, cnt, n_tokens):
    rows_blk = MOE_ROWS
    e = route[:, 0:2].astype(jnp.int32)
    rank = route[:, 2:4].astype(jnp.int32)
    counts = cnt[0, :N_EXPERTS].astype(jnp.int32)
    padded = (counts + rows_blk - 1) // rows_blk * rows_blk
    pad_end = jnp.cumsum(padded)
    pad_start = pad_end - padded
    pos = pad_start[e] + rank
    m = n_tokens * MOE_TOPK
    n_blocks = (m + N_EXPERTS * (rows_blk - 1) + rows_blk - 1) // rows_blk
    n_used = (pad_end[-1] // rows_blk).astype(jnp.int32).reshape(1)
    blk_row0 = jnp.arange(n_blocks, dtype=jnp.int32) * rows_blk
    block_expert = jnp.minimum(jnp.sum((pad_end[None, :] <= blk_row0[:, None]).astype(jnp.int32), axis=1),
                               N_EXPERTS - 1)
    tok = jnp.repeat(jnp.arange(n_tokens, dtype=jnp.int32), MOE_TOPK)
    row_tok = jnp.zeros((n_blocks * rows_blk,), jnp.int32).at[pos.reshape(-1)].set(tok * TOK_SUB, unique_indices=True)
    pos2 = (pos * TOK_SUB).reshape(n_tokens // TM_COMB, TM_COMB, MOE_TOPK).transpose(0, 2, 1).reshape(
        -1, 1, MOE_TOPK * TM_COMB)
    return block_expert, n_used, row_tok.reshape(n_blocks, 1, rows_blk), pos2


def kernel(x, c, positions, w_ada, b_ada, norm1, w_in, q_norm, k_norm, conv_w, conv_b, conv_ln_g, conv_ln_b, conv_pw_w, conv_pw_b, pool_w, pool_b, pool_scale, sgu_ln_g, sgu_ln_b, sgu_w, sgu_b, out_norm, w_out, norm2, w_rg, b_rg, w_re, b_re, w_gate, w_up, w_down):
    B, S, D = x.shape
    N = B * S
    assert D == D_MODEL and S % TM_PROJ == 0 and S % TK_ATT == 0 and N % TM_COMB == 0
    depth = w_ada.shape[0]
    cos_a, sin_a = _rope_lane_tables(positions, ROPE_DIM, ATT_HEAD_DIM, ATT_HEADS)
    cos_i, sin_i = _rope_lane_tables(positions, IDX_ROPE_DIM, IDX_DIM, IDX_HEADS)
    c_pad = jnp.pad(c, ((0, (-B) % 8), (0, 0)))
    mod_all = _ada_call(c_pad, w_ada, b_ada)
    x2 = x.reshape(N, D)
    for l in range(depth):
        lw = _layer_weights(l, w_in, q_norm, k_norm, conv_w, conv_b, conv_ln_g, conv_ln_b, conv_pw_w, conv_pw_b,
                            pool_w, pool_b, pool_scale, sgu_ln_g, sgu_ln_b, sgu_w, sgu_b, out_norm, w_out, norm2,
                            w_rg, b_rg, w_re, b_re)
        mod = mod_all[l, :B].reshape(B, 6, D)
        q, k, v, iq, ik, iw, pb, pc, pd = _proj_call(
            x2, mod, norm1[l].reshape(1, D), lw["w_in"], lw["q_norm"], lw["k_norm"], cos_a, sin_a, cos_i, sin_i, S)
        oa = _dsa_call(q, k, v, iq, ik, iw, B, S)
        x_mid, h2, route, cnt = _mix_call(x2, oa, pb, pc, pd, mod, lw, S)
        block_expert, n_used, row_tok2, pos2 = _dispatch_tables(route, cnt, N)
        y_rows = _expert_call(block_expert, n_used, row_tok2, h2, w_gate, w_up, w_down, l)
        x2 = _combine_call(pos2, y_rows, x_mid, route, mod, S)
    return x2.reshape(B, S, D)
```

```python
import functools

import numpy as np
import jax
import jax.numpy as jnp
from jax import lax
from jax.experimental import pallas as pl
from jax.experimental.pallas import tpu as pltpu

F32 = jnp.float32
BF16 = jnp.bfloat16
NEG_INF = float("-inf")

D_MODEL = 1024
DEPTH = 2
CHUNK = 64
N_MIXERS = 4
GROUP_W = D_MODEL // N_MIXERS
ATT_HEAD_DIM = 64
ATT_HEADS = GROUP_W // ATT_HEAD_DIM
ROPE_DIM = ATT_HEAD_DIM // 4
ROPE_THETA = 500000.0
IDX_HEADS = 4
IDX_DIM = 32
IDX_ROPE_DIM = IDX_DIM // 4
TOPK_MAX = 256
CONV_WIDTH = 31
CONV_GROUPS = 4
POOL_WINDOWS = (2, 4, 8, 16)
POOL_CH = GROUP_W // 4
SGU_CHUNK = 128
SGU_HEADS = 4
N_EXP_GROUPS = 4
EXP_PER_GROUP = 8
N_EXPERTS = N_EXP_GROUPS * EXP_PER_GROUP
EXPERT_FF = 512
MOE_TOPK = 2
EPS = 1e-6
IN_SPLITS = (GROUP_W, GROUP_W, GROUP_W, IDX_HEADS * IDX_DIM, IDX_DIM, IDX_HEADS, 2 * GROUP_W, GROUP_W, 2 * GROUP_W)

LANES = 128
HALO = 32
W_IN_COLS = 3 * GROUP_W + 3 * LANES + 5 * GROUP_W

TM_PROJ = 512
TQ_ATT = 128
TK_ATT = 512
TM_MIX = 256
MOE_ROWS = 256
TM_COMB = 256
BISECT_ITERS = 18
VMEM_LIMIT = 56 * 1024 * 1024


def _cparams(sem):
    return pltpu.CompilerParams(dimension_semantics=sem, vmem_limit_bytes=VMEM_LIMIT)


def _lane_iota(shape):
    return lax.broadcasted_iota(jnp.int32, shape, len(shape) - 1)


def _seg_mean(y, width):
    shift = int(np.log2(width))
    grp = _lane_iota(y.shape) >> shift
    out = jnp.zeros_like(y)
    for g in range(y.shape[-1] // width):
        msk = grp == g
        s = jnp.sum(jnp.where(msk, y, 0.0), axis=-1, keepdims=True)
        out = jnp.where(msk, s, out)
    return out * (1.0 / width)


def _rope(x, cos_f, sin_s, head_w, half):
    c = x.shape[-1]
    lane = _lane_iota(x.shape) & (head_w - 1)
    partner = jnp.where(lane < half, pltpu.roll(x, c - half, 1), pltpu.roll(x, half, 1))
    return x * cos_f + partner * sin_s


def _silu(x):
    return x * jax.nn.sigmoid(x)


def _ada_kernel(c_ref, w_ref, b_ref, o_ref):
    ca = _silu(c_ref[...])
    o_ref[0] = jnp.dot(ca.astype(BF16), w_ref[0].astype(BF16), preferred_element_type=F32) + b_ref[0]


def _ada_call(c_pad, w_ada, b_ada):
    L, D, D6 = w_ada.shape
    rows = c_pad.shape[0]
    tn = D
    return pl.pallas_call(
        _ada_kernel,
        grid=(L, D6 // tn),
        in_specs=[
            pl.BlockSpec((rows, D), lambda l, j: (0, 0)),
            pl.BlockSpec((1, D, tn), lambda l, j: (l, 0, j)),
            pl.BlockSpec((1, 1, tn), lambda l, j: (l, 0, j)),
        ],
        out_specs=pl.BlockSpec((1, rows, tn), lambda l, j: (l, 0, j)),
        out_shape=jax.ShapeDtypeStruct((L, rows, D6), F32),
        compiler_params=_cparams(("arbitrary", "arbitrary")),
        name="ada_mod",
    )(c_pad, w_ada, b_ada.reshape(L, 1, D6))


def _proj_kernel(x_ref, mod_ref, n1_ref, w_ref, qn_ref, kn_ref, cosa_ref, sina_ref, cosi_ref, sini_ref,
                 q_ref, k_ref, v_ref, iq_ref, ik_ref, iw_ref, pb_ref, pc_ref, pd_ref):
    x = x_ref[...]
    ms = jnp.mean(x * x, axis=-1, keepdims=True)
    h = (x * lax.rsqrt(ms + EPS)) * n1_ref[...]
    h = h * (1.0 + mod_ref[0, 1:2, :]) + mod_ref[0, 0:1, :]
    proj = jnp.dot(h.astype(BF16), w_ref[...], preferred_element_type=F32)
    G = GROUP_W
    cos_a, sin_a = cosa_ref[...], sina_ref[...]
    cos_i, sin_i = cosi_ref[...], sini_ref[...]

    def qk(t, g_ref):
        tn = (t * lax.rsqrt(_seg_mean(t * t, ATT_HEAD_DIM) + EPS)) * g_ref[...]
        return _rope(tn, cos_a, sin_a, ATT_HEAD_DIM, ROPE_DIM // 2)

    q_ref[...] = (qk(proj[:, 0:G], qn_ref) * (ATT_HEAD_DIM ** -0.5)).T.astype(BF16)
    k_ref[...] = qk(proj[:, G:2 * G], kn_ref).astype(BF16)
    v_ref[0] = proj[:, 2 * G:3 * G].T.astype(BF16)
    o = 3 * G
    iq_ref[...] = _rope(proj[:, o:o + LANES], cos_i, sin_i, IDX_DIM, IDX_ROPE_DIM // 2).T.astype(BF16)
    ik_ref[...] = _rope(proj[:, o + LANES:o + 2 * LANES], cos_i, sin_i, IDX_DIM, IDX_ROPE_DIM // 2).astype(BF16)
    iw_ref[...] = (proj[:, o + 2 * LANES:o + 3 * LANES] * (IDX_HEADS ** -0.5)).T[0:8, :]
    o += 3 * LANES
    pb_ref[...] = proj[:, o:o + 2 * G]
    pc_ref[...] = proj[:, o + 2 * G:o + 3 * G]
    pd_ref[...] = proj[:, o + 3 * G:o + 5 * G]


def _proj_call(x2, mod, n1, w_in_p, qn_t, kn_t, cos_a, sin_a, cos_i, sin_i, seq):
    N, D = x2.shape
    tm = TM_PROJ
    tpb = seq // tm
    G = GROUP_W
    assert tm == TK_ATT
    row = lambda i: (i, 0)
    col = lambda i: (0, i)
    fixed = lambda i: (0, 0)
    sds = jax.ShapeDtypeStruct
    out_specs = [
        pl.BlockSpec((G, tm), col),
        pl.BlockSpec((tm, G), row),
        pl.BlockSpec((1, G, tm), lambda i: (i, 0, 0)),
        pl.BlockSpec((LANES, tm), col),
        pl.BlockSpec((tm, LANES), row),
        pl.BlockSpec((8, tm), col),
        pl.BlockSpec((tm, 2 * G), row),
        pl.BlockSpec((tm, G), row),
        pl.BlockSpec((tm, 2 * G), row),
    ]
    out_shape = [sds((G, N), BF16), sds((N, G), BF16), sds((N // tm, G, tm), BF16), sds((LANES, N), BF16),
                 sds((N, LANES), BF16), sds((8, N), F32), sds((N, 2 * G), F32), sds((N, G), F32), sds((N, 2 * G), F32)]
    return pl.pallas_call(
        _proj_kernel,
        grid=(N // tm,),
        in_specs=[
            pl.BlockSpec((tm, D), row),
            pl.BlockSpec((1, 6, D), lambda i: (i // tpb, 0, 0)),
            pl.BlockSpec((1, D), fixed),
            pl.BlockSpec((D, W_IN_COLS), fixed),
            pl.BlockSpec((1, G), fixed),
            pl.BlockSpec((1, G), fixed),
            pl.BlockSpec((tm, G), row),
            pl.BlockSpec((tm, G), row),
            pl.BlockSpec((tm, LANES), row),
            pl.BlockSpec((tm, LANES), row),
        ],
        out_specs=out_specs,
        out_shape=out_shape,
        compiler_params=_cparams(("parallel",)),
        name="norm_in_proj",
    )(x2, mod, n1, w_in_p, qn_t, kn_t, cos_a, sin_a, cos_i, sin_i)


def _pair_rhs(xt, head_rows, h0):
    head = lax.broadcasted_iota(jnp.int32, xt.shape, 0) >> int(np.log2(head_rows))
    zero = jnp.zeros_like(xt)
    return jnp.concatenate([jnp.where(head == h0, xt, zero), jnp.where(head == h0 + 1, xt, zero)], axis=1)


def _dsa_kernel(q_ref, k_ref, v_ref, iq_ref, ik_ref, iw_ref, ltri_ref, o_ref, sc_ref, lg_ref, *, topk):
    tq, tk = TQ_ATT, TK_ATT
    i = pl.program_id(1)
    q0 = i * tq
    n_kv = (q0 + tq + tk - 1) // tk
    kf = float(topk)

    q_pos = _lane_iota((1, tq)) + q0
    key_end = ((q_pos >> 6) + 1) << 6
    key_i = lax.broadcasted_iota(jnp.int32, (tk, tq), 0)

    def fold8(x, op):
        parts = [x[r * 8:(r + 1) * 8, :] for r in range(tk // 8)]
        while len(parts) > 1:
            parts = [op(parts[a], parts[a + 1]) for a in range(0, len(parts), 2)]
        return parts[0]

    iqt = iq_ref[...]
    iq_pairs = [_pair_rhs(iqt, IDX_DIM, h0) for h0 in range(0, IDX_HEADS, 2)]
    iw_h = [iw_ref[h:h + 1, :] for h in range(IDX_HEADS)]

    def score_body(kc, carry):
        ikc = ik_ref[pl.ds(pl.multiple_of(kc * tk, tk), tk), :]
        s = jnp.zeros((tk, tq), F32)
        for pi, rhs in enumerate(iq_pairs):
            d2 = jnp.dot(ikc, rhs, preferred_element_type=F32)
            for j in range(2):
                d = d2[:, j * tq:(j + 1) * tq]
                s = s + jnp.maximum(d * (IDX_DIM ** -0.5), 0.0) * iw_h[2 * pi + j]
        sc_ref[kc] = jnp.where(key_i + kc * tk < key_end, s, NEG_INF)
        return carry

    lax.fori_loop(0, n_kv, score_body, 0)

    def reduce_chunks(fn, init, combine, fold):
        def body(kc, part):
            return combine(part, fold8(fn(sc_ref[kc], kc * tk), combine))
        part = lax.fori_loop(0, n_kv, body, jnp.full((8, tq), init, F32))
        return fold(part, axis=0, keepdims=True)

    def count(ind):
        return reduce_chunks(ind, 0.0, jnp.add, jnp.sum)

    def col_maximum(val):
        return reduce_chunks(val, NEG_INF, jnp.maximum, jnp.max)

    small = key_end <= topk

    @pl.when(q0 + tq > topk)
    def _select():
        col_max = col_maximum(lambda b, off: b)
        col_min = -col_maximum(lambda b, off: jnp.where(b == NEG_INF, NEG_INF, -b))

        def bis_body(_, c):
            lo, hi = c
            mid = jnp.where(hi == jnp.inf, col_max, lo + (hi - lo) * 0.5)
            ge = count(lambda b, off: jnp.where(b >= mid, 1.0, 0.0)) >= kf
            return jnp.where(ge, mid, lo), jnp.where(ge, hi, mid)

        lo, hi = lax.fori_loop(0, BISECT_ITERS, bis_body, (col_min, jnp.full((1, tq), jnp.inf, F32)))

        def sd_cond(c):
            return c[0] > 0.0

        def sd_body(c):
            _, hi, thr, done = c
            cand = col_maximum(lambda b, off: jnp.where(b < hi, b, NEG_INF))
            ok = count(lambda b, off: jnp.where(b >= cand, 1.0, 0.0)) >= kf
            thr = jnp.where(done > 0.0, thr, cand)
            hi = jnp.where(done > 0.0, hi, cand)
            done = jnp.where(ok, 1.0, done)
            return jnp.sum(1.0 - done), hi, thr, done

        done0 = jnp.where(small, 1.0, 0.0)
        n0 = jnp.sum(1.0 - done0)
        _, _, thr, _ = lax.while_loop(sd_cond, sd_body, (n0, hi, jnp.full((1, tq), NEG_INF, F32), done0))
        thr = jnp.where(small, NEG_INF, thr)

        need = kf - count(lambda b, off: jnp.where(b > thr, 1.0, 0.0))
        n_tied = count(lambda b, off: jnp.where(b == thr, 1.0, 0.0))
        excess = jnp.sum(jnp.where(jnp.where(small, 0.0, n_tied) > need, 1.0, 0.0))

        @pl.when(excess <= 0.0)
        def _keep_all_ties():
            def bias_body(kc, carry):
                blk = sc_ref[kc]
                sc_ref[kc] = jnp.where(blk == NEG_INF, NEG_INF, jnp.where(blk >= thr, 0.0, NEG_INF))
                return carry
            lax.fori_loop(0, n_kv, bias_body, 0)

        @pl.when(excess > 0.0)
        def _rank_ties():
            half = tk // 2
            ltri_top = ltri_ref[0:half, 0:half]
            ltri_bot = ltri_ref[half:tk, :]

            def bias_body(kc, seen):
                blk = sc_ref[kc]
                tied = jnp.where(blk == thr, 1.0, 0.0)
                tied16 = tied.astype(BF16)
                rank = jnp.concatenate([jnp.dot(ltri_top, tied16[0:half, :], preferred_element_type=F32),
                                        jnp.dot(ltri_bot, tied16, preferred_element_type=F32)], axis=0) + seen
                tie = jnp.where(blk == thr, jnp.where(rank <= need, 0.0, NEG_INF), NEG_INF)
                bias = jnp.where(blk > thr, 0.0, tie)
                sc_ref[kc] = jnp.where(blk == NEG_INF, NEG_INF, bias)
                return seen + jnp.sum(fold8(tied, jnp.add), axis=0, keepdims=True)

            lax.fori_loop(0, n_kv, bias_body, jnp.zeros((1, tq), F32))

    @pl.when(q0 + tq <= topk)
    def _all():
        def bias_body(kc, carry):
            sc_ref[kc] = jnp.where(sc_ref[kc] == NEG_INF, NEG_INF, 0.0)
            return carry
        lax.fori_loop(0, n_kv, bias_body, 0)

    qt = q_ref[...]
    q_pairs = [_pair_rhs(qt, ATT_HEAD_DIM, h0) for h0 in range(0, ATT_HEADS, 2)]
    dh = ATT_HEAD_DIM

    def logit_body(kc, ms):
        kblk = k_ref[pl.ds(pl.multiple_of(kc * tk, tk), tk), :]
        bias = sc_ref[kc]
        ms_n = []
        for pi, rhs in enumerate(q_pairs):
            s2 = jnp.dot(kblk, rhs, preferred_element_type=F32)
            for j in range(2):
                h = 2 * pi + j
                s = s2[:, j * tq:(j + 1) * tq] + bias
                lg_ref[kc, h] = s
                ms_n.append(jnp.maximum(ms[h], fold8(s, jnp.maximum)))
        return tuple(ms_n)

    ms = lax.fori_loop(0, n_kv, logit_body, tuple(jnp.full((8, tq), NEG_INF, F32) for _ in range(ATT_HEADS)))
    ms = [jnp.max(m, axis=0, keepdims=True) for m in ms]

    def pv_body(kc, c):
        ls, accs = c
        vt = v_ref[kc]
        ls_n, accs_n = [], []
        for h in range(ATT_HEADS):
            p = jnp.exp(lg_ref[kc, h] - ms[h])
            ls_n.append(ls[h] + fold8(p, jnp.add))
            accs_n.append(accs[h] + jnp.dot(vt[h * dh:(h + 1) * dh, :], p.astype(BF16), preferred_element_type=F32))
        return tuple(ls_n), tuple(accs_n)

    init = (tuple(jnp.zeros((8, tq), F32) for _ in range(ATT_HEADS)),
            tuple(jnp.zeros((dh, tq), F32) for _ in range(ATT_HEADS)))
    ls, accs = lax.fori_loop(0, n_kv, pv_body, init)
    out_t = jnp.concatenate([accs[h] / jnp.sum(ls[h], axis=0, keepdims=True) for h in range(ATT_HEADS)], axis=0)
    o_ref[...] = out_t.T


def _dsa_call(qt, k, vt, iqt, ik, iwt, batch, seq):
    N, G = k.shape
    tq = TQ_ATT
    nq = seq // tq
    nkc = seq // TK_ATT
    topk = min(TOPK_MAX, seq // 4)
    qcol = lambda b, i: (0, b * nq + i)
    brow = lambda b, i: (b, 0)
    return pl.pallas_call(
        functools.partial(_dsa_kernel, topk=topk),
        grid=(batch, nq),
        in_specs=[
            pl.BlockSpec((G, tq), qcol),
            pl.BlockSpec((seq, G), brow),
            pl.BlockSpec((nkc, G, TK_ATT), lambda b, i: (b, 0, 0)),
            pl.BlockSpec((LANES, tq), qcol),
            pl.BlockSpec((seq, LANES), brow),
            pl.BlockSpec((8, tq), qcol),
            pl.BlockSpec((TK_ATT, TK_ATT), lambda b, i: (0, 0)),
        ],
        out_specs=pl.BlockSpec((tq, G), lambda b, i: (b * nq + i, 0)),
        out_shape=jax.ShapeDtypeStruct((N, G), F32),
        scratch_shapes=[pltpu.VMEM((nkc, TK_ATT, tq), F32), pltpu.VMEM((nkc, ATT_HEADS, TK_ATT, tq), F32)],
        compiler_params=_cparams(("parallel", "arbitrary")),
        name="dsa_attention",
    )(qt, k, vt, iqt, ik, iwt, jnp.tril(jnp.ones((TK_ATT, TK_ATT), BF16)))


def _mix_kernel(x_ref, oa_ref, pb_ref, pbh_ref, pc_ref, pch_ref, pd_ref, mod_ref,
                cw_ref, cb_ref, clg_ref, clb_ref, cpw_ref, cpb_ref,
                pw_ref, pbias_ref, ps_ref, slg_ref, slb_ref, sw_ref, sb_ref,
                on_ref, wo_ref, n2_ref, wr_ref, br_ref,
                xo_ref, h2_ref, route_ref, cnt_ref,
                ypad_ref, ppad_ref, s2_ref, s4_ref, s8_ref, run_ref, *, tiles_per_batch):
    tm = TM_MIX
    G = GROUP_W
    i = pl.program_id(0)
    t_in_b = i % tiles_per_batch
    first = t_in_b == 0
    lane_g = _lane_iota((tm, G))

    def glu(pb):
        return pb[:, 0:G] * jax.nn.sigmoid(pb[:, G:2 * G])

    ypad_ref[0:HALO, :] = jnp.where(first, 0.0, glu(pbh_ref[...]))
    ypad_ref[HALO:HALO + tm, :] = glu(pb_ref[...])
    acc = jnp.zeros((tm, G), F32)
    first_off = HALO - (CONV_WIDTH - 1)
    for phase in range(8):
        offs = [o for o in range(first_off, HALO + 1) if o % 8 == phase]
        if not offs:
            continue
        slab = ypad_ref[offs[0]:offs[-1] + tm, :]
        for o in offs:
            acc = acc + cw_ref[o - first_off:o - first_off + 1, :] * slab[o - offs[0]:o - offs[0] + tm, :]
    y = acc + cb_ref[...]
    gw = G // CONV_GROUPS
    mu = _seg_mean(y, gw)
    yc = y - mu
    var = _seg_mean(yc * yc, gw)
    y = (yc * lax.rsqrt(var + EPS)) * clg_ref[...] + clb_ref[...]
    o_b = jnp.dot(_silu(y).astype(BF16), cpw_ref[...], preferred_element_type=F32) + cpb_ref[...]

    p = pc_ref[...]
    ppad_ref[0:HALO, :] = jnp.where(first, 0.0, pch_ref[...])
    ppad_ref[HALO:HALO + tm, :] = p
    n8 = tm + HALO - 8
    s2_ref[8:8 + n8, :] = ppad_ref[8:8 + n8, :] + ppad_ref[7:7 + n8, :]
    n16 = tm + HALO - 16
    s4_ref[16:16 + n16, :] = s2_ref[16:16 + n16, :] + s2_ref[14:14 + n16, :]
    n24 = tm + HALO - 24
    s8_ref[24:24 + n24, :] = s4_ref[24:24 + n24, :] + s4_ref[20:20 + n24, :]
    s2 = s2_ref[HALO:HALO + tm, :]
    s4 = s4_ref[HALO:HALO + tm, :]
    s8 = s8_ref[HALO:HALO + tm, :]
    s16 = s8 + s8_ref[HALO - 8:HALO - 8 + tm, :]
    pgrp = lane_g >> 6
    wsum = jnp.where(pgrp == 0, s2, jnp.where(pgrp == 1, s4, jnp.where(pgrp == 2, s8, s16)))
    wlen = jnp.where(pgrp == 0, 2.0, jnp.where(pgrp == 1, 4.0, jnp.where(pgrp == 2, 8.0, 16.0)))
    tpos = (lax.broadcasted_iota(jnp.int32, (tm, G), 0) + t_in_b * tm + 1).astype(F32)
    pooled = wsum / jnp.minimum(tpos, wlen) - p
    o_c = (jnp.dot(pooled.astype(BF16), pw_ref[...], preferred_element_type=F32) + pbias_ref[...]) * ps_ref[...]

    pd = pd_ref[...]
    u, v = pd[:, 0:G], pd[:, G:2 * G]
    mu = jnp.mean(v, axis=-1, keepdims=True)
    vc = v - mu
    var = jnp.mean(vc * vc, axis=-1, keepdims=True)
    vn = ((vc * lax.rsqrt(var + EPS)) * slg_ref[...] + slb_ref[...]).astype(BF16)
    r_i = lax.broadcasted_iota(jnp.int32, (SGU_CHUNK, SGU_CHUNK), 0)
    c_i = lax.broadcasted_iota(jnp.int32, (SGU_CHUNK, SGU_CHUNK), 1)
    w_heads = [jnp.where(r_i >= c_i, sw_ref[h], 0.0).astype(BF16) for h in range(SGU_HEADS)]
    lane_c = _lane_iota((SGU_CHUNK, G)) >> 6
    mixed = []
    for n in range(tm // SGU_CHUNK):
        vch = vn[n * SGU_CHUNK:(n + 1) * SGU_CHUNK, :]
        mx = jnp.zeros((SGU_CHUNK, G), F32)
        for h in range(SGU_HEADS):
            mx = jnp.where(lane_c == h, jnp.dot(w_heads[h], vch, preferred_element_type=F32), mx)
        mixed.append(mx + sb_ref[...])
    o_d = u * jnp.concatenate(mixed, axis=0)

    proj = jnp.zeros((tm, D_MODEL), F32)
    for g, piece in enumerate((oa_ref[...], o_b, o_c, o_d)):
        ms = jnp.mean(piece * piece, axis=-1, keepdims=True)
        pn = (piece * lax.rsqrt(ms + EPS)) * on_ref[:, g * G:(g + 1) * G]
        proj = proj + jnp.dot(pn.astype(BF16), wo_ref[g * G:(g + 1) * G, :], preferred_element_type=F32)
    x_new = x_ref[...] + mod_ref[0, 2:3, :] * proj
    xo_ref[...] = x_new

    ms = jnp.mean(x_new * x_new, axis=-1, keepdims=True)
    h2 = (x_new * lax.rsqrt(ms + EPS)) * n2_ref[...]
    h2 = h2 * (1.0 + mod_ref[0, 4:5, :]) + mod_ref[0, 3:4, :]
    _to_token_major(h2_ref, h2)
    logits = jnp.dot(h2.astype(BF16), wr_ref[...], preferred_element_type=F32) + br_ref[...]
    lane = _lane_iota((tm, LANES))
    lane_f = lane.astype(F32)
    big = float(LANES)
    glog = jnp.where(lane < N_EXP_GROUPS, logits, NEG_INF)
    gmax = jnp.max(glog, axis=-1, keepdims=True)
    p_sel = 1.0 / jnp.sum(jnp.exp(glog - gmax), axis=-1, keepdims=True)
    g_idx = jnp.min(jnp.where(glog == gmax, lane_f, big), axis=-1, keepdims=True)
    e_lane = lane - N_EXP_GROUPS
    elog = jnp.where((e_lane >> 3).astype(F32) == g_idx, logits, NEG_INF)
    top1 = jnp.max(elog, axis=-1, keepdims=True)
    j1 = jnp.min(jnp.where(elog == top1, lane_f, big), axis=-1, keepdims=True)
    elog2 = jnp.where(lane_f == j1, NEG_INF, elog)
    top2 = jnp.max(elog2, axis=-1, keepdims=True)
    j2 = jnp.min(jnp.where(elog2 == top2, lane_f, big), axis=-1, keepdims=True)
    e2w = jnp.exp(top2 - top1)
    gate1 = p_sel * (1.0 / (1.0 + e2w))
    gate2 = p_sel * (e2w / (1.0 + e2w))
    e1 = j1 - float(N_EXP_GROUPS)
    e2 = j2 - float(N_EXP_GROUPS)

    @pl.when(i == 0)
    def _init():
        run_ref[...] = jnp.zeros_like(run_ref)

    onehot = jnp.where(jnp.logical_or(lane_f == e1, lane_f == e2), 1.0, 0.0)
    rr = lax.broadcasted_iota(jnp.int32, (tm, tm), 0)
    cc = lax.broadcasted_iota(jnp.int32, (tm, tm), 1)
    before = jnp.where(rr > cc, 1.0, 0.0).astype(BF16)
    prior = jnp.dot(before, onehot.astype(BF16), preferred_element_type=F32) + run_ref[0:1, :]
    rank1 = jnp.sum(jnp.where(lane_f == e1, prior, 0.0), axis=-1, keepdims=True)
    rank2 = jnp.sum(jnp.where(lane_f == e2, prior, 0.0), axis=-1, keepdims=True)
    run_new = run_ref[0:1, :] + jnp.sum(onehot, axis=0, keepdims=True)
    run_ref[...] = jnp.broadcast_to(run_new, run_ref.shape)
    cnt_ref[...] = jnp.broadcast_to(run_new, cnt_ref.shape)
    route = jnp.where(lane == 0, e1, jnp.where(lane == 1, e2, jnp.where(lane == 2, rank1, jnp.where(
        lane == 3, rank2, jnp.where(lane == 4, gate1, jnp.where(lane == 5, gate2, 0.0))))))
    route_ref[...] = route


def _mix_call(x2, oa, pb, pc, pd, mod, lw, seq):
    N, D = x2.shape
    tm = TM_MIX
    tpb = seq // tm
    G = GROUP_W
    row = lambda i: (i, 0)
    halo = lambda i: (jnp.maximum(i * (tm // HALO) - 1, 0), 0)
    fixed2 = lambda i: (0, 0)
    fixed3 = lambda i: (0, 0, 0)
    params = [lw["conv_w"], lw["conv_b"], lw["conv_ln_g"], lw["conv_ln_b"], lw["conv_pw_w"], lw["conv_pw_b"],
              lw["pool_w"], lw["pool_b"], lw["pool_scale"], lw["sgu_ln_g"], lw["sgu_ln_b"], lw["sgu_w"], lw["sgu_b"],
              lw["out_norm"], lw["w_out"], lw["norm2"], lw["w_router"], lw["b_router"]]
    param_specs = [pl.BlockSpec(p.shape, fixed3 if p.ndim == 3 else fixed2) for p in params]
    return pl.pallas_call(
        functools.partial(_mix_kernel, tiles_per_batch=tpb),
        grid=(N // tm,),
        in_specs=[
            pl.BlockSpec((tm, D), row),
            pl.BlockSpec((tm, G), row),
            pl.BlockSpec((tm, 2 * G), row),
            pl.BlockSpec((HALO, 2 * G), halo),
            pl.BlockSpec((tm, G), row),
            pl.BlockSpec((HALO, G), halo),
            pl.BlockSpec((tm, 2 * G), row),
            pl.BlockSpec((1, 6, D), lambda i: (i // tpb, 0, 0)),
        ] + param_specs,
        out_specs=[
            pl.BlockSpec((tm, D), row),
            pl.BlockSpec((tm * TOK_SUB, LANES), row),
            pl.BlockSpec((tm, LANES), row),
            pl.BlockSpec((8, LANES), fixed2),
        ],
        out_shape=[
            jax.ShapeDtypeStruct((N, D), F32),
            jax.ShapeDtypeStruct((N * TOK_SUB, LANES), F32),
            jax.ShapeDtypeStruct((N, LANES), F32),
            jax.ShapeDtypeStruct((8, LANES), F32),
        ],
        scratch_shapes=[pltpu.VMEM((tm + HALO, G), F32) for _ in range(5)] + [pltpu.VMEM((8, LANES), F32)],
        compiler_params=_cparams(("arbitrary",)),
        name="mixers_out_router",
    )(x2, oa, pb, pb, pc, pc, pd, mod, *params)


TOK_SUB = D_MODEL // LANES
ROW_DMA_UNROLL = 8


def _to_token_major(ref, x):
    n = x.shape[0]
    for j in range(TOK_SUB):
        ref[pl.ds(j, n, stride=TOK_SUB), :] = x[:, j * LANES:(j + 1) * LANES]


def _from_token_major(ref, tok0, n):
    return jnp.concatenate([ref[pl.ds(tok0 * TOK_SUB + j, n, stride=TOK_SUB), :] for j in range(TOK_SUB)], axis=1)


def _token_copy(src_hbm, row8, dst_ref, slot, r, sem):
    src = src_hbm.at[pl.ds(pl.multiple_of(row8, TOK_SUB), TOK_SUB), :]
    dst = dst_ref.at[slot, pl.ds(pl.multiple_of(r * TOK_SUB, TOK_SUB), TOK_SUB), :]
    return pltpu.make_async_copy(src, dst, sem.at[slot])


def _start_tokens(idx_ref, src_hbm, dst_ref, slot, sem, n):
    def body(g, carry):
        for j in range(ROW_DMA_UNROLL):
            r = g * ROW_DMA_UNROLL + j
            _token_copy(src_hbm, idx_ref[0, 0, r], dst_ref, slot, r, sem).start(priority=j % 2)
        return carry
    lax.fori_loop(0, n // ROW_DMA_UNROLL, body, 0)


def _wait_tokens(src_hbm, dst_ref, slot, sem, n):
    pltpu.make_async_copy(src_hbm.at[pl.ds(0, n * TOK_SUB), :], dst_ref.at[slot], sem.at[slot]).wait()


def _dispatch_kernel(tail_ref, nu_ref, pos_ref, h_ref, xs_hbm, zbuf, zsem, sem):
    i = pl.program_id(0)
    tm = TM_COMB
    blk = MOE_ROWS * TOK_SUB
    n_blocks = xs_hbm.shape[0] // blk

    def zero_block(row0):
        return pltpu.make_async_copy(zbuf, xs_hbm.at[pl.ds(pl.multiple_of(row0, TOK_SUB), blk), :], zsem.at[0])

    @pl.when(i == 0)
    def _zero_fill():
        zbuf[...] = jnp.zeros_like(zbuf)
        n_used = nu_ref[0]
        for e in range(N_EXPERTS):
            @pl.when(tail_ref[e] >= 0)
            def _():
                zero_block(tail_ref[e]).start()
        lax.fori_loop(n_used, n_blocks, lambda b, c: (zero_block(b * blk).start(), c)[1], 0)
        for e in range(N_EXPERTS):
            @pl.when(tail_ref[e] >= 0)
            def _():
                zero_block(tail_ref[e]).wait()
        lax.fori_loop(n_used, n_blocks, lambda b, c: (zero_block(b * blk).wait(), c)[1], 0)

    def body(g, carry):
        for j in range(ROW_DMA_UNROLL):
            r = g * ROW_DMA_UNROLL + j
            t = jnp.where(r < tm, r, r - tm)
            src = h_ref.at[pl.ds(pl.multiple_of(t * TOK_SUB, TOK_SUB), TOK_SUB), :]
            dst = xs_hbm.at[pl.ds(pl.multiple_of(pos_ref[0, 0, r], TOK_SUB), TOK_SUB), :]
            pltpu.make_async_copy(src, dst, sem.at[0]).start(priority=j % 2)
        return carry
    lax.fori_loop(0, 2 * tm // ROW_DMA_UNROLL, body, 0)
    for _ in range(MOE_TOPK):
        pltpu.make_async_copy(h_ref, xs_hbm.at[pl.ds(0, tm * TOK_SUB), :], sem.at[0]).wait()


def _dispatch_call(tail, n_used, pos2, h2, n_rows):
    nt = pos2.shape[0]
    tm = TM_COMB
    grid_spec = pltpu.PrefetchScalarGridSpec(
        num_scalar_prefetch=2,
        grid=(nt,),
        in_specs=[
            pl.BlockSpec((1, 1, 2 * tm), lambda i, *_: (i, 0, 0), memory_space=pltpu.SMEM),
            pl.BlockSpec((tm * TOK_SUB, LANES), lambda i, *_: (i, 0)),
        ],
        out_specs=pl.BlockSpec(memory_space=pl.ANY),
        scratch_shapes=[
            pltpu.VMEM((MOE_ROWS * TOK_SUB, LANES), F32),
            pltpu.SemaphoreType.DMA((1,)),
            pltpu.SemaphoreType.DMA((1,)),
        ],
    )
    return pl.pallas_call(
        _dispatch_kernel,
        grid_spec=grid_spec,
        out_shape=jax.ShapeDtypeStruct((n_rows * TOK_SUB, LANES), F32),
        compiler_params=_cparams(("arbitrary",)),
        name="moe_dispatch",
    )(tail, n_used, pos2, h2)


def _expert_kernel(be_ref, nu_ref, x_ref, wg_ref, wu_ref, wd_ref, y_ref, wg_bf, wu_bf, wd_bf):
    i = pl.program_id(0)
    n_used = nu_ref[0]
    rows = MOE_ROWS

    changed = jnp.logical_or(i == 0, be_ref[i] != be_ref[jnp.maximum(i - 1, 0)])

    @pl.when(jnp.logical_and(changed, i < n_used))
    def _cast():
        wg_bf[...] = wg_ref[0, 0].astype(BF16)
        wu_bf[...] = wu_ref[0, 0].astype(BF16)
        wd_bf[...] = wd_ref[0, 0].astype(BF16)

    @pl.when(i < n_used)
    def _compute():
        xb = _from_token_major(x_ref, 0, rows).astype(BF16)
        g = jnp.dot(xb, wg_bf[...], preferred_element_type=F32)
        u = jnp.dot(xb, wu_bf[...], preferred_element_type=F32)
        hid = (_silu(g) * u).astype(BF16)
        _to_token_major(y_ref, jnp.dot(hid, wd_bf[...], preferred_element_type=F32))

    @pl.when(i >= n_used)
    def _skip():
        y_ref[...] = jnp.zeros_like(y_ref)


def _expert_call(block_expert, n_used, xs, w_gate, w_up, w_down, layer):
    rows = MOE_ROWS
    n_blocks = xs.shape[0] // (rows * TOK_SUB)
    D = D_MODEL
    FF = w_gate.shape[-1]
    grid_spec = pltpu.PrefetchScalarGridSpec(
        num_scalar_prefetch=2,
        grid=(n_blocks,),
        in_specs=[
            pl.BlockSpec((rows * TOK_SUB, LANES), lambda i, be, nu: (jnp.minimum(i, jnp.maximum(nu[0] - 1, 0)), 0)),
            pl.BlockSpec((1, 1, D, FF), lambda i, be, nu: (layer, be[i], 0, 0)),
            pl.BlockSpec((1, 1, D, FF), lambda i, be, nu: (layer, be[i], 0, 0)),
            pl.BlockSpec((1, 1, FF, D), lambda i, be, nu: (layer, be[i], 0, 0)),
        ],
        out_specs=pl.BlockSpec((rows * TOK_SUB, LANES), lambda i, be, nu: (i, 0)),
        scratch_shapes=[
            pltpu.VMEM((D, FF), BF16),
            pltpu.VMEM((D, FF), BF16),
            pltpu.VMEM((FF, D), BF16),
        ],
    )
    return pl.pallas_call(
        _expert_kernel,
        grid_spec=grid_spec,
        out_shape=jax.ShapeDtypeStruct((n_blocks * rows * TOK_SUB, LANES), F32),
        compiler_params=_cparams(("arbitrary",)),
        name="expert_mlp",
    )(block_expert, n_used, xs, w_gate, w_up, w_down)


def _combine_kernel(pos_ref, posn_ref, y_hbm, x_ref, route_ref, mod_ref, o_ref, ybuf, sem):
    i = pl.program_id(0)
    n = pl.num_programs(0)
    slot = i % 2
    tm = TM_COMB

    @pl.when(i == 0)
    def _first():
        _start_tokens(pos_ref, y_hbm, ybuf, 0, sem, 2 * tm)

    @pl.when(i + 1 < n)
    def _prefetch():
        _start_tokens(posn_ref, y_hbm, ybuf, 1 - slot, sem, 2 * tm)

    _wait_tokens(y_hbm, ybuf, slot, sem, 2 * tm)
    route = route_ref[...]
    yb = ybuf.at[slot]
    y = route[:, 4:5] * _from_token_major(yb, 0, tm) + route[:, 5:6] * _from_token_major(yb, tm, tm)
    o_ref[...] = x_ref[...] + mod_ref[0, 5:6, :] * y


def _combine_call(pos2, y_rows, x2, route, mod, seq):
    N, D = x2.shape
    tm = TM_COMB
    tpb = seq // tm
    nt = N // tm
    row = lambda i: (i, 0)
    return pl.pallas_call(
        _combine_kernel,
        grid=(nt,),
        in_specs=[
            pl.BlockSpec((1, 1, 2 * tm), lambda i: (i, 0, 0), memory_space=pltpu.SMEM),
            pl.BlockSpec((1, 1, 2 * tm), lambda i: (jnp.minimum(i + 1, nt - 1), 0, 0), memory_space=pltpu.SMEM),
            pl.BlockSpec(memory_space=pl.ANY),
            pl.BlockSpec((tm, D), row),
            pl.BlockSpec((tm, LANES), row),
            pl.BlockSpec((1, 6, D), lambda i: (i // tpb, 0, 0)),
        ],
        out_specs=pl.BlockSpec((tm, D), row),
        out_shape=jax.ShapeDtypeStruct((N, D), F32),
        scratch_shapes=[pltpu.VMEM((2, 2 * tm * TOK_SUB, LANES), F32), pltpu.SemaphoreType.DMA((2,))],
        compiler_params=_cparams(("arbitrary",)),
        name="moe_combine",
    )(pos2, pos2, y_rows, x2, route, mod)


def _rope_lane_tables(positions, rot_dim, head_w, n_rep):
    half = rot_dim // 2
    inv = jnp.power(jnp.float32(ROPE_THETA), -2.0 * jnp.arange(half, dtype=jnp.float32) / rot_dim)
    ang = positions.astype(jnp.float32)[..., None] * inv
    cos, sin = jnp.cos(ang), jnp.sin(ang)
    rest = head_w - rot_dim
    cos_h = jnp.concatenate([cos, cos, jnp.ones(cos.shape[:-1] + (rest,), F32)], axis=-1)
    sin_h = jnp.concatenate([-sin, sin, jnp.zeros(sin.shape[:-1] + (rest,), F32)], axis=-1)
    n = positions.shape[0] * positions.shape[1]
    return (jnp.tile(cos_h, (1, 1, n_rep)).reshape(n, head_w * n_rep),
            jnp.tile(sin_h, (1, 1, n_rep)).reshape(n, head_w * n_rep))


def _layer_weights(l, w_in, q_norm, k_norm, conv_w, conv_b, conv_ln_g, conv_ln_b, conv_pw_w, conv_pw_b,
                   pool_w, pool_b, pool_scale, sgu_ln_g, sgu_ln_b, sgu_w, sgu_b, out_norm, w_out, norm2,
                   w_rg, b_rg, w_re, b_re):
    G = GROUP_W
    D = D_MODEL
    pts = np.cumsum(IN_SPLITS)[:-1].tolist()
    wq, wk, wv, wiq, wik, wiw, wb, wc, wd = jnp.split(w_in[l], pts, axis=-1)
    wiw_p = jnp.pad(wiw, ((0, 0), (0, LANES - IDX_HEADS)))
    w_in_p = jnp.concatenate([wq, wk, wv, wiq, jnp.tile(wik, (1, IDX_HEADS)), wiw_p, wb, wc, wd], axis=-1).astype(BF16)
    npool = len(POOL_WINDOWS)
    pool_bd = jnp.zeros((G, G), F32)
    for g in range(npool):
        pool_bd = lax.dynamic_update_slice(pool_bd, pool_w[l, g], (g * POOL_CH, g * POOL_CH))
    sgu_bias = jnp.repeat(sgu_b[l].T, G // SGU_HEADS, axis=1)
    w_router = jnp.concatenate([w_rg[l], w_re[l].reshape(D, N_EXPERTS),
                                jnp.zeros((D, LANES - N_EXP_GROUPS - N_EXPERTS), F32)], axis=-1).astype(BF16)
    b_router = jnp.concatenate([b_rg[l], b_re[l].reshape(N_EXPERTS),
                                jnp.zeros((LANES - N_EXP_GROUPS - N_EXPERTS,), F32)]).reshape(1, LANES)
    r1 = lambda a: a.reshape(1, -1)
    return dict(
        w_in=w_in_p,
        q_norm=jnp.tile(q_norm[l], ATT_HEADS).reshape(1, G), k_norm=jnp.tile(k_norm[l], ATT_HEADS).reshape(1, G),
        conv_w=conv_w[l], conv_b=r1(conv_b[l]), conv_ln_g=r1(conv_ln_g[l]), conv_ln_b=r1(conv_ln_b[l]),
        conv_pw_w=conv_pw_w[l].astype(BF16), conv_pw_b=r1(conv_pw_b[l]),
        pool_w=pool_bd.astype(BF16), pool_b=r1(pool_b[l]), pool_scale=r1(pool_scale[l]),
        sgu_ln_g=r1(sgu_ln_g[l]), sgu_ln_b=r1(sgu_ln_b[l]), sgu_w=sgu_w[l], sgu_b=sgu_bias,
        out_norm=r1(out_norm[l]), w_out=w_out[l].astype(BF16), norm2=r1(norm2[l]),
        w_router=w_router, b_router=b_router,
    )


def _dispatch_tables(route, cnt, n_tokens):
    rows_blk = MOE_ROWS
    e = route[:, 0:2].astype(jnp.int32)
    rank = route[:, 2:4].astype(jnp.int32)
    counts = cnt[0, :N_EXPERTS].astype(jnp.int32)
    padded = (counts + rows_blk - 1) // rows_blk * rows_blk
    pad_end = jnp.cumsum(padded)
    pad_start = pad_end - padded
    pos = pad_start[e] + rank
    m = n_tokens * MOE_TOPK
    n_blocks = (m + N_EXPERTS * (rows_blk - 1) + rows_blk - 1) // rows_blk
    n_used = (pad_end[-1] // rows_blk).astype(jnp.int32).reshape(1)
    blk_row0 = jnp.arange(n_blocks, dtype=jnp.int32) * rows_blk
    block_expert = jnp.minimum(jnp.sum((pad_end[None, :] <= blk_row0[:, None]).astype(jnp.int32), axis=1),
                               N_EXPERTS - 1)
    pos2 = (pos * TOK_SUB).reshape(n_tokens // TM_COMB, TM_COMB, MOE_TOPK).transpose(0, 2, 1).reshape(
        -1, 1, MOE_TOPK * TM_COMB)
    tail = jnp.where(counts > 0, (pad_end - rows_blk) * TOK_SUB, -1).astype(jnp.int32)
    return block_expert, n_used, tail, pos2, n_blocks * rows_blk


def kernel(x, c, positions, w_ada, b_ada, norm1, w_in, q_norm, k_norm, conv_w, conv_b, conv_ln_g, conv_ln_b, conv_pw_w, conv_pw_b, pool_w, pool_b, pool_scale, sgu_ln_g, sgu_ln_b, sgu_w, sgu_b, out_norm, w_out, norm2, w_rg, b_rg, w_re, b_re, w_gate, w_up, w_down):
    B, S, D = x.shape
    N = B * S
    assert D == D_MODEL and S % TM_PROJ == 0 and S % TK_ATT == 0 and N % TM_COMB == 0
    depth = w_ada.shape[0]
    cos_a, sin_a = _rope_lane_tables(positions, ROPE_DIM, ATT_HEAD_DIM, ATT_HEADS)
    cos_i, sin_i = _rope_lane_tables(positions, IDX_ROPE_DIM, IDX_DIM, IDX_HEADS)
    c_pad = jnp.pad(c, ((0, (-B) % 8), (0, 0)))
    mod_all = _ada_call(c_pad, w_ada, b_ada)
    x2 = x.reshape(N, D)
    for l in range(depth):
        lw = _layer_weights(l, w_in, q_norm, k_norm, conv_w, conv_b, conv_ln_g, conv_ln_b, conv_pw_w, conv_pw_b,
                            pool_w, pool_b, pool_scale, sgu_ln_g, sgu_ln_b, sgu_w, sgu_b, out_norm, w_out, norm2,
                            w_rg, b_rg, w_re, b_re)
        mod = mod_all[l, :B].reshape(B, 6, D)
        q, k, v, iq, ik, iw, pb, pc, pd = _proj_call(
            x2, mod, norm1[l].reshape(1, D), lw["w_in"], lw["q_norm"], lw["k_norm"], cos_a, sin_a, cos_i, sin_i, S)
        oa = _dsa_call(q, k, v, iq, ik, iw, B, S)
        x_mid, h2, route, cnt = _mix_call(x2, oa, pb, pc, pd, mod, lw, S)
        block_expert, n_used, tail, pos2, n_rows = _dispatch_tables(route, cnt, N)
        xs = _dispatch_call(tail, n_used, pos2, h2, n_rows)
        y_rows = _expert_call(block_expert, n_used, xs, w_gate, w_up, w_down, l)
        x2 = _combine_call(pos2, y_rows, x_mid, route, mod, S)
    return x2.reshape(B, S, D)
```

```python
import functools

import numpy as np
import jax
import jax.numpy as jnp
from jax import lax
from jax.experimental import pallas as pl
from jax.experimental.pallas import tpu as pltpu

F32 = jnp.float32
BF16 = jnp.bfloat16
NEG_INF = float("-inf")

D_MODEL = 1024
DEPTH = 2
CHUNK = 64
N_MIXERS = 4
GROUP_W = D_MODEL // N_MIXERS
ATT_HEAD_DIM = 64
ATT_HEADS = GROUP_W // ATT_HEAD_DIM
ROPE_DIM = ATT_HEAD_DIM // 4
ROPE_THETA = 500000.0
IDX_HEADS = 4
IDX_DIM = 32
IDX_ROPE_DIM = IDX_DIM // 4
TOPK_MAX = 256
CONV_WIDTH = 31
CONV_GROUPS = 4
POOL_WINDOWS = (2, 4, 8, 16)
POOL_CH = GROUP_W // 4
SGU_CHUNK = 128
SGU_HEADS = 4
N_EXP_GROUPS = 4
EXP_PER_GROUP = 8
N_EXPERTS = N_EXP_GROUPS * EXP_PER_GROUP
EXPERT_FF = 512
MOE_TOPK = 2
EPS = 1e-6
IN_SPLITS = (GROUP_W, GROUP_W, GROUP_W, IDX_HEADS * IDX_DIM, IDX_DIM, IDX_HEADS, 2 * GROUP_W, GROUP_W, 2 * GROUP_W)

LANES = 128
HALO = 32
W_IN_COLS = 3 * GROUP_W + 3 * LANES + 5 * GROUP_W

TM_PROJ = 512
TQ_ATT = 128
TK_ATT = 512
TM_MIX = 256
MOE_ROWS = 256
TM_COMB = 256
BISECT_ITERS = 18
VMEM_LIMIT = 56 * 1024 * 1024


def _cparams(sem):
    return pltpu.CompilerParams(dimension_semantics=sem, vmem_limit_bytes=VMEM_LIMIT)


def _lane_iota(shape):
    return lax.broadcasted_iota(jnp.int32, shape, len(shape) - 1)


def _seg_mean(y, width):
    shift = int(np.log2(width))
    grp = _lane_iota(y.shape) >> shift
    out = jnp.zeros_like(y)
    for g in range(y.shape[-1] // width):
        msk = grp == g
        s = jnp.sum(jnp.where(msk, y, 0.0), axis=-1, keepdims=True)
        out = jnp.where(msk, s, out)
    return out * (1.0 / width)


def _rope(x, cos_f, sin_s, head_w, half):
    c = x.shape[-1]
    lane = _lane_iota(x.shape) & (head_w - 1)
    partner = jnp.where(lane < half, pltpu.roll(x, c - half, 1), pltpu.roll(x, half, 1))
    return x * cos_f + partner * sin_s


def _silu(x):
    return x * jax.nn.sigmoid(x)


def _ada_kernel(c_ref, w_ref, b_ref, o_ref):
    ca = _silu(c_ref[...])
    o_ref[0] = jnp.dot(ca.astype(BF16), w_ref[0].astype(BF16), preferred_element_type=F32) + b_ref[0]


def _ada_call(c_pad, w_ada, b_ada):
    L, D, D6 = w_ada.shape
    rows = c_pad.shape[0]
    tn = D
    return pl.pallas_call(
        _ada_kernel,
        grid=(L, D6 // tn),
        in_specs=[
            pl.BlockSpec((rows, D), lambda l, j: (0, 0)),
            pl.BlockSpec((1, D, tn), lambda l, j: (l, 0, j)),
            pl.BlockSpec((1, 1, tn), lambda l, j: (l, 0, j)),
        ],
        out_specs=pl.BlockSpec((1, rows, tn), lambda l, j: (l, 0, j)),
        out_shape=jax.ShapeDtypeStruct((L, rows, D6), F32),
        compiler_params=_cparams(("arbitrary", "arbitrary")),
        name="ada_mod",
    )(c_pad, w_ada, b_ada.reshape(L, 1, D6))


def _proj_kernel(x_ref, mod_ref, n1_ref, w_ref, qn_ref, kn_ref, cosa_ref, sina_ref, cosi_ref, sini_ref,
                 q_ref, k_ref, v_ref, iq_ref, ik_ref, iw_ref, pb_ref, pc_ref, pd_ref):
    x = x_ref[...]
    ms = jnp.mean(x * x, axis=-1, keepdims=True)
    h = (x * lax.rsqrt(ms + EPS)) * n1_ref[...]
    h = h * (1.0 + mod_ref[0, 1:2, :]) + mod_ref[0, 0:1, :]
    proj = jnp.dot(h.astype(BF16), w_ref[...], preferred_element_type=F32)
    G = GROUP_W
    cos_a, sin_a = cosa_ref[...], sina_ref[...]
    cos_i, sin_i = cosi_ref[...], sini_ref[...]

    def qk(t, g_ref):
        tn = (t * lax.rsqrt(_seg_mean(t * t, ATT_HEAD_DIM) + EPS)) * g_ref[...]
        return _rope(tn, cos_a, sin_a, ATT_HEAD_DIM, ROPE_DIM // 2)

    q_ref[...] = (qk(proj[:, 0:G], qn_ref) * (ATT_HEAD_DIM ** -0.5)).T.astype(BF16)
    k_ref[...] = qk(proj[:, G:2 * G], kn_ref).astype(BF16)
    v_ref[0] = proj[:, 2 * G:3 * G].T.astype(BF16)
    o = 3 * G
    iq_ref[...] = _rope(proj[:, o:o + LANES], cos_i, sin_i, IDX_DIM, IDX_ROPE_DIM // 2).T.astype(BF16)
    ik_ref[...] = _rope(proj[:, o + LANES:o + 2 * LANES], cos_i, sin_i, IDX_DIM, IDX_ROPE_DIM // 2).astype(BF16)
    iw_ref[...] = (proj[:, o + 2 * LANES:o + 3 * LANES] * (IDX_HEADS ** -0.5)).T[0:8, :]
    o += 3 * LANES
    pb_ref[...] = proj[:, o:o + 2 * G]
    pc_ref[...] = proj[:, o + 2 * G:o + 3 * G]
    pd_ref[...] = proj[:, o + 3 * G:o + 5 * G]


def _proj_call(x2, mod, n1, w_in_p, qn_t, kn_t, cos_a, sin_a, cos_i, sin_i, seq):
    N, D = x2.shape
    tm = TM_PROJ
    tpb = seq // tm
    G = GROUP_W
    assert tm == TK_ATT
    row = lambda i: (i, 0)
    col = lambda i: (0, i)
    fixed = lambda i: (0, 0)
    sds = jax.ShapeDtypeStruct
    out_specs = [
        pl.BlockSpec((G, tm), col),
        pl.BlockSpec((tm, G), row),
        pl.BlockSpec((1, G, tm), lambda i: (i, 0, 0)),
        pl.BlockSpec((LANES, tm), col),
        pl.BlockSpec((tm, LANES), row),
        pl.BlockSpec((8, tm), col),
        pl.BlockSpec((tm, 2 * G), row),
        pl.BlockSpec((tm, G), row),
        pl.BlockSpec((tm, 2 * G), row),
    ]
    out_shape = [sds((G, N), BF16), sds((N, G), BF16), sds((N // tm, G, tm), BF16), sds((LANES, N), BF16),
                 sds((N, LANES), BF16), sds((8, N), F32), sds((N, 2 * G), F32), sds((N, G), F32), sds((N, 2 * G), F32)]
    return pl.pallas_call(
        _proj_kernel,
        grid=(N // tm,),
        in_specs=[
            pl.BlockSpec((tm, D), row),
            pl.BlockSpec((1, 6, D), lambda i: (i // tpb, 0, 0)),
            pl.BlockSpec((1, D), fixed),
            pl.BlockSpec((D, W_IN_COLS), fixed),
            pl.BlockSpec((1, G), fixed),
            pl.BlockSpec((1, G), fixed),
            pl.BlockSpec((tm, G), row),
            pl.BlockSpec((tm, G), row),
            pl.BlockSpec((tm, LANES), row),
            pl.BlockSpec((tm, LANES), row),
        ],
        out_specs=out_specs,
        out_shape=out_shape,
        compiler_params=_cparams(("parallel",)),
        name="norm_in_proj",
    )(x2, mod, n1, w_in_p, qn_t, kn_t, cos_a, sin_a, cos_i, sin_i)


def _pair_rhs(xt, head_rows, h0):
    head = lax.broadcasted_iota(jnp.int32, xt.shape, 0) >> int(np.log2(head_rows))
    zero = jnp.zeros_like(xt)
    return jnp.concatenate([jnp.where(head == h0, xt, zero), jnp.where(head == h0 + 1, xt, zero)], axis=1)


def _dsa_kernel(q_ref, k_ref, v_ref, iq_ref, ik_ref, iw_ref, ltri_ref, o_ref, sc_ref, lge_ref, lgo_ref, *, topk):
    tq, tk = TQ_ATT, TK_ATT
    i = pl.program_id(1)
    q0 = i * tq
    n_kv = (q0 + tq + tk - 1) // tk
    kf = float(topk)

    q_pos = _lane_iota((1, tq)) + q0
    key_end = ((q_pos >> 6) + 1) << 6
    key_i = lax.broadcasted_iota(jnp.int32, (tk, tq), 0)

    def fold8(x, op):
        parts = [x[r * 8:(r + 1) * 8, :] for r in range(tk // 8)]
        while len(parts) > 1:
            parts = [op(parts[a], parts[a + 1]) for a in range(0, len(parts), 2)]
        return parts[0]

    iqt = iq_ref[...]
    iq_pairs = [_pair_rhs(iqt, IDX_DIM, h0) for h0 in range(0, IDX_HEADS, 2)]
    iw_h = [iw_ref[h:h + 1, :] for h in range(IDX_HEADS)]

    def score_body(kc, carry):
        hi8, lo8 = carry
        ikc = ik_ref[pl.ds(pl.multiple_of(kc * tk, tk), tk), :]
        s = jnp.zeros((tk, tq), F32)
        for pi, rhs in enumerate(iq_pairs):
            d2 = jnp.dot(ikc, rhs, preferred_element_type=F32)
            for j in range(2):
                d = d2[:, j * tq:(j + 1) * tq]
                s = s + jnp.maximum(d * (IDX_DIM ** -0.5), 0.0) * iw_h[2 * pi + j]
        adm = key_i + kc * tk < key_end
        s_top = jnp.where(adm, s, NEG_INF)
        sc_ref[kc] = s_top
        return (jnp.maximum(hi8, fold8(s_top, jnp.maximum)),
                jnp.minimum(lo8, fold8(jnp.where(adm, s, jnp.inf), jnp.minimum)))

    hi8, lo8 = lax.fori_loop(0, n_kv, score_body,
                             (jnp.full((8, tq), NEG_INF, F32), jnp.full((8, tq), jnp.inf, F32)))
    col_max = jnp.max(hi8, axis=0, keepdims=True)
    col_min = jnp.min(lo8, axis=0, keepdims=True)

    def reduce_chunks(fns, init, combine, fold):
        def body(kc, parts):
            blk = sc_ref[kc]
            return tuple(combine(p, fold8(fn(blk), combine)) for p, fn in zip(parts, fns))
        parts = lax.fori_loop(0, n_kv, body, tuple(jnp.full((8, tq), init, F32) for _ in fns))
        return [fold(p, axis=0, keepdims=True) for p in parts]

    def count(*inds):
        return reduce_chunks(inds, 0.0, jnp.add, jnp.sum)

    def col_maximum(val):
        return reduce_chunks((val,), NEG_INF, jnp.maximum, jnp.max)[0]

    small = key_end <= topk

    @pl.when(q0 + tq > topk)
    def _select():
        def bis_body(_, c):
            lo, hi = c
            mid = jnp.where(hi == jnp.inf, col_max, lo + (hi - lo) * 0.5)
            ge = count(lambda b: jnp.where(b >= mid, 1.0, 0.0))[0] >= kf
            return jnp.where(ge, mid, lo), jnp.where(ge, hi, mid)

        lo, hi = lax.fori_loop(0, BISECT_ITERS, bis_body, (col_min, jnp.full((1, tq), jnp.inf, F32)))

        def sd_cond(c):
            return c[0] > 0.0

        def sd_body(c):
            _, hi, thr, done = c
            cand = col_maximum(lambda b: jnp.where(b < hi, b, NEG_INF))
            ok = count(lambda b: jnp.where(b >= cand, 1.0, 0.0))[0] >= kf
            thr = jnp.where(done > 0.0, thr, cand)
            hi = jnp.where(done > 0.0, hi, cand)
            done = jnp.where(ok, 1.0, done)
            return jnp.sum(1.0 - done), hi, thr, done

        done0 = jnp.where(small, 1.0, 0.0)
        n0 = jnp.sum(1.0 - done0)
        _, _, thr, _ = lax.while_loop(sd_cond, sd_body, (n0, hi, jnp.full((1, tq), NEG_INF, F32), done0))
        thr = jnp.where(small, NEG_INF, thr)

        n_above, n_tied = count(lambda b: jnp.where(b > thr, 1.0, 0.0), lambda b: jnp.where(b == thr, 1.0, 0.0))
        need = kf - n_above
        excess = jnp.sum(jnp.where(jnp.where(small, 0.0, n_tied) > need, 1.0, 0.0))

        @pl.when(excess <= 0.0)
        def _keep_all_ties():
            def bias_body(kc, carry):
                blk = sc_ref[kc]
                sc_ref[kc] = jnp.where(blk == NEG_INF, NEG_INF, jnp.where(blk >= thr, 0.0, NEG_INF))
                return carry
            lax.fori_loop(0, n_kv, bias_body, 0)

        @pl.when(excess > 0.0)
        def _rank_ties():
            half = tk // 2
            ltri_top = ltri_ref[0:half, 0:half]
            ltri_bot = ltri_ref[half:tk, :]

            def bias_body(kc, seen):
                blk = sc_ref[kc]
                tied = jnp.where(blk == thr, 1.0, 0.0)
                tied16 = tied.astype(BF16)
                rank = jnp.concatenate([jnp.dot(ltri_top, tied16[0:half, :], preferred_element_type=F32),
                                        jnp.dot(ltri_bot, tied16, preferred_element_type=F32)], axis=0) + seen
                tie = jnp.where(blk == thr, jnp.where(rank <= need, 0.0, NEG_INF), NEG_INF)
                bias = jnp.where(blk > thr, 0.0, tie)
                sc_ref[kc] = jnp.where(blk == NEG_INF, NEG_INF, bias)
                return seen + jnp.sum(fold8(tied, jnp.add), axis=0, keepdims=True)

            lax.fori_loop(0, n_kv, bias_body, jnp.zeros((1, tq), F32))

    @pl.when(q0 + tq <= topk)
    def _all():
        def bias_body(kc, carry):
            sc_ref[kc] = jnp.where(sc_ref[kc] == NEG_INF, NEG_INF, 0.0)
            return carry
        lax.fori_loop(0, n_kv, bias_body, 0)

    qt = q_ref[...]
    q_pairs = [_pair_rhs(qt, ATT_HEAD_DIM, h0) for h0 in range(0, ATT_HEADS, 2)]
    dh = ATT_HEAD_DIM

    def store_logits(buf, kc):
        kblk = k_ref[pl.ds(pl.multiple_of(kc * tk, tk), tk), :]
        bias = sc_ref[kc]
        for pi, rhs in enumerate(q_pairs):
            s2 = jnp.dot(kblk, rhs, preferred_element_type=F32)
            for j in range(2):
                buf[2 * pi + j] = s2[:, j * tq:(j + 1) * tq] + bias

    def absorb(buf, kc, state):
        ms, ls, accs = state
        vt = v_ref[kc]
        ms_n, ls_n, accs_n = [], [], []
        for h in range(ATT_HEADS):
            s = buf[h]
            m_new = jnp.maximum(ms[h], jnp.max(fold8(s, jnp.maximum), axis=0, keepdims=True))
            m_safe = jnp.where(m_new == NEG_INF, 0.0, m_new)
            alpha = jnp.exp(ms[h] - m_safe)
            p = jnp.exp(s - m_safe)
            ls_n.append(alpha * ls[h] + fold8(p, jnp.add))
            pv = jnp.dot(vt[h * dh:(h + 1) * dh, :], p.astype(BF16), preferred_element_type=F32)
            accs_n.append(alpha * accs[h] + pv)
            ms_n.append(m_new)
        return tuple(ms_n), tuple(ls_n), tuple(accs_n)

    def pair_body(jj, state):
        s = 2 * jj + 1
        store_logits(lgo_ref, s)
        state = absorb(lge_ref, s - 1, state)
        store_logits(lge_ref, s + 1)
        return absorb(lgo_ref, s, state)

    state = (tuple(jnp.full((1, tq), NEG_INF, F32) for _ in range(ATT_HEADS)),
             tuple(jnp.zeros((8, tq), F32) for _ in range(ATT_HEADS)),
             tuple(jnp.zeros((dh, tq), F32) for _ in range(ATT_HEADS)))
    store_logits(lge_ref, 0)
    n_pairs = (n_kv - 1) // 2
    state = lax.fori_loop(0, n_pairs, pair_body, state)
    last_even = 2 * n_pairs

    def tail_two(state):
        store_logits(lgo_ref, last_even + 1)
        return absorb(lgo_ref, last_even + 1, absorb(lge_ref, last_even, state))

    def tail_one(state):
        return absorb(lge_ref, last_even, state)

    _, ls, accs = lax.cond(n_kv - 1 - last_even > 0, tail_two, tail_one, state)
    out_t = jnp.concatenate([accs[h] / jnp.sum(ls[h], axis=0, keepdims=True) for h in range(ATT_HEADS)], axis=0)
    o_ref[...] = out_t.T


def _dsa_call(qt, k, vt, iqt, ik, iwt, batch, seq):
    N, G = k.shape
    tq = TQ_ATT
    nq = seq // tq
    nkc = seq // TK_ATT
    topk = min(TOPK_MAX, seq // 4)
    qcol = lambda b, i: (0, b * nq + i)
    brow = lambda b, i: (b, 0)
    return pl.pallas_call(
        functools.partial(_dsa_kernel, topk=topk),
        grid=(batch, nq),
        in_specs=[
            pl.BlockSpec((G, tq), qcol),
            pl.BlockSpec((seq, G), brow),
            pl.BlockSpec((nkc, G, TK_ATT), lambda b, i: (b, 0, 0)),
            pl.BlockSpec((LANES, tq), qcol),
            pl.BlockSpec((seq, LANES), brow),
            pl.BlockSpec((8, tq), qcol),
            pl.BlockSpec((TK_ATT, TK_ATT), lambda b, i: (0, 0)),
        ],
        out_specs=pl.BlockSpec((tq, G), lambda b, i: (b * nq + i, 0)),
        out_shape=jax.ShapeDtypeStruct((N, G), F32),
        scratch_shapes=[pltpu.VMEM((nkc, TK_ATT, tq), F32), pltpu.VMEM((ATT_HEADS, TK_ATT, tq), F32),
                        pltpu.VMEM((ATT_HEADS, TK_ATT, tq), F32)],
        compiler_params=_cparams(("parallel", "arbitrary")),
        name="dsa_attention",
    )(qt, k, vt, iqt, ik, iwt, jnp.tril(jnp.ones((TK_ATT, TK_ATT), BF16)))


def _mix_kernel(x_ref, oa_ref, pb_ref, pbh_ref, pc_ref, pch_ref, pd_ref, mod_ref,
                cw_ref, cb_ref, clg_ref, clb_ref, cpw_ref, cpb_ref,
                pw_ref, pbias_ref, ps_ref, slg_ref, slb_ref, sw_ref, sb_ref,
                on_ref, wo_ref, n2_ref, wr_ref, br_ref,
                xo_ref, h2_ref, route_ref, cnt_ref,
                ypad_ref, ppad_ref, s2_ref, s4_ref, s8_ref, run_ref, cph_ref, *, tiles_per_batch):
    tm = TM_MIX
    G = GROUP_W
    i = pl.program_id(0)
    t_in_b = i % tiles_per_batch
    first = t_in_b == 0
    lane_g = _lane_iota((tm, G))

    def glu(pb):
        return pb[:, 0:G] * jax.nn.sigmoid(pb[:, G:2 * G])

    ypad_ref[0:HALO, :] = jnp.where(first, 0.0, glu(pbh_ref[...]))
    ypad_ref[HALO:HALO + tm, :] = glu(pb_ref[...])
    acc = jnp.zeros((tm, G), F32)
    first_off = HALO - (CONV_WIDTH - 1)
    for phase in range(8):
        offs = [o for o in range(first_off, HALO + 1) if o % 8 == phase]
        if not offs:
            continue
        span = offs[-1] - offs[0] + tm
        cph_ref[phase, 0:span, :] = ypad_ref[offs[0]:offs[0] + span, :]
        for o in offs:
            acc = acc + cw_ref[o - first_off:o - first_off + 1, :] * cph_ref[phase, o - offs[0]:o - offs[0] + tm, :]
    y = acc + cb_ref[...]
    gw = G // CONV_GROUPS
    mu = _seg_mean(y, gw)
    yc = y - mu
    var = _seg_mean(yc * yc, gw)
    y = (yc * lax.rsqrt(var + EPS)) * clg_ref[...] + clb_ref[...]
    o_b = jnp.dot(_silu(y).astype(BF16), cpw_ref[...], preferred_element_type=F32) + cpb_ref[...]

    p = pc_ref[...]
    ppad_ref[0:HALO, :] = jnp.where(first, 0.0, pch_ref[...])
    ppad_ref[HALO:HALO + tm, :] = p
    n8 = tm + HALO - 8
    s2_ref[8:8 + n8, :] = ppad_ref[8:8 + n8, :] + ppad_ref[7:7 + n8, :]
    n16 = tm + HALO - 16
    s4_ref[16:16 + n16, :] = s2_ref[16:16 + n16, :] + s2_ref[14:14 + n16, :]
    n24 = tm + HALO - 24
    s8_ref[24:24 + n24, :] = s4_ref[24:24 + n24, :] + s4_ref[20:20 + n24, :]
    s2 = s2_ref[HALO:HALO + tm, :]
    s4 = s4_ref[HALO:HALO + tm, :]
    s8 = s8_ref[HALO:HALO + tm, :]
    s16 = s8 + s8_ref[HALO - 8:HALO - 8 + tm, :]
    pgrp = lane_g >> 6
    wsum = jnp.where(pgrp == 0, s2, jnp.where(pgrp == 1, s4, jnp.where(pgrp == 2, s8, s16)))
    wlen = jnp.where(pgrp == 0, 2.0, jnp.where(pgrp == 1, 4.0, jnp.where(pgrp == 2, 8.0, 16.0)))
    tpos = (lax.broadcasted_iota(jnp.int32, (tm, G), 0) + t_in_b * tm + 1).astype(F32)
    pooled = wsum / jnp.minimum(tpos, wlen) - p
    o_c = (jnp.dot(pooled.astype(BF16), pw_ref[...], preferred_element_type=F32) + pbias_ref[...]) * ps_ref[...]

    pd = pd_ref[...]
    u, v = pd[:, 0:G], pd[:, G:2 * G]
    mu = jnp.mean(v, axis=-1, keepdims=True)
    vc = v - mu
    var = jnp.mean(vc * vc, axis=-1, keepdims=True)
    vn = ((vc * lax.rsqrt(var + EPS)) * slg_ref[...] + slb_ref[...]).astype(BF16)
    r_i = lax.broadcasted_iota(jnp.int32, (SGU_CHUNK, SGU_CHUNK), 0)
    c_i = lax.broadcasted_iota(jnp.int32, (SGU_CHUNK, SGU_CHUNK), 1)
    w_heads = [jnp.where(r_i >= c_i, sw_ref[h], 0.0).astype(BF16) for h in range(SGU_HEADS)]
    lane_c = _lane_iota((SGU_CHUNK, G)) >> 6
    mixed = []
    for n in range(tm // SGU_CHUNK):
        vch = vn[n * SGU_CHUNK:(n + 1) * SGU_CHUNK, :]
        mx = jnp.zeros((SGU_CHUNK, G), F32)
        for h in range(SGU_HEADS):
            mx = jnp.where(lane_c == h, jnp.dot(w_heads[h], vch, preferred_element_type=F32), mx)
        mixed.append(mx + sb_ref[...])
    o_d = u * jnp.concatenate(mixed, axis=0)

    proj = jnp.zeros((tm, D_MODEL), F32)
    for g, piece in enumerate((oa_ref[...], o_b, o_c, o_d)):
        ms = jnp.mean(piece * piece, axis=-1, keepdims=True)
        pn = (piece * lax.rsqrt(ms + EPS)) * on_ref[:, g * G:(g + 1) * G]
        proj = proj + jnp.dot(pn.astype(BF16), wo_ref[g * G:(g + 1) * G, :], preferred_element_type=F32)
    x_new = x_ref[...] + mod_ref[0, 2:3, :] * proj
    xo_ref[...] = x_new

    ms = jnp.mean(x_new * x_new, axis=-1, keepdims=True)
    h2 = (x_new * lax.rsqrt(ms + EPS)) * n2_ref[...]
    h2 = h2 * (1.0 + mod_ref[0, 4:5, :]) + mod_ref[0, 3:4, :]
    _to_token_major(h2_ref, h2)
    logits = jnp.dot(h2.astype(BF16), wr_ref[...], preferred_element_type=F32) + br_ref[...]
    lane = _lane_iota((tm, LANES))
    lane_f = lane.astype(F32)
    big = float(LANES)
    glog = jnp.where(lane < N_EXP_GROUPS, logits, NEG_INF)
    gmax = jnp.max(glog, axis=-1, keepdims=True)
    p_sel = 1.0 / jnp.sum(jnp.exp(glog - gmax), axis=-1, keepdims=True)
    g_idx = jnp.min(jnp.where(glog == gmax, lane_f, big), axis=-1, keepdims=True)
    e_lane = lane - N_EXP_GROUPS
    elog = jnp.where((e_lane >> 3).astype(F32) == g_idx, logits, NEG_INF)
    top1 = jnp.max(elog, axis=-1, keepdims=True)
    j1 = jnp.min(jnp.where(elog == top1, lane_f, big), axis=-1, keepdims=True)
    elog2 = jnp.where(lane_f == j1, NEG_INF, elog)
    top2 = jnp.max(elog2, axis=-1, keepdims=True)
    j2 = jnp.min(jnp.where(elog2 == top2, lane_f, big), axis=-1, keepdims=True)
    e2w = jnp.exp(top2 - top1)
    gate1 = p_sel * (1.0 / (1.0 + e2w))
    gate2 = p_sel * (e2w / (1.0 + e2w))
    e1 = j1 - float(N_EXP_GROUPS)
    e2 = j2 - float(N_EXP_GROUPS)

    @pl.when(i == 0)
    def _init():
        run_ref[...] = jnp.zeros_like(run_ref)

    onehot = jnp.where(jnp.logical_or(lane_f == e1, lane_f == e2), 1.0, 0.0)
    rr = lax.broadcasted_iota(jnp.int32, (tm, tm), 0)
    cc = lax.broadcasted_iota(jnp.int32, (tm, tm), 1)
    before = jnp.where(rr > cc, 1.0, 0.0).astype(BF16)
    prior = jnp.dot(before, onehot.astype(BF16), preferred_element_type=F32) + run_ref[0:1, :]
    rank1 = jnp.sum(jnp.where(lane_f == e1, prior, 0.0), axis=-1, keepdims=True)
    rank2 = jnp.sum(jnp.where(lane_f == e2, prior, 0.0), axis=-1, keepdims=True)
    run_new = run_ref[0:1, :] + jnp.sum(onehot, axis=0, keepdims=True)
    run_ref[...] = jnp.broadcast_to(run_new, run_ref.shape)
    cnt_ref[...] = jnp.broadcast_to(run_new, cnt_ref.shape)
    route = jnp.where(lane == 0, e1, jnp.where(lane == 1, e2, jnp.where(lane == 2, rank1, jnp.where(
        lane == 3, rank2, jnp.where(lane == 4, gate1, jnp.where(lane == 5, gate2, 0.0))))))
    route_ref[...] = route


def _mix_call(x2, oa, pb, pc, pd, mod, lw, seq):
    N, D = x2.shape
    tm = TM_MIX
    tpb = seq // tm
    G = GROUP_W
    row = lambda i: (i, 0)
    halo = lambda i: (jnp.maximum(i * (tm // HALO) - 1, 0), 0)
    fixed2 = lambda i: (0, 0)
    fixed3 = lambda i: (0, 0, 0)
    params = [lw["conv_w"], lw["conv_b"], lw["conv_ln_g"], lw["conv_ln_b"], lw["conv_pw_w"], lw["conv_pw_b"],
              lw["pool_w"], lw["pool_b"], lw["pool_scale"], lw["sgu_ln_g"], lw["sgu_ln_b"], lw["sgu_w"], lw["sgu_b"],
              lw["out_norm"], lw["w_out"], lw["norm2"], lw["w_router"], lw["b_router"]]
    param_specs = [pl.BlockSpec(p.shape, fixed3 if p.ndim == 3 else fixed2) for p in params]
    return pl.pallas_call(
        functools.partial(_mix_kernel, tiles_per_batch=tpb),
        grid=(N // tm,),
        in_specs=[
            pl.BlockSpec((tm, D), row),
            pl.BlockSpec((tm, G), row),
            pl.BlockSpec((tm, 2 * G), row),
            pl.BlockSpec((HALO, 2 * G), halo),
            pl.BlockSpec((tm, G), row),
            pl.BlockSpec((HALO, G), halo),
            pl.BlockSpec((tm, 2 * G), row),
            pl.BlockSpec((1, 6, D), lambda i: (i // tpb, 0, 0)),
        ] + param_specs,
        out_specs=[
            pl.BlockSpec((tm, D), row),
            pl.BlockSpec((tm * TOK_SUB, LANES), row),
            pl.BlockSpec((tm, LANES), row),
            pl.BlockSpec((8, LANES), fixed2),
        ],
        out_shape=[
            jax.ShapeDtypeStruct((N, D), F32),
            jax.ShapeDtypeStruct((N * TOK_SUB, LANES), F32),
            jax.ShapeDtypeStruct((N, LANES), F32),
            jax.ShapeDtypeStruct((8, LANES), F32),
        ],
        scratch_shapes=[pltpu.VMEM((tm + HALO, G), F32) for _ in range(5)] + [
            pltpu.VMEM((8, LANES), F32), pltpu.VMEM((8, tm + HALO, G), F32)],
        compiler_params=_cparams(("arbitrary",)),
        name="mixers_out_router",
    )(x2, oa, pb, pb, pc, pc, pd, mod, *params)


TOK_SUB = D_MODEL // LANES
ROW_DMA_UNROLL = 8


def _to_token_major(ref, x):
    n = x.shape[0]
    for j in range(TOK_SUB):
        ref[pl.ds(j, n, stride=TOK_SUB), :] = x[:, j * LANES:(j + 1) * LANES]


def _from_token_major(ref, tok0, n):
    return jnp.concatenate([ref[pl.ds(tok0 * TOK_SUB + j, n, stride=TOK_SUB), :] for j in range(TOK_SUB)], axis=1)


def _token_copy(src_hbm, row8, dst_ref, slot, r, sem):
    src = src_hbm.at[pl.ds(pl.multiple_of(row8, TOK_SUB), TOK_SUB), :]
    dst = dst_ref.at[slot, pl.ds(pl.multiple_of(r * TOK_SUB, TOK_SUB), TOK_SUB), :]
    return pltpu.make_async_copy(src, dst, sem.at[slot])


def _start_tokens(idx_ref, src_hbm, dst_ref, slot, sem, n):
    def body(g, carry):
        for j in range(ROW_DMA_UNROLL):
            r = g * ROW_DMA_UNROLL + j
            _token_copy(src_hbm, idx_ref[0, 0, r], dst_ref, slot, r, sem).start(priority=j % 2)
        return carry
    lax.fori_loop(0, n // ROW_DMA_UNROLL, body, 0)


def _wait_tokens(src_hbm, dst_ref, slot, sem, n):
    pltpu.make_async_copy(src_hbm.at[pl.ds(0, n * TOK_SUB), :], dst_ref.at[slot], sem.at[slot]).wait()


def _dispatch_kernel(tail_ref, nu_ref, pos_ref, h_ref, xs_hbm, zbuf, zsem, sem):
    i = pl.program_id(0)
    tm = TM_COMB
    blk = MOE_ROWS * TOK_SUB
    n_blocks = xs_hbm.shape[0] // blk

    def zero_block(row0):
        return pltpu.make_async_copy(zbuf, xs_hbm.at[pl.ds(pl.multiple_of(row0, TOK_SUB), blk), :], zsem.at[0])

    @pl.when(i == 0)
    def _zero_fill():
        zbuf[...] = jnp.zeros_like(zbuf)
        n_used = nu_ref[0]
        for e in range(N_EXPERTS):
            @pl.when(tail_ref[e] >= 0)
            def _():
                zero_block(tail_ref[e]).start()
        lax.fori_loop(n_used, n_blocks, lambda b, c: (zero_block(b * blk).start(), c)[1], 0)
        for e in range(N_EXPERTS):
            @pl.when(tail_ref[e] >= 0)
            def _():
                zero_block(tail_ref[e]).wait()
        lax.fori_loop(n_used, n_blocks, lambda b, c: (zero_block(b * blk).wait(), c)[1], 0)

    def body(g, carry):
        for j in range(ROW_DMA_UNROLL):
            r = g * ROW_DMA_UNROLL + j
            t = jnp.where(r < tm, r, r - tm)
            src = h_ref.at[pl.ds(pl.multiple_of(t * TOK_SUB, TOK_SUB), TOK_SUB), :]
            dst = xs_hbm.at[pl.ds(pl.multiple_of(pos_ref[0, 0, r], TOK_SUB), TOK_SUB), :]
            pltpu.make_async_copy(src, dst, sem.at[0]).start(priority=j % 2)
        return carry
    lax.fori_loop(0, 2 * tm // ROW_DMA_UNROLL, body, 0)
    for _ in range(MOE_TOPK):
        pltpu.make_async_copy(h_ref, xs_hbm.at[pl.ds(0, tm * TOK_SUB), :], sem.at[0]).wait()


def _dispatch_call(tail, n_used, pos2, h2, n_rows):
    nt = pos2.shape[0]
    tm = TM_COMB
    grid_spec = pltpu.PrefetchScalarGridSpec(
        num_scalar_prefetch=2,
        grid=(nt,),
        in_specs=[
            pl.BlockSpec((1, 1, 2 * tm), lambda i, *_: (i, 0, 0), memory_space=pltpu.SMEM),
            pl.BlockSpec((tm * TOK_SUB, LANES), lambda i, *_: (i, 0)),
        ],
        out_specs=pl.BlockSpec(memory_space=pl.ANY),
        scratch_shapes=[
            pltpu.VMEM((MOE_ROWS * TOK_SUB, LANES), F32),
            pltpu.SemaphoreType.DMA((1,)),
            pltpu.SemaphoreType.DMA((1,)),
        ],
    )
    return pl.pallas_call(
        _dispatch_kernel,
        grid_spec=grid_spec,
        out_shape=jax.ShapeDtypeStruct((n_rows * TOK_SUB, LANES), F32),
        compiler_params=_cparams(("arbitrary",)),
        name="moe_dispatch",
    )(tail, n_used, pos2, h2)


def _expert_kernel(be_ref, nu_ref, x_ref, wg_ref, wu_ref, wd_ref, y_ref, wg_bf, wu_bf, wd_bf):
    i = pl.program_id(0)
    n_used = nu_ref[0]
    rows = MOE_ROWS

    changed = jnp.logical_or(i == 0, be_ref[i] != be_ref[jnp.maximum(i - 1, 0)])

    @pl.when(jnp.logical_and(changed, i < n_used))
    def _cast():
        wg_bf[...] = wg_ref[0, 0].astype(BF16)
        wu_bf[...] = wu_ref[0, 0].astype(BF16)
        wd_bf[...] = wd_ref[0, 0].astype(BF16)

    @pl.when(i < n_used)
    def _compute():
        xb = _from_token_major(x_ref, 0, rows).astype(BF16)
        g = jnp.dot(xb, wg_bf[...], preferred_element_type=F32)
        u = jnp.dot(xb, wu_bf[...], preferred_element_type=F32)
        hid = (_silu(g) * u).astype(BF16)
        _to_token_major(y_ref, jnp.dot(hid, wd_bf[...], preferred_element_type=F32))

    @pl.when(i >= n_used)
    def _skip():
        y_ref[...] = jnp.zeros_like(y_ref)


def _expert_call(block_expert, n_used, xs, w_gate, w_up, w_down, layer):
    rows = MOE_ROWS
    n_blocks = xs.shape[0] // (rows * TOK_SUB)
    D = D_MODEL
    FF = w_gate.shape[-1]
    grid_spec = pltpu.PrefetchScalarGridSpec(
        num_scalar_prefetch=2,
        grid=(n_blocks,),
        in_specs=[
            pl.BlockSpec((rows * TOK_SUB, LANES), lambda i, be, nu: (jnp.minimum(i, jnp.maximum(nu[0] - 1, 0)), 0)),
            pl.BlockSpec((1, 1, D, FF), lambda i, be, nu: (layer, be[i], 0, 0)),
            pl.BlockSpec((1, 1, D, FF), lambda i, be, nu: (layer, be[i], 0, 0)),
            pl.BlockSpec((1, 1, FF, D), lambda i, be, nu: (layer, be[i], 0, 0)),
        ],
        out_specs=pl.BlockSpec((rows * TOK_SUB, LANES), lambda i, be, nu: (i, 0)),
        scratch_shapes=[
            pltpu.VMEM((D, FF), BF16),
            pltpu.VMEM((D, FF), BF16),
            pltpu.VMEM((FF, D), BF16),
        ],
    )
    return pl.pallas_call(
        _expert_kernel,
        grid_spec=grid_spec,
        out_shape=jax.ShapeDtypeStruct((n_blocks * rows * TOK_SUB, LANES), F32),
        compiler_params=_cparams(("arbitrary",)),
        name="expert_mlp",
    )(block_expert, n_used, xs, w_gate, w_up, w_down)


def _combine_kernel(pos_ref, posn_ref, y_hbm, x_ref, route_ref, mod_ref, o_ref, ybuf, sem):
    i = pl.program_id(0)
    n = pl.num_programs(0)
    slot = i % 2
    tm = TM_COMB

    @pl.when(i == 0)
    def _first():
        _start_tokens(pos_ref, y_hbm, ybuf, 0, sem, 2 * tm)

    @pl.when(i + 1 < n)
    def _prefetch():
        _start_tokens(posn_ref, y_hbm, ybuf, 1 - slot, sem, 2 * tm)

    _wait_tokens(y_hbm, ybuf, slot, sem, 2 * tm)
    route = route_ref[...]
    yb = ybuf.at[slot]
    y = route[:, 4:5] * _from_token_major(yb, 0, tm) + route[:, 5:6] * _from_token_major(yb, tm, tm)
    o_ref[...] = x_ref[...] + mod_ref[0, 5:6, :] * y


def _combine_call(pos2, y_rows, x2, route, mod, seq):
    N, D = x2.shape
    tm = TM_COMB
    tpb = seq // tm
    nt = N // tm
    row = lambda i: (i, 0)
    return pl.pallas_call(
        _combine_kernel,
        grid=(nt,),
        in_specs=[
            pl.BlockSpec((1, 1, 2 * tm), lambda i: (i, 0, 0), memory_space=pltpu.SMEM),
            pl.BlockSpec((1, 1, 2 * tm), lambda i: (jnp.minimum(i + 1, nt - 1), 0, 0), memory_space=pltpu.SMEM),
            pl.BlockSpec(memory_space=pl.ANY),
            pl.BlockSpec((tm, D), row),
            pl.BlockSpec((tm, LANES), row),
            pl.BlockSpec((1, 6, D), lambda i: (i // tpb, 0, 0)),
        ],
        out_specs=pl.BlockSpec((tm, D), row),
        out_shape=jax.ShapeDtypeStruct((N, D), F32),
        scratch_shapes=[pltpu.VMEM((2, 2 * tm * TOK_SUB, LANES), F32), pltpu.SemaphoreType.DMA((2,))],
        compiler_params=_cparams(("arbitrary",)),
        name="moe_combine",
    )(pos2, pos2, y_rows, x2, route, mod)


def _rope_lane_tables(positions, rot_dim, head_w, n_rep):
    half = rot_dim // 2
    inv = jnp.power(jnp.float32(ROPE_THETA), -2.0 * jnp.arange(half, dtype=jnp.float32) / rot_dim)
    ang = positions.astype(jnp.float32)[..., None] * inv
    cos, sin = jnp.cos(ang), jnp.sin(ang)
    rest = head_w - rot_dim
    cos_h = jnp.concatenate([cos, cos, jnp.ones(cos.shape[:-1] + (rest,), F32)], axis=-1)
    sin_h = jnp.concatenate([-sin, sin, jnp.zeros(sin.shape[:-1] + (rest,), F32)], axis=-1)
    n = positions.shape[0] * positions.shape[1]
    return (jnp.tile(cos_h, (1, 1, n_rep)).reshape(n, head_w * n_rep),
            jnp.tile(sin_h, (1, 1, n_rep)).reshape(n, head_w * n_rep))


def _layer_weights(l, w_in, q_norm, k_norm, conv_w, conv_b, conv_ln_g, conv_ln_b, conv_pw_w, conv_pw_b,
                   pool_w, pool_b, pool_scale, sgu_ln_g, sgu_ln_b, sgu_w, sgu_b, out_norm, w_out, norm2,
                   w_rg, b_rg, w_re, b_re):
    G = GROUP_W
    D = D_MODEL
    pts = np.cumsum(IN_SPLITS)[:-1].tolist()
    wq, wk, wv, wiq, wik, wiw, wb, wc, wd = jnp.split(w_in[l], pts, axis=-1)
    wiw_p = jnp.pad(wiw, ((0, 0), (0, LANES - IDX_HEADS)))
    w_in_p = jnp.concatenate([wq, wk, wv, wiq, jnp.tile(wik, (1, IDX_HEADS)), wiw_p, wb, wc, wd], axis=-1).astype(BF16)
    npool = len(POOL_WINDOWS)
    pool_bd = jnp.zeros((G, G), F32)
    for g in range(npool):
        pool_bd = lax.dynamic_update_slice(pool_bd, pool_w[l, g], (g * POOL_CH, g * POOL_CH))
    sgu_bias = jnp.repeat(sgu_b[l].T, G // SGU_HEADS, axis=1)
    w_router = jnp.concatenate([w_rg[l], w_re[l].reshape(D, N_EXPERTS),
                                jnp.zeros((D, LANES - N_EXP_GROUPS - N_EXPERTS), F32)], axis=-1).astype(BF16)
    b_router = jnp.concatenate([b_rg[l], b_re[l].reshape(N_EXPERTS),
                                jnp.zeros((LANES - N_EXP_GROUPS - N_EXPERTS,), F32)]).reshape(1, LANES)
    r1 = lambda a: a.reshape(1, -1)
    return dict(
        w_in=w_in_p,
        q_norm=jnp.tile(q_norm[l], ATT_HEADS).reshape(1, G), k_norm=jnp.tile(k_norm[l], ATT_HEADS).reshape(1, G),
        conv_w=conv_w[l], conv_b=r1(conv_b[l]), conv_ln_g=r1(conv_ln_g[l]), conv_ln_b=r1(conv_ln_b[l]),
        conv_pw_w=conv_pw_w[l].astype(BF16), conv_pw_b=r1(conv_pw_b[l]),
        pool_w=pool_bd.astype(BF16), pool_b=r1(pool_b[l]), pool_scale=r1(pool_scale[l]),
        sgu_ln_g=r1(sgu_ln_g[l]), sgu_ln_b=r1(sgu_ln_b[l]), sgu_w=sgu_w[l], sgu_b=sgu_bias,
        out_norm=r1(out_norm[l]), w_out=w_out[l].astype(BF16), norm2=r1(norm2[l]),
        w_router=w_router, b_router=b_router,
    )


def _dispatch_tables(route, cnt, n_tokens):
    rows_blk = MOE_ROWS
    e = route[:, 0:2].astype(jnp.int32)
    rank = route[:, 2:4].astype(jnp.int32)
    counts = cnt[0, :N_EXPERTS].astype(jnp.int32)
    padded = (counts + rows_blk - 1) // rows_blk * rows_blk
    pad_end = jnp.cumsum(padded)
    pad_start = pad_end - padded
    pos = pad_start[e] + rank
    m = n_tokens * MOE_TOPK
    n_blocks = (m + N_EXPERTS * (rows_blk - 1) + rows_blk - 1) // rows_blk
    n_used = (pad_end[-1] // rows_blk).astype(jnp.int32).reshape(1)
    blk_row0 = jnp.arange(n_blocks, dtype=jnp.int32) * rows_blk
    block_expert = jnp.minimum(jnp.sum((pad_end[None, :] <= blk_row0[:, None]).astype(jnp.int32), axis=1),
                               N_EXPERTS - 1)
    pos2 = (pos * TOK_SUB).reshape(n_tokens // TM_COMB, TM_COMB, MOE_TOPK).transpose(0, 2, 1).reshape(
        -1, 1, MOE_TOPK * TM_COMB)
    tail = jnp.where(counts > 0, (pad_end - rows_blk) * TOK_SUB, -1).astype(jnp.int32)
    return block_expert, n_used, tail, pos2, n_blocks * rows_blk


def kernel(x, c, positions, w_ada, b_ada, norm1, w_in, q_norm, k_norm, conv_w, conv_b, conv_ln_g, conv_ln_b, conv_pw_w, conv_pw_b, pool_w, pool_b, pool_scale, sgu_ln_g, sgu_ln_b, sgu_w, sgu_b, out_norm, w_out, norm2, w_rg, b_rg, w_re, b_re, w_gate, w_up, w_down):
    B, S, D = x.shape
    N = B * S
    assert D == D_MODEL and S % TM_PROJ == 0 and S % TK_ATT == 0 and N % TM_COMB == 0
    depth = w_ada.shape[0]
    cos_a, sin_a = _rope_lane_tables(positions, ROPE_DIM, ATT_HEAD_DIM, ATT_HEADS)
    cos_i, sin_i = _rope_lane_tables(positions, IDX_ROPE_DIM, IDX_DIM, IDX_HEADS)
    c_pad = jnp.pad(c, ((0, (-B) % 8), (0, 0)))
    mod_all = _ada_call(c_pad, w_ada, b_ada)
    x2 = x.reshape(N, D)
    for l in range(depth):
        lw = _layer_weights(l, w_in, q_norm, k_norm, conv_w, conv_b, conv_ln_g, conv_ln_b, conv_pw_w, conv_pw_b,
                            pool_w, pool_b, pool_scale, sgu_ln_g, sgu_ln_b, sgu_w, sgu_b, out_norm, w_out, norm2,
                            w_rg, b_rg, w_re, b_re)
        mod = mod_all[l, :B].reshape(B, 6, D)
        q, k, v, iq, ik, iw, pb, pc, pd = _proj_call(
            x2, mod, norm1[l].reshape(1, D), lw["w_in"], lw["q_norm"], lw["k_norm"], cos_a, sin_a, cos_i, sin_i, S)
        oa = _dsa_call(q, k, v, iq, ik, iw, B, S)
        x_mid, h2, route, cnt = _mix_call(x2, oa, pb, pc, pd, mod, lw, S)
        block_expert, n_used, tail, pos2, n_rows = _dispatch_tables(route, cnt, N)
        xs = _dispatch_call(tail, n_used, pos2, h2, n_rows)
        y_rows = _expert_call(block_expert, n_used, xs, w_gate, w_up, w_down, l)
        x2 = _combine_call(pos2, y_rows, x_mid, route, mod, S)
    return x2.reshape(B, S, D)
```

```python
import functools

import numpy as np
import jax
import jax.numpy as jnp
from jax import lax
from jax.experimental import pallas as pl
from jax.experimental.pallas import tpu as pltpu

F32 = jnp.float32
BF16 = jnp.bfloat16
NEG_INF = float("-inf")

D_MODEL = 1024
DEPTH = 2
CHUNK = 64
N_MIXERS = 4
GROUP_W = D_MODEL // N_MIXERS
ATT_HEAD_DIM = 64
ATT_HEADS = GROUP_W // ATT_HEAD_DIM
ROPE_DIM = ATT_HEAD_DIM // 4
ROPE_THETA = 500000.0
IDX_HEADS = 4
IDX_DIM = 32
IDX_ROPE_DIM = IDX_DIM // 4
TOPK_MAX = 256
CONV_WIDTH = 31
CONV_GROUPS = 4
POOL_WINDOWS = (2, 4, 8, 16)
POOL_CH = GROUP_W // 4
SGU_CHUNK = 128
SGU_HEADS = 4
N_EXP_GROUPS = 4
EXP_PER_GROUP = 8
N_EXPERTS = N_EXP_GROUPS * EXP_PER_GROUP
EXPERT_FF = 512
MOE_TOPK = 2
EPS = 1e-6
IN_SPLITS = (GROUP_W, GROUP_W, GROUP_W, IDX_HEADS * IDX_DIM, IDX_DIM, IDX_HEADS, 2 * GROUP_W, GROUP_W, 2 * GROUP_W)

LANES = 128
HALO = 32
W_IN_COLS = 3 * GROUP_W + 3 * LANES + 5 * GROUP_W

TM_PROJ = 512
TQ_ATT = 256
TK_ATT = 512
TM_MIX = 256
MOE_ROWS = 256
TM_COMB = 256
BISECT_ITERS = 18
VMEM_LIMIT = 56 * 1024 * 1024


def _cparams(sem):
    return pltpu.CompilerParams(dimension_semantics=sem, vmem_limit_bytes=VMEM_LIMIT)


def _lane_iota(shape):
    return lax.broadcasted_iota(jnp.int32, shape, len(shape) - 1)


def _seg_mean(y, width):
    shift = int(np.log2(width))
    grp = _lane_iota(y.shape) >> shift
    out = jnp.zeros_like(y)
    for g in range(y.shape[-1] // width):
        msk = grp == g
        s = jnp.sum(jnp.where(msk, y, 0.0), axis=-1, keepdims=True)
        out = jnp.where(msk, s, out)
    return out * (1.0 / width)


def _rope(x, cos_f, sin_s, head_w, half):
    c = x.shape[-1]
    lane = _lane_iota(x.shape) & (head_w - 1)
    partner = jnp.where(lane < half, pltpu.roll(x, c - half, 1), pltpu.roll(x, half, 1))
    return x * cos_f + partner * sin_s


def _silu(x):
    return x * jax.nn.sigmoid(x)


def _ada_kernel(c_ref, w_ref, b_ref, o_ref):
    ca = _silu(c_ref[...])
    o_ref[0] = jnp.dot(ca.astype(BF16), w_ref[0].astype(BF16), preferred_element_type=F32) + b_ref[0]


def _ada_call(c_pad, w_ada, b_ada):
    L, D, D6 = w_ada.shape
    rows = c_pad.shape[0]
    tn = D
    return pl.pallas_call(
        _ada_kernel,
        grid=(L, D6 // tn),
        in_specs=[
            pl.BlockSpec((rows, D), lambda l, j: (0, 0)),
            pl.BlockSpec((1, D, tn), lambda l, j: (l, 0, j)),
            pl.BlockSpec((1, 1, tn), lambda l, j: (l, 0, j)),
        ],
        out_specs=pl.BlockSpec((1, rows, tn), lambda l, j: (l, 0, j)),
        out_shape=jax.ShapeDtypeStruct((L, rows, D6), F32),
        compiler_params=_cparams(("arbitrary", "arbitrary")),
        name="ada_mod",
    )(c_pad, w_ada, b_ada.reshape(L, 1, D6))


def _proj_kernel(x_ref, mod_ref, n1_ref, w_ref, qn_ref, kn_ref, cosa_ref, sina_ref, cosi_ref, sini_ref,
                 q_ref, k_ref, v_ref, iq_ref, ik_ref, iw_ref, pb_ref, pc_ref, pd_ref):
    x = x_ref[...]
    ms = jnp.mean(x * x, axis=-1, keepdims=True)
    h = (x * lax.rsqrt(ms + EPS)) * n1_ref[...]
    h = h * (1.0 + mod_ref[0, 1:2, :]) + mod_ref[0, 0:1, :]
    proj = jnp.dot(h.astype(BF16), w_ref[...], preferred_element_type=F32)
    G = GROUP_W
    cos_a, sin_a = cosa_ref[...], sina_ref[...]
    cos_i, sin_i = cosi_ref[...], sini_ref[...]

    def qk(t, g_ref):
        tn = (t * lax.rsqrt(_seg_mean(t * t, ATT_HEAD_DIM) + EPS)) * g_ref[...]
        return _rope(tn, cos_a, sin_a, ATT_HEAD_DIM, ROPE_DIM // 2)

    q_ref[...] = (qk(proj[:, 0:G], qn_ref) * (ATT_HEAD_DIM ** -0.5)).T.astype(BF16)
    k_ref[...] = qk(proj[:, G:2 * G], kn_ref).astype(BF16)
    v_ref[0] = proj[:, 2 * G:3 * G].T.astype(BF16)
    o = 3 * G
    iq_ref[...] = _rope(proj[:, o:o + LANES], cos_i, sin_i, IDX_DIM, IDX_ROPE_DIM // 2).T.astype(BF16)
    ik_ref[...] = _rope(proj[:, o + LANES:o + 2 * LANES], cos_i, sin_i, IDX_DIM, IDX_ROPE_DIM // 2).astype(BF16)
    iw_ref[...] = (proj[:, o + 2 * LANES:o + 3 * LANES] * (IDX_HEADS ** -0.5)).T[0:8, :]
    o += 3 * LANES
    pb_ref[...] = proj[:, o:o + 2 * G]
    pc_ref[...] = proj[:, o + 2 * G:o + 3 * G]
    pd_ref[...] = proj[:, o + 3 * G:o + 5 * G]


def _proj_call(x2, mod, n1, w_in_p, qn_t, kn_t, cos_a, sin_a, cos_i, sin_i, seq):
    N, D = x2.shape
    tm = TM_PROJ
    tpb = seq // tm
    G = GROUP_W
    assert tm == TK_ATT
    row = lambda i: (i, 0)
    col = lambda i: (0, i)
    fixed = lambda i: (0, 0)
    sds = jax.ShapeDtypeStruct
    out_specs = [
        pl.BlockSpec((G, tm), col),
        pl.BlockSpec((tm, G), row),
        pl.BlockSpec((1, G, tm), lambda i: (i, 0, 0)),
        pl.BlockSpec((LANES, tm), col),
        pl.BlockSpec((tm, LANES), row),
        pl.BlockSpec((8, tm), col),
        pl.BlockSpec((tm, 2 * G), row),
        pl.BlockSpec((tm, G), row),
        pl.BlockSpec((tm, 2 * G), row),
    ]
    out_shape = [sds((G, N), BF16), sds((N, G), BF16), sds((N // tm, G, tm), BF16), sds((LANES, N), BF16),
                 sds((N, LANES), BF16), sds((8, N), F32), sds((N, 2 * G), F32), sds((N, G), F32), sds((N, 2 * G), F32)]
    return pl.pallas_call(
        _proj_kernel,
        grid=(N // tm,),
        in_specs=[
            pl.BlockSpec((tm, D), row),
            pl.BlockSpec((1, 6, D), lambda i: (i // tpb, 0, 0)),
            pl.BlockSpec((1, D), fixed),
            pl.BlockSpec((D, W_IN_COLS), fixed),
            pl.BlockSpec((1, G), fixed),
            pl.BlockSpec((1, G), fixed),
            pl.BlockSpec((tm, G), row),
            pl.BlockSpec((tm, G), row),
            pl.BlockSpec((tm, LANES), row),
            pl.BlockSpec((tm, LANES), row),
        ],
        out_specs=out_specs,
        out_shape=out_shape,
        compiler_params=_cparams(("parallel",)),
        name="norm_in_proj",
    )(x2, mod, n1, w_in_p, qn_t, kn_t, cos_a, sin_a, cos_i, sin_i)


def _pair_rhs(xt, head_rows, h0):
    head = lax.broadcasted_iota(jnp.int32, xt.shape, 0) >> int(np.log2(head_rows))
    zero = jnp.zeros_like(xt)
    return jnp.concatenate([jnp.where(head == h0, xt, zero), jnp.where(head == h0 + 1, xt, zero)], axis=1)


def _dsa_kernel(q_ref, k_ref, v_ref, iq_ref, ik_ref, iw_ref, ltri_ref, o_ref, sc_ref, lge_ref, lgo_ref, *, topk):
    tq, tk = TQ_ATT, TK_ATT
    i = pl.program_id(1)
    q0 = i * tq
    n_kv = (q0 + tq + tk - 1) // tk
    kf = float(topk)

    q_pos = _lane_iota((1, tq)) + q0
    key_end = ((q_pos >> 6) + 1) << 6
    key_i = lax.broadcasted_iota(jnp.int32, (tk, tq), 0)

    def fold8(x, op):
        parts = [x[r * 8:(r + 1) * 8, :] for r in range(tk // 8)]
        while len(parts) > 1:
            parts = [op(parts[a], parts[a + 1]) for a in range(0, len(parts), 2)]
        return parts[0]

    iqt = iq_ref[...]
    iq_pairs = [_pair_rhs(iqt, IDX_DIM, h0) for h0 in range(0, IDX_HEADS, 2)]
    iw_h = [iw_ref[h:h + 1, :] for h in range(IDX_HEADS)]

    def score_body(kc, carry):
        hi8, lo8 = carry
        ikc = ik_ref[pl.ds(pl.multiple_of(kc * tk, tk), tk), :]
        s = jnp.zeros((tk, tq), F32)
        for pi, rhs in enumerate(iq_pairs):
            d2 = jnp.dot(ikc, rhs, preferred_element_type=F32)
            for j in range(2):
                d = d2[:, j * tq:(j + 1) * tq]
                s = s + jnp.maximum(d * (IDX_DIM ** -0.5), 0.0) * iw_h[2 * pi + j]
        adm = key_i + kc * tk < key_end
        s_top = jnp.where(adm, s, NEG_INF)
        sc_ref[kc] = s_top
        return (jnp.maximum(hi8, fold8(s_top, jnp.maximum)),
                jnp.minimum(lo8, fold8(jnp.where(adm, s, jnp.inf), jnp.minimum)))

    hi8, lo8 = lax.fori_loop(0, n_kv, score_body,
                             (jnp.full((8, tq), NEG_INF, F32), jnp.full((8, tq), jnp.inf, F32)))
    col_max = jnp.max(hi8, axis=0, keepdims=True)
    col_min = jnp.min(lo8, axis=0, keepdims=True)

    def reduce_chunks(fns, init, combine, fold):
        def body(kc, parts):
            blk = sc_ref[kc]
            return tuple(combine(p, fold8(fn(blk), combine)) for p, fn in zip(parts, fns))
        parts = lax.fori_loop(0, n_kv, body, tuple(jnp.full((8, tq), init, F32) for _ in fns))
        return [fold(p, axis=0, keepdims=True) for p in parts]

    def count(*inds):
        return reduce_chunks(inds, 0.0, jnp.add, jnp.sum)

    def col_maximum(val):
        return reduce_chunks((val,), NEG_INF, jnp.maximum, jnp.max)[0]

    small = key_end <= topk

    @pl.when(q0 + tq > topk)
    def _select():
        def bis_body(_, c):
            lo, hi = c
            mid = jnp.where(hi == jnp.inf, col_max, lo + (hi - lo) * 0.5)
            ge = count(lambda b: jnp.where(b >= mid, 1.0, 0.0))[0] >= kf
            return jnp.where(ge, mid, lo), jnp.where(ge, hi, mid)

        lo, hi = lax.fori_loop(0, BISECT_ITERS, bis_body, (col_min, jnp.full((1, tq), jnp.inf, F32)))

        def sd_cond(c):
            return c[0] > 0.0

        def sd_body(c):
            _, hi, thr, done = c
            cand = col_maximum(lambda b: jnp.where(b < hi, b, NEG_INF))
            ok = count(lambda b: jnp.where(b >= cand, 1.0, 0.0))[0] >= kf
            thr = jnp.where(done > 0.0, thr, cand)
            hi = jnp.where(done > 0.0, hi, cand)
            done = jnp.where(ok, 1.0, done)
            return jnp.sum(1.0 - done), hi, thr, done

        done0 = jnp.where(small, 1.0, 0.0)
        n0 = jnp.sum(1.0 - done0)
        _, _, thr, _ = lax.while_loop(sd_cond, sd_body, (n0, hi, jnp.full((1, tq), NEG_INF, F32), done0))
        thr = jnp.where(small, NEG_INF, thr)

        n_above, n_tied = count(lambda b: jnp.where(b > thr, 1.0, 0.0), lambda b: jnp.where(b == thr, 1.0, 0.0))
        need = kf - n_above
        excess = jnp.sum(jnp.where(jnp.where(small, 0.0, n_tied) > need, 1.0, 0.0))

        @pl.when(excess <= 0.0)
        def _keep_all_ties():
            def bias_body(kc, carry):
                blk = sc_ref[kc]
                sc_ref[kc] = jnp.where(blk == NEG_INF, NEG_INF, jnp.where(blk >= thr, 0.0, NEG_INF))
                return carry
            lax.fori_loop(0, n_kv, bias_body, 0)

        @pl.when(excess > 0.0)
        def _rank_ties():
            half = tk // 2
            ltri_top = ltri_ref[0:half, 0:half]
            ltri_bot = ltri_ref[half:tk, :]

            def bias_body(kc, seen):
                blk = sc_ref[kc]
                tied = jnp.where(blk == thr, 1.0, 0.0)
                tied16 = tied.astype(BF16)
                rank = jnp.concatenate([jnp.dot(ltri_top, tied16[0:half, :], preferred_element_type=F32),
                                        jnp.dot(ltri_bot, tied16, preferred_element_type=F32)], axis=0) + seen
                tie = jnp.where(blk == thr, jnp.where(rank <= need, 0.0, NEG_INF), NEG_INF)
                bias = jnp.where(blk > thr, 0.0, tie)
                sc_ref[kc] = jnp.where(blk == NEG_INF, NEG_INF, bias)
                return seen + jnp.sum(fold8(tied, jnp.add), axis=0, keepdims=True)

            lax.fori_loop(0, n_kv, bias_body, jnp.zeros((1, tq), F32))

    @pl.when(q0 + tq <= topk)
    def _all():
        def bias_body(kc, carry):
            sc_ref[kc] = jnp.where(sc_ref[kc] == NEG_INF, NEG_INF, 0.0)
            return carry
        lax.fori_loop(0, n_kv, bias_body, 0)

    qt = q_ref[...]
    q_pairs = [_pair_rhs(qt, ATT_HEAD_DIM, h0) for h0 in range(0, ATT_HEADS, 2)]
    dh = ATT_HEAD_DIM

    def store_logits(buf, kc):
        kblk = k_ref[pl.ds(pl.multiple_of(kc * tk, tk), tk), :]
        bias = sc_ref[kc]
        for pi, rhs in enumerate(q_pairs):
            s2 = jnp.dot(kblk, rhs, preferred_element_type=F32)
            for j in range(2):
                buf[2 * pi + j] = s2[:, j * tq:(j + 1) * tq] + bias

    def absorb(buf, kc, state):
        ms, ls, accs = state
        vt = v_ref[kc]
        ms_n, ls_n, accs_n = [], [], []
        for h in range(ATT_HEADS):
            s = buf[h]
            m_new = jnp.maximum(ms[h], jnp.max(fold8(s, jnp.maximum), axis=0, keepdims=True))
            m_safe = jnp.where(m_new == NEG_INF, 0.0, m_new)
            alpha = jnp.exp(ms[h] - m_safe)
            p = jnp.exp(s - m_safe)
            ls_n.append(alpha * ls[h] + fold8(p, jnp.add))
            pv = jnp.dot(vt[h * dh:(h + 1) * dh, :], p.astype(BF16), preferred_element_type=F32)
            accs_n.append(alpha * accs[h] + pv)
            ms_n.append(m_new)
        return tuple(ms_n), tuple(ls_n), tuple(accs_n)

    def pair_body(jj, state):
        s = 2 * jj + 1
        store_logits(lgo_ref, s)
        state = absorb(lge_ref, s - 1, state)
        store_logits(lge_ref, s + 1)
        return absorb(lgo_ref, s, state)

    state = (tuple(jnp.full((1, tq), NEG_INF, F32) for _ in range(ATT_HEADS)),
             tuple(jnp.zeros((8, tq), F32) for _ in range(ATT_HEADS)),
             tuple(jnp.zeros((dh, tq), F32) for _ in range(ATT_HEADS)))
    store_logits(lge_ref, 0)
    n_pairs = (n_kv - 1) // 2
    state = lax.fori_loop(0, n_pairs, pair_body, state)
    last_even = 2 * n_pairs

    def tail_two(state):
        store_logits(lgo_ref, last_even + 1)
        return absorb(lgo_ref, last_even + 1, absorb(lge_ref, last_even, state))

    def tail_one(state):
        return absorb(lge_ref, last_even, state)

    _, ls, accs = lax.cond(n_kv - 1 - last_even > 0, tail_two, tail_one, state)
    out_t = jnp.concatenate([accs[h] / jnp.sum(ls[h], axis=0, keepdims=True) for h in range(ATT_HEADS)], axis=0)
    o_ref[...] = out_t.T


def _dsa_call(qt, k, vt, iqt, ik, iwt, batch, seq):
    N, G = k.shape
    tq = TQ_ATT
    nq = seq // tq
    nkc = seq // TK_ATT
    topk = min(TOPK_MAX, seq // 4)
    qcol = lambda b, i: (0, b * nq + i)
    brow = lambda b, i: (b, 0)
    return pl.pallas_call(
        functools.partial(_dsa_kernel, topk=topk),
        grid=(batch, nq),
        in_specs=[
            pl.BlockSpec((G, tq), qcol),
            pl.BlockSpec((seq, G), brow),
            pl.BlockSpec((nkc, G, TK_ATT), lambda b, i: (b, 0, 0)),
            pl.BlockSpec((LANES, tq), qcol),
            pl.BlockSpec((seq, LANES), brow),
            pl.BlockSpec((8, tq), qcol),
            pl.BlockSpec((TK_ATT, TK_ATT), lambda b, i: (0, 0)),
        ],
        out_specs=pl.BlockSpec((tq, G), lambda b, i: (b * nq + i, 0)),
        out_shape=jax.ShapeDtypeStruct((N, G), F32),
        scratch_shapes=[pltpu.VMEM((nkc, TK_ATT, tq), F32), pltpu.VMEM((ATT_HEADS, TK_ATT, tq), F32),
                        pltpu.VMEM((ATT_HEADS, TK_ATT, tq), F32)],
        compiler_params=_cparams(("parallel", "arbitrary")),
        name="dsa_attention",
    )(qt, k, vt, iqt, ik, iwt, jnp.tril(jnp.ones((TK_ATT, TK_ATT), BF16)))


def _mix_kernel(x_ref, oa_ref, pb_ref, pbh_ref, pc_ref, pch_ref, pd_ref, mod_ref,
                cw_ref, cb_ref, clg_ref, clb_ref, cpw_ref, cpb_ref,
                pw_ref, pbias_ref, ps_ref, slg_ref, slb_ref, sw_ref, sb_ref,
                on_ref, wo_ref, n2_ref, wr_ref, br_ref,
                xo_ref, h2_ref, route_ref, cnt_ref,
                ypad_ref, ppad_ref, s2_ref, s4_ref, s8_ref, run_ref, cph_ref, *, tiles_per_batch):
    tm = TM_MIX
    G = GROUP_W
    i = pl.program_id(0)
    t_in_b = i % tiles_per_batch
    first = t_in_b == 0
    lane_g = _lane_iota((tm, G))

    def glu(pb):
        return pb[:, 0:G] * jax.nn.sigmoid(pb[:, G:2 * G])

    ypad_ref[0:HALO, :] = jnp.where(first, 0.0, glu(pbh_ref[...]))
    ypad_ref[HALO:HALO + tm, :] = glu(pb_ref[...])
    acc = jnp.zeros((tm, G), F32)
    first_off = HALO - (CONV_WIDTH - 1)
    for phase in range(8):
        offs = [o for o in range(first_off, HALO + 1) if o % 8 == phase]
        if not offs:
            continue
        span = offs[-1] - offs[0] + tm
        cph_ref[phase, 0:span, :] = ypad_ref[offs[0]:offs[0] + span, :]
        for o in offs:
            acc = acc + cw_ref[o - first_off:o - first_off + 1, :] * cph_ref[phase, o - offs[0]:o - offs[0] + tm, :]
    y = acc + cb_ref[...]
    gw = G // CONV_GROUPS
    mu = _seg_mean(y, gw)
    yc = y - mu
    var = _seg_mean(yc * yc, gw)
    y = (yc * lax.rsqrt(var + EPS)) * clg_ref[...] + clb_ref[...]
    o_b = jnp.dot(_silu(y).astype(BF16), cpw_ref[...], preferred_element_type=F32) + cpb_ref[...]

    p = pc_ref[...]
    ppad_ref[0:HALO, :] = jnp.where(first, 0.0, pch_ref[...])
    ppad_ref[HALO:HALO + tm, :] = p
    n8 = tm + HALO - 8
    s2_ref[8:8 + n8, :] = ppad_ref[8:8 + n8, :] + ppad_ref[7:7 + n8, :]
    n16 = tm + HALO - 16
    s4_ref[16:16 + n16, :] = s2_ref[16:16 + n16, :] + s2_ref[14:14 + n16, :]
    n24 = tm + HALO - 24
    s8_ref[24:24 + n24, :] = s4_ref[24:24 + n24, :] + s4_ref[20:20 + n24, :]
    s2 = s2_ref[HALO:HALO + tm, :]
    s4 = s4_ref[HALO:HALO + tm, :]
    s8 = s8_ref[HALO:HALO + tm, :]
    s16 = s8 + s8_ref[HALO - 8:HALO - 8 + tm, :]
    pgrp = lane_g >> 6
    wsum = jnp.where(pgrp == 0, s2, jnp.where(pgrp == 1, s4, jnp.where(pgrp == 2, s8, s16)))
    wlen = jnp.where(pgrp == 0, 2.0, jnp.where(pgrp == 1, 4.0, jnp.where(pgrp == 2, 8.0, 16.0)))
    tpos = (lax.broadcasted_iota(jnp.int32, (tm, G), 0) + t_in_b * tm + 1).astype(F32)
    pooled = wsum / jnp.minimum(tpos, wlen) - p
    o_c = (jnp.dot(pooled.astype(BF16), pw_ref[...], preferred_element_type=F32) + pbias_ref[...]) * ps_ref[...]

    pd = pd_ref[...]
    u, v = pd[:, 0:G], pd[:, G:2 * G]
    mu = jnp.mean(v, axis=-1, keepdims=True)
    vc = v - mu
    var = jnp.mean(vc * vc, axis=-1, keepdims=True)
    vn = ((vc * lax.rsqrt(var + EPS)) * slg_ref[...] + slb_ref[...]).astype(BF16)
    r_i = lax.broadcasted_iota(jnp.int32, (SGU_CHUNK, SGU_CHUNK), 0)
    c_i = lax.broadcasted_iota(jnp.int32, (SGU_CHUNK, SGU_CHUNK), 1)
    w_heads = [jnp.where(r_i >= c_i, sw_ref[h], 0.0).astype(BF16) for h in range(SGU_HEADS)]
    lane_c = _lane_iota((SGU_CHUNK, G)) >> 6
    mixed = []
    for n in range(tm // SGU_CHUNK):
        vch = vn[n * SGU_CHUNK:(n + 1) * SGU_CHUNK, :]
        mx = jnp.zeros((SGU_CHUNK, G), F32)
        for h in range(SGU_HEADS):
            mx = jnp.where(lane_c == h, jnp.dot(w_heads[h], vch, preferred_element_type=F32), mx)
        mixed.append(mx + sb_ref[...])
    o_d = u * jnp.concatenate(mixed, axis=0)

    proj = jnp.zeros((tm, D_MODEL), F32)
    for g, piece in enumerate((oa_ref[...], o_b, o_c, o_d)):
        ms = jnp.mean(piece * piece, axis=-1, keepdims=True)
        pn = (piece * lax.rsqrt(ms + EPS)) * on_ref[:, g * G:(g + 1) * G]
        proj = proj + jnp.dot(pn.astype(BF16), wo_ref[g * G:(g + 1) * G, :], preferred_element_type=F32)
    x_new = x_ref[...] + mod_ref[0, 2:3, :] * proj
    xo_ref[...] = x_new

    ms = jnp.mean(x_new * x_new, axis=-1, keepdims=True)
    h2 = (x_new * lax.rsqrt(ms + EPS)) * n2_ref[...]
    h2 = h2 * (1.0 + mod_ref[0, 4:5, :]) + mod_ref[0, 3:4, :]
    _to_token_major(h2_ref, h2)
    logits = jnp.dot(h2.astype(BF16), wr_ref[...], preferred_element_type=F32) + br_ref[...]
    lane = _lane_iota((tm, LANES))
    lane_f = lane.astype(F32)
    big = float(LANES)
    glog = jnp.where(lane < N_EXP_GROUPS, logits, NEG_INF)
    gmax = jnp.max(glog, axis=-1, keepdims=True)
    p_sel = 1.0 / jnp.sum(jnp.exp(glog - gmax), axis=-1, keepdims=True)
    g_idx = jnp.min(jnp.where(glog == gmax, lane_f, big), axis=-1, keepdims=True)
    e_lane = lane - N_EXP_GROUPS
    elog = jnp.where((e_lane >> 3).astype(F32) == g_idx, logits, NEG_INF)
    top1 = jnp.max(elog, axis=-1, keepdims=True)
    j1 = jnp.min(jnp.where(elog == top1, lane_f, big), axis=-1, keepdims=True)
    elog2 = jnp.where(lane_f == j1, NEG_INF, elog)
    top2 = jnp.max(elog2, axis=-1, keepdims=True)
    j2 = jnp.min(jnp.where(elog2 == top2, lane_f, big), axis=-1, keepdims=True)
    e2w = jnp.exp(top2 - top1)
    gate1 = p_sel * (1.0 / (1.0 + e2w))
    gate2 = p_sel * (e2w / (1.0 + e2w))
    e1 = j1 - float(N_EXP_GROUPS)
    e2 = j2 - float(N_EXP_GROUPS)

    @pl.when(i == 0)
    def _init():
        run_ref[...] = jnp.zeros_like(run_ref)

    onehot = jnp.where(jnp.logical_or(lane_f == e1, lane_f == e2), 1.0, 0.0)
    rr = lax.broadcasted_iota(jnp.int32, (tm, tm), 0)
    cc = lax.broadcasted_iota(jnp.int32, (tm, tm), 1)
    before = jnp.where(rr > cc, 1.0, 0.0).astype(BF16)
    prior = jnp.dot(before, onehot.astype(BF16), preferred_element_type=F32) + run_ref[0:1, :]
    rank1 = jnp.sum(jnp.where(lane_f == e1, prior, 0.0), axis=-1, keepdims=True)
    rank2 = jnp.sum(jnp.where(lane_f == e2, prior, 0.0), axis=-1, keepdims=True)
    run_new = run_ref[0:1, :] + jnp.sum(onehot, axis=0, keepdims=True)
    run_ref[...] = jnp.broadcast_to(run_new, run_ref.shape)
    cnt_ref[...] = jnp.broadcast_to(run_new, cnt_ref.shape)
    route = jnp.where(lane == 0, e1, jnp.where(lane == 1, e2, jnp.where(lane == 2, rank1, jnp.where(
        lane == 3, rank2, jnp.where(lane == 4, gate1, jnp.where(lane == 5, gate2, 0.0))))))
    route_ref[...] = route


def _mix_call(x2, oa, pb, pc, pd, mod, lw, seq):
    N, D = x2.shape
    tm = TM_MIX
    tpb = seq // tm
    G = GROUP_W
    row = lambda i: (i, 0)
    halo = lambda i: (jnp.maximum(i * (tm // HALO) - 1, 0), 0)
    fixed2 = lambda i: (0, 0)
    fixed3 = lambda i: (0, 0, 0)
    params = [lw["conv_w"], lw["conv_b"], lw["conv_ln_g"], lw["conv_ln_b"], lw["conv_pw_w"], lw["conv_pw_b"],
              lw["pool_w"], lw["pool_b"], lw["pool_scale"], lw["sgu_ln_g"], lw["sgu_ln_b"], lw["sgu_w"], lw["sgu_b"],
              lw["out_norm"], lw["w_out"], lw["norm2"], lw["w_router"], lw["b_router"]]
    param_specs = [pl.BlockSpec(p.shape, fixed3 if p.ndim == 3 else fixed2) for p in params]
    return pl.pallas_call(
        functools.partial(_mix_kernel, tiles_per_batch=tpb),
        grid=(N // tm,),
        in_specs=[
            pl.BlockSpec((tm, D), row),
            pl.BlockSpec((tm, G), row),
            pl.BlockSpec((tm, 2 * G), row),
            pl.BlockSpec((HALO, 2 * G), halo),
            pl.BlockSpec((tm, G), row),
            pl.BlockSpec((HALO, G), halo),
            pl.BlockSpec((tm, 2 * G), row),
            pl.BlockSpec((1, 6, D), lambda i: (i // tpb, 0, 0)),
        ] + param_specs,
        out_specs=[
            pl.BlockSpec((tm, D), row),
            pl.BlockSpec((tm * TOK_SUB, LANES), row),
            pl.BlockSpec((tm, LANES), row),
            pl.BlockSpec((8, LANES), fixed2),
        ],
        out_shape=[
            jax.ShapeDtypeStruct((N, D), F32),
            jax.ShapeDtypeStruct((N * TOK_SUB, LANES), F32),
            jax.ShapeDtypeStruct((N, LANES), F32),
            jax.ShapeDtypeStruct((8, LANES), F32),
        ],
        scratch_shapes=[pltpu.VMEM((tm + HALO, G), F32) for _ in range(5)] + [
            pltpu.VMEM((8, LANES), F32), pltpu.VMEM((8, tm + HALO, G), F32)],
        compiler_params=_cparams(("arbitrary",)),
        name="mixers_out_router",
    )(x2, oa, pb, pb, pc, pc, pd, mod, *params)


TOK_SUB = D_MODEL // LANES
ROW_DMA_UNROLL = 8


def _to_token_major(ref, x):
    n = x.shape[0]
    for j in range(TOK_SUB):
        ref[pl.ds(j, n, stride=TOK_SUB), :] = x[:, j * LANES:(j + 1) * LANES]


def _from_token_major(ref, tok0, n):
    return jnp.concatenate([ref[pl.ds(tok0 * TOK_SUB + j, n, stride=TOK_SUB), :] for j in range(TOK_SUB)], axis=1)


def _token_copy(src_hbm, row8, dst_ref, slot, r, sem):
    src = src_hbm.at[pl.ds(pl.multiple_of(row8, TOK_SUB), TOK_SUB), :]
    dst = dst_ref.at[slot, pl.ds(pl.multiple_of(r * TOK_SUB, TOK_SUB), TOK_SUB), :]
    return pltpu.make_async_copy(src, dst, sem.at[slot])


def _start_tokens(idx_ref, src_hbm, dst_ref, slot, sem, n):
    def body(g, carry):
        for j in range(ROW_DMA_UNROLL):
            r = g * ROW_DMA_UNROLL + j
            _token_copy(src_hbm, idx_ref[0, 0, r], dst_ref, slot, r, sem).start(priority=j % 2)
        return carry
    lax.fori_loop(0, n // ROW_DMA_UNROLL, body, 0)


def _wait_tokens(src_hbm, dst_ref, slot, sem, n):
    pltpu.make_async_copy(src_hbm.at[pl.ds(0, n * TOK_SUB), :], dst_ref.at[slot], sem.at[slot]).wait()


def _dispatch_kernel(tail_ref, nu_ref, pos_ref, h_ref, xs_hbm, zbuf, zsem, sem):
    i = pl.program_id(0)
    tm = TM_COMB
    blk = MOE_ROWS * TOK_SUB
    n_blocks = xs_hbm.shape[0] // blk

    def zero_block(row0):
        return pltpu.make_async_copy(zbuf, xs_hbm.at[pl.ds(pl.multiple_of(row0, TOK_SUB), blk), :], zsem.at[0])

    @pl.when(i == 0)
    def _zero_fill():
        zbuf[...] = jnp.zeros_like(zbuf)
        n_used = nu_ref[0]
        for e in range(N_EXPERTS):
            @pl.when(tail_ref[e] >= 0)
            def _():
                zero_block(tail_ref[e]).start()
        lax.fori_loop(n_used, n_blocks, lambda b, c: (zero_block(b * blk).start(), c)[1], 0)
        for e in range(N_EXPERTS):
            @pl.when(tail_ref[e] >= 0)
            def _():
                zero_block(tail_ref[e]).wait()
        lax.fori_loop(n_used, n_blocks, lambda b, c: (zero_block(b * blk).wait(), c)[1], 0)

    def body(g, carry):
        for j in range(ROW_DMA_UNROLL):
            r = g * ROW_DMA_UNROLL + j
            t = jnp.where(r < tm, r, r - tm)
            src = h_ref.at[pl.ds(pl.multiple_of(t * TOK_SUB, TOK_SUB), TOK_SUB), :]
            dst = xs_hbm.at[pl.ds(pl.multiple_of(pos_ref[0, 0, r], TOK_SUB), TOK_SUB), :]
            pltpu.make_async_copy(src, dst, sem.at[0]).start(priority=j % 2)
        return carry
    lax.fori_loop(0, 2 * tm // ROW_DMA_UNROLL, body, 0)
    for _ in range(MOE_TOPK):
        pltpu.make_async_copy(h_ref, xs_hbm.at[pl.ds(0, tm * TOK_SUB), :], sem.at[0]).wait()


def _dispatch_call(tail, n_used, pos2, h2, n_rows):
    nt = pos2.shape[0]
    tm = TM_COMB
    grid_spec = pltpu.PrefetchScalarGridSpec(
        num_scalar_prefetch=2,
        grid=(nt,),
        in_specs=[
            pl.BlockSpec((1, 1, 2 * tm), lambda i, *_: (i, 0, 0), memory_space=pltpu.SMEM),
            pl.BlockSpec((tm * TOK_SUB, LANES), lambda i, *_: (i, 0)),
        ],
        out_specs=pl.BlockSpec(memory_space=pl.ANY),
        scratch_shapes=[
            pltpu.VMEM((MOE_ROWS * TOK_SUB, LANES), F32),
            pltpu.SemaphoreType.DMA((1,)),
            pltpu.SemaphoreType.DMA((1,)),
        ],
    )
    return pl.pallas_call(
        _dispatch_kernel,
        grid_spec=grid_spec,
        out_shape=jax.ShapeDtypeStruct((n_rows * TOK_SUB, LANES), F32),
        compiler_params=_cparams(("arbitrary",)),
        name="moe_dispatch",
    )(tail, n_used, pos2, h2)


def _expert_kernel(be_ref, nu_ref, x_ref, wg_ref, wu_ref, wd_ref, y_ref, wg_bf, wu_bf, wd_bf):
    i = pl.program_id(0)
    n_used = nu_ref[0]
    rows = MOE_ROWS

    changed = jnp.logical_or(i == 0, be_ref[i] != be_ref[jnp.maximum(i - 1, 0)])

    @pl.when(jnp.logical_and(changed, i < n_used))
    def _cast():
        wg_bf[...] = wg_ref[0, 0].astype(BF16)
        wu_bf[...] = wu_ref[0, 0].astype(BF16)
        wd_bf[...] = wd_ref[0, 0].astype(BF16)

    @pl.when(i < n_used)
    def _compute():
        xb = _from_token_major(x_ref, 0, rows).astype(BF16)
        g = jnp.dot(xb, wg_bf[...], preferred_element_type=F32)
        u = jnp.dot(xb, wu_bf[...], preferred_element_type=F32)
        hid = (_silu(g) * u).astype(BF16)
        _to_token_major(y_ref, jnp.dot(hid, wd_bf[...], preferred_element_type=F32))

    @pl.when(i >= n_used)
    def _skip():
        y_ref[...] = jnp.zeros_like(y_ref)


def _expert_call(block_expert, n_used, xs, w_gate, w_up, w_down, layer):
    rows = MOE_ROWS
    n_blocks = xs.shape[0] // (rows * TOK_SUB)
    D = D_MODEL
    FF = w_gate.shape[-1]
    grid_spec = pltpu.PrefetchScalarGridSpec(
        num_scalar_prefetch=2,
        grid=(n_blocks,),
        in_specs=[
            pl.BlockSpec((rows * TOK_SUB, LANES), lambda i, be, nu: (jnp.minimum(i, jnp.maximum(nu[0] - 1, 0)), 0)),
            pl.BlockSpec((1, 1, D, FF), lambda i, be, nu: (layer, be[i], 0, 0)),
            pl.BlockSpec((1, 1, D, FF), lambda i, be, nu: (layer, be[i], 0, 0)),
            pl.BlockSpec((1, 1, FF, D), lambda i, be, nu: (layer, be[i], 0, 0)),
        ],
        out_specs=pl.BlockSpec((rows * TOK_SUB, LANES), lambda i, be, nu: (i, 0)),
        scratch_shapes=[
            pltpu.VMEM((D, FF), BF16),
            pltpu.VMEM((D, FF), BF16),
            pltpu.VMEM((FF, D), BF16),
        ],
    )
    return pl.pallas_call(
        _expert_kernel,
        grid_spec=grid_spec,
        out_shape=jax.ShapeDtypeStruct((n_blocks * rows * TOK_SUB, LANES), F32),
        compiler_params=_cparams(("arbitrary",)),
        name="expert_mlp",
    )(block_expert, n_used, xs, w_gate, w_up, w_down)


def _combine_kernel(pos_ref, posn_ref, y_hbm, x_ref, route_ref, mod_ref, o_ref, ybuf, sem):
    i = pl.program_id(0)
    n = pl.num_programs(0)
    slot = i % 2
    tm = TM_COMB

    @pl.when(i == 0)
    def _first():
        _start_tokens(pos_ref, y_hbm, ybuf, 0, sem, 2 * tm)

    @pl.when(i + 1 < n)
    def _prefetch():
        _start_tokens(posn_ref, y_hbm, ybuf, 1 - slot, sem, 2 * tm)

    _wait_tokens(y_hbm, ybuf, slot, sem, 2 * tm)
    route = route_ref[...]
    yb = ybuf.at[slot]
    y = route[:, 4:5] * _from_token_major(yb, 0, tm) + route[:, 5:6] * _from_token_major(yb, tm, tm)
    o_ref[...] = x_ref[...] + mod_ref[0, 5:6, :] * y


def _combine_call(pos2, y_rows, x2, route, mod, seq):
    N, D = x2.shape
    tm = TM_COMB
    tpb = seq // tm
    nt = N // tm
    row = lambda i: (i, 0)
    return pl.pallas_call(
        _combine_kernel,
        grid=(nt,),
        in_specs=[
            pl.BlockSpec((1, 1, 2 * tm), lambda i: (i, 0, 0), memory_space=pltpu.SMEM),
            pl.BlockSpec((1, 1, 2 * tm), lambda i: (jnp.minimum(i + 1, nt - 1), 0, 0), memory_space=pltpu.SMEM),
            pl.BlockSpec(memory_space=pl.ANY),
            pl.BlockSpec((tm, D), row),
            pl.BlockSpec((tm, LANES), row),
            pl.BlockSpec((1, 6, D), lambda i: (i // tpb, 0, 0)),
        ],
        out_specs=pl.BlockSpec((tm, D), row),
        out_shape=jax.ShapeDtypeStruct((N, D), F32),
        scratch_shapes=[pltpu.VMEM((2, 2 * tm * TOK_SUB, LANES), F32), pltpu.SemaphoreType.DMA((2,))],
        compiler_params=_cparams(("arbitrary",)),
        name="moe_combine",
    )(pos2, pos2, y_rows, x2, route, mod)


def _rope_lane_tables(positions, rot_dim, head_w, n_rep):
    half = rot_dim // 2
    inv = jnp.power(jnp.float32(ROPE_THETA), -2.0 * jnp.arange(half, dtype=jnp.float32) / rot_dim)
    ang = positions.astype(jnp.float32)[..., None] * inv
    n = positions.shape[0] * positions.shape[1]
    cos, sin = jnp.cos(ang).reshape(n, half), jnp.sin(ang).reshape(n, half)
    lane = np.arange(head_w * n_rep) % head_w
    sel = (lane[None, :] % half == np.arange(half)[:, None]) & (lane[None, :] < rot_dim)
    sign = np.where(lane < half, -1.0, 1.0)[None, :]
    hi = lax.Precision.HIGHEST
    cos_f = jnp.dot(cos, jnp.asarray(sel, F32), precision=hi) + jnp.asarray(lane >= rot_dim, F32)[None, :]
    sin_s = jnp.dot(sin, jnp.asarray(sel * sign, F32), precision=hi)
    return cos_f, sin_s


def _layer_weights(l, w_in, q_norm, k_norm, conv_w, conv_b, conv_ln_g, conv_ln_b, conv_pw_w, conv_pw_b,
                   pool_w, pool_b, pool_scale, sgu_ln_g, sgu_ln_b, sgu_w, sgu_b, out_norm, w_out, norm2,
                   w_rg, b_rg, w_re, b_re):
    G = GROUP_W
    D = D_MODEL
    pts = np.cumsum(IN_SPLITS)[:-1].tolist()
    wq, wk, wv, wiq, wik, wiw, wb, wc, wd = jnp.split(w_in[l], pts, axis=-1)
    wiw_p = jnp.pad(wiw, ((0, 0), (0, LANES - IDX_HEADS)))
    w_in_p = jnp.concatenate([wq, wk, wv, wiq, jnp.tile(wik, (1, IDX_HEADS)), wiw_p, wb, wc, wd], axis=-1).astype(BF16)
    npool = len(POOL_WINDOWS)
    pool_bd = jnp.zeros((G, G), F32)
    for g in range(npool):
        pool_bd = lax.dynamic_update_slice(pool_bd, pool_w[l, g], (g * POOL_CH, g * POOL_CH))
    sgu_bias = jnp.repeat(sgu_b[l].T, G // SGU_HEADS, axis=1)
    w_router = jnp.concatenate([w_rg[l], w_re[l].reshape(D, N_EXPERTS),
                                jnp.zeros((D, LANES - N_EXP_GROUPS - N_EXPERTS), F32)], axis=-1).astype(BF16)
    b_router = jnp.concatenate([b_rg[l], b_re[l].reshape(N_EXPERTS),
                                jnp.zeros((LANES - N_EXP_GROUPS - N_EXPERTS,), F32)]).reshape(1, LANES)
    r1 = lambda a: a.reshape(1, -1)
    return dict(
        w_in=w_in_p,
        q_norm=jnp.tile(q_norm[l], ATT_HEADS).reshape(1, G), k_norm=jnp.tile(k_norm[l], ATT_HEADS).reshape(1, G),
        conv_w=conv_w[l], conv_b=r1(conv_b[l]), conv_ln_g=r1(conv_ln_g[l]), conv_ln_b=r1(conv_ln_b[l]),
        conv_pw_w=conv_pw_w[l].astype(BF16), conv_pw_b=r1(conv_pw_b[l]),
        pool_w=pool_bd.astype(BF16), pool_b=r1(pool_b[l]), pool_scale=r1(pool_scale[l]),
        sgu_ln_g=r1(sgu_ln_g[l]), sgu_ln_b=r1(sgu_ln_b[l]), sgu_w=sgu_w[l], sgu_b=sgu_bias,
        out_norm=r1(out_norm[l]), w_out=w_out[l].astype(BF16), norm2=r1(norm2[l]),
        w_router=w_router, b_router=b_router,
    )


def _dispatch_tables(route, cnt, n_tokens):
    rows_blk = MOE_ROWS
    e = route[:, 0:2].astype(jnp.int32)
    rank = route[:, 2:4].astype(jnp.int32)
    counts = cnt[0, :N_EXPERTS].astype(jnp.int32)
    padded = (counts + rows_blk - 1) // rows_blk * rows_blk
    pad_end = jnp.cumsum(padded)
    pad_start = pad_end - padded
    pos = pad_start[e] + rank
    m = n_tokens * MOE_TOPK
    n_blocks = (m + N_EXPERTS * (rows_blk - 1) + rows_blk - 1) // rows_blk
    n_used = (pad_end[-1] // rows_blk).astype(jnp.int32).reshape(1)
    blk_row0 = jnp.arange(n_blocks, dtype=jnp.int32) * rows_blk
    block_expert = jnp.minimum(jnp.sum((pad_end[None, :] <= blk_row0[:, None]).astype(jnp.int32), axis=1),
                               N_EXPERTS - 1)
    pos2 = (pos * TOK_SUB).reshape(n_tokens // TM_COMB, TM_COMB, MOE_TOPK).transpose(0, 2, 1).reshape(
        -1, 1, MOE_TOPK * TM_COMB)
    tail = jnp.where(counts > 0, (pad_end - rows_blk) * TOK_SUB, -1).astype(jnp.int32)
    return block_expert, n_used, tail, pos2, n_blocks * rows_blk


def kernel(x, c, positions, w_ada, b_ada, norm1, w_in, q_norm, k_norm, conv_w, conv_b, conv_ln_g, conv_ln_b, conv_pw_w, conv_pw_b, pool_w, pool_b, pool_scale, sgu_ln_g, sgu_ln_b, sgu_w, sgu_b, out_norm, w_out, norm2, w_rg, b_rg, w_re, b_re, w_gate, w_up, w_down):
    B, S, D = x.shape
    N = B * S
    assert D == D_MODEL and S % TM_PROJ == 0 and S % TK_ATT == 0 and N % TM_COMB == 0
    depth = w_ada.shape[0]
    cos_a, sin_a = _rope_lane_tables(positions, ROPE_DIM, ATT_HEAD_DIM, ATT_HEADS)
    cos_i, sin_i = _rope_lane_tables(positions, IDX_ROPE_DIM, IDX_DIM, IDX_HEADS)
    c_pad = jnp.pad(c, ((0, (-B) % 8), (0, 0)))
    mod_all = _ada_call(c_pad, w_ada, b_ada)
    x2 = x.reshape(N, D)
    for l in range(depth):
        lw = _layer_weights(l, w_in, q_norm, k_norm, conv_w, conv_b, conv_ln_g, conv_ln_b, conv_pw_w, conv_pw_b,
                            pool_w, pool_b, pool_scale, sgu_ln_g, sgu_ln_b, sgu_w, sgu_b, out_norm, w_out, norm2,
                            w_rg, b_rg, w_re, b_re)
        mod = mod_all[l, :B].reshape(B, 6, D)
        q, k, v, iq, ik, iw, pb, pc, pd = _proj_call(
            x2, mod, norm1[l].reshape(1, D), lw["w_in"], lw["q_norm"], lw["k_norm"], cos_a, sin_a, cos_i, sin_i, S)
        oa = _dsa_call(q, k, v, iq, ik, iw, B, S)
        x_mid, h2, route, cnt = _mix_call(x2, oa, pb, pc, pd, mod, lw, S)
        block_expert, n_used, tail, pos2, n_rows = _dispatch_tables(route, cnt, N)
        xs = _dispatch_call(tail, n_used, pos2, h2, n_rows)
        y_rows = _expert_call(block_expert, n_used, xs, w_gate, w_up, w_down, l)
        x2 = _combine_call(pos2, y_rows, x_mid, route, mod, S)
    return x2.reshape(B, S, D)
```

```python
import functools

import numpy as np
import jax
import jax.numpy as jnp
from jax import lax
from jax.experimental import pallas as pl
from jax.experimental.pallas import tpu as pltpu

F32 = jnp.float32
BF16 = jnp.bfloat16
NEG_INF = float("-inf")

D_MODEL = 1024
DEPTH = 2
CHUNK = 64
N_MIXERS = 4
GROUP_W = D_MODEL // N_MIXERS
ATT_HEAD_DIM = 64
ATT_HEADS = GROUP_W // ATT_HEAD_DIM
ROPE_DIM = ATT_HEAD_DIM // 4
ROPE_THETA = 500000.0
IDX_HEADS = 4
IDX_DIM = 32
IDX_ROPE_DIM = IDX_DIM // 4
TOPK_MAX = 256
CONV_WIDTH = 31
CONV_GROUPS = 4
POOL_WINDOWS = (2, 4, 8, 16)
POOL_CH = GROUP_W // 4
SGU_CHUNK = 128
SGU_HEADS = 4
N_EXP_GROUPS = 4
EXP_PER_GROUP = 8
N_EXPERTS = N_EXP_GROUPS * EXP_PER_GROUP
EXPERT_FF = 512
MOE_TOPK = 2
EPS = 1e-6
IN_SPLITS = (GROUP_W, GROUP_W, GROUP_W, IDX_HEADS * IDX_DIM, IDX_DIM, IDX_HEADS, 2 * GROUP_W, GROUP_W, 2 * GROUP_W)

LANES = 128
HALO = 32
W_IN_COLS = 3 * GROUP_W + 3 * LANES + 5 * GROUP_W

TM_PROJ = 512
TQ_ATT = 256
TK_ATT = 512
TM_MIX = 256
MOE_ROWS = 256
TM_COMB = 256
BISECT_ITERS = 18
VMEM_LIMIT = 56 * 1024 * 1024


def _cparams(sem):
    return pltpu.CompilerParams(dimension_semantics=sem, vmem_limit_bytes=VMEM_LIMIT)


def _lane_iota(shape):
    return lax.broadcasted_iota(jnp.int32, shape, len(shape) - 1)


def _seg_mean(y, width):
    shift = int(np.log2(width))
    grp = _lane_iota(y.shape) >> shift
    out = jnp.zeros_like(y)
    for g in range(y.shape[-1] // width):
        msk = grp == g
        s = jnp.sum(jnp.where(msk, y, 0.0), axis=-1, keepdims=True)
        out = jnp.where(msk, s, out)
    return out * (1.0 / width)


def _rope(x, cos_f, sin_s, head_w, half):
    c = x.shape[-1]
    lane = _lane_iota(x.shape) & (head_w - 1)
    partner = jnp.where(lane < half, pltpu.roll(x, c - half, 1), pltpu.roll(x, half, 1))
    return x * cos_f + partner * sin_s


def _silu(x):
    return x * jax.nn.sigmoid(x)


def _ada_kernel(c_ref, w_ref, b_ref, o_ref):
    ca = _silu(c_ref[...])
    o_ref[0] = jnp.dot(ca.astype(BF16), w_ref[0].astype(BF16), preferred_element_type=F32) + b_ref[0]


def _ada_call(c_pad, w_ada, b_ada):
    L, D, D6 = w_ada.shape
    rows = c_pad.shape[0]
    tn = D
    return pl.pallas_call(
        _ada_kernel,
        grid=(L, D6 // tn),
        in_specs=[
            pl.BlockSpec((rows, D), lambda l, j: (0, 0)),
            pl.BlockSpec((1, D, tn), lambda l, j: (l, 0, j)),
            pl.BlockSpec((1, 1, tn), lambda l, j: (l, 0, j)),
        ],
        out_specs=pl.BlockSpec((1, rows, tn), lambda l, j: (l, 0, j)),
        out_shape=jax.ShapeDtypeStruct((L, rows, D6), F32),
        compiler_params=_cparams(("arbitrary", "arbitrary")),
        name="ada_mod",
    )(c_pad, w_ada, b_ada.reshape(L, 1, D6))


def _proj_kernel(x_ref, mod_ref, n1_ref, w_ref, qn_ref, kn_ref, cosa_ref, sina_ref, cosi_ref, sini_ref,
                 q_ref, k_ref, v_ref, iq_ref, ik_ref, iw_ref, pb_ref, pc_ref, pd_ref):
    x = x_ref[...]
    ms = jnp.mean(x * x, axis=-1, keepdims=True)
    h = (x * lax.rsqrt(ms + EPS)) * n1_ref[...]
    h = h * (1.0 + mod_ref[0, 1:2, :]) + mod_ref[0, 0:1, :]
    proj = jnp.dot(h.astype(BF16), w_ref[...], preferred_element_type=F32)
    G = GROUP_W
    cos_a, sin_a = cosa_ref[...], sina_ref[...]
    cos_i, sin_i = cosi_ref[...], sini_ref[...]

    def qk(t, g_ref):
        tn = (t * lax.rsqrt(_seg_mean(t * t, ATT_HEAD_DIM) + EPS)) * g_ref[...]
        return _rope(tn, cos_a, sin_a, ATT_HEAD_DIM, ROPE_DIM // 2)

    q_ref[...] = (qk(proj[:, 0:G], qn_ref) * (ATT_HEAD_DIM ** -0.5)).T.astype(BF16)
    k_ref[...] = qk(proj[:, G:2 * G], kn_ref).astype(BF16)
    v_ref[0] = proj[:, 2 * G:3 * G].T.astype(BF16)
    o = 3 * G
    iq_ref[...] = _rope(proj[:, o:o + LANES], cos_i, sin_i, IDX_DIM, IDX_ROPE_DIM // 2).T.astype(BF16)
    ik_ref[...] = _rope(proj[:, o + LANES:o + 2 * LANES], cos_i, sin_i, IDX_DIM, IDX_ROPE_DIM // 2).astype(BF16)
    iw_ref[...] = (proj[:, o + 2 * LANES:o + 3 * LANES] * (IDX_HEADS ** -0.5)).T[0:8, :]
    o += 3 * LANES
    pb_ref[...] = proj[:, o:o + 2 * G]
    pc_ref[...] = proj[:, o + 2 * G:o + 3 * G]
    pd_ref[...] = proj[:, o + 3 * G:o + 5 * G]


def _proj_call(x2, mod, n1, w_in_p, qn_t, kn_t, cos_a, sin_a, cos_i, sin_i, seq):
    N, D = x2.shape
    tm = TM_PROJ
    tpb = seq // tm
    G = GROUP_W
    assert tm == TK_ATT
    row = lambda i: (i, 0)
    col = lambda i: (0, i)
    fixed = lambda i: (0, 0)
    sds = jax.ShapeDtypeStruct
    out_specs = [
        pl.BlockSpec((G, tm), col),
        pl.BlockSpec((tm, G), row),
        pl.BlockSpec((1, G, tm), lambda i: (i, 0, 0)),
        pl.BlockSpec((LANES, tm), col),
        pl.BlockSpec((tm, LANES), row),
        pl.BlockSpec((8, tm), col),
        pl.BlockSpec((tm, 2 * G), row),
        pl.BlockSpec((tm, G), row),
        pl.BlockSpec((tm, 2 * G), row),
    ]
    out_shape = [sds((G, N), BF16), sds((N, G), BF16), sds((N // tm, G, tm), BF16), sds((LANES, N), BF16),
                 sds((N, LANES), BF16), sds((8, N), F32), sds((N, 2 * G), F32), sds((N, G), F32), sds((N, 2 * G), F32)]
    return pl.pallas_call(
        _proj_kernel,
        grid=(N // tm,),
        in_specs=[
            pl.BlockSpec((tm, D), row),
            pl.BlockSpec((1, 6, D), lambda i: (i // tpb, 0, 0)),
            pl.BlockSpec((1, D), fixed),
            pl.BlockSpec((D, W_IN_COLS), fixed),
            pl.BlockSpec((1, G), fixed),
            pl.BlockSpec((1, G), fixed),
            pl.BlockSpec((tm, G), row),
            pl.BlockSpec((tm, G), row),
            pl.BlockSpec((tm, LANES), row),
            pl.BlockSpec((tm, LANES), row),
        ],
        out_specs=out_specs,
        out_shape=out_shape,
        compiler_params=_cparams(("parallel",)),
        name="norm_in_proj",
    )(x2, mod, n1, w_in_p, qn_t, kn_t, cos_a, sin_a, cos_i, sin_i)


def _pair_rhs(xt, head_rows, h0):
    head = lax.broadcasted_iota(jnp.int32, xt.shape, 0) >> int(np.log2(head_rows))
    zero = jnp.zeros_like(xt)
    return jnp.concatenate([jnp.where(head == h0, xt, zero), jnp.where(head == h0 + 1, xt, zero)], axis=1)


def _dsa_kernel(q_ref, k_ref, v_ref, iq_ref, ik_ref, iw_ref, ltri_ref, o_ref, sc_ref, lge_ref, lgo_ref, *, topk):
    tq, tk = TQ_ATT, TK_ATT
    i = pl.program_id(1)
    q0 = i * tq
    n_kv = (q0 + tq + tk - 1) // tk
    kf = float(topk)

    q_pos = _lane_iota((1, tq)) + q0
    key_end = ((q_pos >> 6) + 1) << 6
    key_i = lax.broadcasted_iota(jnp.int32, (tk, tq), 0)

    def fold8(x, op):
        n_acc = 4
        accs = [x[r * 8:(r + 1) * 8, :] for r in range(n_acc)]
        for r in range(n_acc, tk // 8):
            accs[r % n_acc] = op(accs[r % n_acc], x[r * 8:(r + 1) * 8, :])
        return op(op(accs[0], accs[1]), op(accs[2], accs[3]))

    iqt = iq_ref[...]
    iq_pairs = [_pair_rhs(iqt, IDX_DIM, h0) for h0 in range(0, IDX_HEADS, 2)]
    iw_h = [iw_ref[h:h + 1, :] for h in range(IDX_HEADS)]

    def score_body(kc, carry):
        hi8, lo8 = carry
        ikc = ik_ref[pl.ds(pl.multiple_of(kc * tk, tk), tk), :]
        s = jnp.zeros((tk, tq), F32)
        for pi, rhs in enumerate(iq_pairs):
            d2 = jnp.dot(ikc, rhs, preferred_element_type=F32)
            for j in range(2):
                d = d2[:, j * tq:(j + 1) * tq]
                s = s + jnp.maximum(d * (IDX_DIM ** -0.5), 0.0) * iw_h[2 * pi + j]
        adm = key_i + kc * tk < key_end
        s_top = jnp.where(adm, s, NEG_INF)
        sc_ref[kc] = s_top
        return (jnp.maximum(hi8, fold8(s_top, jnp.maximum)),
                jnp.minimum(lo8, fold8(jnp.where(adm, s, jnp.inf), jnp.minimum)))

    hi8, lo8 = lax.fori_loop(0, n_kv, score_body,
                             (jnp.full((8, tq), NEG_INF, F32), jnp.full((8, tq), jnp.inf, F32)))
    col_max = jnp.max(hi8, axis=0, keepdims=True)
    col_min = jnp.min(lo8, axis=0, keepdims=True)

    def reduce_chunks(fns, init, combine, fold):
        def body(kc, parts):
            blk = sc_ref[kc]
            return tuple(combine(p, fold8(fn(blk), combine)) for p, fn in zip(parts, fns))
        parts = lax.fori_loop(0, n_kv, body, tuple(jnp.full((8, tq), init, F32) for _ in fns))
        return [fold(p, axis=0, keepdims=True) for p in parts]

    def count(*inds):
        return reduce_chunks(inds, 0.0, jnp.add, jnp.sum)

    def col_maximum(val):
        return reduce_chunks((val,), NEG_INF, jnp.maximum, jnp.max)[0]

    small = key_end <= topk

    @pl.when(q0 + tq > topk)
    def _select():
        def bis_body(_, c):
            lo, hi = c
            mid = jnp.where(hi == jnp.inf, col_max, lo + (hi - lo) * 0.5)
            ge = count(lambda b: jnp.where(b >= mid, 1.0, 0.0))[0] >= kf
            return jnp.where(ge, mid, lo), jnp.where(ge, hi, mid)

        lo, hi = lax.fori_loop(0, BISECT_ITERS, bis_body, (col_min, jnp.full((1, tq), jnp.inf, F32)))

        def sd_cond(c):
            return c[0] > 0.0

        def sd_body(c):
            _, hi, thr, done = c
            cand = col_maximum(lambda b: jnp.where(b < hi, b, NEG_INF))
            ok = count(lambda b: jnp.where(b >= cand, 1.0, 0.0))[0] >= kf
            thr = jnp.where(done > 0.0, thr, cand)
            hi = jnp.where(done > 0.0, hi, cand)
            done = jnp.where(ok, 1.0, done)
            return jnp.sum(1.0 - done), hi, thr, done

        done0 = jnp.where(small, 1.0, 0.0)
        n0 = jnp.sum(1.0 - done0)
        _, _, thr, _ = lax.while_loop(sd_cond, sd_body, (n0, hi, jnp.full((1, tq), NEG_INF, F32), done0))
        thr = jnp.where(small, NEG_INF, thr)

        n_above, n_tied = count(lambda b: jnp.where(b > thr, 1.0, 0.0), lambda b: jnp.where(b == thr, 1.0, 0.0))
        need = kf - n_above
        excess = jnp.sum(jnp.where(jnp.where(small, 0.0, n_tied) > need, 1.0, 0.0))

        @pl.when(excess <= 0.0)
        def _keep_all_ties():
            def bias_body(kc, carry):
                blk = sc_ref[kc]
                sc_ref[kc] = jnp.where(blk == NEG_INF, NEG_INF, jnp.where(blk >= thr, 0.0, NEG_INF))
                return carry
            lax.fori_loop(0, n_kv, bias_body, 0)

        @pl.when(excess > 0.0)
        def _rank_ties():
            half = tk // 2
            ltri_top = ltri_ref[0:half, 0:half]
            ltri_bot = ltri_ref[half:tk, :]

            def bias_body(kc, seen):
                blk = sc_ref[kc]
                tied = jnp.where(blk == thr, 1.0, 0.0)
                tied16 = tied.astype(BF16)
                rank = jnp.concatenate([jnp.dot(ltri_top, tied16[0:half, :], preferred_element_type=F32),
                                        jnp.dot(ltri_bot, tied16, preferred_element_type=F32)], axis=0) + seen
                tie = jnp.where(blk == thr, jnp.where(rank <= need, 0.0, NEG_INF), NEG_INF)
                bias = jnp.where(blk > thr, 0.0, tie)
                sc_ref[kc] = jnp.where(blk == NEG_INF, NEG_INF, bias)
                return seen + jnp.sum(fold8(tied, jnp.add), axis=0, keepdims=True)

            lax.fori_loop(0, n_kv, bias_body, jnp.zeros((1, tq), F32))

    @pl.when(q0 + tq <= topk)
    def _all():
        def bias_body(kc, carry):
            sc_ref[kc] = jnp.where(sc_ref[kc] == NEG_INF, NEG_INF, 0.0)
            return carry
        lax.fori_loop(0, n_kv, bias_body, 0)

    qt = q_ref[...]
    q_pairs = [_pair_rhs(qt, ATT_HEAD_DIM, h0) for h0 in range(0, ATT_HEADS, 2)]
    dh = ATT_HEAD_DIM

    def store_logits(buf, kc):
        kblk = k_ref[pl.ds(pl.multiple_of(kc * tk, tk), tk), :]
        bias = sc_ref[kc]
        for pi, rhs in enumerate(q_pairs):
            s2 = jnp.dot(kblk, rhs, preferred_element_type=F32)
            for j in range(2):
                buf[2 * pi + j] = s2[:, j * tq:(j + 1) * tq] + bias

    def absorb(buf, kc, state):
        ms, ls, accs = state
        vt = v_ref[kc]
        ms_n, ls_n, accs_n = [], [], []
        for h in range(ATT_HEADS):
            s = buf[h]
            m_new = jnp.maximum(ms[h], jnp.max(fold8(s, jnp.maximum), axis=0, keepdims=True))
            m_safe = jnp.where(m_new == NEG_INF, 0.0, m_new)
            alpha = jnp.exp(ms[h] - m_safe)
            p = jnp.exp(s - m_safe)
            ls_n.append(alpha * ls[h] + fold8(p, jnp.add))
            pv = jnp.dot(vt[h * dh:(h + 1) * dh, :], p.astype(BF16), preferred_element_type=F32)
            accs_n.append(alpha * accs[h] + pv)
            ms_n.append(m_new)
        return tuple(ms_n), tuple(ls_n), tuple(accs_n)

    def pair_body(jj, state):
        s = 2 * jj + 1
        store_logits(lgo_ref, s)
        state = absorb(lge_ref, s - 1, state)
        store_logits(lge_ref, s + 1)
        return absorb(lgo_ref, s, state)

    state = (tuple(jnp.full((1, tq), NEG_INF, F32) for _ in range(ATT_HEADS)),
             tuple(jnp.zeros((8, tq), F32) for _ in range(ATT_HEADS)),
             tuple(jnp.zeros((dh, tq), F32) for _ in range(ATT_HEADS)))
    store_logits(lge_ref, 0)
    n_pairs = (n_kv - 1) // 2
    state = lax.fori_loop(0, n_pairs, pair_body, state)
    last_even = 2 * n_pairs

    def tail_two(state):
        store_logits(lgo_ref, last_even + 1)
        return absorb(lgo_ref, last_even + 1, absorb(lge_ref, last_even, state))

    def tail_one(state):
        return absorb(lge_ref, last_even, state)

    _, ls, accs = lax.cond(n_kv - 1 - last_even > 0, tail_two, tail_one, state)
    out_t = jnp.concatenate([accs[h] / jnp.sum(ls[h], axis=0, keepdims=True) for h in range(ATT_HEADS)], axis=0)
    o_ref[...] = out_t.T


def _dsa_call(qt, k, vt, iqt, ik, iwt, batch, seq):
    N, G = k.shape
    tq = TQ_ATT
    nq = seq // tq
    nkc = seq // TK_ATT
    topk = min(TOPK_MAX, seq // 4)
    qcol = lambda b, i: (0, b * nq + i)
    brow = lambda b, i: (b, 0)
    return pl.pallas_call(
        functools.partial(_dsa_kernel, topk=topk),
        grid=(batch, nq),
        in_specs=[
            pl.BlockSpec((G, tq), qcol),
            pl.BlockSpec((seq, G), brow),
            pl.BlockSpec((nkc, G, TK_ATT), lambda b, i: (b, 0, 0)),
            pl.BlockSpec((LANES, tq), qcol),
            pl.BlockSpec((seq, LANES), brow),
            pl.BlockSpec((8, tq), qcol),
            pl.BlockSpec((TK_ATT, TK_ATT), lambda b, i: (0, 0)),
        ],
        out_specs=pl.BlockSpec((tq, G), lambda b, i: (b * nq + i, 0)),
        out_shape=jax.ShapeDtypeStruct((N, G), F32),
        scratch_shapes=[pltpu.VMEM((nkc, TK_ATT, tq), F32), pltpu.VMEM((ATT_HEADS, TK_ATT, tq), F32),
                        pltpu.VMEM((ATT_HEADS, TK_ATT, tq), F32)],
        compiler_params=_cparams(("parallel", "arbitrary")),
        name="dsa_attention",
    )(qt, k, vt, iqt, ik, iwt, jnp.tril(jnp.ones((TK_ATT, TK_ATT), BF16)))


def _mix_kernel(x_ref, oa_ref, pb_ref, pbh_ref, pc_ref, pch_ref, pd_ref, mod_ref,
                cw_ref, cb_ref, clg_ref, clb_ref, cpw_ref, cpb_ref,
                pw_ref, pbias_ref, ps_ref, slg_ref, slb_ref, sw_ref, sb_ref,
                on_ref, wo_ref, n2_ref, wr_ref, br_ref,
                xo_ref, h2_ref, route_ref, cnt_ref,
                ypad_ref, ppad_ref, s2_ref, s4_ref, s8_ref, run_ref, cph_ref, *, tiles_per_batch):
    tm = TM_MIX
    G = GROUP_W
    i = pl.program_id(0)
    t_in_b = i % tiles_per_batch
    first = t_in_b == 0
    lane_g = _lane_iota((tm, G))

    def glu(pb):
        return pb[:, 0:G] * jax.nn.sigmoid(pb[:, G:2 * G])

    ypad_ref[0:HALO, :] = jnp.where(first, 0.0, glu(pbh_ref[...]))
    ypad_ref[HALO:HALO + tm, :] = glu(pb_ref[...])
    acc = jnp.zeros((tm, G), F32)
    first_off = HALO - (CONV_WIDTH - 1)
    for phase in range(8):
        offs = [o for o in range(first_off, HALO + 1) if o % 8 == phase]
        if not offs:
            continue
        span = offs[-1] - offs[0] + tm
        cph_ref[phase, 0:span, :] = ypad_ref[offs[0]:offs[0] + span, :]
        for o in offs:
            acc = acc + cw_ref[o - first_off:o - first_off + 1, :] * cph_ref[phase, o - offs[0]:o - offs[0] + tm, :]
    y = acc + cb_ref[...]
    gw = G // CONV_GROUPS
    mu = _seg_mean(y, gw)
    yc = y - mu
    var = _seg_mean(yc * yc, gw)
    y = (yc * lax.rsqrt(var + EPS)) * clg_ref[...] + clb_ref[...]
    o_b = jnp.dot(_silu(y).astype(BF16), cpw_ref[...], preferred_element_type=F32) + cpb_ref[...]

    p = pc_ref[...]
    ppad_ref[0:HALO, :] = jnp.where(first, 0.0, pch_ref[...])
    ppad_ref[HALO:HALO + tm, :] = p
    n8 = tm + HALO - 8
    s2_ref[8:8 + n8, :] = ppad_ref[8:8 + n8, :] + ppad_ref[7:7 + n8, :]
    n16 = tm + HALO - 16
    s4_ref[16:16 + n16, :] = s2_ref[16:16 + n16, :] + s2_ref[14:14 + n16, :]
    n24 = tm + HALO - 24
    s8_ref[24:24 + n24, :] = s4_ref[24:24 + n24, :] + s4_ref[20:20 + n24, :]
    s2 = s2_ref[HALO:HALO + tm, :]
    s4 = s4_ref[HALO:HALO + tm, :]
    s8 = s8_ref[HALO:HALO + tm, :]
    s16 = s8 + s8_ref[HALO - 8:HALO - 8 + tm, :]
    pgrp = lane_g >> 6
    wsum = jnp.where(pgrp == 0, s2, jnp.where(pgrp == 1, s4, jnp.where(pgrp == 2, s8, s16)))
    wlen = jnp.where(pgrp == 0, 2.0, jnp.where(pgrp == 1, 4.0, jnp.where(pgrp == 2, 8.0, 16.0)))
    tpos = (lax.broadcasted_iota(jnp.int32, (tm, G), 0) + t_in_b * tm + 1).astype(F32)
    pooled = wsum / jnp.minimum(tpos, wlen) - p
    o_c = (jnp.dot(pooled.astype(BF16), pw_ref[...], preferred_element_type=F32) + pbias_ref[...]) * ps_ref[...]

    pd = pd_ref[...]
    u, v = pd[:, 0:G], pd[:, G:2 * G]
    mu = jnp.mean(v, axis=-1, keepdims=True)
    vc = v - mu
    var = jnp.mean(vc * vc, axis=-1, keepdims=True)
    vn = ((vc * lax.rsqrt(var + EPS)) * slg_ref[...] + slb_ref[...]).astype(BF16)
    r_i = lax.broadcasted_iota(jnp.int32, (SGU_CHUNK, SGU_CHUNK), 0)
    c_i = lax.broadcasted_iota(jnp.int32, (SGU_CHUNK, SGU_CHUNK), 1)
    w_heads = [jnp.where(r_i >= c_i, sw_ref[h], 0.0).astype(BF16) for h in range(SGU_HEADS)]
    lane_c = _lane_iota((SGU_CHUNK, G)) >> 6
    mixed = []
    for n in range(tm // SGU_CHUNK):
        vch = vn[n * SGU_CHUNK:(n + 1) * SGU_CHUNK, :]
        mx = jnp.zeros((SGU_CHUNK, G), F32)
        for h in range(SGU_HEADS):
            mx = jnp.where(lane_c == h, jnp.dot(w_heads[h], vch, preferred_element_type=F32), mx)
        mixed.append(mx + sb_ref[...])
    o_d = u * jnp.concatenate(mixed, axis=0)

    proj = jnp.zeros((tm, D_MODEL), F32)
    for g, piece in enumerate((oa_ref[...], o_b, o_c, o_d)):
        ms = jnp.mean(piece * piece, axis=-1, keepdims=True)
        pn = (piece * lax.rsqrt(ms + EPS)) * on_ref[:, g * G:(g + 1) * G]
        proj = proj + jnp.dot(pn.astype(BF16), wo_ref[g * G:(g + 1) * G, :], preferred_element_type=F32)
    x_new = x_ref[...] + mod_ref[0, 2:3, :] * proj
    xo_ref[...] = x_new

    ms = jnp.mean(x_new * x_new, axis=-1, keepdims=True)
    h2 = (x_new * lax.rsqrt(ms + EPS)) * n2_ref[...]
    h2 = h2 * (1.0 + mod_ref[0, 4:5, :]) + mod_ref[0, 3:4, :]
    _to_token_major(h2_ref, h2)
    logits = jnp.dot(h2.astype(BF16), wr_ref[...], preferred_element_type=F32) + br_ref[...]
    lane = _lane_iota((tm, LANES))
    lane_f = lane.astype(F32)
    big = float(LANES)
    glog = jnp.where(lane < N_EXP_GROUPS, logits, NEG_INF)
    gmax = jnp.max(glog, axis=-1, keepdims=True)
    p_sel = 1.0 / jnp.sum(jnp.exp(glog - gmax), axis=-1, keepdims=True)
    g_idx = jnp.min(jnp.where(glog == gmax, lane_f, big), axis=-1, keepdims=True)
    e_lane = lane - N_EXP_GROUPS
    elog = jnp.where((e_lane >> 3).astype(F32) == g_idx, logits, NEG_INF)
    top1 = jnp.max(elog, axis=-1, keepdims=True)
    j1 = jnp.min(jnp.where(elog == top1, lane_f, big), axis=-1, keepdims=True)
    elog2 = jnp.where(lane_f == j1, NEG_INF, elog)
    top2 = jnp.max(elog2, axis=-1, keepdims=True)
    j2 = jnp.min(jnp.where(elog2 == top2, lane_f, big), axis=-1, keepdims=True)
    e2w = jnp.exp(top2 - top1)
    gate1 = p_sel * (1.0 / (1.0 + e2w))
    gate2 = p_sel * (e2w / (1.0 + e2w))
    e1 = j1 - float(N_EXP_GROUPS)
    e2 = j2 - float(N_EXP_GROUPS)

    @pl.when(i == 0)
    def _init():
        run_ref[...] = jnp.zeros_like(run_ref)

    onehot = jnp.where(jnp.logical_or(lane_f == e1, lane_f == e2), 1.0, 0.0)
    rr = lax.broadcasted_iota(jnp.int32, (tm, tm), 0)
    cc = lax.broadcasted_iota(jnp.int32, (tm, tm), 1)
    before = jnp.where(rr > cc, 1.0, 0.0).astype(BF16)
    prior = jnp.dot(before, onehot.astype(BF16), preferred_element_type=F32) + run_ref[0:1, :]
    rank1 = jnp.sum(jnp.where(lane_f == e1, prior, 0.0), axis=-1, keepdims=True)
    rank2 = jnp.sum(jnp.where(lane_f == e2, prior, 0.0), axis=-1, keepdims=True)
    run_new = run_ref[0:1, :] + jnp.sum(onehot, axis=0, keepdims=True)
    run_ref[...] = jnp.broadcast_to(run_new, run_ref.shape)
    cnt_ref[...] = jnp.broadcast_to(run_new, cnt_ref.shape)
    route = jnp.where(lane == 0, e1, jnp.where(lane == 1, e2, jnp.where(lane == 2, rank1, jnp.where(
        lane == 3, rank2, jnp.where(lane == 4, gate1, jnp.where(lane == 5, gate2, 0.0))))))
    route_ref[...] = route


def _mix_call(x2, oa, pb, pc, pd, mod, lw, seq):
    N, D = x2.shape
    tm = TM_MIX
    tpb = seq // tm
    G = GROUP_W
    row = lambda i: (i, 0)
    halo = lambda i: (jnp.maximum(i * (tm // HALO) - 1, 0), 0)
    fixed2 = lambda i: (0, 0)
    fixed3 = lambda i: (0, 0, 0)
    params = [lw["conv_w"], lw["conv_b"], lw["conv_ln_g"], lw["conv_ln_b"], lw["conv_pw_w"], lw["conv_pw_b"],
              lw["pool_w"], lw["pool_b"], lw["pool_scale"], lw["sgu_ln_g"], lw["sgu_ln_b"], lw["sgu_w"], lw["sgu_b"],
              lw["out_norm"], lw["w_out"], lw["norm2"], lw["w_router"], lw["b_router"]]
    param_specs = [pl.BlockSpec(p.shape, fixed3 if p.ndim == 3 else fixed2) for p in params]
    return pl.pallas_call(
        functools.partial(_mix_kernel, tiles_per_batch=tpb),
        grid=(N // tm,),
        in_specs=[
            pl.BlockSpec((tm, D), row),
            pl.BlockSpec((tm, G), row),
            pl.BlockSpec((tm, 2 * G), row),
            pl.BlockSpec((HALO, 2 * G), halo),
            pl.BlockSpec((tm, G), row),
            pl.BlockSpec((HALO, G), halo),
            pl.BlockSpec((tm, 2 * G), row),
            pl.BlockSpec((1, 6, D), lambda i: (i // tpb, 0, 0)),
        ] + param_specs,
        out_specs=[
            pl.BlockSpec((tm, D), row),
            pl.BlockSpec((tm * TOK_SUB, LANES), row),
            pl.BlockSpec((tm, LANES), row),
            pl.BlockSpec((8, LANES), fixed2),
        ],
        out_shape=[
            jax.ShapeDtypeStruct((N, D), F32),
            jax.ShapeDtypeStruct((N * TOK_SUB, LANES), F32),
            jax.ShapeDtypeStruct((N, LANES), F32),
            jax.ShapeDtypeStruct((8, LANES), F32),
        ],
        scratch_shapes=[pltpu.VMEM((tm + HALO, G), F32) for _ in range(5)] + [
            pltpu.VMEM((8, LANES), F32), pltpu.VMEM((8, tm + HALO, G), F32)],
        compiler_params=_cparams(("arbitrary",)),
        name="mixers_out_router",
    )(x2, oa, pb, pb, pc, pc, pd, mod, *params)


TOK_SUB = D_MODEL // LANES
ROW_DMA_UNROLL = 8


def _to_token_major(ref, x):
    n = x.shape[0]
    for j in range(TOK_SUB):
        ref[pl.ds(j, n, stride=TOK_SUB), :] = x[:, j * LANES:(j + 1) * LANES]


def _from_token_major(ref, tok0, n):
    return jnp.concatenate([ref[pl.ds(tok0 * TOK_SUB + j, n, stride=TOK_SUB), :] for j in range(TOK_SUB)], axis=1)


def _token_copy(src_hbm, row8, dst_ref, slot, r, sem):
    src = src_hbm.at[pl.ds(pl.multiple_of(row8, TOK_SUB), TOK_SUB), :]
    dst = dst_ref.at[slot, pl.ds(pl.multiple_of(r * TOK_SUB, TOK_SUB), TOK_SUB), :]
    return pltpu.make_async_copy(src, dst, sem.at[slot])


def _start_tokens(idx_ref, src_hbm, dst_ref, slot, sem, n):
    def body(g, carry):
        for j in range(ROW_DMA_UNROLL):
            r = g * ROW_DMA_UNROLL + j
            _token_copy(src_hbm, idx_ref[0, 0, r], dst_ref, slot, r, sem).start(priority=j % 2)
        return carry
    lax.fori_loop(0, n // ROW_DMA_UNROLL, body, 0)


def _wait_tokens(src_hbm, dst_ref, slot, sem, n):
    pltpu.make_async_copy(src_hbm.at[pl.ds(0, n * TOK_SUB), :], dst_ref.at[slot], sem.at[slot]).wait()


def _dispatch_kernel(tail_ref, nu_ref, pos_ref, h_ref, xs_hbm, zbuf, zsem, sem):
    i = pl.program_id(0)
    tm = TM_COMB
    blk = MOE_ROWS * TOK_SUB
    n_blocks = xs_hbm.shape[0] // blk

    def zero_block(row0):
        return pltpu.make_async_copy(zbuf, xs_hbm.at[pl.ds(pl.multiple_of(row0, TOK_SUB), blk), :], zsem.at[0])

    @pl.when(i == 0)
    def _zero_fill():
        zbuf[...] = jnp.zeros_like(zbuf)
        n_used = nu_ref[0]
        for e in range(N_EXPERTS):
            @pl.when(tail_ref[e] >= 0)
            def _():
                zero_block(tail_ref[e]).start()
        lax.fori_loop(n_used, n_blocks, lambda b, c: (zero_block(b * blk).start(), c)[1], 0)
        for e in range(N_EXPERTS):
            @pl.when(tail_ref[e] >= 0)
            def _():
                zero_block(tail_ref[e]).wait()
        lax.fori_loop(n_used, n_blocks, lambda b, c: (zero_block(b * blk).wait(), c)[1], 0)

    def body(g, carry):
        for j in range(ROW_DMA_UNROLL):
            r = g * ROW_DMA_UNROLL + j
            t = jnp.where(r < tm, r, r - tm)
            src = h_ref.at[pl.ds(pl.multiple_of(t * TOK_SUB, TOK_SUB), TOK_SUB), :]
            dst = xs_hbm.at[pl.ds(pl.multiple_of(pos_ref[0, 0, r], TOK_SUB), TOK_SUB), :]
            pltpu.make_async_copy(src, dst, sem.at[0]).start(priority=j % 2)
        return carry
    lax.fori_loop(0, 2 * tm // ROW_DMA_UNROLL, body, 0)
    for _ in range(MOE_TOPK):
        pltpu.make_async_copy(h_ref, xs_hbm.at[pl.ds(0, tm * TOK_SUB), :], sem.at[0]).wait()


def _dispatch_call(tail, n_used, pos2, h2, n_rows):
    nt = pos2.shape[0]
    tm = TM_COMB
    grid_spec = pltpu.PrefetchScalarGridSpec(
        num_scalar_prefetch=2,
        grid=(nt,),
        in_specs=[
            pl.BlockSpec((1, 1, 2 * tm), lambda i, *_: (i, 0, 0), memory_space=pltpu.SMEM),
            pl.BlockSpec((tm * TOK_SUB, LANES), lambda i, *_: (i, 0)),
        ],
        out_specs=pl.BlockSpec(memory_space=pl.ANY),
        scratch_shapes=[
            pltpu.VMEM((MOE_ROWS * TOK_SUB, LANES), F32),
            pltpu.SemaphoreType.DMA((1,)),
            pltpu.SemaphoreType.DMA((1,)),
        ],
    )
    return pl.pallas_call(
        _dispatch_kernel,
        grid_spec=grid_spec,
        out_shape=jax.ShapeDtypeStruct((n_rows * TOK_SUB, LANES), F32),
        compiler_params=_cparams(("arbitrary",)),
        name="moe_dispatch",
    )(tail, n_used, pos2, h2)


def _expert_kernel(be_ref, nu_ref, run_ref, x_ref, wg_hbm, wu_hbm, wd_hbm, y_ref,
                   wbuf_g, wbuf_u, wbuf_d, wsem, wg_bf, wu_bf, wd_bf, *, layer):
    i = pl.program_id(0)
    n_used = nu_ref[0]
    rows = MOE_ROWS

    def weight_copies(e, s):
        return [pltpu.make_async_copy(w_hbm.at[layer, e], buf.at[s], wsem.at[s])
                for w_hbm, buf in ((wg_hbm, wbuf_g), (wu_hbm, wbuf_u), (wd_hbm, wbuf_d))]

    @pl.when(jnp.logical_and(i == 0, n_used > 0))
    def _first():
        for cp in weight_copies(be_ref[0], 0):
            cp.start()

    @pl.when(jnp.logical_and(run_ref[0, i] == 1, i < n_used))
    def _new_expert():
        ws = run_ref[1, i]
        for cp in weight_copies(be_ref[i], ws):
            cp.wait()
        wg_bf[...] = wbuf_g[ws].astype(BF16)
        wu_bf[...] = wbuf_u[ws].astype(BF16)
        wd_bf[...] = wbuf_d[ws].astype(BF16)

        @pl.when(run_ref[2, i] >= 0)
        def _next_weights():
            for cp in weight_copies(run_ref[2, i], 1 - ws):
                cp.start()

    @pl.when(i < n_used)
    def _compute():
        xb = _from_token_major(x_ref, 0, rows).astype(BF16)
        g = jnp.dot(xb, wg_bf[...], preferred_element_type=F32)
        u = jnp.dot(xb, wu_bf[...], preferred_element_type=F32)
        hid = (_silu(g) * u).astype(BF16)
        _to_token_major(y_ref, jnp.dot(hid, wd_bf[...], preferred_element_type=F32))

    @pl.when(i >= n_used)
    def _skip():
        y_ref[...] = jnp.zeros_like(y_ref)


def _expert_call(block_expert, n_used, xs, w_gate, w_up, w_down, layer):
    rows = MOE_ROWS
    n_blocks = xs.shape[0] // (rows * TOK_SUB)
    D = D_MODEL
    FF = w_gate.shape[-1]
    idx = jnp.arange(n_blocks, dtype=jnp.int32)
    first = jnp.logical_and(idx < n_used[0], jnp.concatenate(
        [jnp.ones((1,), jnp.bool_), block_expert[1:] != block_expert[:-1]]))
    run_slot = (jnp.cumsum(first.astype(jnp.int32)) - 1) % 2
    big = jnp.int32(n_blocks)
    first_at = jnp.where(first, idx, big)
    next_first = jnp.concatenate([lax.cummin(first_at[::-1])[::-1][1:], big[None]])
    next_expert = jnp.where(next_first < big, block_expert[jnp.minimum(next_first, n_blocks - 1)], -1)
    runs = jnp.stack([first.astype(jnp.int32), run_slot, next_expert]).astype(jnp.int32)
    grid_spec = pltpu.PrefetchScalarGridSpec(
        num_scalar_prefetch=3,
        grid=(n_blocks,),
        in_specs=[
            pl.BlockSpec((rows * TOK_SUB, LANES),
                         lambda i, be, nu, rn: (jnp.minimum(i, jnp.maximum(nu[0] - 1, 0)), 0)),
            pl.BlockSpec(memory_space=pl.ANY),
            pl.BlockSpec(memory_space=pl.ANY),
            pl.BlockSpec(memory_space=pl.ANY),
        ],
        out_specs=pl.BlockSpec((rows * TOK_SUB, LANES), lambda i, be, nu, rn: (i, 0)),
        scratch_shapes=[
            pltpu.VMEM((2, D, FF), F32),
            pltpu.VMEM((2, D, FF), F32),
            pltpu.VMEM((2, FF, D), F32),
            pltpu.SemaphoreType.DMA((2,)),
            pltpu.VMEM((D, FF), BF16),
            pltpu.VMEM((D, FF), BF16),
            pltpu.VMEM((FF, D), BF16),
        ],
    )
    return pl.pallas_call(
        functools.partial(_expert_kernel, layer=layer),
        grid_spec=grid_spec,
        out_shape=jax.ShapeDtypeStruct((n_blocks * rows * TOK_SUB, LANES), F32),
        compiler_params=_cparams(("arbitrary",)),
        name="expert_mlp",
    )(block_expert, n_used, runs, xs, w_gate, w_up, w_down)


def _combine_kernel(pos_ref, posn_ref, y_hbm, x_ref, route_ref, mod_ref, o_ref, ybuf, sem):
    i = pl.program_id(0)
    n = pl.num_programs(0)
    slot = i % 2
    tm = TM_COMB

    @pl.when(i == 0)
    def _first():
        _start_tokens(pos_ref, y_hbm, ybuf, 0, sem, 2 * tm)

    @pl.when(i + 1 < n)
    def _prefetch():
        _start_tokens(posn_ref, y_hbm, ybuf, 1 - slot, sem, 2 * tm)

    _wait_tokens(y_hbm, ybuf, slot, sem, 2 * tm)
    route = route_ref[...]
    yb = ybuf.at[slot]
    y = route[:, 4:5] * _from_token_major(yb, 0, tm) + route[:, 5:6] * _from_token_major(yb, tm, tm)
    o_ref[...] = x_ref[...] + mod_ref[0, 5:6, :] * y


def _combine_call(pos2, y_rows, x2, route, mod, seq):
    N, D = x2.shape
    tm = TM_COMB
    tpb = seq // tm
    nt = N // tm
    row = lambda i: (i, 0)
    return pl.pallas_call(
        _combine_kernel,
        grid=(nt,),
        in_specs=[
            pl.BlockSpec((1, 1, 2 * tm), lambda i: (i, 0, 0), memory_space=pltpu.SMEM),
            pl.BlockSpec((1, 1, 2 * tm), lambda i: (jnp.minimum(i + 1, nt - 1), 0, 0), memory_space=pltpu.SMEM),
            pl.BlockSpec(memory_space=pl.ANY),
            pl.BlockSpec((tm, D), row),
            pl.BlockSpec((tm, LANES), row),
            pl.BlockSpec((1, 6, D), lambda i: (i // tpb, 0, 0)),
        ],
        out_specs=pl.BlockSpec((tm, D), row),
        out_shape=jax.ShapeDtypeStruct((N, D), F32),
        scratch_shapes=[pltpu.VMEM((2, 2 * tm * TOK_SUB, LANES), F32), pltpu.SemaphoreType.DMA((2,))],
        compiler_params=_cparams(("arbitrary",)),
        name="moe_combine",
    )(pos2, pos2, y_rows, x2, route, mod)


def _rope_lane_tables(positions, rot_dim, head_w, n_rep):
    half = rot_dim // 2
    inv = jnp.power(jnp.float32(ROPE_THETA), -2.0 * jnp.arange(half, dtype=jnp.float32) / rot_dim)
    ang = positions.astype(jnp.float32)[..., None] * inv
    n = positions.shape[0] * positions.shape[1]
    cos, sin = jnp.cos(ang).reshape(n, half), jnp.sin(ang).reshape(n, half)
    lane = np.arange(head_w * n_rep) % head_w
    sel = (lane[None, :] % half == np.arange(half)[:, None]) & (lane[None, :] < rot_dim)
    sign = np.where(lane < half, -1.0, 1.0)[None, :]
    hi = lax.Precision.HIGHEST
    cos_f = jnp.dot(cos, jnp.asarray(sel, F32), precision=hi) + jnp.asarray(lane >= rot_dim, F32)[None, :]
    sin_s = jnp.dot(sin, jnp.asarray(sel * sign, F32), precision=hi)
    return cos_f, sin_s


def _layer_weights(l, w_in, q_norm, k_norm, conv_w, conv_b, conv_ln_g, conv_ln_b, conv_pw_w, conv_pw_b,
                   pool_w, pool_b, pool_scale, sgu_ln_g, sgu_ln_b, sgu_w, sgu_b, out_norm, w_out, norm2,
                   w_rg, b_rg, w_re, b_re):
    G = GROUP_W
    D = D_MODEL
    pts = np.cumsum(IN_SPLITS)[:-1].tolist()
    wq, wk, wv, wiq, wik, wiw, wb, wc, wd = jnp.split(w_in[l], pts, axis=-1)
    wiw_p = jnp.pad(wiw, ((0, 0), (0, LANES - IDX_HEADS)))
    w_in_p = jnp.concatenate([wq, wk, wv, wiq, jnp.tile(wik, (1, IDX_HEADS)), wiw_p, wb, wc, wd], axis=-1).astype(BF16)
    npool = len(POOL_WINDOWS)
    pool_bd = jnp.zeros((G, G), F32)
    for g in range(npool):
        pool_bd = lax.dynamic_update_slice(pool_bd, pool_w[l, g], (g * POOL_CH, g * POOL_CH))
    sgu_bias = jnp.repeat(sgu_b[l].T, G // SGU_HEADS, axis=1)
    w_router = jnp.concatenate([w_rg[l], w_re[l].reshape(D, N_EXPERTS),
                                jnp.zeros((D, LANES - N_EXP_GROUPS - N_EXPERTS), F32)], axis=-1).astype(BF16)
    b_router = jnp.concatenate([b_rg[l], b_re[l].reshape(N_EXPERTS),
                                jnp.zeros((LANES - N_EXP_GROUPS - N_EXPERTS,), F32)]).reshape(1, LANES)
    r1 = lambda a: a.reshape(1, -1)
    return dict(
        w_in=w_in_p,
        q_norm=jnp.tile(q_norm[l], ATT_HEADS).reshape(1, G), k_norm=jnp.tile(k_norm[l], ATT_HEADS).reshape(1, G),
        conv_w=conv_w[l], conv_b=r1(conv_b[l]), conv_ln_g=r1(conv_ln_g[l]), conv_ln_b=r1(conv_ln_b[l]),
        conv_pw_w=conv_pw_w[l].astype(BF16), conv_pw_b=r1(conv_pw_b[l]),
        pool_w=pool_bd.astype(BF16), pool_b=r1(pool_b[l]), pool_scale=r1(pool_scale[l]),
        sgu_ln_g=r1(sgu_ln_g[l]), sgu_ln_b=r1(sgu_ln_b[l]), sgu_w=sgu_w[l], sgu_b=sgu_bias,
        out_norm=r1(out_norm[l]), w_out=w_out[l].astype(BF16), norm2=r1(norm2[l]),
        w_router=w_router, b_router=b_router,
    )


def _dispatch_tables(route, cnt, n_tokens):
    rows_blk = MOE_ROWS
    e = route[:, 0:2].astype(jnp.int32)
    rank = route[:, 2:4].astype(jnp.int32)
    counts = cnt[0, :N_EXPERTS].astype(jnp.int32)
    padded = (counts + rows_blk - 1) // rows_blk * rows_blk
    pad_end = jnp.cumsum(padded)
    pad_start = pad_end - padded
    seg0 = jnp.sum(jnp.where(e[..., None] == jnp.arange(N_EXPERTS, dtype=jnp.int32), pad_start, 0), axis=-1)
    pos = seg0 + rank
    m = n_tokens * MOE_TOPK
    n_blocks = (m + N_EXPERTS * (rows_blk - 1) + rows_blk - 1) // rows_blk
    n_used = (pad_end[-1] // rows_blk).astype(jnp.int32).reshape(1)
    blk_row0 = jnp.arange(n_blocks, dtype=jnp.int32) * rows_blk
    block_expert = jnp.minimum(jnp.sum((pad_end[None, :] <= blk_row0[:, None]).astype(jnp.int32), axis=1),
                               N_EXPERTS - 1)
    pos2 = (pos * TOK_SUB).reshape(n_tokens // TM_COMB, TM_COMB, MOE_TOPK).transpose(0, 2, 1).reshape(
        -1, 1, MOE_TOPK * TM_COMB)
    tail = jnp.where(counts > 0, (pad_end - rows_blk) * TOK_SUB, -1).astype(jnp.int32)
    return block_expert, n_used, tail, pos2, n_blocks * rows_blk


def kernel(x, c, positions, w_ada, b_ada, norm1, w_in, q_norm, k_norm, conv_w, conv_b, conv_ln_g, conv_ln_b, conv_pw_w, conv_pw_b, pool_w, pool_b, pool_scale, sgu_ln_g, sgu_ln_b, sgu_w, sgu_b, out_norm, w_out, norm2, w_rg, b_rg, w_re, b_re, w_gate, w_up, w_down):
    B, S, D = x.shape
    N = B * S
    assert D == D_MODEL and S % TM_PROJ == 0 and S % TK_ATT == 0 and N % TM_COMB == 0
    depth = w_ada.shape[0]
    cos_a, sin_a = _rope_lane_tables(positions, ROPE_DIM, ATT_HEAD_DIM, ATT_HEADS)
    cos_i, sin_i = _rope_lane_tables(positions, IDX_ROPE_DIM, IDX_DIM, IDX_HEADS)
    c_pad = jnp.pad(c, ((0, (-B) % 8), (0, 0)))
    mod_all = _ada_call(c_pad, w_ada, b_ada)
    x2 = x.reshape(N, D)
    for l in range(depth):
        lw = _layer_weights(l, w_in, q_norm, k_norm, conv_w, conv_b, conv_ln_g, conv_ln_b, conv_pw_w, conv_pw_b,
                            pool_w, pool_b, pool_scale, sgu_ln_g, sgu_ln_b, sgu_w, sgu_b, out_norm, w_out, norm2,
                            w_rg, b_rg, w_re, b_re)
        mod = mod_all[l, :B].reshape(B, 6, D)
        q, k, v, iq, ik, iw, pb, pc, pd = _proj_call(
            x2, mod, norm1[l].reshape(1, D), lw["w_in"], lw["q_norm"], lw["k_norm"], cos_a, sin_a, cos_i, sin_i, S)
        oa = _dsa_call(q, k, v, iq, ik, iw, B, S)
        x_mid, h2, route, cnt = _mix_call(x2, oa, pb, pc, pd, mod, lw, S)
        block_expert, n_used, tail, pos2, n_rows = _dispatch_tables(route, cnt, N)
        xs = _dispatch_call(tail, n_used, pos2, h2, n_rows)
        y_rows = _expert_call(block_expert, n_used, xs, w_gate, w_up, w_down, l)
        x2 = _combine_call(pos2, y_rows, x_mid, route, mod, S)
    return x2.reshape(B, S, D)
```

```python
import functools

import numpy as np
import jax
import jax.numpy as jnp
from jax import lax
from jax.experimental import pallas as pl
from jax.experimental.pallas import tpu as pltpu

F32 = jnp.float32
BF16 = jnp.bfloat16
NEG_INF = float("-inf")

D_MODEL = 1024
CHUNK = 64
N_MIXERS = 4
GROUP_W = D_MODEL // N_MIXERS
ATT_HEAD_DIM = 64
ATT_HEADS = GROUP_W // ATT_HEAD_DIM
ROPE_DIM = ATT_HEAD_DIM // 4
ROPE_THETA = 500000.0
IDX_HEADS = 4
IDX_DIM = 32
IDX_ROPE_DIM = IDX_DIM // 4
TOPK_MAX = 256
CONV_WIDTH = 31
CONV_GROUPS = 4
POOL_WINDOWS = (2, 4, 8, 16)
POOL_CH = GROUP_W // 4
SGU_CHUNK = 128
SGU_HEADS = 4
N_EXP_GROUPS = 4
EXP_PER_GROUP = 8
N_EXPERTS = N_EXP_GROUPS * EXP_PER_GROUP
EXPERT_FF = 512
MOE_TOPK = 2
EPS = 1e-6
IN_SPLITS = (GROUP_W, GROUP_W, GROUP_W, IDX_HEADS * IDX_DIM, IDX_DIM, IDX_HEADS, 2 * GROUP_W, GROUP_W, 2 * GROUP_W)

LANES = 128
HALO = 32
W_IN_COLS = 3 * GROUP_W + 3 * LANES + 5 * GROUP_W

TM_PROJ = 512
TQ_ATT = 256
TK_ATT = 512
TM_MIX = 256
MOE_ROWS = 256
TM_COMB = 256
BISECT_ITERS = 18
VMEM_LIMIT = 56 * 1024 * 1024


def _cparams(sem):
    return pltpu.CompilerParams(dimension_semantics=sem, vmem_limit_bytes=VMEM_LIMIT)


def _lane_iota(shape):
    return lax.broadcasted_iota(jnp.int32, shape, len(shape) - 1)


def _seg_mean(y, width):
    shift = int(np.log2(width))
    grp = _lane_iota(y.shape) >> shift
    out = jnp.zeros_like(y)
    for g in range(y.shape[-1] // width):
        msk = grp == g
        s = jnp.sum(jnp.where(msk, y, 0.0), axis=-1, keepdims=True)
        out = jnp.where(msk, s, out)
    return out * (1.0 / width)


def _rope(x, cos_f, sin_s, head_w, half):
    c = x.shape[-1]
    lane = _lane_iota(x.shape) & (head_w - 1)
    partner = jnp.where(lane < half, pltpu.roll(x, c - half, 1), pltpu.roll(x, half, 1))
    return x * cos_f + partner * sin_s


def _silu(x):
    return x * jax.nn.sigmoid(x)


def _ada_kernel(c_ref, w_ref, b_ref, o_ref):
    ca = _silu(c_ref[...])
    o_ref[0] = jnp.dot(ca.astype(BF16), w_ref[0].astype(BF16), preferred_element_type=F32) + b_ref[0]


def _ada_call(c_pad, w_ada, b_ada):
    L, D, D6 = w_ada.shape
    rows = c_pad.shape[0]
    tn = D
    return pl.pallas_call(
        _ada_kernel,
        grid=(L, D6 // tn),
        in_specs=[
            pl.BlockSpec((rows, D), lambda l, j: (0, 0)),
            pl.BlockSpec((1, D, tn), lambda l, j: (l, 0, j)),
            pl.BlockSpec((1, 1, tn), lambda l, j: (l, 0, j)),
        ],
        out_specs=pl.BlockSpec((1, rows, tn), lambda l, j: (l, 0, j)),
        out_shape=jax.ShapeDtypeStruct((L, rows, D6), F32),
        compiler_params=_cparams(("arbitrary", "arbitrary")),
        name="ada_mod",
    )(c_pad, w_ada, b_ada.reshape(L, 1, D6))


def _proj_kernel(x_ref, mod_ref, n1_ref, w_ref, qn_ref, kn_ref, cosa_ref, sina_ref, cosi_ref, sini_ref,
                 q_ref, k_ref, v_ref, iq_ref, ik_ref, iw_ref, pb_ref, pc_ref, pd_ref):
    x = x_ref[...]
    ms = jnp.mean(x * x, axis=-1, keepdims=True)
    h = (x * lax.rsqrt(ms + EPS)) * n1_ref[...]
    h = h * (1.0 + mod_ref[0, 1:2, :]) + mod_ref[0, 0:1, :]
    proj = jnp.dot(h.astype(BF16), w_ref[...], preferred_element_type=F32)
    G = GROUP_W
    cos_a, sin_a = cosa_ref[...], sina_ref[...]
    cos_i, sin_i = cosi_ref[...], sini_ref[...]

    def qk(t, g_ref):
        tn = (t * lax.rsqrt(_seg_mean(t * t, ATT_HEAD_DIM) + EPS)) * g_ref[...]
        return _rope(tn, cos_a, sin_a, ATT_HEAD_DIM, ROPE_DIM // 2)

    q_ref[...] = (qk(proj[:, 0:G], qn_ref) * (ATT_HEAD_DIM ** -0.5)).T.astype(BF16)
    k_ref[...] = qk(proj[:, G:2 * G], kn_ref).astype(BF16)
    v_ref[0] = proj[:, 2 * G:3 * G].T.astype(BF16)
    o = 3 * G
    iq_ref[...] = _rope(proj[:, o:o + LANES], cos_i, sin_i, IDX_DIM, IDX_ROPE_DIM // 2).T.astype(BF16)
    ik_ref[...] = _rope(proj[:, o + LANES:o + 2 * LANES], cos_i, sin_i, IDX_DIM, IDX_ROPE_DIM // 2).astype(BF16)
    iw_ref[...] = (proj[:, o + 2 * LANES:o + 3 * LANES] * (IDX_HEADS ** -0.5)).T[0:8, :]
    o += 3 * LANES
    pb_ref[...] = proj[:, o:o + 2 * G]
    pc_ref[...] = proj[:, o + 2 * G:o + 3 * G]
    pd_ref[...] = proj[:, o + 3 * G:o + 5 * G]


def _proj_call(x2, mod, n1, w_in_p, qn_t, kn_t, cos_a, sin_a, cos_i, sin_i, seq):
    N, D = x2.shape
    tm = TM_PROJ
    tpb = seq // tm
    G = GROUP_W
    assert tm == TK_ATT
    row = lambda i: (i, 0)
    col = lambda i: (0, i)
    fixed = lambda i: (0, 0)
    sds = jax.ShapeDtypeStruct
    out_specs = [
        pl.BlockSpec((G, tm), col),
        pl.BlockSpec((tm, G), row),
        pl.BlockSpec((1, G, tm), lambda i: (i, 0, 0)),
        pl.BlockSpec((LANES, tm), col),
        pl.BlockSpec((tm, LANES), row),
        pl.BlockSpec((8, tm), col),
        pl.BlockSpec((tm, 2 * G), row),
        pl.BlockSpec((tm, G), row),
        pl.BlockSpec((tm, 2 * G), row),
    ]
    out_shape = [sds((G, N), BF16), sds((N, G), BF16), sds((N // tm, G, tm), BF16), sds((LANES, N), BF16),
                 sds((N, LANES), BF16), sds((8, N), F32), sds((N, 2 * G), F32), sds((N, G), F32), sds((N, 2 * G), F32)]
    return pl.pallas_call(
        _proj_kernel,
        grid=(N // tm,),
        in_specs=[
            pl.BlockSpec((tm, D), row),
            pl.BlockSpec((1, 6, D), lambda i: (i // tpb, 0, 0)),
            pl.BlockSpec((1, D), fixed),
            pl.BlockSpec((D, W_IN_COLS), fixed),
            pl.BlockSpec((1, G), fixed),
            pl.BlockSpec((1, G), fixed),
            pl.BlockSpec((tm, G), row),
            pl.BlockSpec((tm, G), row),
            pl.BlockSpec((tm, LANES), row),
            pl.BlockSpec((tm, LANES), row),
        ],
        out_specs=out_specs,
        out_shape=out_shape,
        compiler_params=_cparams(("parallel",)),
        name="norm_in_proj",
    )(x2, mod, n1, w_in_p, qn_t, kn_t, cos_a, sin_a, cos_i, sin_i)


def _pair_rhs(xt, head_rows, h0):
    head = lax.broadcasted_iota(jnp.int32, xt.shape, 0) >> int(np.log2(head_rows))
    zero = jnp.zeros_like(xt)
    return jnp.concatenate([jnp.where(head == h0, xt, zero), jnp.where(head == h0 + 1, xt, zero)], axis=1)


def _dsa_kernel(q_ref, k_ref, v_ref, iq_ref, ik_ref, iw_ref, ltri_ref, o_ref, sc_ref, lge_ref, lgo_ref, *, topk):
    tq, tk = TQ_ATT, TK_ATT
    i = pl.program_id(1)
    q0 = i * tq
    n_kv = (q0 + tq + tk - 1) // tk
    kf = float(topk)

    q_pos = _lane_iota((1, tq)) + q0
    chunk_bits = int(np.log2(CHUNK))
    key_end = ((q_pos >> chunk_bits) + 1) << chunk_bits
    key_i = lax.broadcasted_iota(jnp.int32, (tk, tq), 0)

    def fold8(x, op, fn=None):
        n_acc = 4
        piece = (lambda r: x[r * 8:(r + 1) * 8, :]) if fn is None else (lambda r: fn(x[r * 8:(r + 1) * 8, :]))
        accs = [piece(r) for r in range(n_acc)]
        for r in range(n_acc, tk // 8):
            accs[r % n_acc] = op(accs[r % n_acc], piece(r))
        return op(op(accs[0], accs[1]), op(accs[2], accs[3]))

    iqt = iq_ref[...]
    iq_pairs = [_pair_rhs(iqt, IDX_DIM, h0) for h0 in range(0, IDX_HEADS, 2)]
    iw_h = [iw_ref[h:h + 1, :] for h in range(IDX_HEADS)]

    def score_body(kc, carry):
        hi8, lo8 = carry
        ikc = ik_ref[pl.ds(pl.multiple_of(kc * tk, tk), tk), :]
        s = jnp.zeros((tk, tq), F32)
        for pi, rhs in enumerate(iq_pairs):
            d2 = jnp.dot(ikc, rhs, preferred_element_type=F32)
            for j in range(2):
                d = d2[:, j * tq:(j + 1) * tq]
                s = s + jnp.maximum(d * (IDX_DIM ** -0.5), 0.0) * iw_h[2 * pi + j]
        adm = key_i + kc * tk < key_end
        s_top = jnp.where(adm, s, NEG_INF)
        sc_ref[kc] = s_top
        return (jnp.maximum(hi8, fold8(s_top, jnp.maximum)),
                jnp.minimum(lo8, fold8(jnp.where(adm, s, jnp.inf), jnp.minimum)))

    hi8, lo8 = lax.fori_loop(0, n_kv, score_body,
                             (jnp.full((8, tq), NEG_INF, F32), jnp.full((8, tq), jnp.inf, F32)))
    col_max = jnp.max(hi8, axis=0, keepdims=True)
    col_min = jnp.min(lo8, axis=0, keepdims=True)

    def reduce_chunks(fns, init, combine, fold):
        def body(kc, parts):
            blk = sc_ref.at[kc]
            return tuple(combine(p, fold8(blk, combine, fn)) for p, fn in zip(parts, fns))
        parts = lax.fori_loop(0, n_kv, body, tuple(jnp.full((8, tq), init, F32) for _ in fns))
        return [fold(p, axis=0, keepdims=True) for p in parts]

    def count(*inds):
        return reduce_chunks(inds, 0.0, jnp.add, jnp.sum)

    def col_maximum(val):
        return reduce_chunks((val,), NEG_INF, jnp.maximum, jnp.max)[0]

    small = key_end <= topk

    @pl.when(q0 + tq > topk)
    def _select():
        def bis_body(_, c):
            lo, hi = c
            mid = jnp.where(hi == jnp.inf, col_max, lo + (hi - lo) * 0.5)
            ge = count(lambda b: jnp.where(b >= mid, 1.0, 0.0))[0] >= kf
            return jnp.where(ge, mid, lo), jnp.where(ge, hi, mid)

        lo, hi = lax.fori_loop(0, BISECT_ITERS, bis_body, (col_min, jnp.full((1, tq), jnp.inf, F32)))

        def sd_cond(c):
            return c[0] > 0.0

        def sd_body(c):
            _, hi, thr, done = c
            cand = col_maximum(lambda b: jnp.where(b < hi, b, NEG_INF))
            ok = count(lambda b: jnp.where(b >= cand, 1.0, 0.0))[0] >= kf
            thr = jnp.where(done > 0.0, thr, cand)
            hi = jnp.where(done > 0.0, hi, cand)
            done = jnp.where(ok, 1.0, done)
            return jnp.sum(1.0 - done), hi, thr, done

        done0 = jnp.where(small, 1.0, 0.0)
        n0 = jnp.sum(1.0 - done0)
        _, _, thr, _ = lax.while_loop(sd_cond, sd_body, (n0, hi, jnp.full((1, tq), NEG_INF, F32), done0))
        thr = jnp.where(small, NEG_INF, thr)

        n_above, n_tied = count(lambda b: jnp.where(b > thr, 1.0, 0.0), lambda b: jnp.where(b == thr, 1.0, 0.0))
        need = kf - n_above
        excess = jnp.sum(jnp.where(jnp.where(small, 0.0, n_tied) > need, 1.0, 0.0))

        @pl.when(excess <= 0.0)
        def _keep_all_ties():
            def bias_body(kc, carry):
                blk = sc_ref[kc]
                sc_ref[kc] = jnp.where(blk == NEG_INF, NEG_INF, jnp.where(blk >= thr, 0.0, NEG_INF))
                return carry
            lax.fori_loop(0, n_kv, bias_body, 0)

        @pl.when(excess > 0.0)
        def _rank_ties():
            half = tk // 2
            ltri_top = ltri_ref[0:half, 0:half]
            ltri_bot = ltri_ref[half:tk, :]

            def bias_body(kc, seen):
                blk = sc_ref[kc]
                tied = jnp.where(blk == thr, 1.0, 0.0)
                tied16 = tied.astype(BF16)
                rank = jnp.concatenate([jnp.dot(ltri_top, tied16[0:half, :], preferred_element_type=F32),
                                        jnp.dot(ltri_bot, tied16, preferred_element_type=F32)], axis=0) + seen
                tie = jnp.where(blk == thr, jnp.where(rank <= need, 0.0, NEG_INF), NEG_INF)
                bias = jnp.where(blk > thr, 0.0, tie)
                sc_ref[kc] = jnp.where(blk == NEG_INF, NEG_INF, bias)
                return seen + jnp.sum(fold8(tied, jnp.add), axis=0, keepdims=True)

            lax.fori_loop(0, n_kv, bias_body, jnp.zeros((1, tq), F32))

    @pl.when(q0 + tq <= topk)
    def _all():
        def bias_body(kc, carry):
            sc_ref[kc] = jnp.where(sc_ref[kc] == NEG_INF, NEG_INF, 0.0)
            return carry
        lax.fori_loop(0, n_kv, bias_body, 0)

    qt = q_ref[...]
    q_pairs = [_pair_rhs(qt, ATT_HEAD_DIM, h0) for h0 in range(0, ATT_HEADS, 2)]
    dh = ATT_HEAD_DIM

    def store_logits(buf, kc):
        kblk = k_ref[pl.ds(pl.multiple_of(kc * tk, tk), tk), :]
        bias = sc_ref[kc]
        for pi, rhs in enumerate(q_pairs):
            s2 = jnp.dot(kblk, rhs, preferred_element_type=F32)
            for j in range(2):
                buf[2 * pi + j] = s2[:, j * tq:(j + 1) * tq] + bias

    def absorb(buf, kc, state):
        ms, ls, accs = state
        vt = v_ref[kc]
        ms_n, ls_n, accs_n = [], [], []
        for h in range(ATT_HEADS):
            s = buf[h]
            m_new = jnp.maximum(ms[h], jnp.max(fold8(s, jnp.maximum), axis=0, keepdims=True))
            m_safe = jnp.where(m_new == NEG_INF, 0.0, m_new)
            alpha = jnp.exp(ms[h] - m_safe)
            p = jnp.exp(s - m_safe)
            ls_n.append(alpha * ls[h] + fold8(p, jnp.add))
            pv = jnp.dot(vt[h * dh:(h + 1) * dh, :], p.astype(BF16), preferred_element_type=F32)
            accs_n.append(alpha * accs[h] + pv)
            ms_n.append(m_new)
        return tuple(ms_n), tuple(ls_n), tuple(accs_n)

    def pair_body(jj, state):
        s = 2 * jj + 1
        store_logits(lgo_ref, s)
        state = absorb(lge_ref, s - 1, state)
        store_logits(lge_ref, s + 1)
        return absorb(lgo_ref, s, state)

    state = (tuple(jnp.full((1, tq), NEG_INF, F32) for _ in range(ATT_HEADS)),
             tuple(jnp.zeros((8, tq), F32) for _ in range(ATT_HEADS)),
             tuple(jnp.zeros((dh, tq), F32) for _ in range(ATT_HEADS)))
    store_logits(lge_ref, 0)
    n_pairs = (n_kv - 1) // 2
    state = lax.fori_loop(0, n_pairs, pair_body, state)
    last_even = 2 * n_pairs

    def tail_two(state):
        store_logits(lgo_ref, last_even + 1)
        return absorb(lgo_ref, last_even + 1, absorb(lge_ref, last_even, state))

    def tail_one(state):
        return absorb(lge_ref, last_even, state)

    _, ls, accs = lax.cond(n_kv - 1 - last_even > 0, tail_two, tail_one, state)
    out_t = jnp.concatenate([accs[h] / jnp.sum(ls[h], axis=0, keepdims=True) for h in range(ATT_HEADS)], axis=0)
    o_ref[...] = out_t.T


def _dsa_call(qt, k, vt, iqt, ik, iwt, batch, seq):
    N, G = k.shape
    tq = TQ_ATT
    nq = seq // tq
    nkc = seq // TK_ATT
    topk = min(TOPK_MAX, seq // 4)
    qcol = lambda b, i: (0, b * nq + i)
    brow = lambda b, i: (b, 0)
    return pl.pallas_call(
        functools.partial(_dsa_kernel, topk=topk),
        grid=(batch, nq),
        in_specs=[
            pl.BlockSpec((G, tq), qcol),
            pl.BlockSpec((seq, G), brow),
            pl.BlockSpec((nkc, G, TK_ATT), lambda b, i: (b, 0, 0)),
            pl.BlockSpec((LANES, tq), qcol),
            pl.BlockSpec((seq, LANES), brow),
            pl.BlockSpec((8, tq), qcol),
            pl.BlockSpec((TK_ATT, TK_ATT), lambda b, i: (0, 0)),
        ],
        out_specs=pl.BlockSpec((tq, G), lambda b, i: (b * nq + i, 0)),
        out_shape=jax.ShapeDtypeStruct((N, G), F32),
        scratch_shapes=[pltpu.VMEM((nkc, TK_ATT, tq), F32), pltpu.VMEM((ATT_HEADS, TK_ATT, tq), F32),
                        pltpu.VMEM((ATT_HEADS, TK_ATT, tq), F32)],
        compiler_params=_cparams(("parallel", "arbitrary")),
        name="dsa_attention",
    )(qt, k, vt, iqt, ik, iwt, jnp.tril(jnp.ones((TK_ATT, TK_ATT), BF16)))


def _mix_kernel(x_ref, oa_ref, pb_ref, pbh_ref, pc_ref, pch_ref, pd_ref, mod_ref,
                cw_ref, cb_ref, clg_ref, clb_ref, cpw_ref, cpb_ref,
                pw_ref, pbias_ref, ps_ref, slg_ref, slb_ref, sw_ref, sb_ref,
                on_ref, wo_ref, n2_ref, wr_ref, br_ref,
                xo_ref, h2_ref, route_ref, cnt_ref,
                ypad_ref, ppad_ref, s2_ref, s4_ref, s8_ref, run_ref, cph_ref, *, tiles_per_batch):
    tm = TM_MIX
    G = GROUP_W
    i = pl.program_id(0)
    t_in_b = i % tiles_per_batch
    first = t_in_b == 0
    lane_g = _lane_iota((tm, G))

    def glu(pb):
        return pb[:, 0:G] * jax.nn.sigmoid(pb[:, G:2 * G])

    ypad_ref[0:HALO, :] = jnp.where(first, 0.0, glu(pbh_ref[...]))
    ypad_ref[HALO:HALO + tm, :] = glu(pb_ref[...])
    acc = jnp.zeros((tm, G), F32)
    first_off = HALO - (CONV_WIDTH - 1)
    for phase in range(8):
        offs = [o for o in range(first_off, HALO + 1) if o % 8 == phase]
        if not offs:
            continue
        span = offs[-1] - offs[0] + tm
        cph_ref[phase, 0:span, :] = ypad_ref[offs[0]:offs[0] + span, :]
        for o in offs:
            acc = acc + cw_ref[o - first_off:o - first_off + 1, :] * cph_ref[phase, o - offs[0]:o - offs[0] + tm, :]
    y = acc + cb_ref[...]
    gw = G // CONV_GROUPS
    mu = _seg_mean(y, gw)
    yc = y - mu
    var = _seg_mean(yc * yc, gw)
    y = (yc * lax.rsqrt(var + EPS)) * clg_ref[...] + clb_ref[...]
    o_b = jnp.dot(_silu(y).astype(BF16), cpw_ref[...], preferred_element_type=F32) + cpb_ref[...]

    p = pc_ref[...]
    ppad_ref[0:HALO, :] = jnp.where(first, 0.0, pch_ref[...])
    ppad_ref[HALO:HALO + tm, :] = p
    n8 = tm + HALO - 8
    s2_ref[8:8 + n8, :] = ppad_ref[8:8 + n8, :] + ppad_ref[7:7 + n8, :]
    n16 = tm + HALO - 16
    s4_ref[16:16 + n16, :] = s2_ref[16:16 + n16, :] + s2_ref[14:14 + n16, :]
    n24 = tm + HALO - 24
    s8_ref[24:24 + n24, :] = s4_ref[24:24 + n24, :] + s4_ref[20:20 + n24, :]
    s2 = s2_ref[HALO:HALO + tm, :]
    s4 = s4_ref[HALO:HALO + tm, :]
    s8 = s8_ref[HALO:HALO + tm, :]
    s16 = s8 + s8_ref[HALO - 8:HALO - 8 + tm, :]
    pgrp = lane_g >> int(np.log2(POOL_CH))
    wsum = jnp.where(pgrp == 0, s2, jnp.where(pgrp == 1, s4, jnp.where(pgrp == 2, s8, s16)))
    wlen = jnp.where(pgrp == 0, 2.0, jnp.where(pgrp == 1, 4.0, jnp.where(pgrp == 2, 8.0, 16.0)))
    tpos = (lax.broadcasted_iota(jnp.int32, (tm, G), 0) + t_in_b * tm + 1).astype(F32)
    pooled = wsum / jnp.minimum(tpos, wlen) - p
    o_c = (jnp.dot(pooled.astype(BF16), pw_ref[...], preferred_element_type=F32) + pbias_ref[...]) * ps_ref[...]

    pd = pd_ref[...]
    u, v = pd[:, 0:G], pd[:, G:2 * G]
    mu = jnp.mean(v, axis=-1, keepdims=True)
    vc = v - mu
    var = jnp.mean(vc * vc, axis=-1, keepdims=True)
    vn = ((vc * lax.rsqrt(var + EPS)) * slg_ref[...] + slb_ref[...]).astype(BF16)
    r_i = lax.broadcasted_iota(jnp.int32, (SGU_CHUNK, SGU_CHUNK), 0)
    c_i = lax.broadcasted_iota(jnp.int32, (SGU_CHUNK, SGU_CHUNK), 1)
    w_heads = [jnp.where(r_i >= c_i, sw_ref[h], 0.0).astype(BF16) for h in range(SGU_HEADS)]
    lane_c = _lane_iota((SGU_CHUNK, G)) >> int(np.log2(G // SGU_HEADS))
    mixed = []
    for n in range(tm // SGU_CHUNK):
        vch = vn[n * SGU_CHUNK:(n + 1) * SGU_CHUNK, :]
        mx = jnp.zeros((SGU_CHUNK, G), F32)
        for h in range(SGU_HEADS):
            mx = jnp.where(lane_c == h, jnp.dot(w_heads[h], vch, preferred_element_type=F32), mx)
        mixed.append(mx + sb_ref[...])
    o_d = u * jnp.concatenate(mixed, axis=0)

    proj = jnp.zeros((tm, D_MODEL), F32)
    for g, piece in enumerate((oa_ref[...], o_b, o_c, o_d)):
        ms = jnp.mean(piece * piece, axis=-1, keepdims=True)
        pn = (piece * lax.rsqrt(ms + EPS)) * on_ref[:, g * G:(g + 1) * G]
        proj = proj + jnp.dot(pn.astype(BF16), wo_ref[g * G:(g + 1) * G, :], preferred_element_type=F32)
    x_new = x_ref[...] + mod_ref[0, 2:3, :] * proj
    xo_ref[...] = x_new

    ms = jnp.mean(x_new * x_new, axis=-1, keepdims=True)
    h2 = (x_new * lax.rsqrt(ms + EPS)) * n2_ref[...]
    h2 = h2 * (1.0 + mod_ref[0, 4:5, :]) + mod_ref[0, 3:4, :]
    _to_token_major(h2_ref, h2)
    logits = jnp.dot(h2.astype(BF16), wr_ref[...], preferred_element_type=F32) + br_ref[...]
    lane = _lane_iota((tm, LANES))
    lane_f = lane.astype(F32)
    big = float(LANES)
    glog = jnp.where(lane < N_EXP_GROUPS, logits, NEG_INF)
    gmax = jnp.max(glog, axis=-1, keepdims=True)
    p_sel = 1.0 / jnp.sum(jnp.exp(glog - gmax), axis=-1, keepdims=True)
    g_idx = jnp.min(jnp.where(glog == gmax, lane_f, big), axis=-1, keepdims=True)
    e_lane = lane - N_EXP_GROUPS
    elog = jnp.where((e_lane >> 3).astype(F32) == g_idx, logits, NEG_INF)
    top1 = jnp.max(elog, axis=-1, keepdims=True)
    j1 = jnp.min(jnp.where(elog == top1, lane_f, big), axis=-1, keepdims=True)
    elog2 = jnp.where(lane_f == j1, NEG_INF, elog)
    top2 = jnp.max(elog2, axis=-1, keepdims=True)
    j2 = jnp.min(jnp.where(elog2 == top2, lane_f, big), axis=-1, keepdims=True)
    e2w = jnp.exp(top2 - top1)
    gate1 = p_sel * (1.0 / (1.0 + e2w))
    gate2 = p_sel * (e2w / (1.0 + e2w))
    e1 = j1 - float(N_EXP_GROUPS)
    e2 = j2 - float(N_EXP_GROUPS)

    @pl.when(i == 0)
    def _init():
        run_ref[...] = jnp.zeros_like(run_ref)

    onehot = jnp.where(jnp.logical_or(lane_f == e1, lane_f == e2), 1.0, 0.0)
    rr = lax.broadcasted_iota(jnp.int32, (tm, tm), 0)
    cc = lax.broadcasted_iota(jnp.int32, (tm, tm), 1)
    before = jnp.where(rr > cc, 1.0, 0.0).astype(BF16)
    prior = jnp.dot(before, onehot.astype(BF16), preferred_element_type=F32) + run_ref[0:1, :]
    rank1 = jnp.sum(jnp.where(lane_f == e1, prior, 0.0), axis=-1, keepdims=True)
    rank2 = jnp.sum(jnp.where(lane_f == e2, prior, 0.0), axis=-1, keepdims=True)
    run_new = run_ref[0:1, :] + jnp.sum(onehot, axis=0, keepdims=True)
    run_ref[...] = jnp.broadcast_to(run_new, run_ref.shape)
    cnt_ref[...] = jnp.broadcast_to(run_new, cnt_ref.shape)
    route = jnp.where(lane == 0, e1, jnp.where(lane == 1, e2, jnp.where(lane == 2, rank1, jnp.where(
        lane == 3, rank2, jnp.where(lane == 4, gate1, jnp.where(lane == 5, gate2, 0.0))))))
    route_ref[...] = route


def _mix_call(x2, oa, pb, pc, pd, mod, lw, seq):
    N, D = x2.shape
    tm = TM_MIX
    tpb = seq // tm
    G = GROUP_W
    row = lambda i: (i, 0)
    halo = lambda i: (jnp.maximum(i * (tm // HALO) - 1, 0), 0)
    fixed2 = lambda i: (0, 0)
    fixed3 = lambda i: (0, 0, 0)
    params = [lw["conv_w"], lw["conv_b"], lw["conv_ln_g"], lw["conv_ln_b"], lw["conv_pw_w"], lw["conv_pw_b"],
              lw["pool_w"], lw["pool_b"], lw["pool_scale"], lw["sgu_ln_g"], lw["sgu_ln_b"], lw["sgu_w"], lw["sgu_b"],
              lw["out_norm"], lw["w_out"], lw["norm2"], lw["w_router"], lw["b_router"]]
    param_specs = [pl.BlockSpec(p.shape, fixed3 if p.ndim == 3 else fixed2) for p in params]
    return pl.pallas_call(
        functools.partial(_mix_kernel, tiles_per_batch=tpb),
        grid=(N // tm,),
        in_specs=[
            pl.BlockSpec((tm, D), row),
            pl.BlockSpec((tm, G), row),
            pl.BlockSpec((tm, 2 * G), row),
            pl.BlockSpec((HALO, 2 * G), halo),
            pl.BlockSpec((tm, G), row),
            pl.BlockSpec((HALO, G), halo),
            pl.BlockSpec((tm, 2 * G), row),
            pl.BlockSpec((1, 6, D), lambda i: (i // tpb, 0, 0)),
        ] + param_specs,
        out_specs=[
            pl.BlockSpec((tm, D), row),
            pl.BlockSpec((tm * TOK_SUB, LANES), row),
            pl.BlockSpec((tm, LANES), row),
            pl.BlockSpec((8, LANES), fixed2),
        ],
        out_shape=[
            jax.ShapeDtypeStruct((N, D), F32),
            jax.ShapeDtypeStruct((N * TOK_SUB, LANES), F32),
            jax.ShapeDtypeStruct((N, LANES), F32),
            jax.ShapeDtypeStruct((8, LANES), F32),
        ],
        scratch_shapes=[pltpu.VMEM((tm + HALO, G), F32) for _ in range(5)] + [
            pltpu.VMEM((8, LANES), F32), pltpu.VMEM((8, tm + HALO, G), F32)],
        compiler_params=_cparams(("arbitrary",)),
        name="mixers_out_router",
    )(x2, oa, pb, pb, pc, pc, pd, mod, *params)


TOK_SUB = D_MODEL // LANES
ROW_DMA_UNROLL = 8


def _to_token_major(ref, x):
    n = x.shape[0]
    for j in range(TOK_SUB):
        ref[pl.ds(j, n, stride=TOK_SUB), :] = x[:, j * LANES:(j + 1) * LANES]


def _from_token_major(ref, tok0, n):
    return jnp.concatenate([ref[pl.ds(tok0 * TOK_SUB + j, n, stride=TOK_SUB), :] for j in range(TOK_SUB)], axis=1)


def _token_copy(src_hbm, row8, dst_ref, slot, r, sem):
    src = src_hbm.at[pl.ds(pl.multiple_of(row8, TOK_SUB), TOK_SUB), :]
    dst = dst_ref.at[slot, pl.ds(pl.multiple_of(r * TOK_SUB, TOK_SUB), TOK_SUB), :]
    return pltpu.make_async_copy(src, dst, sem.at[slot])


def _start_tokens(idx_ref, src_hbm, dst_ref, slot, sem, n):
    def body(g, carry):
        for j in range(ROW_DMA_UNROLL):
            r = g * ROW_DMA_UNROLL + j
            _token_copy(src_hbm, idx_ref[0, 0, r], dst_ref, slot, r, sem).start(priority=j % 2)
        return carry
    lax.fori_loop(0, n // ROW_DMA_UNROLL, body, 0)


def _wait_tokens(src_hbm, dst_ref, slot, sem, n):
    pltpu.make_async_copy(src_hbm.at[pl.ds(0, n * TOK_SUB), :], dst_ref.at[slot], sem.at[slot]).wait()


def _dispatch_kernel(tail_ref, nu_ref, pos_ref, h_ref, xs_hbm, zbuf, zsem, sem):
    i = pl.program_id(0)
    tm = TM_COMB
    blk = MOE_ROWS * TOK_SUB
    n_blocks = xs_hbm.shape[0] // blk

    def zero_block(row0):
        return pltpu.make_async_copy(zbuf, xs_hbm.at[pl.ds(pl.multiple_of(row0, TOK_SUB), blk), :], zsem.at[0])

    @pl.when(i == 0)
    def _zero_fill():
        zbuf[...] = jnp.zeros_like(zbuf)
        n_used = nu_ref[0]
        for e in range(N_EXPERTS):
            @pl.when(tail_ref[e] >= 0)
            def _():
                zero_block(tail_ref[e]).start()
        lax.fori_loop(n_used, n_blocks, lambda b, c: (zero_block(b * blk).start(), c)[1], 0)
        for e in range(N_EXPERTS):
            @pl.when(tail_ref[e] >= 0)
            def _():
                zero_block(tail_ref[e]).wait()
        lax.fori_loop(n_used, n_blocks, lambda b, c: (zero_block(b * blk).wait(), c)[1], 0)

    def body(g, carry):
        for j in range(ROW_DMA_UNROLL):
            r = g * ROW_DMA_UNROLL + j
            t = jnp.where(r < tm, r, r - tm)
            src = h_ref.at[pl.ds(pl.multiple_of(t * TOK_SUB, TOK_SUB), TOK_SUB), :]
            dst = xs_hbm.at[pl.ds(pl.multiple_of(pos_ref[0, 0, r], TOK_SUB), TOK_SUB), :]
            pltpu.make_async_copy(src, dst, sem.at[0]).start(priority=j % 2)
        return carry
    lax.fori_loop(0, 2 * tm // ROW_DMA_UNROLL, body, 0)
    for _ in range(MOE_TOPK):
        pltpu.make_async_copy(h_ref, xs_hbm.at[pl.ds(0, tm * TOK_SUB), :], sem.at[0]).wait()


def _dispatch_call(tail, n_used, pos2, h2, n_rows):
    nt = pos2.shape[0]
    tm = TM_COMB
    grid_spec = pltpu.PrefetchScalarGridSpec(
        num_scalar_prefetch=2,
        grid=(nt,),
        in_specs=[
            pl.BlockSpec((1, 1, 2 * tm), lambda i, *_: (i, 0, 0), memory_space=pltpu.SMEM),
            pl.BlockSpec((tm * TOK_SUB, LANES), lambda i, *_: (i, 0)),
        ],
        out_specs=pl.BlockSpec(memory_space=pl.ANY),
        scratch_shapes=[
            pltpu.VMEM((MOE_ROWS * TOK_SUB, LANES), F32),
            pltpu.SemaphoreType.DMA((1,)),
            pltpu.SemaphoreType.DMA((1,)),
        ],
    )
    return pl.pallas_call(
        _dispatch_kernel,
        grid_spec=grid_spec,
        out_shape=jax.ShapeDtypeStruct((n_rows * TOK_SUB, LANES), F32),
        compiler_params=_cparams(("arbitrary",)),
        name="moe_dispatch",
    )(tail, n_used, pos2, h2)


def _expert_kernel(be_ref, nu_ref, run_ref, x_ref, wg_hbm, wu_hbm, wd_hbm, y_ref,
                   wbuf_g, wbuf_u, wbuf_d, wsem, wg_bf, wu_bf, wd_bf, *, layer):
    i = pl.program_id(0)
    n_used = nu_ref[0]
    rows = MOE_ROWS

    def weight_copies(e, s):
        return [pltpu.make_async_copy(w_hbm.at[layer, e], buf.at[s], wsem.at[s])
                for w_hbm, buf in ((wg_hbm, wbuf_g), (wu_hbm, wbuf_u), (wd_hbm, wbuf_d))]

    @pl.when(jnp.logical_and(i == 0, n_used > 0))
    def _first():
        for cp in weight_copies(be_ref[0], 0):
            cp.start()

    @pl.when(jnp.logical_and(run_ref[0, i] == 1, i < n_used))
    def _new_expert():
        ws = run_ref[1, i]
        for cp in weight_copies(be_ref[i], ws):
            cp.wait()
        wg_bf[...] = wbuf_g[ws].astype(BF16)
        wu_bf[...] = wbuf_u[ws].astype(BF16)
        wd_bf[...] = wbuf_d[ws].astype(BF16)

        @pl.when(run_ref[2, i] >= 0)
        def _next_weights():
            for cp in weight_copies(run_ref[2, i], 1 - ws):
                cp.start()

    @pl.when(i < n_used)
    def _compute():
        xb = _from_token_major(x_ref, 0, rows).astype(BF16)
        g = jnp.dot(xb, wg_bf[...], preferred_element_type=F32)
        u = jnp.dot(xb, wu_bf[...], preferred_element_type=F32)
        hid = (_silu(g) * u).astype(BF16)
        _to_token_major(y_ref, jnp.dot(hid, wd_bf[...], preferred_element_type=F32))

    @pl.when(i >= n_used)
    def _skip():
        y_ref[...] = jnp.zeros_like(y_ref)


def _expert_call(block_expert, n_used, xs, w_gate, w_up, w_down, layer):
    rows = MOE_ROWS
    n_blocks = xs.shape[0] // (rows * TOK_SUB)
    D = D_MODEL
    FF = w_gate.shape[-1]
    idx = jnp.arange(n_blocks, dtype=jnp.int32)
    first = jnp.logical_and(idx < n_used[0], jnp.concatenate(
        [jnp.ones((1,), jnp.bool_), block_expert[1:] != block_expert[:-1]]))
    run_slot = (jnp.cumsum(first.astype(jnp.int32)) - 1) % 2
    big = jnp.int32(n_blocks)
    first_at = jnp.where(first, idx, big)
    next_first = jnp.concatenate([lax.cummin(first_at[::-1])[::-1][1:], big[None]])
    next_expert = jnp.where(next_first < big, block_expert[jnp.minimum(next_first, n_blocks - 1)], -1)
    runs = jnp.stack([first.astype(jnp.int32), run_slot, next_expert]).astype(jnp.int32)
    grid_spec = pltpu.PrefetchScalarGridSpec(
        num_scalar_prefetch=3,
        grid=(n_blocks,),
        in_specs=[
            pl.BlockSpec((rows * TOK_SUB, LANES),
                         lambda i, be, nu, rn: (jnp.minimum(i, jnp.maximum(nu[0] - 1, 0)), 0)),
            pl.BlockSpec(memory_space=pl.ANY),
            pl.BlockSpec(memory_space=pl.ANY),
            pl.BlockSpec(memory_space=pl.ANY),
        ],
        out_specs=pl.BlockSpec((rows * TOK_SUB, LANES), lambda i, be, nu, rn: (i, 0)),
        scratch_shapes=[
            pltpu.VMEM((2, D, FF), F32),
            pltpu.VMEM((2, D, FF), F32),
            pltpu.VMEM((2, FF, D), F32),
            pltpu.SemaphoreType.DMA((2,)),
            pltpu.VMEM((D, FF), BF16),
            pltpu.VMEM((D, FF), BF16),
            pltpu.VMEM((FF, D), BF16),
        ],
    )
    return pl.pallas_call(
        functools.partial(_expert_kernel, layer=layer),
        grid_spec=grid_spec,
        out_shape=jax.ShapeDtypeStruct((n_blocks * rows * TOK_SUB, LANES), F32),
        compiler_params=_cparams(("arbitrary",)),
        name="expert_mlp",
    )(block_expert, n_used, runs, xs, w_gate, w_up, w_down)


def _combine_kernel(pos_ref, posn_ref, y_hbm, x_ref, route_ref, mod_ref, o_ref, ybuf, sem):
    i = pl.program_id(0)
    n = pl.num_programs(0)
    slot = i % 2
    tm = TM_COMB

    @pl.when(i == 0)
    def _first():
        _start_tokens(pos_ref, y_hbm, ybuf, 0, sem, 2 * tm)

    @pl.when(i + 1 < n)
    def _prefetch():
        _start_tokens(posn_ref, y_hbm, ybuf, 1 - slot, sem, 2 * tm)

    _wait_tokens(y_hbm, ybuf, slot, sem, 2 * tm)
    route = route_ref[...]
    yb = ybuf.at[slot]
    y = route[:, 4:5] * _from_token_major(yb, 0, tm) + route[:, 5:6] * _from_token_major(yb, tm, tm)
    o_ref[...] = x_ref[...] + mod_ref[0, 5:6, :] * y


def _combine_call(pos2, y_rows, x2, route, mod, seq):
    N, D = x2.shape
    tm = TM_COMB
    tpb = seq // tm
    nt = N // tm
    row = lambda i: (i, 0)
    return pl.pallas_call(
        _combine_kernel,
        grid=(nt,),
        in_specs=[
            pl.BlockSpec((1, 1, 2 * tm), lambda i: (i, 0, 0), memory_space=pltpu.SMEM),
            pl.BlockSpec((1, 1, 2 * tm), lambda i: (jnp.minimum(i + 1, nt - 1), 0, 0), memory_space=pltpu.SMEM),
            pl.BlockSpec(memory_space=pl.ANY),
            pl.BlockSpec((tm, D), row),
            pl.BlockSpec((tm, LANES), row),
            pl.BlockSpec((1, 6, D), lambda i: (i // tpb, 0, 0)),
        ],
        out_specs=pl.BlockSpec((tm, D), row),
        out_shape=jax.ShapeDtypeStruct((N, D), F32),
        scratch_shapes=[pltpu.VMEM((2, 2 * tm * TOK_SUB, LANES), F32), pltpu.SemaphoreType.DMA((2,))],
        compiler_params=_cparams(("arbitrary",)),
        name="moe_combine",
    )(pos2, pos2, y_rows, x2, route, mod)


def _rope_lane_tables(positions, rot_dim, head_w, n_rep):
    half = rot_dim // 2
    inv = jnp.power(jnp.float32(ROPE_THETA), -2.0 * jnp.arange(half, dtype=jnp.float32) / rot_dim)
    ang = positions.astype(jnp.float32)[..., None] * inv
    n = positions.shape[0] * positions.shape[1]
    cos, sin = jnp.cos(ang).reshape(n, half), jnp.sin(ang).reshape(n, half)
    lane = np.arange(head_w * n_rep) % head_w
    sel = (lane[None, :] % half == np.arange(half)[:, None]) & (lane[None, :] < rot_dim)
    sign = np.where(lane < half, -1.0, 1.0)[None, :]
    hi = lax.Precision.HIGHEST
    cos_f = jnp.dot(cos, jnp.asarray(sel, F32), precision=hi) + jnp.asarray(lane >= rot_dim, F32)[None, :]
    sin_s = jnp.dot(sin, jnp.asarray(sel * sign, F32), precision=hi)
    return cos_f, sin_s


def _layer_weights(l, w_in, q_norm, k_norm, conv_w, conv_b, conv_ln_g, conv_ln_b, conv_pw_w, conv_pw_b,
                   pool_w, pool_b, pool_scale, sgu_ln_g, sgu_ln_b, sgu_w, sgu_b, out_norm, w_out, norm2,
                   w_rg, b_rg, w_re, b_re):
    G = GROUP_W
    D = D_MODEL
    pts = np.cumsum(IN_SPLITS)[:-1].tolist()
    wq, wk, wv, wiq, wik, wiw, wb, wc, wd = jnp.split(w_in[l], pts, axis=-1)
    wiw_p = jnp.pad(wiw, ((0, 0), (0, LANES - IDX_HEADS)))
    w_in_p = jnp.concatenate([wq, wk, wv, wiq, jnp.tile(wik, (1, IDX_HEADS)), wiw_p, wb, wc, wd], axis=-1).astype(BF16)
    npool = len(POOL_WINDOWS)
    pool_bd = jnp.zeros((G, G), F32)
    for g in range(npool):
        pool_bd = lax.dynamic_update_slice(pool_bd, pool_w[l, g], (g * POOL_CH, g * POOL_CH))
    sgu_bias = jnp.repeat(sgu_b[l].T, G // SGU_HEADS, axis=1)
    w_router = jnp.concatenate([w_rg[l], w_re[l].reshape(D, N_EXPERTS),
                                jnp.zeros((D, LANES - N_EXP_GROUPS - N_EXPERTS), F32)], axis=-1).astype(BF16)
    b_router = jnp.concatenate([b_rg[l], b_re[l].reshape(N_EXPERTS),
                                jnp.zeros((LANES - N_EXP_GROUPS - N_EXPERTS,), F32)]).reshape(1, LANES)
    r1 = lambda a: a.reshape(1, -1)
    return dict(
        w_in=w_in_p,
        q_norm=jnp.tile(q_norm[l], ATT_HEADS).reshape(1, G), k_norm=jnp.tile(k_norm[l], ATT_HEADS).reshape(1, G),
        conv_w=conv_w[l], conv_b=r1(conv_b[l]), conv_ln_g=r1(conv_ln_g[l]), conv_ln_b=r1(conv_ln_b[l]),
        conv_pw_w=conv_pw_w[l].astype(BF16), conv_pw_b=r1(conv_pw_b[l]),
        pool_w=pool_bd.astype(BF16), pool_b=r1(pool_b[l]), pool_scale=r1(pool_scale[l]),
        sgu_ln_g=r1(sgu_ln_g[l]), sgu_ln_b=r1(sgu_ln_b[l]), sgu_w=sgu_w[l], sgu_b=sgu_bias,
        out_norm=r1(out_norm[l]), w_out=w_out[l].astype(BF16), norm2=r1(norm2[l]),
        w_router=w_router, b_router=b_router,
    )


def _dispatch_tables(route, cnt, n_tokens):
    rows_blk = MOE_ROWS
    e = route[:, 0:2].astype(jnp.int32)
    rank = route[:, 2:4].astype(jnp.int32)
    counts = cnt[0, :N_EXPERTS].astype(jnp.int32)
    padded = (counts + rows_blk - 1) // rows_blk * rows_blk
    pad_end = jnp.cumsum(padded)
    pad_start = pad_end - padded
    seg0 = jnp.sum(jnp.where(e[..., None] == jnp.arange(N_EXPERTS, dtype=jnp.int32), pad_start, 0), axis=-1)
    pos = seg0 + rank
    m = n_tokens * MOE_TOPK
    n_blocks = (m + N_EXPERTS * (rows_blk - 1) + rows_blk - 1) // rows_blk
    n_used = (pad_end[-1] // rows_blk).astype(jnp.int32).reshape(1)
    blk_row0 = jnp.arange(n_blocks, dtype=jnp.int32) * rows_blk
    block_expert = jnp.minimum(jnp.sum((pad_end[None, :] <= blk_row0[:, None]).astype(jnp.int32), axis=1),
                               N_EXPERTS - 1)
    pos2 = (pos * TOK_SUB).reshape(n_tokens // TM_COMB, TM_COMB, MOE_TOPK).transpose(0, 2, 1).reshape(
        -1, 1, MOE_TOPK * TM_COMB)
    tail = jnp.where(counts > 0, (pad_end - rows_blk) * TOK_SUB, -1).astype(jnp.int32)
    return block_expert, n_used, tail, pos2, n_blocks * rows_blk


def kernel(x, c, positions, w_ada, b_ada, norm1, w_in, q_norm, k_norm, conv_w, conv_b, conv_ln_g, conv_ln_b, conv_pw_w, conv_pw_b, pool_w, pool_b, pool_scale, sgu_ln_g, sgu_ln_b, sgu_w, sgu_b, out_norm, w_out, norm2, w_rg, b_rg, w_re, b_re, w_gate, w_up, w_down):
    B, S, D = x.shape
    N = B * S
    assert D == D_MODEL and S % TM_PROJ == 0 and S % TK_ATT == 0 and N % TM_COMB == 0
    depth = w_ada.shape[0]
    cos_a, sin_a = _rope_lane_tables(positions, ROPE_DIM, ATT_HEAD_DIM, ATT_HEADS)
    cos_i, sin_i = _rope_lane_tables(positions, IDX_ROPE_DIM, IDX_DIM, IDX_HEADS)
    c_pad = jnp.pad(c, ((0, (-B) % 8), (0, 0)))
    mod_all = _ada_call(c_pad, w_ada, b_ada)
    x2 = x.reshape(N, D)
    for l in range(depth):
        lw = _layer_weights(l, w_in, q_norm, k_norm, conv_w, conv_b, conv_ln_g, conv_ln_b, conv_pw_w, conv_pw_b,
                            pool_w, pool_b, pool_scale, sgu_ln_g, sgu_ln_b, sgu_w, sgu_b, out_norm, w_out, norm2,
                            w_rg, b_rg, w_re, b_re)
        mod = mod_all[l, :B].reshape(B, 6, D)
        q, k, v, iq, ik, iw, pb, pc, pd = _proj_call(
            x2, mod, norm1[l].reshape(1, D), lw["w_in"], lw["q_norm"], lw["k_norm"], cos_a, sin_a, cos_i, sin_i, S)
        oa = _dsa_call(q, k, v, iq, ik, iw, B, S)
        x_mid, h2, route, cnt = _mix_call(x2, oa, pb, pc, pd, mod, lw, S)
        block_expert, n_used, tail, pos2, n_rows = _dispatch_tables(route, cnt, N)
        xs = _dispatch_call(tail, n_used, pos2, h2, n_rows)
        y_rows = _expert_call(block_expert, n_used, xs, w_gate, w_up, w_down, l)
        x2 = _combine_call(pos2, y_rows, x_mid, route, mod, S)
    return x2.reshape(B, S, D)
```

```python
import functools

import numpy as np
import jax
import jax.numpy as jnp
from jax import lax
from jax.experimental import pallas as pl
from jax.experimental.pallas import tpu as pltpu

F32 = jnp.float32
BF16 = jnp.bfloat16
NEG_INF = float("-inf")

D_MODEL = 1024
CHUNK = 64
N_MIXERS = 4
GROUP_W = D_MODEL // N_MIXERS
ATT_HEAD_DIM = 64
ATT_HEADS = GROUP_W // ATT_HEAD_DIM
ROPE_DIM = ATT_HEAD_DIM // 4
ROPE_THETA = 500000.0
IDX_HEADS = 4
IDX_DIM = 32
IDX_ROPE_DIM = IDX_DIM // 4
TOPK_MAX = 256
CONV_WIDTH = 31
CONV_GROUPS = 4
POOL_WINDOWS = (2, 4, 8, 16)
POOL_CH = GROUP_W // 4
SGU_CHUNK = 128
SGU_HEADS = 4
N_EXP_GROUPS = 4
EXP_PER_GROUP = 8
N_EXPERTS = N_EXP_GROUPS * EXP_PER_GROUP
EXPERT_FF = 512
MOE_TOPK = 2
EPS = 1e-6
IN_SPLITS = (GROUP_W, GROUP_W, GROUP_W, IDX_HEADS * IDX_DIM, IDX_DIM, IDX_HEADS, 2 * GROUP_W, GROUP_W, 2 * GROUP_W)

LANES = 128
SUBLANES = 8
HALO = 32
W_IN_COLS = 3 * GROUP_W + 3 * LANES + 5 * GROUP_W

TM_PROJ = 512
TQ_ATT = 256
TK_ATT = 512
TM_MIX = 256
MOE_ROWS = 256
TM_COMB = 256
BISECT_ITERS = 18
VMEM_LIMIT = 56 * 1024 * 1024


def _cparams(sem):
    return pltpu.CompilerParams(dimension_semantics=sem, vmem_limit_bytes=VMEM_LIMIT)


def _lane_iota(shape):
    return lax.broadcasted_iota(jnp.int32, shape, len(shape) - 1)


def _seg_mean(y, width):
    shift = int(np.log2(width))
    grp = _lane_iota(y.shape) >> shift
    out = jnp.zeros_like(y)
    for g in range(y.shape[-1] // width):
        msk = grp == g
        s = jnp.sum(jnp.where(msk, y, 0.0), axis=-1, keepdims=True)
        out = jnp.where(msk, s, out)
    return out * (1.0 / width)


def _rope(x, cos_f, sin_s, head_w, half):
    c = x.shape[-1]
    lane = _lane_iota(x.shape) & (head_w - 1)
    partner = jnp.where(lane < half, pltpu.roll(x, c - half, 1), pltpu.roll(x, half, 1))
    return x * cos_f + partner * sin_s


def _silu(x):
    return x * jax.nn.sigmoid(x)


def _ada_kernel(c_ref, w_ref, b_ref, o_ref):
    ca = _silu(c_ref[...])
    o_ref[0] = jnp.dot(ca.astype(BF16), w_ref[0].astype(BF16), preferred_element_type=F32) + b_ref[0]


def _ada_call(c_pad, w_ada, b_ada):
    L, D, D6 = w_ada.shape
    rows = c_pad.shape[0]
    tn = D
    return pl.pallas_call(
        _ada_kernel,
        grid=(L, D6 // tn),
        in_specs=[
            pl.BlockSpec((rows, D), lambda l, j: (0, 0)),
            pl.BlockSpec((1, D, tn), lambda l, j: (l, 0, j)),
            pl.BlockSpec((1, 1, tn), lambda l, j: (l, 0, j)),
        ],
        out_specs=pl.BlockSpec((1, rows, tn), lambda l, j: (l, 0, j)),
        out_shape=jax.ShapeDtypeStruct((L, rows, D6), F32),
        compiler_params=_cparams(("arbitrary", "arbitrary")),
        name="ada_mod",
    )(c_pad, w_ada, b_ada.reshape(L, 1, D6))


def _proj_kernel(x_ref, mod_ref, n1_ref, w_ref, qn_ref, kn_ref, cosa_ref, sina_ref, cosi_ref, sini_ref,
                 q_ref, k_ref, v_ref, iq_ref, ik_ref, iw_ref, pb_ref, pc_ref, pd_ref):
    x = x_ref[...]
    ms = jnp.mean(x * x, axis=-1, keepdims=True)
    h = (x * lax.rsqrt(ms + EPS)) * n1_ref[...]
    h = h * (1.0 + mod_ref[0, 1:2, :]) + mod_ref[0, 0:1, :]
    proj = jnp.dot(h.astype(BF16), w_ref[...], preferred_element_type=F32)
    G = GROUP_W
    cos_a, sin_a = cosa_ref[...], sina_ref[...]
    cos_i, sin_i = cosi_ref[...], sini_ref[...]

    def qk(t, g_ref):
        tn = (t * lax.rsqrt(_seg_mean(t * t, ATT_HEAD_DIM) + EPS)) * g_ref[...]
        return _rope(tn, cos_a, sin_a, ATT_HEAD_DIM, ROPE_DIM // 2)

    q_ref[...] = (qk(proj[:, 0:G], qn_ref) * (ATT_HEAD_DIM ** -0.5)).T.astype(BF16)
    k_ref[...] = qk(proj[:, G:2 * G], kn_ref).astype(BF16)
    v_ref[0] = proj[:, 2 * G:3 * G].T.astype(BF16)
    o = 3 * G
    iq_ref[...] = _rope(proj[:, o:o + LANES], cos_i, sin_i, IDX_DIM, IDX_ROPE_DIM // 2).T.astype(BF16)
    ik_ref[...] = _rope(proj[:, o + LANES:o + 2 * LANES], cos_i, sin_i, IDX_DIM, IDX_ROPE_DIM // 2).astype(BF16)
    iw_ref[...] = (proj[:, o + 2 * LANES:o + 3 * LANES] * (IDX_HEADS ** -0.5)).T[0:SUBLANES, :]
    o += 3 * LANES
    pb_ref[...] = proj[:, o:o + 2 * G]
    pc_ref[...] = proj[:, o + 2 * G:o + 3 * G]
    pd_ref[...] = proj[:, o + 3 * G:o + 5 * G]


def _proj_call(x2, mod, n1, w_in_p, qn_t, kn_t, cos_a, sin_a, cos_i, sin_i, seq):
    N, D = x2.shape
    tm = TM_PROJ
    tpb = seq // tm
    G = GROUP_W
    assert tm == TK_ATT
    row = lambda i: (i, 0)
    col = lambda i: (0, i)
    fixed = lambda i: (0, 0)
    sds = jax.ShapeDtypeStruct
    out_specs = [
        pl.BlockSpec((G, tm), col),
        pl.BlockSpec((tm, G), row),
        pl.BlockSpec((1, G, tm), lambda i: (i, 0, 0)),
        pl.BlockSpec((LANES, tm), col),
        pl.BlockSpec((tm, LANES), row),
        pl.BlockSpec((SUBLANES, tm), col),
        pl.BlockSpec((tm, 2 * G), row),
        pl.BlockSpec((tm, G), row),
        pl.BlockSpec((tm, 2 * G), row),
    ]
    out_shape = [sds((G, N), BF16), sds((N, G), BF16), sds((N // tm, G, tm), BF16), sds((LANES, N), BF16),
                 sds((N, LANES), BF16), sds((SUBLANES, N), F32), sds((N, 2 * G), F32), sds((N, G), F32),
                 sds((N, 2 * G), F32)]
    return pl.pallas_call(
        _proj_kernel,
        grid=(N // tm,),
        in_specs=[
            pl.BlockSpec((tm, D), row),
            pl.BlockSpec((1, 6, D), lambda i: (i // tpb, 0, 0)),
            pl.BlockSpec((1, D), fixed),
            pl.BlockSpec((D, W_IN_COLS), fixed),
            pl.BlockSpec((1, G), fixed),
            pl.BlockSpec((1, G), fixed),
            pl.BlockSpec((tm, G), row),
            pl.BlockSpec((tm, G), row),
            pl.BlockSpec((tm, LANES), row),
            pl.BlockSpec((tm, LANES), row),
        ],
        out_specs=out_specs,
        out_shape=out_shape,
        compiler_params=_cparams(("parallel",)),
        name="norm_in_proj",
    )(x2, mod, n1, w_in_p, qn_t, kn_t, cos_a, sin_a, cos_i, sin_i)


def _pair_rhs(xt, head_rows, h0):
    head = lax.broadcasted_iota(jnp.int32, xt.shape, 0) >> int(np.log2(head_rows))
    zero = jnp.zeros_like(xt)
    return jnp.concatenate([jnp.where(head == h0, xt, zero), jnp.where(head == h0 + 1, xt, zero)], axis=1)


def _dsa_kernel(q_ref, k_ref, v_ref, iq_ref, ik_ref, iw_ref, ltri_ref, o_ref, sc_ref, lge_ref, lgo_ref, *, topk):
    tq, tk = TQ_ATT, TK_ATT
    i = pl.program_id(1)
    q0 = i * tq
    n_kv = (q0 + tq + tk - 1) // tk
    kf = float(topk)

    q_pos = _lane_iota((1, tq)) + q0
    chunk_bits = int(np.log2(CHUNK))
    key_end = ((q_pos >> chunk_bits) + 1) << chunk_bits
    key_i = lax.broadcasted_iota(jnp.int32, (tk, tq), 0)

    def fold8(x, op, fn=None):
        n_acc = 4
        rows = lambda r: slice(r * SUBLANES, (r + 1) * SUBLANES)
        piece = (lambda r: x[rows(r), :]) if fn is None else (lambda r: fn(x[rows(r), :]))
        accs = [piece(r) for r in range(n_acc)]
        for r in range(n_acc, tk // SUBLANES):
            accs[r % n_acc] = op(accs[r % n_acc], piece(r))
        return op(op(accs[0], accs[1]), op(accs[2], accs[3]))

    iqt = iq_ref[...]
    iq_pairs = [_pair_rhs(iqt, IDX_DIM, h0) for h0 in range(0, IDX_HEADS, 2)]
    iw_h = [iw_ref[h:h + 1, :] for h in range(IDX_HEADS)]

    def score_body(kc, carry):
        hi8, lo8 = carry
        ikc = ik_ref[pl.ds(pl.multiple_of(kc * tk, tk), tk), :]
        s = jnp.zeros((tk, tq), F32)
        for pi, rhs in enumerate(iq_pairs):
            d2 = jnp.dot(ikc, rhs, preferred_element_type=F32)
            for j in range(2):
                d = d2[:, j * tq:(j + 1) * tq]
                s = s + jnp.maximum(d * (IDX_DIM ** -0.5), 0.0) * iw_h[2 * pi + j]
        adm = key_i + kc * tk < key_end
        s_top = jnp.where(adm, s, NEG_INF)
        sc_ref[kc] = s_top
        return (jnp.maximum(hi8, fold8(s_top, jnp.maximum)),
                jnp.minimum(lo8, fold8(jnp.where(adm, s, jnp.inf), jnp.minimum)))

    hi8, lo8 = lax.fori_loop(0, n_kv, score_body,
                             (jnp.full((SUBLANES, tq), NEG_INF, F32), jnp.full((SUBLANES, tq), jnp.inf, F32)))
    col_max = jnp.max(hi8, axis=0, keepdims=True)
    col_min = jnp.min(lo8, axis=0, keepdims=True)

    def reduce_chunks(fns, init, combine, fold):
        def body(kc, parts):
            blk = sc_ref.at[kc]
            return tuple(combine(p, fold8(blk, combine, fn)) for p, fn in zip(parts, fns))
        parts = lax.fori_loop(0, n_kv, body, tuple(jnp.full((SUBLANES, tq), init, F32) for _ in fns))
        return [fold(p, axis=0, keepdims=True) for p in parts]

    def count(*inds):
        return reduce_chunks(inds, 0.0, jnp.add, jnp.sum)

    def col_maximum(val):
        return reduce_chunks((val,), NEG_INF, jnp.maximum, jnp.max)[0]

    small = key_end <= topk

    @pl.when(q0 + tq > topk)
    def _select():
        def bis_body(_, c):
            lo, hi = c
            mid = jnp.where(hi == jnp.inf, col_max, lo + (hi - lo) * 0.5)
            ge = count(lambda b: jnp.where(b >= mid, 1.0, 0.0))[0] >= kf
            return jnp.where(ge, mid, lo), jnp.where(ge, hi, mid)

        lo, hi = lax.fori_loop(0, BISECT_ITERS, bis_body, (col_min, jnp.full((1, tq), jnp.inf, F32)))

        def sd_cond(c):
            return c[0] > 0.0

        def sd_body(c):
            _, hi, thr, done = c
            cand = col_maximum(lambda b: jnp.where(b < hi, b, NEG_INF))
            ok = count(lambda b: jnp.where(b >= cand, 1.0, 0.0))[0] >= kf
            thr = jnp.where(done > 0.0, thr, cand)
            hi = jnp.where(done > 0.0, hi, cand)
            done = jnp.where(ok, 1.0, done)
            return jnp.sum(1.0 - done), hi, thr, done

        done0 = jnp.where(small, 1.0, 0.0)
        n0 = jnp.sum(1.0 - done0)
        _, _, thr, _ = lax.while_loop(sd_cond, sd_body, (n0, hi, jnp.full((1, tq), NEG_INF, F32), done0))
        thr = jnp.where(small, NEG_INF, thr)

        n_above, n_tied = count(lambda b: jnp.where(b > thr, 1.0, 0.0), lambda b: jnp.where(b == thr, 1.0, 0.0))
        need = kf - n_above
        excess = jnp.sum(jnp.where(jnp.where(small, 0.0, n_tied) > need, 1.0, 0.0))

        @pl.when(excess <= 0.0)
        def _keep_all_ties():
            def bias_body(kc, carry):
                blk = sc_ref[kc]
                sc_ref[kc] = jnp.where(blk == NEG_INF, NEG_INF, jnp.where(blk >= thr, 0.0, NEG_INF))
                return carry
            lax.fori_loop(0, n_kv, bias_body, 0)

        @pl.when(excess > 0.0)
        def _rank_ties():
            half = tk // 2
            ltri_top = ltri_ref[0:half, 0:half]
            ltri_bot = ltri_ref[half:tk, :]

            def bias_body(kc, seen):
                blk = sc_ref[kc]
                tied = jnp.where(blk == thr, 1.0, 0.0)
                tied16 = tied.astype(BF16)
                rank = jnp.concatenate([jnp.dot(ltri_top, tied16[0:half, :], preferred_element_type=F32),
                                        jnp.dot(ltri_bot, tied16, preferred_element_type=F32)], axis=0) + seen
                tie = jnp.where(blk == thr, jnp.where(rank <= need, 0.0, NEG_INF), NEG_INF)
                bias = jnp.where(blk > thr, 0.0, tie)
                sc_ref[kc] = jnp.where(blk == NEG_INF, NEG_INF, bias)
                return seen + jnp.sum(fold8(tied, jnp.add), axis=0, keepdims=True)

            lax.fori_loop(0, n_kv, bias_body, jnp.zeros((1, tq), F32))

    @pl.when(q0 + tq <= topk)
    def _all():
        def bias_body(kc, carry):
            sc_ref[kc] = jnp.where(sc_ref[kc] == NEG_INF, NEG_INF, 0.0)
            return carry
        lax.fori_loop(0, n_kv, bias_body, 0)

    qt = q_ref[...]
    q_pairs = [_pair_rhs(qt, ATT_HEAD_DIM, h0) for h0 in range(0, ATT_HEADS, 2)]
    dh = ATT_HEAD_DIM

    def store_logits(buf, kc):
        kblk = k_ref[pl.ds(pl.multiple_of(kc * tk, tk), tk), :]
        bias = sc_ref[kc]
        for pi, rhs in enumerate(q_pairs):
            s2 = jnp.dot(kblk, rhs, preferred_element_type=F32)
            for j in range(2):
                buf[2 * pi + j] = s2[:, j * tq:(j + 1) * tq] + bias

    def absorb(buf, kc, state):
        ms, ls, accs = state
        vt = v_ref[kc]
        ms_n, ls_n, accs_n = [], [], []
        for h in range(ATT_HEADS):
            s = buf[h]
            m_new = jnp.maximum(ms[h], jnp.max(fold8(s, jnp.maximum), axis=0, keepdims=True))
            m_safe = jnp.where(m_new == NEG_INF, 0.0, m_new)
            alpha = jnp.exp(ms[h] - m_safe)
            p = jnp.exp(s - m_safe)
            ls_n.append(alpha * ls[h] + fold8(p, jnp.add))
            pv = jnp.dot(vt[h * dh:(h + 1) * dh, :], p.astype(BF16), preferred_element_type=F32)
            accs_n.append(alpha * accs[h] + pv)
            ms_n.append(m_new)
        return tuple(ms_n), tuple(ls_n), tuple(accs_n)

    def pair_body(jj, state):
        s = 2 * jj + 1
        store_logits(lgo_ref, s)
        state = absorb(lge_ref, s - 1, state)
        store_logits(lge_ref, s + 1)
        return absorb(lgo_ref, s, state)

    state = (tuple(jnp.full((1, tq), NEG_INF, F32) for _ in range(ATT_HEADS)),
             tuple(jnp.zeros((SUBLANES, tq), F32) for _ in range(ATT_HEADS)),
             tuple(jnp.zeros((dh, tq), F32) for _ in range(ATT_HEADS)))
    store_logits(lge_ref, 0)
    n_pairs = (n_kv - 1) // 2
    state = lax.fori_loop(0, n_pairs, pair_body, state)
    last_even = 2 * n_pairs

    def tail_two(state):
        store_logits(lgo_ref, last_even + 1)
        return absorb(lgo_ref, last_even + 1, absorb(lge_ref, last_even, state))

    def tail_one(state):
        return absorb(lge_ref, last_even, state)

    _, ls, accs = lax.cond(n_kv - 1 - last_even > 0, tail_two, tail_one, state)
    out_t = jnp.concatenate([accs[h] / jnp.sum(ls[h], axis=0, keepdims=True) for h in range(ATT_HEADS)], axis=0)
    o_ref[...] = out_t.T


def _dsa_call(qt, k, vt, iqt, ik, iwt, batch, seq):
    N, G = k.shape
    tq = TQ_ATT
    nq = seq // tq
    nkc = seq // TK_ATT
    topk = min(TOPK_MAX, seq // 4)
    qcol = lambda b, i: (0, b * nq + i)
    brow = lambda b, i: (b, 0)
    return pl.pallas_call(
        functools.partial(_dsa_kernel, topk=topk),
        grid=(batch, nq),
        in_specs=[
            pl.BlockSpec((G, tq), qcol),
            pl.BlockSpec((seq, G), brow),
            pl.BlockSpec((nkc, G, TK_ATT), lambda b, i: (b, 0, 0)),
            pl.BlockSpec((LANES, tq), qcol),
            pl.BlockSpec((seq, LANES), brow),
            pl.BlockSpec((SUBLANES, tq), qcol),
            pl.BlockSpec((TK_ATT, TK_ATT), lambda b, i: (0, 0)),
        ],
        out_specs=pl.BlockSpec((tq, G), lambda b, i: (b * nq + i, 0)),
        out_shape=jax.ShapeDtypeStruct((N, G), F32),
        scratch_shapes=[pltpu.VMEM((nkc, TK_ATT, tq), F32), pltpu.VMEM((ATT_HEADS, TK_ATT, tq), F32),
                        pltpu.VMEM((ATT_HEADS, TK_ATT, tq), F32)],
        compiler_params=_cparams(("parallel", "arbitrary")),
        name="dsa_attention",
    )(qt, k, vt, iqt, ik, iwt, jnp.tril(jnp.ones((TK_ATT, TK_ATT), BF16)))


def _mix_kernel(x_ref, oa_ref, pb_ref, pbh_ref, pc_ref, pch_ref, pd_ref, mod_ref,
                cw_ref, cb_ref, clg_ref, clb_ref, cpw_ref, cpb_ref,
                pw_ref, pbias_ref, ps_ref, slg_ref, slb_ref, sw_ref, sb_ref,
                on_ref, wo_ref, n2_ref, wr_ref, br_ref,
                xo_ref, h2_ref, route_ref, cnt_ref,
                ypad_ref, ppad_ref, s2_ref, s4_ref, s8_ref, run_ref, cph_ref, *, tiles_per_batch):
    tm = TM_MIX
    G = GROUP_W
    i = pl.program_id(0)
    t_in_b = i % tiles_per_batch
    first = t_in_b == 0
    lane_g = _lane_iota((tm, G))

    def glu(pb):
        return pb[:, 0:G] * jax.nn.sigmoid(pb[:, G:2 * G])

    ypad_ref[0:HALO, :] = jnp.where(first, 0.0, glu(pbh_ref[...]))
    ypad_ref[HALO:HALO + tm, :] = glu(pb_ref[...])
    acc = jnp.zeros((tm, G), F32)
    first_off = HALO - (CONV_WIDTH - 1)
    for phase in range(SUBLANES):
        offs = [o for o in range(first_off, HALO + 1) if o % SUBLANES == phase]
        if not offs:
            continue
        span = offs[-1] - offs[0] + tm
        cph_ref[phase, 0:span, :] = ypad_ref[offs[0]:offs[0] + span, :]
        for o in offs:
            acc = acc + cw_ref[o - first_off:o - first_off + 1, :] * cph_ref[phase, o - offs[0]:o - offs[0] + tm, :]
    y = acc + cb_ref[...]
    gw = G // CONV_GROUPS
    mu = _seg_mean(y, gw)
    yc = y - mu
    var = _seg_mean(yc * yc, gw)
    y = (yc * lax.rsqrt(var + EPS)) * clg_ref[...] + clb_ref[...]
    o_b = jnp.dot(_silu(y).astype(BF16), cpw_ref[...], preferred_element_type=F32) + cpb_ref[...]

    p = pc_ref[...]
    ppad_ref[0:HALO, :] = jnp.where(first, 0.0, pch_ref[...])
    ppad_ref[HALO:HALO + tm, :] = p
    n8 = tm + HALO - 8
    s2_ref[8:8 + n8, :] = ppad_ref[8:8 + n8, :] + ppad_ref[7:7 + n8, :]
    n16 = tm + HALO - 16
    s4_ref[16:16 + n16, :] = s2_ref[16:16 + n16, :] + s2_ref[14:14 + n16, :]
    n24 = tm + HALO - 24
    s8_ref[24:24 + n24, :] = s4_ref[24:24 + n24, :] + s4_ref[20:20 + n24, :]
    s2 = s2_ref[HALO:HALO + tm, :]
    s4 = s4_ref[HALO:HALO + tm, :]
    s8 = s8_ref[HALO:HALO + tm, :]
    s16 = s8 + s8_ref[HALO - 8:HALO - 8 + tm, :]
    pgrp = lane_g >> int(np.log2(POOL_CH))
    wsum = jnp.where(pgrp == 0, s2, jnp.where(pgrp == 1, s4, jnp.where(pgrp == 2, s8, s16)))
    wlen = jnp.where(pgrp == 0, 2.0, jnp.where(pgrp == 1, 4.0, jnp.where(pgrp == 2, 8.0, 16.0)))
    tpos = (lax.broadcasted_iota(jnp.int32, (tm, G), 0) + t_in_b * tm + 1).astype(F32)
    pooled = wsum / jnp.minimum(tpos, wlen) - p
    o_c = (jnp.dot(pooled.astype(BF16), pw_ref[...], preferred_element_type=F32) + pbias_ref[...]) * ps_ref[...]

    pd = pd_ref[...]
    u, v = pd[:, 0:G], pd[:, G:2 * G]
    mu = jnp.mean(v, axis=-1, keepdims=True)
    vc = v - mu
    var = jnp.mean(vc * vc, axis=-1, keepdims=True)
    vn = ((vc * lax.rsqrt(var + EPS)) * slg_ref[...] + slb_ref[...]).astype(BF16)
    r_i = lax.broadcasted_iota(jnp.int32, (SGU_CHUNK, SGU_CHUNK), 0)
    c_i = lax.broadcasted_iota(jnp.int32, (SGU_CHUNK, SGU_CHUNK), 1)
    w_heads = [jnp.where(r_i >= c_i, sw_ref[h], 0.0).astype(BF16) for h in range(SGU_HEADS)]
    lane_c = _lane_iota((SGU_CHUNK, G)) >> int(np.log2(G // SGU_HEADS))
    mixed = []
    for n in range(tm // SGU_CHUNK):
        vch = vn[n * SGU_CHUNK:(n + 1) * SGU_CHUNK, :]
        mx = jnp.zeros((SGU_CHUNK, G), F32)
        for h in range(SGU_HEADS):
            mx = jnp.where(lane_c == h, jnp.dot(w_heads[h], vch, preferred_element_type=F32), mx)
        mixed.append(mx + sb_ref[...])
    o_d = u * jnp.concatenate(mixed, axis=0)

    proj = jnp.zeros((tm, D_MODEL), F32)
    for g, piece in enumerate((oa_ref[...], o_b, o_c, o_d)):
        ms = jnp.mean(piece * piece, axis=-1, keepdims=True)
        pn = (piece * lax.rsqrt(ms + EPS)) * on_ref[:, g * G:(g + 1) * G]
        proj = proj + jnp.dot(pn.astype(BF16), wo_ref[g * G:(g + 1) * G, :], preferred_element_type=F32)
    x_new = x_ref[...] + mod_ref[0, 2:3, :] * proj
    xo_ref[...] = x_new

    ms = jnp.mean(x_new * x_new, axis=-1, keepdims=True)
    h2 = (x_new * lax.rsqrt(ms + EPS)) * n2_ref[...]
    h2 = h2 * (1.0 + mod_ref[0, 4:5, :]) + mod_ref[0, 3:4, :]
    _to_token_major(h2_ref, h2)
    logits = jnp.dot(h2.astype(BF16), wr_ref[...], preferred_element_type=F32) + br_ref[...]
    lane = _lane_iota((tm, LANES))
    lane_f = lane.astype(F32)
    big = float(LANES)
    glog = jnp.where(lane < N_EXP_GROUPS, logits, NEG_INF)
    gmax = jnp.max(glog, axis=-1, keepdims=True)
    p_sel = 1.0 / jnp.sum(jnp.exp(glog - gmax), axis=-1, keepdims=True)
    g_idx = jnp.min(jnp.where(glog == gmax, lane_f, big), axis=-1, keepdims=True)
    e_lane = lane - N_EXP_GROUPS
    elog = jnp.where((e_lane >> 3).astype(F32) == g_idx, logits, NEG_INF)
    top1 = jnp.max(elog, axis=-1, keepdims=True)
    j1 = jnp.min(jnp.where(elog == top1, lane_f, big), axis=-1, keepdims=True)
    elog2 = jnp.where(lane_f == j1, NEG_INF, elog)
    top2 = jnp.max(elog2, axis=-1, keepdims=True)
    j2 = jnp.min(jnp.where(elog2 == top2, lane_f, big), axis=-1, keepdims=True)
    e2w = jnp.exp(top2 - top1)
    gate1 = p_sel * (1.0 / (1.0 + e2w))
    gate2 = p_sel * (e2w / (1.0 + e2w))
    e1 = j1 - float(N_EXP_GROUPS)
    e2 = j2 - float(N_EXP_GROUPS)

    @pl.when(i == 0)
    def _init():
        run_ref[...] = jnp.zeros_like(run_ref)

    onehot = jnp.where(jnp.logical_or(lane_f == e1, lane_f == e2), 1.0, 0.0)
    rr = lax.broadcasted_iota(jnp.int32, (tm, tm), 0)
    cc = lax.broadcasted_iota(jnp.int32, (tm, tm), 1)
    before = jnp.where(rr > cc, 1.0, 0.0).astype(BF16)
    prior = jnp.dot(before, onehot.astype(BF16), preferred_element_type=F32) + run_ref[0:1, :]
    rank1 = jnp.sum(jnp.where(lane_f == e1, prior, 0.0), axis=-1, keepdims=True)
    rank2 = jnp.sum(jnp.where(lane_f == e2, prior, 0.0), axis=-1, keepdims=True)
    run_new = run_ref[0:1, :] + jnp.sum(onehot, axis=0, keepdims=True)
    run_ref[...] = jnp.broadcast_to(run_new, run_ref.shape)
    cnt_ref[...] = jnp.broadcast_to(run_new, cnt_ref.shape)
    route = jnp.where(lane == 0, e1, jnp.where(lane == 1, e2, jnp.where(lane == 2, rank1, jnp.where(
        lane == 3, rank2, jnp.where(lane == 4, gate1, jnp.where(lane == 5, gate2, 0.0))))))
    route_ref[...] = route


def _mix_call(x2, oa, pb, pc, pd, mod, lw, seq):
    N, D = x2.shape
    tm = TM_MIX
    tpb = seq // tm
    G = GROUP_W
    row = lambda i: (i, 0)
    halo = lambda i: (jnp.maximum(i * (tm // HALO) - 1, 0), 0)
    fixed2 = lambda i: (0, 0)
    fixed3 = lambda i: (0, 0, 0)
    params = [lw["conv_w"], lw["conv_b"], lw["conv_ln_g"], lw["conv_ln_b"], lw["conv_pw_w"], lw["conv_pw_b"],
              lw["pool_w"], lw["pool_b"], lw["pool_scale"], lw["sgu_ln_g"], lw["sgu_ln_b"], lw["sgu_w"], lw["sgu_b"],
              lw["out_norm"], lw["w_out"], lw["norm2"], lw["w_router"], lw["b_router"]]
    param_specs = [pl.BlockSpec(p.shape, fixed3 if p.ndim == 3 else fixed2) for p in params]
    return pl.pallas_call(
        functools.partial(_mix_kernel, tiles_per_batch=tpb),
        grid=(N // tm,),
        in_specs=[
            pl.BlockSpec((tm, D), row),
            pl.BlockSpec((tm, G), row),
            pl.BlockSpec((tm, 2 * G), row),
            pl.BlockSpec((HALO, 2 * G), halo),
            pl.BlockSpec((tm, G), row),
            pl.BlockSpec((HALO, G), halo),
            pl.BlockSpec((tm, 2 * G), row),
            pl.BlockSpec((1, 6, D), lambda i: (i // tpb, 0, 0)),
        ] + param_specs,
        out_specs=[
            pl.BlockSpec((tm, D), row),
            pl.BlockSpec((tm * TOK_SUB, LANES), row),
            pl.BlockSpec((tm, LANES), row),
            pl.BlockSpec((SUBLANES, LANES), fixed2),
        ],
        out_shape=[
            jax.ShapeDtypeStruct((N, D), F32),
            jax.ShapeDtypeStruct((N * TOK_SUB, LANES), F32),
            jax.ShapeDtypeStruct((N, LANES), F32),
            jax.ShapeDtypeStruct((SUBLANES, LANES), F32),
        ],
        scratch_shapes=[pltpu.VMEM((tm + HALO, G), F32) for _ in range(5)] + [
            pltpu.VMEM((SUBLANES, LANES), F32), pltpu.VMEM((SUBLANES, tm + HALO, G), F32)],
        compiler_params=_cparams(("arbitrary",)),
        name="mixers_out_router",
    )(x2, oa, pb, pb, pc, pc, pd, mod, *params)


TOK_SUB = D_MODEL // LANES
ROW_DMA_UNROLL = 8


def _to_token_major(ref, x):
    n = x.shape[0]
    for j in range(TOK_SUB):
        ref[pl.ds(j, n, stride=TOK_SUB), :] = x[:, j * LANES:(j + 1) * LANES]


def _from_token_major(ref, tok0, n):
    return jnp.concatenate([ref[pl.ds(tok0 * TOK_SUB + j, n, stride=TOK_SUB), :] for j in range(TOK_SUB)], axis=1)


def _token_copy(src_hbm, row8, dst_ref, slot, r, sem):
    src = src_hbm.at[pl.ds(pl.multiple_of(row8, TOK_SUB), TOK_SUB), :]
    dst = dst_ref.at[slot, pl.ds(pl.multiple_of(r * TOK_SUB, TOK_SUB), TOK_SUB), :]
    return pltpu.make_async_copy(src, dst, sem.at[slot])


def _start_tokens(idx_ref, src_hbm, dst_ref, slot, sem, n):
    def body(g, carry):
        for j in range(ROW_DMA_UNROLL):
            r = g * ROW_DMA_UNROLL + j
            _token_copy(src_hbm, idx_ref[0, 0, r], dst_ref, slot, r, sem).start(priority=j % 2)
        return carry
    lax.fori_loop(0, n // ROW_DMA_UNROLL, body, 0)


def _wait_tokens(src_hbm, dst_ref, slot, sem, n):
    pltpu.make_async_copy(src_hbm.at[pl.ds(0, n * TOK_SUB), :], dst_ref.at[slot], sem.at[slot]).wait()


def _dispatch_kernel(tail_ref, nu_ref, pos_ref, h_ref, xs_hbm, zbuf, zsem, sem):
    i = pl.program_id(0)
    tm = TM_COMB
    blk = MOE_ROWS * TOK_SUB
    n_blocks = xs_hbm.shape[0] // blk

    def zero_block(row0):
        return pltpu.make_async_copy(zbuf, xs_hbm.at[pl.ds(pl.multiple_of(row0, TOK_SUB), blk), :], zsem.at[0])

    @pl.when(i == 0)
    def _zero_fill():
        zbuf[...] = jnp.zeros_like(zbuf)
        n_used = nu_ref[0]
        for e in range(N_EXPERTS):
            @pl.when(tail_ref[e] >= 0)
            def _():
                zero_block(tail_ref[e]).start()
        lax.fori_loop(n_used, n_blocks, lambda b, c: (zero_block(b * blk).start(), c)[1], 0)
        for e in range(N_EXPERTS):
            @pl.when(tail_ref[e] >= 0)
            def _():
                zero_block(tail_ref[e]).wait()
        lax.fori_loop(n_used, n_blocks, lambda b, c: (zero_block(b * blk).wait(), c)[1], 0)

    def body(g, carry):
        for j in range(ROW_DMA_UNROLL):
            r = g * ROW_DMA_UNROLL + j
            t = jnp.where(r < tm, r, r - tm)
            src = h_ref.at[pl.ds(pl.multiple_of(t * TOK_SUB, TOK_SUB), TOK_SUB), :]
            dst = xs_hbm.at[pl.ds(pl.multiple_of(pos_ref[0, 0, r], TOK_SUB), TOK_SUB), :]
            pltpu.make_async_copy(src, dst, sem.at[0]).start(priority=j % 2)
        return carry
    lax.fori_loop(0, 2 * tm // ROW_DMA_UNROLL, body, 0)
    for _ in range(MOE_TOPK):
        pltpu.make_async_copy(h_ref, xs_hbm.at[pl.ds(0, tm * TOK_SUB), :], sem.at[0]).wait()


def _dispatch_call(tail, n_used, pos2, h2, n_rows):
    nt = pos2.shape[0]
    tm = TM_COMB
    grid_spec = pltpu.PrefetchScalarGridSpec(
        num_scalar_prefetch=2,
        grid=(nt,),
        in_specs=[
            pl.BlockSpec((1, 1, 2 * tm), lambda i, *_: (i, 0, 0), memory_space=pltpu.SMEM),
            pl.BlockSpec((tm * TOK_SUB, LANES), lambda i, *_: (i, 0)),
        ],
        out_specs=pl.BlockSpec(memory_space=pl.ANY),
        scratch_shapes=[
            pltpu.VMEM((MOE_ROWS * TOK_SUB, LANES), F32),
            pltpu.SemaphoreType.DMA((1,)),
            pltpu.SemaphoreType.DMA((1,)),
        ],
    )
    return pl.pallas_call(
        _dispatch_kernel,
        grid_spec=grid_spec,
        out_shape=jax.ShapeDtypeStruct((n_rows * TOK_SUB, LANES), F32),
        compiler_params=_cparams(("arbitrary",)),
        name="moe_dispatch",
    )(tail, n_used, pos2, h2)


def _expert_kernel(be_ref, nu_ref, run_ref, x_ref, wg_hbm, wu_hbm, wd_hbm, y_ref,
                   wbuf_g, wbuf_u, wbuf_d, wsem, wg_bf, wu_bf, wd_bf, *, layer):
    i = pl.program_id(0)
    n_used = nu_ref[0]
    rows = MOE_ROWS

    def weight_copies(e, s):
        return [pltpu.make_async_copy(w_hbm.at[layer, e], buf.at[s], wsem.at[s])
                for w_hbm, buf in ((wg_hbm, wbuf_g), (wu_hbm, wbuf_u), (wd_hbm, wbuf_d))]

    @pl.when(jnp.logical_and(i == 0, n_used > 0))
    def _first():
        for cp in weight_copies(be_ref[0], 0):
            cp.start()

    @pl.when(jnp.logical_and(run_ref[0, i] == 1, i < n_used))
    def _new_expert():
        ws = run_ref[1, i]
        for cp in weight_copies(be_ref[i], ws):
            cp.wait()
        wg_bf[...] = wbuf_g[ws].astype(BF16)
        wu_bf[...] = wbuf_u[ws].astype(BF16)
        wd_bf[...] = wbuf_d[ws].astype(BF16)

        @pl.when(run_ref[2, i] >= 0)
        def _next_weights():
            for cp in weight_copies(run_ref[2, i], 1 - ws):
                cp.start()

    @pl.when(i < n_used)
    def _compute():
        xb = _from_token_major(x_ref, 0, rows).astype(BF16)
        g = jnp.dot(xb, wg_bf[...], preferred_element_type=F32)
        u = jnp.dot(xb, wu_bf[...], preferred_element_type=F32)
        hid = (_silu(g) * u).astype(BF16)
        _to_token_major(y_ref, jnp.dot(hid, wd_bf[...], preferred_element_type=F32))

    @pl.when(i >= n_used)
    def _skip():
        y_ref[...] = jnp.zeros_like(y_ref)


def _expert_call(block_expert, n_used, xs, w_gate, w_up, w_down, layer):
    rows = MOE_ROWS
    n_blocks = xs.shape[0] // (rows * TOK_SUB)
    D = D_MODEL
    FF = w_gate.shape[-1]
    idx = jnp.arange(n_blocks, dtype=jnp.int32)
    first = jnp.logical_and(idx < n_used[0], jnp.concatenate(
        [jnp.ones((1,), jnp.bool_), block_expert[1:] != block_expert[:-1]]))
    run_slot = (jnp.cumsum(first.astype(jnp.int32)) - 1) % 2
    big = jnp.int32(n_blocks)
    first_at = jnp.where(first, idx, big)
    next_first = jnp.concatenate([lax.cummin(first_at[::-1])[::-1][1:], big[None]])
    next_expert = jnp.where(next_first < big, block_expert[jnp.minimum(next_first, n_blocks - 1)], -1)
    runs = jnp.stack([first.astype(jnp.int32), run_slot, next_expert]).astype(jnp.int32)
    grid_spec = pltpu.PrefetchScalarGridSpec(
        num_scalar_prefetch=3,
        grid=(n_blocks,),
        in_specs=[
            pl.BlockSpec((rows * TOK_SUB, LANES),
                         lambda i, be, nu, rn: (jnp.minimum(i, jnp.maximum(nu[0] - 1, 0)), 0)),
            pl.BlockSpec(memory_space=pl.ANY),
            pl.BlockSpec(memory_space=pl.ANY),
            pl.BlockSpec(memory_space=pl.ANY),
        ],
        out_specs=pl.BlockSpec((rows * TOK_SUB, LANES), lambda i, be, nu, rn: (i, 0)),
        scratch_shapes=[
            pltpu.VMEM((2, D, FF), F32),
            pltpu.VMEM((2, D, FF), F32),
            pltpu.VMEM((2, FF, D), F32),
            pltpu.SemaphoreType.DMA((2,)),
            pltpu.VMEM((D, FF), BF16),
            pltpu.VMEM((D, FF), BF16),
            pltpu.VMEM((FF, D), BF16),
        ],
    )
    return pl.pallas_call(
        functools.partial(_expert_kernel, layer=layer),
        grid_spec=grid_spec,
        out_shape=jax.ShapeDtypeStruct((n_blocks * rows * TOK_SUB, LANES), F32),
        compiler_params=_cparams(("arbitrary",)),
        name="expert_mlp",
    )(block_expert, n_used, runs, xs, w_gate, w_up, w_down)


def _combine_kernel(pos_ref, posn_ref, y_hbm, x_ref, route_ref, mod_ref, o_ref, ybuf, sem):
    i = pl.program_id(0)
    n = pl.num_programs(0)
    slot = i % 2
    tm = TM_COMB

    @pl.when(i == 0)
    def _first():
        _start_tokens(pos_ref, y_hbm, ybuf, 0, sem, 2 * tm)

    @pl.when(i + 1 < n)
    def _prefetch():
        _start_tokens(posn_ref, y_hbm, ybuf, 1 - slot, sem, 2 * tm)

    _wait_tokens(y_hbm, ybuf, slot, sem, 2 * tm)
    route = route_ref[...]
    yb = ybuf.at[slot]
    y = route[:, 4:5] * _from_token_major(yb, 0, tm) + route[:, 5:6] * _from_token_major(yb, tm, tm)
    o_ref[...] = x_ref[...] + mod_ref[0, 5:6, :] * y


def _combine_call(pos2, y_rows, x2, route, mod, seq):
    N, D = x2.shape
    tm = TM_COMB
    tpb = seq // tm
    nt = N // tm
    row = lambda i: (i, 0)
    return pl.pallas_call(
        _combine_kernel,
        grid=(nt,),
        in_specs=[
            pl.BlockSpec((1, 1, 2 * tm), lambda i: (i, 0, 0), memory_space=pltpu.SMEM),
            pl.BlockSpec((1, 1, 2 * tm), lambda i: (jnp.minimum(i + 1, nt - 1), 0, 0), memory_space=pltpu.SMEM),
            pl.BlockSpec(memory_space=pl.ANY),
            pl.BlockSpec((tm, D), row),
            pl.BlockSpec((tm, LANES), row),
            pl.BlockSpec((1, 6, D), lambda i: (i // tpb, 0, 0)),
        ],
        out_specs=pl.BlockSpec((tm, D), row),
        out_shape=jax.ShapeDtypeStruct((N, D), F32),
        scratch_shapes=[pltpu.VMEM((2, 2 * tm * TOK_SUB, LANES), F32), pltpu.SemaphoreType.DMA((2,))],
        compiler_params=_cparams(("arbitrary",)),
        name="moe_combine",
    )(pos2, pos2, y_rows, x2, route, mod)


def _rope_lane_tables(positions, rot_dim, head_w, n_rep):
    half = rot_dim // 2
    inv = jnp.power(jnp.float32(ROPE_THETA), -2.0 * jnp.arange(half, dtype=jnp.float32) / rot_dim)
    ang = positions.astype(jnp.float32)[..., None] * inv
    n = positions.shape[0] * positions.shape[1]
    cos, sin = jnp.cos(ang).reshape(n, half), jnp.sin(ang).reshape(n, half)
    lane = np.arange(head_w * n_rep) % head_w
    sel = (lane[None, :] % half == np.arange(half)[:, None]) & (lane[None, :] < rot_dim)
    sign = np.where(lane < half, -1.0, 1.0)[None, :]
    hi = lax.Precision.HIGHEST
    cos_f = jnp.dot(cos, jnp.asarray(sel, F32), precision=hi) + jnp.asarray(lane >= rot_dim, F32)[None, :]
    sin_s = jnp.dot(sin, jnp.asarray(sel * sign, F32), precision=hi)
    return cos_f, sin_s


def _layer_weights(l, w_in, q_norm, k_norm, conv_w, conv_b, conv_ln_g, conv_ln_b, conv_pw_w, conv_pw_b,
                   pool_w, pool_b, pool_scale, sgu_ln_g, sgu_ln_b, sgu_w, sgu_b, out_norm, w_out, norm2,
                   w_rg, b_rg, w_re, b_re):
    G = GROUP_W
    D = D_MODEL
    pts = np.cumsum(IN_SPLITS)[:-1].tolist()
    wq, wk, wv, wiq, wik, wiw, wb, wc, wd = jnp.split(w_in[l], pts, axis=-1)
    wiw_p = jnp.pad(wiw, ((0, 0), (0, LANES - IDX_HEADS)))
    w_in_p = jnp.concatenate([wq, wk, wv, wiq, jnp.tile(wik, (1, IDX_HEADS)), wiw_p, wb, wc, wd], axis=-1).astype(BF16)
    npool = len(POOL_WINDOWS)
    pool_bd = jnp.zeros((G, G), F32)
    for g in range(npool):
        pool_bd = lax.dynamic_update_slice(pool_bd, pool_w[l, g], (g * POOL_CH, g * POOL_CH))
    sgu_bias = jnp.repeat(sgu_b[l].T, G // SGU_HEADS, axis=1)
    w_router = jnp.concatenate([w_rg[l], w_re[l].reshape(D, N_EXPERTS),
                                jnp.zeros((D, LANES - N_EXP_GROUPS - N_EXPERTS), F32)], axis=-1).astype(BF16)
    b_router = jnp.concatenate([b_rg[l], b_re[l].reshape(N_EXPERTS),
                                jnp.zeros((LANES - N_EXP_GROUPS - N_EXPERTS,), F32)]).reshape(1, LANES)
    r1 = lambda a: a.reshape(1, -1)
    return dict(
        w_in=w_in_p,
        q_norm=jnp.tile(q_norm[l], ATT_HEADS).reshape(1, G), k_norm=jnp.tile(k_norm[l], ATT_HEADS).reshape(1, G),
        conv_w=conv_w[l], conv_b=r1(conv_b[l]), conv_ln_g=r1(conv_ln_g[l]), conv_ln_b=r1(conv_ln_b[l]),
        conv_pw_w=conv_pw_w[l].astype(BF16), conv_pw_b=r1(conv_pw_b[l]),
        pool_w=pool_bd.astype(BF16), pool_b=r1(pool_b[l]), pool_scale=r1(pool_scale[l]),
        sgu_ln_g=r1(sgu_ln_g[l]), sgu_ln_b=r1(sgu_ln_b[l]), sgu_w=sgu_w[l], sgu_b=sgu_bias,
        out_norm=r1(out_norm[l]), w_out=w_out[l].astype(BF16), norm2=r1(norm2[l]),
        w_router=w_router, b_router=b_router,
    )


def _dispatch_tables(route, cnt, n_tokens):
    rows_blk = MOE_ROWS
    e = route[:, 0:2].astype(jnp.int32)
    rank = route[:, 2:4].astype(jnp.int32)
    counts = cnt[0, :N_EXPERTS].astype(jnp.int32)
    padded = (counts + rows_blk - 1) // rows_blk * rows_blk
    pad_end = jnp.cumsum(padded)
    pad_start = pad_end - padded
    seg0 = jnp.sum(jnp.where(e[..., None] == jnp.arange(N_EXPERTS, dtype=jnp.int32), pad_start, 0), axis=-1)
    pos = seg0 + rank
    m = n_tokens * MOE_TOPK
    n_blocks = (m + N_EXPERTS * (rows_blk - 1) + rows_blk - 1) // rows_blk
    n_used = (pad_end[-1] // rows_blk).astype(jnp.int32).reshape(1)
    blk_row0 = jnp.arange(n_blocks, dtype=jnp.int32) * rows_blk
    block_expert = jnp.minimum(jnp.sum((pad_end[None, :] <= blk_row0[:, None]).astype(jnp.int32), axis=1),
                               N_EXPERTS - 1)
    pos2 = (pos * TOK_SUB).reshape(n_tokens // TM_COMB, TM_COMB, MOE_TOPK).transpose(0, 2, 1).reshape(
        -1, 1, MOE_TOPK * TM_COMB)
    tail = jnp.where(counts > 0, (pad_end - rows_blk) * TOK_SUB, -1).astype(jnp.int32)
    return block_expert, n_used, tail, pos2, n_blocks * rows_blk


def kernel(x, c, positions, w_ada, b_ada, norm1, w_in, q_norm, k_norm, conv_w, conv_b, conv_ln_g, conv_ln_b, conv_pw_w, conv_pw_b, pool_w, pool_b, pool_scale, sgu_ln_g, sgu_ln_b, sgu_w, sgu_b, out_norm, w_out, norm2, w_rg, b_rg, w_re, b_re, w_gate, w_up, w_down):
    B, S, D = x.shape
    N = B * S
    assert D == D_MODEL and S % TM_PROJ == 0 and S % TK_ATT == 0 and N % TM_COMB == 0
    depth = w_ada.shape[0]
    cos_a, sin_a = _rope_lane_tables(positions, ROPE_DIM, ATT_HEAD_DIM, ATT_HEADS)
    cos_i, sin_i = _rope_lane_tables(positions, IDX_ROPE_DIM, IDX_DIM, IDX_HEADS)
    c_pad = jnp.pad(c, ((0, (-B) % SUBLANES), (0, 0)))
    mod_all = _ada_call(c_pad, w_ada, b_ada)
    x2 = x.reshape(N, D)
    for l in range(depth):
        lw = _layer_weights(l, w_in, q_norm, k_norm, conv_w, conv_b, conv_ln_g, conv_ln_b, conv_pw_w, conv_pw_b,
                            pool_w, pool_b, pool_scale, sgu_ln_g, sgu_ln_b, sgu_w, sgu_b, out_norm, w_out, norm2,
                            w_rg, b_rg, w_re, b_re)
        mod = mod_all[l, :B].reshape(B, 6, D)
        q, k, v, iq, ik, iw, pb, pc, pd = _proj_call(
            x2, mod, norm1[l].reshape(1, D), lw["w_in"], lw["q_norm"], lw["k_norm"], cos_a, sin_a, cos_i, sin_i, S)
        oa = _dsa_call(q, k, v, iq, ik, iw, B, S)
        x_mid, h2, route, cnt = _mix_call(x2, oa, pb, pc, pd, mod, lw, S)
        block_expert, n_used, tail, pos2, n_rows = _dispatch_tables(route, cnt, N)
        xs = _dispatch_call(tail, n_used, pos2, h2, n_rows)
        y_rows = _expert_call(block_expert, n_used, xs, w_gate, w_up, w_down, l)
        x2 = _combine_call(pos2, y_rows, x_mid, route, mod, S)
    return x2.reshape(B, S, D)
```

```python
import functools

import numpy as np
import jax
import jax.numpy as jnp
from jax import lax
from jax.experimental import pallas as pl
from jax.experimental.pallas import tpu as pltpu

F32 = jnp.float32
BF16 = jnp.bfloat16
NEG_INF = float("-inf")

D_MODEL = 1024
CHUNK = 64
N_MIXERS = 4
GROUP_W = D_MODEL // N_MIXERS
ATT_HEAD_DIM = 64
ATT_HEADS = GROUP_W // ATT_HEAD_DIM
ROPE_DIM = ATT_HEAD_DIM // 4
ROPE_THETA = 500000.0
IDX_HEADS = 4
IDX_DIM = 32
IDX_ROPE_DIM = IDX_DIM // 4
TOPK_MAX = 256
CONV_WIDTH = 31
CONV_GROUPS = 4
POOL_WINDOWS = (2, 4, 8, 16)
POOL_CH = GROUP_W // 4
SGU_CHUNK = 128
SGU_HEADS = 4
N_EXP_GROUPS = 4
EXP_PER_GROUP = 8
N_EXPERTS = N_EXP_GROUPS * EXP_PER_GROUP
EXPERT_FF = 512
MOE_TOPK = 2
EPS = 1e-6
IN_SPLITS = (GROUP_W, GROUP_W, GROUP_W, IDX_HEADS * IDX_DIM, IDX_DIM, IDX_HEADS, 2 * GROUP_W, GROUP_W, 2 * GROUP_W)

LANES = 128
SUBLANES = 8
HALO = 32
W_IN_COLS = 3 * GROUP_W + 3 * LANES + 5 * GROUP_W

TM_PROJ = 512
TQ_ATT = 256
TK_ATT = 512
TM_MIX = 256
MOE_ROWS = 256
TM_COMB = 256
BISECT_ITERS = 18
VMEM_LIMIT = 56 * 1024 * 1024


def _cparams(sem):
    return pltpu.CompilerParams(dimension_semantics=sem, vmem_limit_bytes=VMEM_LIMIT)


def _lane_iota(shape):
    return lax.broadcasted_iota(jnp.int32, shape, len(shape) - 1)


def _seg_mean(y, width):
    shift = int(np.log2(width))
    grp = _lane_iota(y.shape) >> shift
    out = jnp.zeros_like(y)
    for g in range(y.shape[-1] // width):
        msk = grp == g
        s = jnp.sum(jnp.where(msk, y, 0.0), axis=-1, keepdims=True)
        out = jnp.where(msk, s, out)
    return out * (1.0 / width)


def _rope(x, cos_f, sin_s, head_w, half):
    c = x.shape[-1]
    lane = _lane_iota(x.shape) & (head_w - 1)
    partner = jnp.where(lane < half, pltpu.roll(x, c - half, 1), pltpu.roll(x, half, 1))
    return x * cos_f + partner * sin_s


def _silu(x):
    return x * jax.nn.sigmoid(x)


def _ada_kernel(c_ref, w_ref, b_ref, o_ref):
    ca = _silu(c_ref[...])
    o_ref[0] = jnp.dot(ca.astype(BF16), w_ref[0].astype(BF16), preferred_element_type=F32) + b_ref[0]


def _ada_call(c_pad, w_ada, b_ada):
    L, D, D6 = w_ada.shape
    rows = c_pad.shape[0]
    tn = D
    return pl.pallas_call(
        _ada_kernel,
        grid=(L, D6 // tn),
        in_specs=[
            pl.BlockSpec((rows, D), lambda l, j: (0, 0)),
            pl.BlockSpec((1, D, tn), lambda l, j: (l, 0, j)),
            pl.BlockSpec((1, 1, tn), lambda l, j: (l, 0, j)),
        ],
        out_specs=pl.BlockSpec((1, rows, tn), lambda l, j: (l, 0, j)),
        out_shape=jax.ShapeDtypeStruct((L, rows, D6), F32),
        compiler_params=_cparams(("arbitrary", "arbitrary")),
        name="ada_mod",
    )(c_pad, w_ada, b_ada.reshape(L, 1, D6))


def _proj_kernel(x_ref, mod_ref, n1_ref, w_ref, qn_ref, kn_ref, cosa_ref, sina_ref, cosi_ref, sini_ref,
                 q_ref, k_ref, v_ref, iq_ref, ik_ref, iw_ref, pb_ref, pc_ref, pd_ref, w_bf):
    @pl.when(pl.program_id(0) == 0)
    def _cast_weights():
        w_bf[...] = w_ref[...].astype(BF16)

    x = x_ref[...]
    ms = jnp.mean(x * x, axis=-1, keepdims=True)
    h = (x * lax.rsqrt(ms + EPS)) * n1_ref[...]
    h = h * (1.0 + mod_ref[0, 1:2, :]) + mod_ref[0, 0:1, :]
    proj = jnp.dot(h.astype(BF16), w_bf[...], preferred_element_type=F32)
    G = GROUP_W
    cos_a, sin_a = cosa_ref[...], sina_ref[...]
    cos_i, sin_i = cosi_ref[...], sini_ref[...]

    def qk(t, g_ref):
        tn = (t * lax.rsqrt(_seg_mean(t * t, ATT_HEAD_DIM) + EPS)) * g_ref[...]
        return _rope(tn, cos_a, sin_a, ATT_HEAD_DIM, ROPE_DIM // 2)

    q_ref[...] = (qk(proj[:, 0:G], qn_ref) * (ATT_HEAD_DIM ** -0.5)).T.astype(BF16)
    k_ref[...] = qk(proj[:, G:2 * G], kn_ref).astype(BF16)
    v_ref[0] = proj[:, 2 * G:3 * G].T.astype(BF16)
    o = 3 * G
    iq_ref[...] = _rope(proj[:, o:o + LANES], cos_i, sin_i, IDX_DIM, IDX_ROPE_DIM // 2).T.astype(BF16)
    ik_ref[...] = _rope(proj[:, o + LANES:o + 2 * LANES], cos_i, sin_i, IDX_DIM, IDX_ROPE_DIM // 2).astype(BF16)
    iw_ref[...] = (proj[:, o + 2 * LANES:o + 3 * LANES] * (IDX_HEADS ** -0.5)).T[0:SUBLANES, :]
    o += 3 * LANES
    pb_ref[...] = proj[:, o:o + 2 * G]
    pc_ref[...] = proj[:, o + 2 * G:o + 3 * G]
    pd_ref[...] = proj[:, o + 3 * G:o + 5 * G]


def _proj_call(x2, mod, n1, w_in_p, qn_t, kn_t, cos_a, sin_a, cos_i, sin_i, seq):
    N, D = x2.shape
    tm = TM_PROJ
    tpb = seq // tm
    G = GROUP_W
    assert tm == TK_ATT
    row = lambda i: (i, 0)
    col = lambda i: (0, i)
    fixed = lambda i: (0, 0)
    sds = jax.ShapeDtypeStruct
    out_specs = [
        pl.BlockSpec((G, tm), col),
        pl.BlockSpec((tm, G), row),
        pl.BlockSpec((1, G, tm), lambda i: (i, 0, 0)),
        pl.BlockSpec((LANES, tm), col),
        pl.BlockSpec((tm, LANES), row),
        pl.BlockSpec((SUBLANES, tm), col),
        pl.BlockSpec((tm, 2 * G), row),
        pl.BlockSpec((tm, G), row),
        pl.BlockSpec((tm, 2 * G), row),
    ]
    out_shape = [sds((G, N), BF16), sds((N, G), BF16), sds((N // tm, G, tm), BF16), sds((LANES, N), BF16),
                 sds((N, LANES), BF16), sds((SUBLANES, N), F32), sds((N, 2 * G), F32), sds((N, G), F32),
                 sds((N, 2 * G), F32)]
    return pl.pallas_call(
        _proj_kernel,
        grid=(N // tm,),
        in_specs=[
            pl.BlockSpec((tm, D), row),
            pl.BlockSpec((1, 6, D), lambda i: (i // tpb, 0, 0)),
            pl.BlockSpec((1, D), fixed),
            pl.BlockSpec((D, W_IN_COLS), fixed, pipeline_mode=pl.Buffered(1)),
            pl.BlockSpec((1, G), fixed),
            pl.BlockSpec((1, G), fixed),
            pl.BlockSpec((tm, G), row),
            pl.BlockSpec((tm, G), row),
            pl.BlockSpec((tm, LANES), row),
            pl.BlockSpec((tm, LANES), row),
        ],
        out_specs=out_specs,
        out_shape=out_shape,
        scratch_shapes=[pltpu.VMEM((D, W_IN_COLS), BF16)],
        compiler_params=_cparams(("arbitrary",)),
        name="norm_in_proj",
    )(x2, mod, n1, w_in_p, qn_t, kn_t, cos_a, sin_a, cos_i, sin_i)


def _pair_rhs(xt, head_rows, h0):
    head = lax.broadcasted_iota(jnp.int32, xt.shape, 0) >> int(np.log2(head_rows))
    zero = jnp.zeros_like(xt)
    return jnp.concatenate([jnp.where(head == h0, xt, zero), jnp.where(head == h0 + 1, xt, zero)], axis=1)


def _dsa_kernel(q_ref, k_ref, v_ref, iq_ref, ik_ref, iw_ref, ltri_ref, o_ref, sc_ref, lge_ref, lgo_ref, *, topk):
    tq, tk = TQ_ATT, TK_ATT
    i = pl.program_id(1)
    q0 = i * tq
    n_kv = (q0 + tq + tk - 1) // tk
    kf = float(topk)

    q_pos = _lane_iota((1, tq)) + q0
    chunk_bits = int(np.log2(CHUNK))
    key_end = ((q_pos >> chunk_bits) + 1) << chunk_bits
    key_i = lax.broadcasted_iota(jnp.int32, (tk, tq), 0)

    def fold8(x, op, fn=None):
        n_acc = 4
        rows = lambda r: slice(r * SUBLANES, (r + 1) * SUBLANES)
        piece = (lambda r: x[rows(r), :]) if fn is None else (lambda r: fn(x[rows(r), :]))
        accs = [piece(r) for r in range(n_acc)]
        for r in range(n_acc, tk // SUBLANES):
            accs[r % n_acc] = op(accs[r % n_acc], piece(r))
        return op(op(accs[0], accs[1]), op(accs[2], accs[3]))

    iqt = iq_ref[...]
    iq_pairs = [_pair_rhs(iqt, IDX_DIM, h0) for h0 in range(0, IDX_HEADS, 2)]
    iw_h = [iw_ref[h:h + 1, :] for h in range(IDX_HEADS)]

    def score_body(kc, carry):
        hi8, lo8 = carry
        ikc = ik_ref[pl.ds(pl.multiple_of(kc * tk, tk), tk), :]
        s = jnp.zeros((tk, tq), F32)
        for pi, rhs in enumerate(iq_pairs):
            d2 = jnp.dot(ikc, rhs, preferred_element_type=F32)
            for j in range(2):
                d = d2[:, j * tq:(j + 1) * tq]
                s = s + jnp.maximum(d * (IDX_DIM ** -0.5), 0.0) * iw_h[2 * pi + j]
        adm = key_i + kc * tk < key_end
        s_top = jnp.where(adm, s, NEG_INF)
        sc_ref[kc] = s_top
        return (jnp.maximum(hi8, fold8(s_top, jnp.maximum)),
                jnp.minimum(lo8, fold8(jnp.where(adm, s, jnp.inf), jnp.minimum)))

    hi8, lo8 = lax.fori_loop(0, n_kv, score_body,
                             (jnp.full((SUBLANES, tq), NEG_INF, F32), jnp.full((SUBLANES, tq), jnp.inf, F32)))
    col_max = jnp.max(hi8, axis=0, keepdims=True)
    col_min = jnp.min(lo8, axis=0, keepdims=True)

    def reduce_chunks(fns, init, combine, fold):
        def body(kc, parts):
            blk = sc_ref.at[kc]
            return tuple(combine(p, fold8(blk, combine, fn)) for p, fn in zip(parts, fns))
        parts = lax.fori_loop(0, n_kv, body, tuple(jnp.full((SUBLANES, tq), init, F32) for _ in fns))
        return [fold(p, axis=0, keepdims=True) for p in parts]

    def count(*inds):
        return reduce_chunks(inds, 0.0, jnp.add, jnp.sum)

    def col_maximum(val):
        return reduce_chunks((val,), NEG_INF, jnp.maximum, jnp.max)[0]

    small = key_end <= topk

    @pl.when(q0 + tq > topk)
    def _select():
        def bis_body(_, c):
            lo, hi = c
            mid = jnp.where(hi == jnp.inf, col_max, lo + (hi - lo) * 0.5)
            ge = count(lambda b: jnp.where(b >= mid, 1.0, 0.0))[0] >= kf
            return jnp.where(ge, mid, lo), jnp.where(ge, hi, mid)

        lo, hi = lax.fori_loop(0, BISECT_ITERS, bis_body, (col_min, jnp.full((1, tq), jnp.inf, F32)))

        def sd_cond(c):
            return c[0] > 0.0

        def sd_body(c):
            _, hi, thr, done = c
            cand = col_maximum(lambda b: jnp.where(b < hi, b, NEG_INF))
            ok = count(lambda b: jnp.where(b >= cand, 1.0, 0.0))[0] >= kf
            thr = jnp.where(done > 0.0, thr, cand)
            hi = jnp.where(done > 0.0, hi, cand)
            done = jnp.where(ok, 1.0, done)
            return jnp.sum(1.0 - done), hi, thr, done

        done0 = jnp.where(small, 1.0, 0.0)
        n0 = jnp.sum(1.0 - done0)
        _, _, thr, _ = lax.while_loop(sd_cond, sd_body, (n0, hi, jnp.full((1, tq), NEG_INF, F32), done0))
        thr = jnp.where(small, NEG_INF, thr)

        n_above, n_tied = count(lambda b: jnp.where(b > thr, 1.0, 0.0), lambda b: jnp.where(b == thr, 1.0, 0.0))
        need = kf - n_above
        excess = jnp.sum(jnp.where(jnp.where(small, 0.0, n_tied) > need, 1.0, 0.0))

        @pl.when(excess <= 0.0)
        def _keep_all_ties():
            def bias_body(kc, carry):
                blk = sc_ref[kc]
                sc_ref[kc] = jnp.where(blk == NEG_INF, NEG_INF, jnp.where(blk >= thr, 0.0, NEG_INF))
                return carry
            lax.fori_loop(0, n_kv, bias_body, 0)

        @pl.when(excess > 0.0)
        def _rank_ties():
            half = tk // 2
            ltri_top = ltri_ref[0:half, 0:half]
            ltri_bot = ltri_ref[half:tk, :]

            def bias_body(kc, seen):
                blk = sc_ref[kc]
                tied = jnp.where(blk == thr, 1.0, 0.0)
                tied16 = tied.astype(BF16)
                rank = jnp.concatenate([jnp.dot(ltri_top, tied16[0:half, :], preferred_element_type=F32),
                                        jnp.dot(ltri_bot, tied16, preferred_element_type=F32)], axis=0) + seen
                tie = jnp.where(blk == thr, jnp.where(rank <= need, 0.0, NEG_INF), NEG_INF)
                bias = jnp.where(blk > thr, 0.0, tie)
                sc_ref[kc] = jnp.where(blk == NEG_INF, NEG_INF, bias)
                return seen + jnp.sum(fold8(tied, jnp.add), axis=0, keepdims=True)

            lax.fori_loop(0, n_kv, bias_body, jnp.zeros((1, tq), F32))

    @pl.when(q0 + tq <= topk)
    def _all():
        def bias_body(kc, carry):
            sc_ref[kc] = jnp.where(sc_ref[kc] == NEG_INF, NEG_INF, 0.0)
            return carry
        lax.fori_loop(0, n_kv, bias_body, 0)

    qt = q_ref[...]
    q_pairs = [_pair_rhs(qt, ATT_HEAD_DIM, h0) for h0 in range(0, ATT_HEADS, 2)]
    dh = ATT_HEAD_DIM

    def store_logits(buf, kc):
        kblk = k_ref[pl.ds(pl.multiple_of(kc * tk, tk), tk), :]
        bias = sc_ref[kc]
        for pi, rhs in enumerate(q_pairs):
            s2 = jnp.dot(kblk, rhs, preferred_element_type=F32)
            for j in range(2):
                buf[2 * pi + j] = s2[:, j * tq:(j + 1) * tq] + bias

    def absorb(buf, kc, state):
        ms, ls, accs = state
        vt = v_ref[kc]
        ms_n, ls_n, accs_n = [], [], []
        for h in range(ATT_HEADS):
            s = buf[h]
            m_new = jnp.maximum(ms[h], jnp.max(fold8(s, jnp.maximum), axis=0, keepdims=True))
            m_safe = jnp.where(m_new == NEG_INF, 0.0, m_new)
            alpha = jnp.exp(ms[h] - m_safe)
            p = jnp.exp(s - m_safe)
            ls_n.append(alpha * ls[h] + fold8(p, jnp.add))
            pv = jnp.dot(vt[h * dh:(h + 1) * dh, :], p.astype(BF16), preferred_element_type=F32)
            accs_n.append(alpha * accs[h] + pv)
            ms_n.append(m_new)
        return tuple(ms_n), tuple(ls_n), tuple(accs_n)

    def pair_body(jj, state):
        s = 2 * jj + 1
        store_logits(lgo_ref, s)
        state = absorb(lge_ref, s - 1, state)
        store_logits(lge_ref, s + 1)
        return absorb(lgo_ref, s, state)

    state = (tuple(jnp.full((1, tq), NEG_INF, F32) for _ in range(ATT_HEADS)),
             tuple(jnp.zeros((SUBLANES, tq), F32) for _ in range(ATT_HEADS)),
             tuple(jnp.zeros((dh, tq), F32) for _ in range(ATT_HEADS)))
    store_logits(lge_ref, 0)
    n_pairs = (n_kv - 1) // 2
    state = lax.fori_loop(0, n_pairs, pair_body, state)
    last_even = 2 * n_pairs

    def tail_two(state):
        store_logits(lgo_ref, last_even + 1)
        return absorb(lgo_ref, last_even + 1, absorb(lge_ref, last_even, state))

    def tail_one(state):
        return absorb(lge_ref, last_even, state)

    _, ls, accs = lax.cond(n_kv - 1 - last_even > 0, tail_two, tail_one, state)
    out_t = jnp.concatenate([accs[h] / jnp.sum(ls[h], axis=0, keepdims=True) for h in range(ATT_HEADS)], axis=0)
    o_ref[...] = out_t.T


def _dsa_call(qt, k, vt, iqt, ik, iwt, batch, seq):
    N, G = k.shape
    tq = TQ_ATT
    nq = seq // tq
    nkc = seq // TK_ATT
    topk = min(TOPK_MAX, seq // 4)
    qcol = lambda b, i: (0, b * nq + i)
    brow = lambda b, i: (b, 0)
    return pl.pallas_call(
        functools.partial(_dsa_kernel, topk=topk),
        grid=(batch, nq),
        in_specs=[
            pl.BlockSpec((G, tq), qcol),
            pl.BlockSpec((seq, G), brow),
            pl.BlockSpec((nkc, G, TK_ATT), lambda b, i: (b, 0, 0)),
            pl.BlockSpec((LANES, tq), qcol),
            pl.BlockSpec((seq, LANES), brow),
            pl.BlockSpec((SUBLANES, tq), qcol),
            pl.BlockSpec((TK_ATT, TK_ATT), lambda b, i: (0, 0)),
        ],
        out_specs=pl.BlockSpec((tq, G), lambda b, i: (b * nq + i, 0)),
        out_shape=jax.ShapeDtypeStruct((N, G), F32),
        scratch_shapes=[pltpu.VMEM((nkc, TK_ATT, tq), F32), pltpu.VMEM((ATT_HEADS, TK_ATT, tq), F32),
                        pltpu.VMEM((ATT_HEADS, TK_ATT, tq), F32)],
        compiler_params=_cparams(("parallel", "arbitrary")),
        name="dsa_attention",
    )(qt, k, vt, iqt, ik, iwt, jnp.tril(jnp.ones((TK_ATT, TK_ATT), BF16)))


def _mix_kernel(x_ref, oa_ref, pb_ref, pbh_ref, pc_ref, pch_ref, pd_ref, mod_ref,
                cw_ref, cb_ref, clg_ref, clb_ref, cpw_ref, cpb_ref,
                pw_ref, pbias_ref, ps_ref, slg_ref, slb_ref, sw_ref, sb_ref,
                on_ref, wo_ref, n2_ref, wr_ref, br_ref,
                xo_ref, h2_ref, route_ref, cnt_ref,
                ypad_ref, ppad_ref, s2_ref, s4_ref, s8_ref, run_ref, cph_ref, *, tiles_per_batch):
    tm = TM_MIX
    G = GROUP_W
    i = pl.program_id(0)
    t_in_b = i % tiles_per_batch
    first = t_in_b == 0
    lane_g = _lane_iota((tm, G))

    def glu(pb):
        return pb[:, 0:G] * jax.nn.sigmoid(pb[:, G:2 * G])

    ypad_ref[0:HALO, :] = jnp.where(first, 0.0, glu(pbh_ref[...]))
    ypad_ref[HALO:HALO + tm, :] = glu(pb_ref[...])
    acc = jnp.zeros((tm, G), F32)
    first_off = HALO - (CONV_WIDTH - 1)
    for phase in range(SUBLANES):
        offs = [o for o in range(first_off, HALO + 1) if o % SUBLANES == phase]
        if not offs:
            continue
        span = offs[-1] - offs[0] + tm
        cph_ref[phase, 0:span, :] = ypad_ref[offs[0]:offs[0] + span, :]
        for o in offs:
            acc = acc + cw_ref[o - first_off:o - first_off + 1, :] * cph_ref[phase, o - offs[0]:o - offs[0] + tm, :]
    y = acc + cb_ref[...]
    gw = G // CONV_GROUPS
    mu = _seg_mean(y, gw)
    yc = y - mu
    var = _seg_mean(yc * yc, gw)
    y = (yc * lax.rsqrt(var + EPS)) * clg_ref[...] + clb_ref[...]
    o_b = jnp.dot(_silu(y).astype(BF16), cpw_ref[...], preferred_element_type=F32) + cpb_ref[...]

    p = pc_ref[...]
    ppad_ref[0:HALO, :] = jnp.where(first, 0.0, pch_ref[...])
    ppad_ref[HALO:HALO + tm, :] = p
    n8 = tm + HALO - 8
    s2_ref[8:8 + n8, :] = ppad_ref[8:8 + n8, :] + ppad_ref[7:7 + n8, :]
    n16 = tm + HALO - 16
    s4_ref[16:16 + n16, :] = s2_ref[16:16 + n16, :] + s2_ref[14:14 + n16, :]
    n24 = tm + HALO - 24
    s8_ref[24:24 + n24, :] = s4_ref[24:24 + n24, :] + s4_ref[20:20 + n24, :]
    s2 = s2_ref[HALO:HALO + tm, :]
    s4 = s4_ref[HALO:HALO + tm, :]
    s8 = s8_ref[HALO:HALO + tm, :]
    s16 = s8 + s8_ref[HALO - 8:HALO - 8 + tm, :]
    pgrp = lane_g >> int(np.log2(POOL_CH))
    wsum = jnp.where(pgrp == 0, s2, jnp.where(pgrp == 1, s4, jnp.where(pgrp == 2, s8, s16)))
    wlen = jnp.where(pgrp == 0, 2.0, jnp.where(pgrp == 1, 4.0, jnp.where(pgrp == 2, 8.0, 16.0)))
    tpos = (lax.broadcasted_iota(jnp.int32, (tm, G), 0) + t_in_b * tm + 1).astype(F32)
    pooled = wsum / jnp.minimum(tpos, wlen) - p
    o_c = (jnp.dot(pooled.astype(BF16), pw_ref[...], preferred_element_type=F32) + pbias_ref[...]) * ps_ref[...]

    pd = pd_ref[...]
    u, v = pd[:, 0:G], pd[:, G:2 * G]
    mu = jnp.mean(v, axis=-1, keepdims=True)
    vc = v - mu
    var = jnp.mean(vc * vc, axis=-1, keepdims=True)
    vn = ((vc * lax.rsqrt(var + EPS)) * slg_ref[...] + slb_ref[...]).astype(BF16)
    r_i = lax.broadcasted_iota(jnp.int32, (SGU_CHUNK, SGU_CHUNK), 0)
    c_i = lax.broadcasted_iota(jnp.int32, (SGU_CHUNK, SGU_CHUNK), 1)
    w_heads = [jnp.where(r_i >= c_i, sw_ref[h], 0.0).astype(BF16) for h in range(SGU_HEADS)]
    lane_c = _lane_iota((SGU_CHUNK, G)) >> int(np.log2(G // SGU_HEADS))
    mixed = []
    for n in range(tm // SGU_CHUNK):
        vch = vn[n * SGU_CHUNK:(n + 1) * SGU_CHUNK, :]
        mx = jnp.zeros((SGU_CHUNK, G), F32)
        for h in range(SGU_HEADS):
            mx = jnp.where(lane_c == h, jnp.dot(w_heads[h], vch, preferred_element_type=F32), mx)
        mixed.append(mx + sb_ref[...])
    o_d = u * jnp.concatenate(mixed, axis=0)

    proj = jnp.zeros((tm, D_MODEL), F32)
    for g, piece in enumerate((oa_ref[...], o_b, o_c, o_d)):
        ms = jnp.mean(piece * piece, axis=-1, keepdims=True)
        pn = (piece * lax.rsqrt(ms + EPS)) * on_ref[:, g * G:(g + 1) * G]
        proj = proj + jnp.dot(pn.astype(BF16), wo_ref[g * G:(g + 1) * G, :], preferred_element_type=F32)
    x_new = x_ref[...] + mod_ref[0, 2:3, :] * proj
    xo_ref[...] = x_new

    ms = jnp.mean(x_new * x_new, axis=-1, keepdims=True)
    h2 = (x_new * lax.rsqrt(ms + EPS)) * n2_ref[...]
    h2 = h2 * (1.0 + mod_ref[0, 4:5, :]) + mod_ref[0, 3:4, :]
    _to_token_major(h2_ref, h2)
    logits = jnp.dot(h2.astype(BF16), wr_ref[...], preferred_element_type=F32) + br_ref[...]
    lane = _lane_iota((tm, LANES))
    lane_f = lane.astype(F32)
    big = float(LANES)
    glog = jnp.where(lane < N_EXP_GROUPS, logits, NEG_INF)
    gmax = jnp.max(glog, axis=-1, keepdims=True)
    p_sel = 1.0 / jnp.sum(jnp.exp(glog - gmax), axis=-1, keepdims=True)
    g_idx = jnp.min(jnp.where(glog == gmax, lane_f, big), axis=-1, keepdims=True)
    e_lane = lane - N_EXP_GROUPS
    elog = jnp.where((e_lane >> 3).astype(F32) == g_idx, logits, NEG_INF)
    top1 = jnp.max(elog, axis=-1, keepdims=True)
    j1 = jnp.min(jnp.where(elog == top1, lane_f, big), axis=-1, keepdims=True)
    elog2 = jnp.where(lane_f == j1, NEG_INF, elog)
    top2 = jnp.max(elog2, axis=-1, keepdims=True)
    j2 = jnp.min(jnp.where(elog2 == top2, lane_f, big), axis=-1, keepdims=True)
    e2w = jnp.exp(top2 - top1)
    gate1 = p_sel * (1.0 / (1.0 + e2w))
    gate2 = p_sel * (e2w / (1.0 + e2w))
    e1 = j1 - float(N_EXP_GROUPS)
    e2 = j2 - float(N_EXP_GROUPS)

    @pl.when(i == 0)
    def _init():
        run_ref[...] = jnp.zeros_like(run_ref)

    onehot = jnp.where(jnp.logical_or(lane_f == e1, lane_f == e2), 1.0, 0.0)
    rr = lax.broadcasted_iota(jnp.int32, (tm, tm), 0)
    cc = lax.broadcasted_iota(jnp.int32, (tm, tm), 1)
    before = jnp.where(rr > cc, 1.0, 0.0).astype(BF16)
    prior = jnp.dot(before, onehot.astype(BF16), preferred_element_type=F32) + run_ref[0:1, :]
    rank1 = jnp.sum(jnp.where(lane_f == e1, prior, 0.0), axis=-1, keepdims=True)
    rank2 = jnp.sum(jnp.where(lane_f == e2, prior, 0.0), axis=-1, keepdims=True)
    run_new = run_ref[0:1, :] + jnp.sum(onehot, axis=0, keepdims=True)
    run_ref[...] = jnp.broadcast_to(run_new, run_ref.shape)
    cnt_ref[...] = jnp.broadcast_to(run_new, cnt_ref.shape)
    route = jnp.where(lane == 0, e1, jnp.where(lane == 1, e2, jnp.where(lane == 2, rank1, jnp.where(
        lane == 3, rank2, jnp.where(lane == 4, gate1, jnp.where(lane == 5, gate2, 0.0))))))
    route_ref[...] = route


def _mix_call(x2, oa, pb, pc, pd, mod, lw, seq):
    N, D = x2.shape
    tm = TM_MIX
    tpb = seq // tm
    G = GROUP_W
    row = lambda i: (i, 0)
    halo = lambda i: (jnp.maximum(i * (tm // HALO) - 1, 0), 0)
    fixed2 = lambda i: (0, 0)
    fixed3 = lambda i: (0, 0, 0)
    params = [lw["conv_w"], lw["conv_b"], lw["conv_ln_g"], lw["conv_ln_b"], lw["conv_pw_w"], lw["conv_pw_b"],
              lw["pool_w"], lw["pool_b"], lw["pool_scale"], lw["sgu_ln_g"], lw["sgu_ln_b"], lw["sgu_w"], lw["sgu_b"],
              lw["out_norm"], lw["w_out"], lw["norm2"], lw["w_router"], lw["b_router"]]
    param_specs = [pl.BlockSpec(p.shape, fixed3 if p.ndim == 3 else fixed2) for p in params]
    return pl.pallas_call(
        functools.partial(_mix_kernel, tiles_per_batch=tpb),
        grid=(N // tm,),
        in_specs=[
            pl.BlockSpec((tm, D), row),
            pl.BlockSpec((tm, G), row),
            pl.BlockSpec((tm, 2 * G), row),
            pl.BlockSpec((HALO, 2 * G), halo),
            pl.BlockSpec((tm, G), row),
            pl.BlockSpec((HALO, G), halo),
            pl.BlockSpec((tm, 2 * G), row),
            pl.BlockSpec((1, 6, D), lambda i: (i // tpb, 0, 0)),
        ] + param_specs,
        out_specs=[
            pl.BlockSpec((tm, D), row),
            pl.BlockSpec((tm * TOK_SUB, LANES), row),
            pl.BlockSpec((tm, LANES), row),
            pl.BlockSpec((SUBLANES, LANES), fixed2),
        ],
        out_shape=[
            jax.ShapeDtypeStruct((N, D), F32),
            jax.ShapeDtypeStruct((N * TOK_SUB, LANES), F32),
            jax.ShapeDtypeStruct((N, LANES), F32),
            jax.ShapeDtypeStruct((SUBLANES, LANES), F32),
        ],
        scratch_shapes=[pltpu.VMEM((tm + HALO, G), F32) for _ in range(5)] + [
            pltpu.VMEM((SUBLANES, LANES), F32), pltpu.VMEM((SUBLANES, tm + HALO, G), F32)],
        compiler_params=_cparams(("arbitrary",)),
        name="mixers_out_router",
    )(x2, oa, pb, pb, pc, pc, pd, mod, *params)


TOK_SUB = D_MODEL // LANES
ROW_DMA_UNROLL = 8


def _to_token_major(ref, x):
    n = x.shape[0]
    for j in range(TOK_SUB):
        ref[pl.ds(j, n, stride=TOK_SUB), :] = x[:, j * LANES:(j + 1) * LANES]


def _from_token_major(ref, tok0, n):
    return jnp.concatenate([ref[pl.ds(tok0 * TOK_SUB + j, n, stride=TOK_SUB), :] for j in range(TOK_SUB)], axis=1)


def _token_copy(src_hbm, row8, dst_ref, slot, r, sem):
    src = src_hbm.at[pl.ds(pl.multiple_of(row8, TOK_SUB), TOK_SUB), :]
    dst = dst_ref.at[slot, pl.ds(pl.multiple_of(r * TOK_SUB, TOK_SUB), TOK_SUB), :]
    return pltpu.make_async_copy(src, dst, sem.at[slot])


def _start_tokens(idx_ref, src_hbm, dst_ref, slot, sem, n):
    def body(g, carry):
        for j in range(ROW_DMA_UNROLL):
            r = g * ROW_DMA_UNROLL + j
            _token_copy(src_hbm, idx_ref[0, 0, r], dst_ref, slot, r, sem).start(priority=j % 2)
        return carry
    lax.fori_loop(0, n // ROW_DMA_UNROLL, body, 0)


def _wait_tokens(src_hbm, dst_ref, slot, sem, n):
    pltpu.make_async_copy(src_hbm.at[pl.ds(0, n * TOK_SUB), :], dst_ref.at[slot], sem.at[slot]).wait()


def _dispatch_kernel(tail_ref, nu_ref, pos_ref, h_ref, xs_hbm, zbuf, zsem, sem):
    i = pl.program_id(0)
    tm = TM_COMB
    blk = MOE_ROWS * TOK_SUB
    n_blocks = xs_hbm.shape[0] // blk

    def zero_block(row0):
        return pltpu.make_async_copy(zbuf, xs_hbm.at[pl.ds(pl.multiple_of(row0, TOK_SUB), blk), :], zsem.at[0])

    @pl.when(i == 0)
    def _zero_fill():
        zbuf[...] = jnp.zeros_like(zbuf)
        n_used = nu_ref[0]
        for e in range(N_EXPERTS):
            @pl.when(tail_ref[e] >= 0)
            def _():
                zero_block(tail_ref[e]).start()
        lax.fori_loop(n_used, n_blocks, lambda b, c: (zero_block(b * blk).start(), c)[1], 0)
        for e in range(N_EXPERTS):
            @pl.when(tail_ref[e] >= 0)
            def _():
                zero_block(tail_ref[e]).wait()
        lax.fori_loop(n_used, n_blocks, lambda b, c: (zero_block(b * blk).wait(), c)[1], 0)

    def body(g, carry):
        for j in range(ROW_DMA_UNROLL):
            r = g * ROW_DMA_UNROLL + j
            t = jnp.where(r < tm, r, r - tm)
            src = h_ref.at[pl.ds(pl.multiple_of(t * TOK_SUB, TOK_SUB), TOK_SUB), :]
            dst = xs_hbm.at[pl.ds(pl.multiple_of(pos_ref[0, 0, r], TOK_SUB), TOK_SUB), :]
            pltpu.make_async_copy(src, dst, sem.at[0]).start(priority=j % 2)
        return carry
    lax.fori_loop(0, 2 * tm // ROW_DMA_UNROLL, body, 0)
    for _ in range(MOE_TOPK):
        pltpu.make_async_copy(h_ref, xs_hbm.at[pl.ds(0, tm * TOK_SUB), :], sem.at[0]).wait()


def _dispatch_call(tail, n_used, pos2, h2, n_rows):
    nt = pos2.shape[0]
    tm = TM_COMB
    grid_spec = pltpu.PrefetchScalarGridSpec(
        num_scalar_prefetch=2,
        grid=(nt,),
        in_specs=[
            pl.BlockSpec((1, 1, 2 * tm), lambda i, *_: (i, 0, 0), memory_space=pltpu.SMEM),
            pl.BlockSpec((tm * TOK_SUB, LANES), lambda i, *_: (i, 0)),
        ],
        out_specs=pl.BlockSpec(memory_space=pl.ANY),
        scratch_shapes=[
            pltpu.VMEM((MOE_ROWS * TOK_SUB, LANES), F32),
            pltpu.SemaphoreType.DMA((1,)),
            pltpu.SemaphoreType.DMA((1,)),
        ],
    )
    return pl.pallas_call(
        _dispatch_kernel,
        grid_spec=grid_spec,
        out_shape=jax.ShapeDtypeStruct((n_rows * TOK_SUB, LANES), F32),
        compiler_params=_cparams(("arbitrary",)),
        name="moe_dispatch",
    )(tail, n_used, pos2, h2)


def _expert_kernel(be_ref, nu_ref, run_ref, x_ref, wg_hbm, wu_hbm, wd_hbm, y_ref,
                   wbuf_g, wbuf_u, wbuf_d, wsem, wg_bf, wu_bf, wd_bf, *, layer):
    i = pl.program_id(0)
    n_used = nu_ref[0]
    rows = MOE_ROWS

    def weight_copies(e, s):
        return [pltpu.make_async_copy(w_hbm.at[layer, e], buf.at[s], wsem.at[s])
                for w_hbm, buf in ((wg_hbm, wbuf_g), (wu_hbm, wbuf_u), (wd_hbm, wbuf_d))]

    @pl.when(jnp.logical_and(i == 0, n_used > 0))
    def _first():
        for cp in weight_copies(be_ref[0], 0):
            cp.start()

    @pl.when(jnp.logical_and(run_ref[0, i] == 1, i < n_used))
    def _new_expert():
        ws = run_ref[1, i]
        for cp in weight_copies(be_ref[i], ws):
            cp.wait()
        wg_bf[...] = wbuf_g[ws].astype(BF16)
        wu_bf[...] = wbuf_u[ws].astype(BF16)
        wd_bf[...] = wbuf_d[ws].astype(BF16)

        @pl.when(run_ref[2, i] >= 0)
        def _next_weights():
            for cp in weight_copies(run_ref[2, i], 1 - ws):
                cp.start()

    @pl.when(i < n_used)
    def _compute():
        xb = _from_token_major(x_ref, 0, rows).astype(BF16)
        g = jnp.dot(xb, wg_bf[...], preferred_element_type=F32)
        u = jnp.dot(xb, wu_bf[...], preferred_element_type=F32)
        hid = (_silu(g) * u).astype(BF16)
        _to_token_major(y_ref, jnp.dot(hid, wd_bf[...], preferred_element_type=F32))

    @pl.when(i >= n_used)
    def _skip():
        y_ref[...] = jnp.zeros_like(y_ref)


def _expert_call(block_expert, n_used, xs, w_gate, w_up, w_down, layer):
    rows = MOE_ROWS
    n_blocks = xs.shape[0] // (rows * TOK_SUB)
    D = D_MODEL
    FF = w_gate.shape[-1]
    idx = jnp.arange(n_blocks, dtype=jnp.int32)
    first = jnp.logical_and(idx < n_used[0], jnp.concatenate(
        [jnp.ones((1,), jnp.bool_), block_expert[1:] != block_expert[:-1]]))
    run_slot = (jnp.cumsum(first.astype(jnp.int32)) - 1) % 2
    big = jnp.int32(n_blocks)
    first_at = jnp.where(first, idx, big)
    next_first = jnp.concatenate([lax.cummin(first_at[::-1])[::-1][1:], big[None]])
    next_expert = jnp.where(next_first < big, block_expert[jnp.minimum(next_first, n_blocks - 1)], -1)
    runs = jnp.stack([first.astype(jnp.int32), run_slot, next_expert]).astype(jnp.int32)
    grid_spec = pltpu.PrefetchScalarGridSpec(
        num_scalar_prefetch=3,
        grid=(n_blocks,),
        in_specs=[
            pl.BlockSpec((rows * TOK_SUB, LANES),
                         lambda i, be, nu, rn: (jnp.minimum(i, jnp.maximum(nu[0] - 1, 0)), 0)),
            pl.BlockSpec(memory_space=pl.ANY),
            pl.BlockSpec(memory_space=pl.ANY),
            pl.BlockSpec(memory_space=pl.ANY),
        ],
        out_specs=pl.BlockSpec((rows * TOK_SUB, LANES), lambda i, be, nu, rn: (i, 0)),
        scratch_shapes=[
            pltpu.VMEM((2, D, FF), F32),
            pltpu.VMEM((2, D, FF), F32),
            pltpu.VMEM((2, FF, D), F32),
            pltpu.SemaphoreType.DMA((2,)),
            pltpu.VMEM((D, FF), BF16),
            pltpu.VMEM((D, FF), BF16),
            pltpu.VMEM((FF, D), BF16),
        ],
    )
    return pl.pallas_call(
        functools.partial(_expert_kernel, layer=layer),
        grid_spec=grid_spec,
        out_shape=jax.ShapeDtypeStruct((n_blocks * rows * TOK_SUB, LANES), F32),
        compiler_params=_cparams(("arbitrary",)),
        name="expert_mlp",
    )(block_expert, n_used, runs, xs, w_gate, w_up, w_down)


def _combine_kernel(pos_ref, posn_ref, y_hbm, x_ref, route_ref, mod_ref, o_ref, ybuf, sem):
    i = pl.program_id(0)
    n = pl.num_programs(0)
    slot = i % 2
    tm = TM_COMB

    @pl.when(i == 0)
    def _first():
        _start_tokens(pos_ref, y_hbm, ybuf, 0, sem, 2 * tm)

    @pl.when(i + 1 < n)
    def _prefetch():
        _start_tokens(posn_ref, y_hbm, ybuf, 1 - slot, sem, 2 * tm)

    _wait_tokens(y_hbm, ybuf, slot, sem, 2 * tm)
    route = route_ref[...]
    yb = ybuf.at[slot]
    y = route[:, 4:5] * _from_token_major(yb, 0, tm) + route[:, 5:6] * _from_token_major(yb, tm, tm)
    o_ref[...] = x_ref[...] + mod_ref[0, 5:6, :] * y


def _combine_call(pos2, y_rows, x2, route, mod, seq):
    N, D = x2.shape
    tm = TM_COMB
    tpb = seq // tm
    nt = N // tm
    row = lambda i: (i, 0)
    return pl.pallas_call(
        _combine_kernel,
        grid=(nt,),
        in_specs=[
            pl.BlockSpec((1, 1, 2 * tm), lambda i: (i, 0, 0), memory_space=pltpu.SMEM),
            pl.BlockSpec((1, 1, 2 * tm), lambda i: (jnp.minimum(i + 1, nt - 1), 0, 0), memory_space=pltpu.SMEM),
            pl.BlockSpec(memory_space=pl.ANY),
            pl.BlockSpec((tm, D), row),
            pl.BlockSpec((tm, LANES), row),
            pl.BlockSpec((1, 6, D), lambda i: (i // tpb, 0, 0)),
        ],
        out_specs=pl.BlockSpec((tm, D), row),
        out_shape=jax.ShapeDtypeStruct((N, D), F32),
        scratch_shapes=[pltpu.VMEM((2, 2 * tm * TOK_SUB, LANES), F32), pltpu.SemaphoreType.DMA((2,))],
        compiler_params=_cparams(("arbitrary",)),
        name="moe_combine",
    )(pos2, pos2, y_rows, x2, route, mod)


def _rope_lane_tables(positions, rot_dim, head_w, n_rep):
    half = rot_dim // 2
    inv = jnp.power(jnp.float32(ROPE_THETA), -2.0 * jnp.arange(half, dtype=jnp.float32) / rot_dim)
    ang = positions.astype(jnp.float32)[..., None] * inv
    n = positions.shape[0] * positions.shape[1]
    cos, sin = jnp.cos(ang).reshape(n, half), jnp.sin(ang).reshape(n, half)
    lane = np.arange(head_w * n_rep) % head_w
    sign = jnp.asarray(np.where(lane < half, -1.0, 1.0), F32)[None, :]
    cos_f = jnp.ones((n, head_w * n_rep), F32)
    sin_s = jnp.zeros((n, head_w * n_rep), F32)
    for f in range(half):
        sel = jnp.asarray((lane % half == f) & (lane < rot_dim))[None, :]
        cos_f = jnp.where(sel, cos[:, f:f + 1], cos_f)
        sin_s = jnp.where(sel, sin[:, f:f + 1] * sign, sin_s)
    return cos_f, sin_s


def _layer_weights(l, w_in, q_norm, k_norm, conv_w, conv_b, conv_ln_g, conv_ln_b, conv_pw_w, conv_pw_b,
                   pool_w, pool_b, pool_scale, sgu_ln_g, sgu_ln_b, sgu_w, sgu_b, out_norm, w_out, norm2,
                   w_rg, b_rg, w_re, b_re):
    G = GROUP_W
    D = D_MODEL
    pts = np.cumsum(IN_SPLITS)[:-1].tolist()
    wq, wk, wv, wiq, wik, wiw, wb, wc, wd = jnp.split(w_in[l], pts, axis=-1)
    wiw_p = jnp.pad(wiw, ((0, 0), (0, LANES - IDX_HEADS)))
    w_in_p = jnp.concatenate([wq, wk, wv, wiq, jnp.tile(wik, (1, IDX_HEADS)), wiw_p, wb, wc, wd], axis=-1)
    npool = len(POOL_WINDOWS)
    pool_bd = jnp.zeros((G, G), F32)
    for g in range(npool):
        pool_bd = lax.dynamic_update_slice(pool_bd, pool_w[l, g], (g * POOL_CH, g * POOL_CH))
    sgu_bias = jnp.repeat(sgu_b[l].T, G // SGU_HEADS, axis=1)
    w_router = jnp.concatenate([w_rg[l], w_re[l].reshape(D, N_EXPERTS),
                                jnp.zeros((D, LANES - N_EXP_GROUPS - N_EXPERTS), F32)], axis=-1).astype(BF16)
    b_router = jnp.concatenate([b_rg[l], b_re[l].reshape(N_EXPERTS),
                                jnp.zeros((LANES - N_EXP_GROUPS - N_EXPERTS,), F32)]).reshape(1, LANES)
    r1 = lambda a: a.reshape(1, -1)
    return dict(
        w_in=w_in_p,
        q_norm=jnp.tile(q_norm[l], ATT_HEADS).reshape(1, G), k_norm=jnp.tile(k_norm[l], ATT_HEADS).reshape(1, G),
        conv_w=conv_w[l], conv_b=r1(conv_b[l]), conv_ln_g=r1(conv_ln_g[l]), conv_ln_b=r1(conv_ln_b[l]),
        conv_pw_w=conv_pw_w[l].astype(BF16), conv_pw_b=r1(conv_pw_b[l]),
        pool_w=pool_bd.astype(BF16), pool_b=r1(pool_b[l]), pool_scale=r1(pool_scale[l]),
        sgu_ln_g=r1(sgu_ln_g[l]), sgu_ln_b=r1(sgu_ln_b[l]), sgu_w=sgu_w[l], sgu_b=sgu_bias,
        out_norm=r1(out_norm[l]), w_out=w_out[l].astype(BF16), norm2=r1(norm2[l]),
        w_router=w_router, b_router=b_router,
    )


def _dispatch_tables(route, cnt, n_tokens):
    rows_blk = MOE_ROWS
    e = route[:, 0:2].astype(jnp.int32)
    rank = route[:, 2:4].astype(jnp.int32)
    counts = cnt[0, :N_EXPERTS].astype(jnp.int32)
    padded = (counts + rows_blk - 1) // rows_blk * rows_blk
    pad_end = jnp.cumsum(padded)
    pad_start = pad_end - padded
    seg0 = jnp.sum(jnp.where(e[..., None] == jnp.arange(N_EXPERTS, dtype=jnp.int32), pad_start, 0), axis=-1)
    pos = seg0 + rank
    m = n_tokens * MOE_TOPK
    n_blocks = (m + N_EXPERTS * (rows_blk - 1) + rows_blk - 1) // rows_blk
    n_used = (pad_end[-1] // rows_blk).astype(jnp.int32).reshape(1)
    blk_row0 = jnp.arange(n_blocks, dtype=jnp.int32) * rows_blk
    block_expert = jnp.minimum(jnp.sum((pad_end[None, :] <= blk_row0[:, None]).astype(jnp.int32), axis=1),
                               N_EXPERTS - 1)
    pos2 = (pos * TOK_SUB).reshape(n_tokens // TM_COMB, TM_COMB, MOE_TOPK).transpose(0, 2, 1).reshape(
        -1, 1, MOE_TOPK * TM_COMB)
    tail = jnp.where(counts > 0, (pad_end - rows_blk) * TOK_SUB, -1).astype(jnp.int32)
    return block_expert, n_used, tail, pos2, n_blocks * rows_blk


def kernel(x, c, positions, w_ada, b_ada, norm1, w_in, q_norm, k_norm, conv_w, conv_b, conv_ln_g, conv_ln_b, conv_pw_w, conv_pw_b, pool_w, pool_b, pool_scale, sgu_ln_g, sgu_ln_b, sgu_w, sgu_b, out_norm, w_out, norm2, w_rg, b_rg, w_re, b_re, w_gate, w_up, w_down):
    B, S, D = x.shape
    N = B * S
    assert D == D_MODEL and S % TM_PROJ == 0 and S % TK_ATT == 0 and N % TM_COMB == 0
    depth = w_ada.shape[0]
    cos_a, sin_a = _rope_lane_tables(positions, ROPE_DIM, ATT_HEAD_DIM, ATT_HEADS)
    cos_i, sin_i = _rope_lane_tables(positions, IDX_ROPE_DIM, IDX_DIM, IDX_HEADS)
    c_pad = jnp.pad(c, ((0, (-B) % SUBLANES), (0, 0)))
    mod_all = _ada_call(c_pad, w_ada, b_ada)
    x2 = x.reshape(N, D)
    for l in range(depth):
        lw = _layer_weights(l, w_in, q_norm, k_norm, conv_w, conv_b, conv_ln_g, conv_ln_b, conv_pw_w, conv_pw_b,
                            pool_w, pool_b, pool_scale, sgu_ln_g, sgu_ln_b, sgu_w, sgu_b, out_norm, w_out, norm2,
                            w_rg, b_rg, w_re, b_re)
        mod = mod_all[l, :B].reshape(B, 6, D)
        q, k, v, iq, ik, iw, pb, pc, pd = _proj_call(
            x2, mod, norm1[l].reshape(1, D), lw["w_in"], lw["q_norm"], lw["k_norm"], cos_a, sin_a, cos_i, sin_i, S)
        oa = _dsa_call(q, k, v, iq, ik, iw, B, S)
        x_mid, h2, route, cnt = _mix_call(x2, oa, pb, pc, pd, mod, lw, S)
        block_expert, n_used, tail, pos2, n_rows = _dispatch_tables(route, cnt, N)
        xs = _dispatch_call(tail, n_used, pos2, h2, n_rows)
        y_rows = _expert_call(block_expert, n_used, xs, w_gate, w_up, w_down, l)
        x2 = _combine_call(pos2, y_rows, x_mid, route, mod, S)
    return x2.reshape(B, S, D)
```

```python
import functools

import numpy as np
import jax
import jax.numpy as jnp
from jax import lax
from jax.experimental import pallas as pl
from jax.experimental.pallas import tpu as pltpu

F32 = jnp.float32
BF16 = jnp.bfloat16
NEG_INF = float("-inf")

D_MODEL = 1024
CHUNK = 64
N_MIXERS = 4
GROUP_W = D_MODEL // N_MIXERS
ATT_HEAD_DIM = 64
ATT_HEADS = GROUP_W // ATT_HEAD_DIM
ROPE_DIM = ATT_HEAD_DIM // 4
ROPE_THETA = 500000.0
IDX_HEADS = 4
IDX_DIM = 32
IDX_ROPE_DIM = IDX_DIM // 4
TOPK_MAX = 256
CONV_WIDTH = 31
CONV_GROUPS = 4
POOL_WINDOWS = (2, 4, 8, 16)
POOL_CH = GROUP_W // 4
SGU_CHUNK = 128
SGU_HEADS = 4
N_EXP_GROUPS = 4
EXP_PER_GROUP = 8
N_EXPERTS = N_EXP_GROUPS * EXP_PER_GROUP
EXPERT_FF = 512
MOE_TOPK = 2
EPS = 1e-6
IN_SPLITS = (GROUP_W, GROUP_W, GROUP_W, IDX_HEADS * IDX_DIM, IDX_DIM, IDX_HEADS, 2 * GROUP_W, GROUP_W, 2 * GROUP_W)

LANES = 128
SUBLANES = 8
HALO = 32
W_IN_COLS = 3 * GROUP_W + 3 * LANES + 5 * GROUP_W

TM_PROJ = 512
TQ_ATT = 256
TK_ATT = 512
TM_MIX = 256
MOE_ROWS = 256
TM_COMB = 256
BISECT_ITERS = 17
VMEM_LIMIT = 56 * 1024 * 1024


def _cparams(sem):
    return pltpu.CompilerParams(dimension_semantics=sem, vmem_limit_bytes=VMEM_LIMIT)


def _lane_iota(shape):
    return lax.broadcasted_iota(jnp.int32, shape, len(shape) - 1)


def _seg_mean(y, width):
    shift = int(np.log2(width))
    grp = _lane_iota(y.shape) >> shift
    out = jnp.zeros_like(y)
    for g in range(y.shape[-1] // width):
        msk = grp == g
        s = jnp.sum(jnp.where(msk, y, 0.0), axis=-1, keepdims=True)
        out = jnp.where(msk, s, out)
    return out * (1.0 / width)


def _rope(x, cos_f, sin_s, head_w, half):
    c = x.shape[-1]
    lane = _lane_iota(x.shape) & (head_w - 1)
    partner = jnp.where(lane < half, pltpu.roll(x, c - half, 1), pltpu.roll(x, half, 1))
    return x * cos_f + partner * sin_s


def _silu(x):
    return x * jax.nn.sigmoid(x)


def _ada_kernel(c_ref, w_ref, b_ref, o_ref):
    ca = _silu(c_ref[...])
    o_ref[0] = jnp.dot(ca.astype(BF16), w_ref[0].astype(BF16), preferred_element_type=F32) + b_ref[0]


def _ada_call(c_pad, w_ada, b_ada):
    L, D, D6 = w_ada.shape
    rows = c_pad.shape[0]
    tn = D
    return pl.pallas_call(
        _ada_kernel,
        grid=(L, D6 // tn),
        in_specs=[
            pl.BlockSpec((rows, D), lambda l, j: (0, 0)),
            pl.BlockSpec((1, D, tn), lambda l, j: (l, 0, j)),
            pl.BlockSpec((1, 1, tn), lambda l, j: (l, 0, j)),
        ],
        out_specs=pl.BlockSpec((1, rows, tn), lambda l, j: (l, 0, j)),
        out_shape=jax.ShapeDtypeStruct((L, rows, D6), F32),
        compiler_params=_cparams(("arbitrary", "arbitrary")),
        name="ada_mod",
    )(c_pad, w_ada, b_ada.reshape(L, 1, D6))


def _proj_kernel(x_ref, mod_ref, n1_ref, w_ref, qn_ref, kn_ref, cosa_ref, sina_ref, cosi_ref, sini_ref,
                 q_ref, k_ref, v_ref, iq_ref, ik_ref, iw_ref, pb_ref, pc_ref, pd_ref):
    x = x_ref[...]
    ms = jnp.mean(x * x, axis=-1, keepdims=True)
    h = (x * lax.rsqrt(ms + EPS)) * n1_ref[...]
    h = h * (1.0 + mod_ref[0, 1:2, :]) + mod_ref[0, 0:1, :]
    proj = jnp.dot(h.astype(BF16), w_ref[...], preferred_element_type=F32)
    G = GROUP_W
    cos_a, sin_a = cosa_ref[...], sina_ref[...]
    cos_i, sin_i = cosi_ref[...], sini_ref[...]

    def qk(t, g_ref):
        tn = (t * lax.rsqrt(_seg_mean(t * t, ATT_HEAD_DIM) + EPS)) * g_ref[...]
        return _rope(tn, cos_a, sin_a, ATT_HEAD_DIM, ROPE_DIM // 2)

    q_ref[...] = (qk(proj[:, 0:G], qn_ref) * (ATT_HEAD_DIM ** -0.5)).T.astype(BF16)
    k_ref[...] = qk(proj[:, G:2 * G], kn_ref).astype(BF16)
    v_ref[0] = proj[:, 2 * G:3 * G].T.astype(BF16)
    o = 3 * G
    iq_ref[...] = _rope(proj[:, o:o + LANES], cos_i, sin_i, IDX_DIM, IDX_ROPE_DIM // 2).T.astype(BF16)
    ik_ref[...] = _rope(proj[:, o + LANES:o + 2 * LANES], cos_i, sin_i, IDX_DIM, IDX_ROPE_DIM // 2).astype(BF16)
    iw_ref[...] = (proj[:, o + 2 * LANES:o + 3 * LANES] * (IDX_HEADS ** -0.5)).T[0:SUBLANES, :]
    o += 3 * LANES
    pb_ref[...] = proj[:, o:o + 2 * G]
    pc_ref[...] = proj[:, o + 2 * G:o + 3 * G]
    pd_ref[...] = proj[:, o + 3 * G:o + 5 * G]


def _proj_call(x2, mod, n1, w_in_p, qn_t, kn_t, cos_a, sin_a, cos_i, sin_i, seq):
    N, D = x2.shape
    tm = TM_PROJ
    tpb = seq // tm
    G = GROUP_W
    assert tm == TK_ATT
    row = lambda i: (i, 0)
    col = lambda i: (0, i)
    fixed = lambda i: (0, 0)
    sds = jax.ShapeDtypeStruct
    out_specs = [
        pl.BlockSpec((G, tm), col),
        pl.BlockSpec((tm, G), row),
        pl.BlockSpec((1, G, tm), lambda i: (i, 0, 0)),
        pl.BlockSpec((LANES, tm), col),
        pl.BlockSpec((tm, LANES), row),
        pl.BlockSpec((SUBLANES, tm), col),
        pl.BlockSpec((tm, 2 * G), row),
        pl.BlockSpec((tm, G), row),
        pl.BlockSpec((tm, 2 * G), row),
    ]
    out_shape = [sds((G, N), BF16), sds((N, G), BF16), sds((N // tm, G, tm), BF16), sds((LANES, N), BF16),
                 sds((N, LANES), BF16), sds((SUBLANES, N), F32), sds((N, 2 * G), F32), sds((N, G), F32),
                 sds((N, 2 * G), F32)]
    return pl.pallas_call(
        _proj_kernel,
        grid=(N // tm,),
        in_specs=[
            pl.BlockSpec((tm, D), row),
            pl.BlockSpec((1, 6, D), lambda i: (i // tpb, 0, 0)),
            pl.BlockSpec((1, D), fixed),
            pl.BlockSpec((D, W_IN_COLS), fixed),
            pl.BlockSpec((1, G), fixed),
            pl.BlockSpec((1, G), fixed),
            pl.BlockSpec((tm, G), row),
            pl.BlockSpec((tm, G), row),
            pl.BlockSpec((tm, LANES), row),
            pl.BlockSpec((tm, LANES), row),
        ],
        out_specs=out_specs,
        out_shape=out_shape,
        compiler_params=_cparams(("parallel",)),
        name="norm_in_proj",
    )(x2, mod, n1, w_in_p, qn_t, kn_t, cos_a, sin_a, cos_i, sin_i)


def _pair_rhs(xt, head_rows, h0):
    head = lax.broadcasted_iota(jnp.int32, xt.shape, 0) >> int(np.log2(head_rows))
    zero = jnp.zeros_like(xt)
    return jnp.concatenate([jnp.where(head == h0, xt, zero), jnp.where(head == h0 + 1, xt, zero)], axis=1)


def _dsa_kernel(q_ref, k_ref, v_ref, iq_ref, ik_ref, iw_ref, ltri_ref, o_ref, sc_ref, lge_ref, lgo_ref, *, topk):
    tq, tk = TQ_ATT, TK_ATT
    i = pl.program_id(1)
    q0 = i * tq
    n_kv = (q0 + tq + tk - 1) // tk
    kf = float(topk)

    q_pos = _lane_iota((1, tq)) + q0
    chunk_bits = int(np.log2(CHUNK))
    key_end = ((q_pos >> chunk_bits) + 1) << chunk_bits
    key_i = lax.broadcasted_iota(jnp.int32, (tk, tq), 0)

    def fold8(x, op, fn=None):
        n_acc = 4
        rows = lambda r: slice(r * SUBLANES, (r + 1) * SUBLANES)
        piece = (lambda r: x[rows(r), :]) if fn is None else (lambda r: fn(x[rows(r), :]))
        accs = [piece(r) for r in range(n_acc)]
        for r in range(n_acc, tk // SUBLANES):
            accs[r % n_acc] = op(accs[r % n_acc], piece(r))
        return op(op(accs[0], accs[1]), op(accs[2], accs[3]))

    iqt = iq_ref[...]
    iq_pairs = [_pair_rhs(iqt, IDX_DIM, h0) for h0 in range(0, IDX_HEADS, 2)]
    iw_h = [iw_ref[h:h + 1, :] for h in range(IDX_HEADS)]

    def score_body(kc, carry):
        hi8, lo8 = carry
        ikc = ik_ref[pl.ds(pl.multiple_of(kc * tk, tk), tk), :]
        s = jnp.zeros((tk, tq), F32)
        for pi, rhs in enumerate(iq_pairs):
            d2 = jnp.dot(ikc, rhs, preferred_element_type=F32)
            for j in range(2):
                d = d2[:, j * tq:(j + 1) * tq]
                s = s + jnp.maximum(d * (IDX_DIM ** -0.5), 0.0) * iw_h[2 * pi + j]
        adm = key_i + kc * tk < key_end
        s_top = jnp.where(adm, s, NEG_INF)
        sc_ref[kc] = s_top
        return (jnp.maximum(hi8, fold8(s_top, jnp.maximum)),
                jnp.minimum(lo8, fold8(jnp.where(adm, s, jnp.inf), jnp.minimum)))

    hi8, lo8 = lax.fori_loop(0, n_kv, score_body,
                             (jnp.full((SUBLANES, tq), NEG_INF, F32), jnp.full((SUBLANES, tq), jnp.inf, F32)))
    col_max = jnp.max(hi8, axis=0, keepdims=True)
    col_min = jnp.min(lo8, axis=0, keepdims=True)

    def reduce_chunks(fns, init, combine, fold):
        def body(kc, parts):
            blk = sc_ref.at[kc]
            return tuple(combine(p, fold8(blk, combine, fn)) for p, fn in zip(parts, fns))
        parts = lax.fori_loop(0, n_kv, body, tuple(jnp.full((SUBLANES, tq), init, F32) for _ in fns))
        return [fold(p, axis=0, keepdims=True) for p in parts]

    def count(*inds):
        return reduce_chunks(inds, 0.0, jnp.add, jnp.sum)

    def col_maximum(val):
        return reduce_chunks((val,), NEG_INF, jnp.maximum, jnp.max)[0]

    small = key_end <= topk

    @pl.when(q0 + tq > topk)
    def _select():
        def bis_body(_, c):
            lo, hi = c
            mid = lo + (hi - lo) * 0.5
            ge = count(lambda b: jnp.where(b >= mid, 1.0, 0.0))[0] >= kf
            return jnp.where(ge, mid, lo), jnp.where(ge, hi, mid)

        hi0 = col_max + jnp.maximum(jnp.abs(col_max), 1e-30) * (2.0 ** -20)
        lo, hi = lax.fori_loop(0, BISECT_ITERS, bis_body, (col_min, hi0))

        def sd_cond(c):
            return c[0] > 0.0

        def sd_body(c):
            _, hi, thr, done = c
            cand = col_maximum(lambda b: jnp.where(b < hi, b, NEG_INF))
            ok = count(lambda b: jnp.where(b >= cand, 1.0, 0.0))[0] >= kf
            thr = jnp.where(done > 0.0, thr, cand)
            hi = jnp.where(done > 0.0, hi, cand)
            done = jnp.where(ok, 1.0, done)
            return jnp.sum(1.0 - done), hi, thr, done

        done0 = jnp.where(small, 1.0, 0.0)
        n0 = jnp.sum(1.0 - done0)
        _, _, thr, _ = lax.while_loop(sd_cond, sd_body, (n0, hi, jnp.full((1, tq), NEG_INF, F32), done0))
        thr = jnp.where(small, NEG_INF, thr)

        n_above, n_tied = count(lambda b: jnp.where(b > thr, 1.0, 0.0), lambda b: jnp.where(b == thr, 1.0, 0.0))
        need = kf - n_above
        excess = jnp.sum(jnp.where(jnp.where(small, 0.0, n_tied) > need, 1.0, 0.0))

        @pl.when(excess <= 0.0)
        def _keep_all_ties():
            def bias_body(kc, carry):
                blk = sc_ref[kc]
                sc_ref[kc] = jnp.where(blk == NEG_INF, NEG_INF, jnp.where(blk >= thr, 0.0, NEG_INF))
                return carry
            lax.fori_loop(0, n_kv, bias_body, 0)

        @pl.when(excess > 0.0)
        def _rank_ties():
            half = tk // 2
            ltri_top = ltri_ref[0:half, 0:half]
            ltri_bot = ltri_ref[half:tk, :]

            def bias_body(kc, seen):
                blk = sc_ref[kc]
                tied = jnp.where(blk == thr, 1.0, 0.0)
                tied16 = tied.astype(BF16)
                rank = jnp.concatenate([jnp.dot(ltri_top, tied16[0:half, :], preferred_element_type=F32),
                                        jnp.dot(ltri_bot, tied16, preferred_element_type=F32)], axis=0) + seen
                tie = jnp.where(blk == thr, jnp.where(rank <= need, 0.0, NEG_INF), NEG_INF)
                bias = jnp.where(blk > thr, 0.0, tie)
                sc_ref[kc] = jnp.where(blk == NEG_INF, NEG_INF, bias)
                return seen + jnp.sum(fold8(tied, jnp.add), axis=0, keepdims=True)

            lax.fori_loop(0, n_kv, bias_body, jnp.zeros((1, tq), F32))

    @pl.when(q0 + tq <= topk)
    def _all():
        def bias_body(kc, carry):
            sc_ref[kc] = jnp.where(sc_ref[kc] == NEG_INF, NEG_INF, 0.0)
            return carry
        lax.fori_loop(0, n_kv, bias_body, 0)

    qt = q_ref[...]
    q_pairs = [_pair_rhs(qt, ATT_HEAD_DIM, h0) for h0 in range(0, ATT_HEADS, 2)]
    dh = ATT_HEAD_DIM

    def store_logits(buf, kc):
        kblk = k_ref[pl.ds(pl.multiple_of(kc * tk, tk), tk), :]
        bias = sc_ref[kc]
        for pi, rhs in enumerate(q_pairs):
            s2 = jnp.dot(kblk, rhs, preferred_element_type=F32)
            for j in range(2):
                buf[2 * pi + j] = s2[:, j * tq:(j + 1) * tq] + bias

    def absorb(buf, kc, state):
        ms, ls, accs = state
        vt = v_ref[kc]
        ms_n, ls_n, accs_n = [], [], []
        for h in range(ATT_HEADS):
            s = buf[h]
            m_new = jnp.maximum(ms[h], jnp.max(fold8(s, jnp.maximum), axis=0, keepdims=True))
            m_safe = jnp.where(m_new == NEG_INF, 0.0, m_new)
            alpha = jnp.exp(ms[h] - m_safe)
            p = jnp.exp(s - m_safe)
            ls_n.append(alpha * ls[h] + fold8(p, jnp.add))
            pv = jnp.dot(vt[h * dh:(h + 1) * dh, :], p.astype(BF16), preferred_element_type=F32)
            accs_n.append(alpha * accs[h] + pv)
            ms_n.append(m_new)
        return tuple(ms_n), tuple(ls_n), tuple(accs_n)

    def pair_body(jj, state):
        s = 2 * jj + 1
        store_logits(lgo_ref, s)
        state = absorb(lge_ref, s - 1, state)
        store_logits(lge_ref, s + 1)
        return absorb(lgo_ref, s, state)

    state = (tuple(jnp.full((1, tq), NEG_INF, F32) for _ in range(ATT_HEADS)),
             tuple(jnp.zeros((SUBLANES, tq), F32) for _ in range(ATT_HEADS)),
             tuple(jnp.zeros((dh, tq), F32) for _ in range(ATT_HEADS)))
    store_logits(lge_ref, 0)
    n_pairs = (n_kv - 1) // 2
    state = lax.fori_loop(0, n_pairs, pair_body, state)
    last_even = 2 * n_pairs

    def tail_two(state):
        store_logits(lgo_ref, last_even + 1)
        return absorb(lgo_ref, last_even + 1, absorb(lge_ref, last_even, state))

    def tail_one(state):
        return absorb(lge_ref, last_even, state)

    _, ls, accs = lax.cond(n_kv - 1 - last_even > 0, tail_two, tail_one, state)
    out_t = jnp.concatenate([accs[h] / jnp.sum(ls[h], axis=0, keepdims=True) for h in range(ATT_HEADS)], axis=0)
    o_ref[...] = out_t.T


def _dsa_call(qt, k, vt, iqt, ik, iwt, batch, seq):
    N, G = k.shape
    tq = TQ_ATT
    nq = seq // tq
    nkc = seq // TK_ATT
    topk = min(TOPK_MAX, seq // 4)
    qcol = lambda b, i: (0, b * nq + i)
    brow = lambda b, i: (b, 0)
    return pl.pallas_call(
        functools.partial(_dsa_kernel, topk=topk),
        grid=(batch, nq),
        in_specs=[
            pl.BlockSpec((G, tq), qcol),
            pl.BlockSpec((seq, G), brow),
            pl.BlockSpec((nkc, G, TK_ATT), lambda b, i: (b, 0, 0)),
            pl.BlockSpec((LANES, tq), qcol),
            pl.BlockSpec((seq, LANES), brow),
            pl.BlockSpec((SUBLANES, tq), qcol),
            pl.BlockSpec((TK_ATT, TK_ATT), lambda b, i: (0, 0)),
        ],
        out_specs=pl.BlockSpec((tq, G), lambda b, i: (b * nq + i, 0)),
        out_shape=jax.ShapeDtypeStruct((N, G), F32),
        scratch_shapes=[pltpu.VMEM((nkc, TK_ATT, tq), F32), pltpu.VMEM((ATT_HEADS, TK_ATT, tq), F32),
                        pltpu.VMEM((ATT_HEADS, TK_ATT, tq), F32)],
        compiler_params=_cparams(("parallel", "arbitrary")),
        name="dsa_attention",
    )(qt, k, vt, iqt, ik, iwt, jnp.tril(jnp.ones((TK_ATT, TK_ATT), BF16)))


def _mix_kernel(x_ref, oa_ref, pb_ref, pbh_ref, pc_ref, pch_ref, pd_ref, mod_ref,
                cw_ref, cb_ref, clg_ref, clb_ref, cpw_ref, cpb_ref,
                pw_ref, pbias_ref, ps_ref, slg_ref, slb_ref, sw_ref, sb_ref,
                on_ref, wo_ref, n2_ref, wr_ref, br_ref,
                xo_ref, h2_ref, route_ref, cnt_ref,
                ypad_ref, ppad_ref, s2_ref, s4_ref, s8_ref, run_ref, cph_ref, *, tiles_per_batch):
    tm = TM_MIX
    G = GROUP_W
    i = pl.program_id(0)
    t_in_b = i % tiles_per_batch
    first = t_in_b == 0
    lane_g = _lane_iota((tm, G))

    def glu(pb):
        return pb[:, 0:G] * jax.nn.sigmoid(pb[:, G:2 * G])

    ypad_ref[0:HALO, :] = jnp.where(first, 0.0, glu(pbh_ref[...]))
    ypad_ref[HALO:HALO + tm, :] = glu(pb_ref[...])
    acc = jnp.zeros((tm, G), F32)
    first_off = HALO - (CONV_WIDTH - 1)
    for phase in range(SUBLANES):
        offs = [o for o in range(first_off, HALO + 1) if o % SUBLANES == phase]
        if not offs:
            continue
        span = offs[-1] - offs[0] + tm
        cph_ref[phase, 0:span, :] = ypad_ref[offs[0]:offs[0] + span, :]
        for o in offs:
            acc = acc + cw_ref[o - first_off:o - first_off + 1, :] * cph_ref[phase, o - offs[0]:o - offs[0] + tm, :]
    y = acc + cb_ref[...]
    gw = G // CONV_GROUPS
    mu = _seg_mean(y, gw)
    yc = y - mu
    var = _seg_mean(yc * yc, gw)
    y = (yc * lax.rsqrt(var + EPS)) * clg_ref[...] + clb_ref[...]
    o_b = jnp.dot(_silu(y).astype(BF16), cpw_ref[...], preferred_element_type=F32) + cpb_ref[...]

    p = pc_ref[...]
    ppad_ref[0:HALO, :] = jnp.where(first, 0.0, pch_ref[...])
    ppad_ref[HALO:HALO + tm, :] = p
    n8 = tm + HALO - 8
    s2_ref[8:8 + n8, :] = ppad_ref[8:8 + n8, :] + ppad_ref[7:7 + n8, :]
    n16 = tm + HALO - 16
    s4_ref[16:16 + n16, :] = s2_ref[16:16 + n16, :] + s2_ref[14:14 + n16, :]
    n24 = tm + HALO - 24
    s8_ref[24:24 + n24, :] = s4_ref[24:24 + n24, :] + s4_ref[20:20 + n24, :]
    s2 = s2_ref[HALO:HALO + tm, :]
    s4 = s4_ref[HALO:HALO + tm, :]
    s8 = s8_ref[HALO:HALO + tm, :]
    s16 = s8 + s8_ref[HALO - 8:HALO - 8 + tm, :]
    pgrp = lane_g >> int(np.log2(POOL_CH))
    wsum = jnp.where(pgrp == 0, s2, jnp.where(pgrp == 1, s4, jnp.where(pgrp == 2, s8, s16)))
    wlen = jnp.where(pgrp == 0, 2.0, jnp.where(pgrp == 1, 4.0, jnp.where(pgrp == 2, 8.0, 16.0)))
    tpos = (lax.broadcasted_iota(jnp.int32, (tm, G), 0) + t_in_b * tm + 1).astype(F32)
    pooled = wsum / jnp.minimum(tpos, wlen) - p
    o_c = (jnp.dot(pooled.astype(BF16), pw_ref[...], preferred_element_type=F32) + pbias_ref[...]) * ps_ref[...]

    pd = pd_ref[...]
    u, v = pd[:, 0:G], pd[:, G:2 * G]
    mu = jnp.mean(v, axis=-1, keepdims=True)
    vc = v - mu
    var = jnp.mean(vc * vc, axis=-1, keepdims=True)
    vn = ((vc * lax.rsqrt(var + EPS)) * slg_ref[...] + slb_ref[...]).astype(BF16)
    r_i = lax.broadcasted_iota(jnp.int32, (SGU_CHUNK, SGU_CHUNK), 0)
    c_i = lax.broadcasted_iota(jnp.int32, (SGU_CHUNK, SGU_CHUNK), 1)
    w_heads = [jnp.where(r_i >= c_i, sw_ref[h], 0.0).astype(BF16) for h in range(SGU_HEADS)]
    lane_c = _lane_iota((SGU_CHUNK, G)) >> int(np.log2(G // SGU_HEADS))
    mixed = []
    for n in range(tm // SGU_CHUNK):
        vch = vn[n * SGU_CHUNK:(n + 1) * SGU_CHUNK, :]
        mx = jnp.zeros((SGU_CHUNK, G), F32)
        for h in range(SGU_HEADS):
            mx = jnp.where(lane_c == h, jnp.dot(w_heads[h], vch, preferred_element_type=F32), mx)
        mixed.append(mx + sb_ref[...])
    o_d = u * jnp.concatenate(mixed, axis=0)

    proj = jnp.zeros((tm, D_MODEL), F32)
    for g, piece in enumerate((oa_ref[...], o_b, o_c, o_d)):
        ms = jnp.mean(piece * piece, axis=-1, keepdims=True)
        pn = (piece * lax.rsqrt(ms + EPS)) * on_ref[:, g * G:(g + 1) * G]
        proj = proj + jnp.dot(pn.astype(BF16), wo_ref[g * G:(g + 1) * G, :], preferred_element_type=F32)
    x_new = x_ref[...] + mod_ref[0, 2:3, :] * proj
    xo_ref[...] = x_new

    ms = jnp.mean(x_new * x_new, axis=-1, keepdims=True)
    h2 = (x_new * lax.rsqrt(ms + EPS)) * n2_ref[...]
    h2 = h2 * (1.0 + mod_ref[0, 4:5, :]) + mod_ref[0, 3:4, :]
    _to_token_major(h2_ref, h2)
    logits = jnp.dot(h2.astype(BF16), wr_ref[...], preferred_element_type=F32) + br_ref[...]
    lane = _lane_iota((tm, LANES))
    lane_f = lane.astype(F32)
    big = float(LANES)
    glog = jnp.where(lane < N_EXP_GROUPS, logits, NEG_INF)
    gmax = jnp.max(glog, axis=-1, keepdims=True)
    p_sel = 1.0 / jnp.sum(jnp.exp(glog - gmax), axis=-1, keepdims=True)
    g_idx = jnp.min(jnp.where(glog == gmax, lane_f, big), axis=-1, keepdims=True)
    e_lane = lane - N_EXP_GROUPS
    elog = jnp.where((e_lane >> 3).astype(F32) == g_idx, logits, NEG_INF)
    top1 = jnp.max(elog, axis=-1, keepdims=True)
    j1 = jnp.min(jnp.where(elog == top1, lane_f, big), axis=-1, keepdims=True)
    elog2 = jnp.where(lane_f == j1, NEG_INF, elog)
    top2 = jnp.max(elog2, axis=-1, keepdims=True)
    j2 = jnp.min(jnp.where(elog2 == top2, lane_f, big), axis=-1, keepdims=True)
    e2w = jnp.exp(top2 - top1)
    gate1 = p_sel * (1.0 / (1.0 + e2w))
    gate2 = p_sel * (e2w / (1.0 + e2w))
    e1 = j1 - float(N_EXP_GROUPS)
    e2 = j2 - float(N_EXP_GROUPS)

    @pl.when(i == 0)
    def _init():
        run_ref[...] = jnp.zeros_like(run_ref)

    onehot = jnp.where(jnp.logical_or(lane_f == e1, lane_f == e2), 1.0, 0.0)
    rr = lax.broadcasted_iota(jnp.int32, (tm, tm), 0)
    cc = lax.broadcasted_iota(jnp.int32, (tm, tm), 1)
    before = jnp.where(rr > cc, 1.0, 0.0).astype(BF16)
    prior = jnp.dot(before, onehot.astype(BF16), preferred_element_type=F32) + run_ref[0:1, :]
    rank1 = jnp.sum(jnp.where(lane_f == e1, prior, 0.0), axis=-1, keepdims=True)
    rank2 = jnp.sum(jnp.where(lane_f == e2, prior, 0.0), axis=-1, keepdims=True)
    run_new = run_ref[0:1, :] + jnp.sum(onehot, axis=0, keepdims=True)
    run_ref[...] = jnp.broadcast_to(run_new, run_ref.shape)
    cnt_ref[...] = jnp.broadcast_to(run_new, cnt_ref.shape)
    route = jnp.where(lane == 0, e1, jnp.where(lane == 1, e2, jnp.where(lane == 2, rank1, jnp.where(
        lane == 3, rank2, jnp.where(lane == 4, gate1, jnp.where(lane == 5, gate2, 0.0))))))
    route_ref[...] = route


def _mix_call(x2, oa, pb, pc, pd, mod, lw, seq):
    N, D = x2.shape
    tm = TM_MIX
    tpb = seq // tm
    G = GROUP_W
    row = lambda i: (i, 0)
    halo = lambda i: (jnp.maximum(i * (tm // HALO) - 1, 0), 0)
    fixed2 = lambda i: (0, 0)
    fixed3 = lambda i: (0, 0, 0)
    params = [lw["conv_w"], lw["conv_b"], lw["conv_ln_g"], lw["conv_ln_b"], lw["conv_pw_w"], lw["conv_pw_b"],
              lw["pool_w"], lw["pool_b"], lw["pool_scale"], lw["sgu_ln_g"], lw["sgu_ln_b"], lw["sgu_w"], lw["sgu_b"],
              lw["out_norm"], lw["w_out"], lw["norm2"], lw["w_router"], lw["b_router"]]
    param_specs = [pl.BlockSpec(p.shape, fixed3 if p.ndim == 3 else fixed2) for p in params]
    return pl.pallas_call(
        functools.partial(_mix_kernel, tiles_per_batch=tpb),
        grid=(N // tm,),
        in_specs=[
            pl.BlockSpec((tm, D), row),
            pl.BlockSpec((tm, G), row),
            pl.BlockSpec((tm, 2 * G), row),
            pl.BlockSpec((HALO, 2 * G), halo),
            pl.BlockSpec((tm, G), row),
            pl.BlockSpec((HALO, G), halo),
            pl.BlockSpec((tm, 2 * G), row),
            pl.BlockSpec((1, 6, D), lambda i: (i // tpb, 0, 0)),
        ] + param_specs,
        out_specs=[
            pl.BlockSpec((tm, D), row),
            pl.BlockSpec((tm * TOK_SUB, LANES), row),
            pl.BlockSpec((tm, LANES), row),
            pl.BlockSpec((SUBLANES, LANES), fixed2),
        ],
        out_shape=[
            jax.ShapeDtypeStruct((N, D), F32),
            jax.ShapeDtypeStruct((N * TOK_SUB, LANES), F32),
            jax.ShapeDtypeStruct((N, LANES), F32),
            jax.ShapeDtypeStruct((SUBLANES, LANES), F32),
        ],
        scratch_shapes=[pltpu.VMEM((tm + HALO, G), F32) for _ in range(5)] + [
            pltpu.VMEM((SUBLANES, LANES), F32), pltpu.VMEM((SUBLANES, tm + HALO, G), F32)],
        compiler_params=_cparams(("arbitrary",)),
        name="mixers_out_router",
    )(x2, oa, pb, pb, pc, pc, pd, mod, *params)


TOK_SUB = D_MODEL // LANES
ROW_DMA_UNROLL = 8


def _to_token_major(ref, x):
    n = x.shape[0]
    for j in range(TOK_SUB):
        ref[pl.ds(j, n, stride=TOK_SUB), :] = x[:, j * LANES:(j + 1) * LANES]


def _from_token_major(ref, tok0, n):
    return jnp.concatenate([ref[pl.ds(tok0 * TOK_SUB + j, n, stride=TOK_SUB), :] for j in range(TOK_SUB)], axis=1)


def _token_copy(src_hbm, row8, dst_ref, slot, r, sem):
    src = src_hbm.at[pl.ds(pl.multiple_of(row8, TOK_SUB), TOK_SUB), :]
    dst = dst_ref.at[slot, pl.ds(pl.multiple_of(r * TOK_SUB, TOK_SUB), TOK_SUB), :]
    return pltpu.make_async_copy(src, dst, sem.at[slot])


def _start_tokens(idx_ref, src_hbm, dst_ref, slot, sem, n):
    def body(g, carry):
        for j in range(ROW_DMA_UNROLL):
            r = g * ROW_DMA_UNROLL + j
            _token_copy(src_hbm, idx_ref[0, 0, r], dst_ref, slot, r, sem).start(priority=j % 2)
        return carry
    lax.fori_loop(0, n // ROW_DMA_UNROLL, body, 0)


def _wait_tokens(src_hbm, dst_ref, slot, sem, n):
    pltpu.make_async_copy(src_hbm.at[pl.ds(0, n * TOK_SUB), :], dst_ref.at[slot], sem.at[slot]).wait()


def _dispatch_kernel(tail_ref, nu_ref, pos_ref, h_ref, xs_hbm, zbuf, zsem, sem):
    i = pl.program_id(0)
    tm = TM_COMB
    blk = MOE_ROWS * TOK_SUB
    n_blocks = xs_hbm.shape[0] // blk

    def zero_block(row0):
        return pltpu.make_async_copy(zbuf, xs_hbm.at[pl.ds(pl.multiple_of(row0, TOK_SUB), blk), :], zsem.at[0])

    @pl.when(i == 0)
    def _zero_fill():
        zbuf[...] = jnp.zeros_like(zbuf)
        n_used = nu_ref[0]
        for e in range(N_EXPERTS):
            @pl.when(tail_ref[e] >= 0)
            def _():
                zero_block(tail_ref[e]).start()
        lax.fori_loop(n_used, n_blocks, lambda b, c: (zero_block(b * blk).start(), c)[1], 0)
        for e in range(N_EXPERTS):
            @pl.when(tail_ref[e] >= 0)
            def _():
                zero_block(tail_ref[e]).wait()
        lax.fori_loop(n_used, n_blocks, lambda b, c: (zero_block(b * blk).wait(), c)[1], 0)

    def body(g, carry):
        for j in range(ROW_DMA_UNROLL):
            r = g * ROW_DMA_UNROLL + j
            t = jnp.where(r < tm, r, r - tm)
            src = h_ref.at[pl.ds(pl.multiple_of(t * TOK_SUB, TOK_SUB), TOK_SUB), :]
            dst = xs_hbm.at[pl.ds(pl.multiple_of(pos_ref[0, 0, r], TOK_SUB), TOK_SUB), :]
            pltpu.make_async_copy(src, dst, sem.at[0]).start(priority=j % 2)
        return carry
    lax.fori_loop(0, 2 * tm // ROW_DMA_UNROLL, body, 0)
    for _ in range(MOE_TOPK):
        pltpu.make_async_copy(h_ref, xs_hbm.at[pl.ds(0, tm * TOK_SUB), :], sem.at[0]).wait()


def _dispatch_call(tail, n_used, pos2, h2, n_rows):
    nt = pos2.shape[0]
    tm = TM_COMB
    grid_spec = pltpu.PrefetchScalarGridSpec(
        num_scalar_prefetch=2,
        grid=(nt,),
        in_specs=[
            pl.BlockSpec((1, 1, 2 * tm), lambda i, *_: (i, 0, 0), memory_space=pltpu.SMEM),
            pl.BlockSpec((tm * TOK_SUB, LANES), lambda i, *_: (i, 0)),
        ],
        out_specs=pl.BlockSpec(memory_space=pl.ANY),
        scratch_shapes=[
            pltpu.VMEM((MOE_ROWS * TOK_SUB, LANES), F32),
            pltpu.SemaphoreType.DMA((1,)),
            pltpu.SemaphoreType.DMA((1,)),
        ],
    )
    return pl.pallas_call(
        _dispatch_kernel,
        grid_spec=grid_spec,
        out_shape=jax.ShapeDtypeStruct((n_rows * TOK_SUB, LANES), F32),
        compiler_params=_cparams(("arbitrary",)),
        name="moe_dispatch",
    )(tail, n_used, pos2, h2)


def _expert_kernel(be_ref, nu_ref, run_ref, x_ref, wg_hbm, wu_hbm, wd_hbm, y_ref,
                   wbuf_g, wbuf_u, wbuf_d, wsem, wg_bf, wu_bf, wd_bf, *, layer):
    i = pl.program_id(0)
    n_used = nu_ref[0]
    rows = MOE_ROWS

    def weight_copies(e, s):
        return [pltpu.make_async_copy(w_hbm.at[layer, e], buf.at[s], wsem.at[s])
                for w_hbm, buf in ((wg_hbm, wbuf_g), (wu_hbm, wbuf_u), (wd_hbm, wbuf_d))]

    @pl.when(jnp.logical_and(i == 0, n_used > 0))
    def _first():
        for cp in weight_copies(be_ref[0], 0):
            cp.start()

    @pl.when(jnp.logical_and(run_ref[0, i] == 1, i < n_used))
    def _new_expert():
        ws = run_ref[1, i]
        for cp in weight_copies(be_ref[i], ws):
            cp.wait()
        wg_bf[...] = wbuf_g[ws].astype(BF16)
        wu_bf[...] = wbuf_u[ws].astype(BF16)
        wd_bf[...] = wbuf_d[ws].astype(BF16)

        @pl.when(run_ref[2, i] >= 0)
        def _next_weights():
            for cp in weight_copies(run_ref[2, i], 1 - ws):
                cp.start()

    @pl.when(i < n_used)
    def _compute():
        xb = _from_token_major(x_ref, 0, rows).astype(BF16)
        g = jnp.dot(xb, wg_bf[...], preferred_element_type=F32)
        u = jnp.dot(xb, wu_bf[...], preferred_element_type=F32)
        hid = (_silu(g) * u).astype(BF16)
        _to_token_major(y_ref, jnp.dot(hid, wd_bf[...], preferred_element_type=F32))

    @pl.when(i >= n_used)
    def _skip():
        y_ref[...] = jnp.zeros_like(y_ref)


def _expert_call(block_expert, n_used, xs, w_gate, w_up, w_down, layer):
    rows = MOE_ROWS
    n_blocks = xs.shape[0] // (rows * TOK_SUB)
    D = D_MODEL
    FF = w_gate.shape[-1]
    idx = jnp.arange(n_blocks, dtype=jnp.int32)
    first = jnp.logical_and(idx < n_used[0], jnp.concatenate(
        [jnp.ones((1,), jnp.bool_), block_expert[1:] != block_expert[:-1]]))
    run_slot = (jnp.cumsum(first.astype(jnp.int32)) - 1) % 2
    big = jnp.int32(n_blocks)
    first_at = jnp.where(first, idx, big)
    next_first = jnp.concatenate([lax.cummin(first_at[::-1])[::-1][1:], big[None]])
    next_expert = jnp.where(next_first < big, block_expert[jnp.minimum(next_first, n_blocks - 1)], -1)
    runs = jnp.stack([first.astype(jnp.int32), run_slot, next_expert]).astype(jnp.int32)
    grid_spec = pltpu.PrefetchScalarGridSpec(
        num_scalar_prefetch=3,
        grid=(n_blocks,),
        in_specs=[
            pl.BlockSpec((rows * TOK_SUB, LANES),
                         lambda i, be, nu, rn: (jnp.minimum(i, jnp.maximum(nu[0] - 1, 0)), 0)),
            pl.BlockSpec(memory_space=pl.ANY),
            pl.BlockSpec(memory_space=pl.ANY),
            pl.BlockSpec(memory_space=pl.ANY),
        ],
        out_specs=pl.BlockSpec((rows * TOK_SUB, LANES), lambda i, be, nu, rn: (i, 0)),
        scratch_shapes=[
            pltpu.VMEM((2, D, FF), F32),
            pltpu.VMEM((2, D, FF), F32),
            pltpu.VMEM((2, FF, D), F32),
            pltpu.SemaphoreType.DMA((2,)),
            pltpu.VMEM((D, FF), BF16),
            pltpu.VMEM((D, FF), BF16),
            pltpu.VMEM((FF, D), BF16),
        ],
    )
    return pl.pallas_call(
        functools.partial(_expert_kernel, layer=layer),
        grid_spec=grid_spec,
        out_shape=jax.ShapeDtypeStruct((n_blocks * rows * TOK_SUB, LANES), F32),
        compiler_params=_cparams(("arbitrary",)),
        name="expert_mlp",
    )(block_expert, n_used, runs, xs, w_gate, w_up, w_down)


def _combine_kernel(pos_ref, posn_ref, y_hbm, x_ref, route_ref, mod_ref, o_ref, ybuf, sem):
    i = pl.program_id(0)
    n = pl.num_programs(0)
    slot = i % 2
    tm = TM_COMB

    @pl.when(i == 0)
    def _first():
        _start_tokens(pos_ref, y_hbm, ybuf, 0, sem, 2 * tm)

    @pl.when(i + 1 < n)
    def _prefetch():
        _start_tokens(posn_ref, y_hbm, ybuf, 1 - slot, sem, 2 * tm)

    _wait_tokens(y_hbm, ybuf, slot, sem, 2 * tm)
    route = route_ref[...]
    yb = ybuf.at[slot]
    y = route[:, 4:5] * _from_token_major(yb, 0, tm) + route[:, 5:6] * _from_token_major(yb, tm, tm)
    o_ref[...] = x_ref[...] + mod_ref[0, 5:6, :] * y


def _combine_call(pos2, y_rows, x2, route, mod, seq):
    N, D = x2.shape
    tm = TM_COMB
    tpb = seq // tm
    nt = N // tm
    row = lambda i: (i, 0)
    return pl.pallas_call(
        _combine_kernel,
        grid=(nt,),
        in_specs=[
            pl.BlockSpec((1, 1, 2 * tm), lambda i: (i, 0, 0), memory_space=pltpu.SMEM),
            pl.BlockSpec((1, 1, 2 * tm), lambda i: (jnp.minimum(i + 1, nt - 1), 0, 0), memory_space=pltpu.SMEM),
            pl.BlockSpec(memory_space=pl.ANY),
            pl.BlockSpec((tm, D), row),
            pl.BlockSpec((tm, LANES), row),
            pl.BlockSpec((1, 6, D), lambda i: (i // tpb, 0, 0)),
        ],
        out_specs=pl.BlockSpec((tm, D), row),
        out_shape=jax.ShapeDtypeStruct((N, D), F32),
        scratch_shapes=[pltpu.VMEM((2, 2 * tm * TOK_SUB, LANES), F32), pltpu.SemaphoreType.DMA((2,))],
        compiler_params=_cparams(("arbitrary",)),
        name="moe_combine",
    )(pos2, pos2, y_rows, x2, route, mod)


def _rope_lane_tables(positions, rot_dim, head_w, n_rep):
    half = rot_dim // 2
    inv = jnp.power(jnp.float32(ROPE_THETA), -2.0 * jnp.arange(half, dtype=jnp.float32) / rot_dim)
    ang = positions.astype(jnp.float32)[..., None] * inv
    n = positions.shape[0] * positions.shape[1]
    cos, sin = jnp.cos(ang).reshape(n, half), jnp.sin(ang).reshape(n, half)
    lane = np.arange(head_w * n_rep) % head_w
    sel = (lane[None, :] % half == np.arange(half)[:, None]) & (lane[None, :] < rot_dim)
    sign = np.where(lane < half, -1.0, 1.0)[None, :]
    hi = lax.Precision.HIGHEST
    cos_f = jnp.dot(cos, jnp.asarray(sel, F32), precision=hi) + jnp.asarray(lane >= rot_dim, F32)[None, :]
    sin_s = jnp.dot(sin, jnp.asarray(sel * sign, F32), precision=hi)
    return cos_f, sin_s


def _layer_weights(l, w_in, q_norm, k_norm, conv_w, conv_b, conv_ln_g, conv_ln_b, conv_pw_w, conv_pw_b,
                   pool_w, pool_b, pool_scale, sgu_ln_g, sgu_ln_b, sgu_w, sgu_b, out_norm, w_out, norm2,
                   w_rg, b_rg, w_re, b_re):
    G = GROUP_W
    D = D_MODEL
    pts = np.cumsum(IN_SPLITS)[:-1].tolist()
    wq, wk, wv, wiq, wik, wiw, wb, wc, wd = jnp.split(w_in[l], pts, axis=-1)
    wiw_p = jnp.pad(wiw, ((0, 0), (0, LANES - IDX_HEADS)))
    w_in_p = jnp.concatenate([wq, wk, wv, wiq, jnp.tile(wik, (1, IDX_HEADS)), wiw_p, wb, wc, wd], axis=-1).astype(BF16)
    npool = len(POOL_WINDOWS)
    pool_bd = jnp.zeros((G, G), F32)
    for g in range(npool):
        pool_bd = lax.dynamic_update_slice(pool_bd, pool_w[l, g], (g * POOL_CH, g * POOL_CH))
    sgu_bias = jnp.repeat(sgu_b[l].T, G // SGU_HEADS, axis=1)
    w_router = jnp.concatenate([w_rg[l], w_re[l].reshape(D, N_EXPERTS),
                                jnp.zeros((D, LANES - N_EXP_GROUPS - N_EXPERTS), F32)], axis=-1).astype(BF16)
    b_router = jnp.concatenate([b_rg[l], b_re[l].reshape(N_EXPERTS),
                                jnp.zeros((LANES - N_EXP_GROUPS - N_EXPERTS,), F32)]).reshape(1, LANES)
    r1 = lambda a: a.reshape(1, -1)
    return dict(
        w_in=w_in_p,
        q_norm=jnp.tile(q_norm[l], ATT_HEADS).reshape(1, G), k_norm=jnp.tile(k_norm[l], ATT_HEADS).reshape(1, G),
        conv_w=conv_w[l], conv_b=r1(conv_b[l]), conv_ln_g=r1(conv_ln_g[l]), conv_ln_b=r1(conv_ln_b[l]),
        conv_pw_w=conv_pw_w[l].astype(BF16), conv_pw_b=r1(conv_pw_b[l]),
        pool_w=pool_bd.astype(BF16), pool_b=r1(pool_b[l]), pool_scale=r1(pool_scale[l]),
        sgu_ln_g=r1(sgu_ln_g[l]), sgu_ln_b=r1(sgu_ln_b[l]), sgu_w=sgu_w[l], sgu_b=sgu_bias,
        out_norm=r1(out_norm[l]), w_out=w_out[l].astype(BF16), norm2=r1(norm2[l]),
        w_router=w_router, b_router=b_router,
    )


def _dispatch_tables(route, cnt, n_tokens):
    rows_blk = MOE_ROWS
    e = route[:, 0:2].astype(jnp.int32)
    rank = route[:, 2:4].astype(jnp.int32)
    counts = cnt[0, :N_EXPERTS].astype(jnp.int32)
    padded = (counts + rows_blk - 1) // rows_blk * rows_blk
    pad_end = jnp.cumsum(padded)
    pad_start = pad_end - padded
    seg0 = jnp.sum(jnp.where(e[..., None] == jnp.arange(N_EXPERTS, dtype=jnp.int32), pad_start, 0), axis=-1)
    pos = seg0 + rank
    m = n_tokens * MOE_TOPK
    n_blocks = (m + N_EXPERTS * (rows_blk - 1) + rows_blk - 1) // rows_blk
    n_used = (pad_end[-1] // rows_blk).astype(jnp.int32).reshape(1)
    blk_row0 = jnp.arange(n_blocks, dtype=jnp.int32) * rows_blk
    block_expert = jnp.minimum(jnp.sum((pad_end[None, :] <= blk_row0[:, None]).astype(jnp.int32), axis=1),
                               N_EXPERTS - 1)
    pos2 = (pos * TOK_SUB).reshape(n_tokens // TM_COMB, TM_COMB, MOE_TOPK).transpose(0, 2, 1).reshape(
        -1, 1, MOE_TOPK * TM_COMB)
    tail = jnp.where(counts > 0, (pad_end - rows_blk) * TOK_SUB, -1).astype(jnp.int32)
    return block_expert, n_used, tail, pos2, n_blocks * rows_blk


def kernel(x, c, positions, w_ada, b_ada, norm1, w_in, q_norm, k_norm, conv_w, conv_b, conv_ln_g, conv_ln_b, conv_pw_w, conv_pw_b, pool_w, pool_b, pool_scale, sgu_ln_g, sgu_ln_b, sgu_w, sgu_b, out_norm, w_out, norm2, w_rg, b_rg, w_re, b_re, w_gate, w_up, w_down):
    B, S, D = x.shape
    N = B * S
    assert D == D_MODEL and S % TM_PROJ == 0 and S % TK_ATT == 0 and N % TM_COMB == 0
    depth = w_ada.shape[0]
    cos_a, sin_a = _rope_lane_tables(positions, ROPE_DIM, ATT_HEAD_DIM, ATT_HEADS)
    cos_i, sin_i = _rope_lane_tables(positions, IDX_ROPE_DIM, IDX_DIM, IDX_HEADS)
    c_pad = jnp.pad(c, ((0, (-B) % SUBLANES), (0, 0)))
    mod_all = _ada_call(c_pad, w_ada, b_ada)
    x2 = x.reshape(N, D)
    for l in range(depth):
        lw = _layer_weights(l, w_in, q_norm, k_norm, conv_w, conv_b, conv_ln_g, conv_ln_b, conv_pw_w, conv_pw_b,
                            pool_w, pool_b, pool_scale, sgu_ln_g, sgu_ln_b, sgu_w, sgu_b, out_norm, w_out, norm2,
                            w_rg, b_rg, w_re, b_re)
        mod = mod_all[l, :B].reshape(B, 6, D)
        q, k, v, iq, ik, iw, pb, pc, pd = _proj_call(
            x2, mod, norm1[l].reshape(1, D), lw["w_in"], lw["q_norm"], lw["k_norm"], cos_a, sin_a, cos_i, sin_i, S)
        oa = _dsa_call(q, k, v, iq, ik, iw, B, S)
        x_mid, h2, route, cnt = _mix_call(x2, oa, pb, pc, pd, mod, lw, S)
        block_expert, n_used, tail, pos2, n_rows = _dispatch_tables(route, cnt, N)
        xs = _dispatch_call(tail, n_used, pos2, h2, n_rows)
        y_rows = _expert_call(block_expert, n_used, xs, w_gate, w_up, w_down, l)
        x2 = _combine_call(pos2, y_rows, x_mid, route, mod, S)
    return x2.reshape(B, S, D)
```

```python
import functools

import numpy as np
import jax
import jax.numpy as jnp
from jax import lax
from jax.experimental import pallas as pl
from jax.experimental.pallas import tpu as pltpu

F32 = jnp.float32
BF16 = jnp.bfloat16
NEG_INF = float("-inf")

D_MODEL = 1024
CHUNK = 64
N_MIXERS = 4
GROUP_W = D_MODEL // N_MIXERS
ATT_HEAD_DIM = 64
ATT_HEADS = GROUP_W // ATT_HEAD_DIM
ROPE_DIM = ATT_HEAD_DIM // 4
ROPE_THETA = 500000.0
IDX_HEADS = 4
IDX_DIM = 32
IDX_ROPE_DIM = IDX_DIM // 4
TOPK_MAX = 256
CONV_WIDTH = 31
CONV_GROUPS = 4
POOL_WINDOWS = (2, 4, 8, 16)
POOL_CH = GROUP_W // 4
SGU_CHUNK = 128
SGU_HEADS = 4
N_EXP_GROUPS = 4
EXP_PER_GROUP = 8
N_EXPERTS = N_EXP_GROUPS * EXP_PER_GROUP
EXPERT_FF = 512
MOE_TOPK = 2
EPS = 1e-6
IN_SPLITS = (GROUP_W, GROUP_W, GROUP_W, IDX_HEADS * IDX_DIM, IDX_DIM, IDX_HEADS, 2 * GROUP_W, GROUP_W, 2 * GROUP_W)

LANES = 128
SUBLANES = 8
HALO = 32
W_IN_COLS = 3 * GROUP_W + 3 * LANES + 5 * GROUP_W

TM_PROJ = 512
TQ_ATT = 256
TK_ATT = 512
TM_MIX = 256
MOE_ROWS = 256
TM_COMB = 256
BISECT_ITERS = 17
VMEM_LIMIT = 56 * 1024 * 1024


def _cparams(sem):
    return pltpu.CompilerParams(dimension_semantics=sem, vmem_limit_bytes=VMEM_LIMIT)


def _lane_iota(shape):
    return lax.broadcasted_iota(jnp.int32, shape, len(shape) - 1)


def _seg_mean(y, width):
    shift = int(np.log2(width))
    grp = _lane_iota(y.shape) >> shift
    out = jnp.zeros_like(y)
    for g in range(y.shape[-1] // width):
        msk = grp == g
        s = jnp.sum(jnp.where(msk, y, 0.0), axis=-1, keepdims=True)
        out = jnp.where(msk, s, out)
    return out * (1.0 / width)


def _rope(x, cos_f, sin_s, head_w, half):
    c = x.shape[-1]
    lane = _lane_iota(x.shape) & (head_w - 1)
    partner = jnp.where(lane < half, pltpu.roll(x, c - half, 1), pltpu.roll(x, half, 1))
    return x * cos_f + partner * sin_s


def _silu(x):
    return x * jax.nn.sigmoid(x)


def _ada_kernel(c_ref, w_ref, b_ref, o_ref):
    ca = _silu(c_ref[...])
    o_ref[0] = jnp.dot(ca.astype(BF16), w_ref[0].astype(BF16), preferred_element_type=F32) + b_ref[0]


def _ada_call(c_pad, w_ada, b_ada):
    L, D, D6 = w_ada.shape
    rows = c_pad.shape[0]
    tn = D
    return pl.pallas_call(
        _ada_kernel,
        grid=(L, D6 // tn),
        in_specs=[
            pl.BlockSpec((rows, D), lambda l, j: (0, 0)),
            pl.BlockSpec((1, D, tn), lambda l, j: (l, 0, j)),
            pl.BlockSpec((1, 1, tn), lambda l, j: (l, 0, j)),
        ],
        out_specs=pl.BlockSpec((1, rows, tn), lambda l, j: (l, 0, j)),
        out_shape=jax.ShapeDtypeStruct((L, rows, D6), F32),
        compiler_params=_cparams(("arbitrary", "arbitrary")),
        name="ada_mod",
    )(c_pad, w_ada, b_ada.reshape(L, 1, D6))


def _proj_kernel(x_ref, mod_ref, n1_ref, w_ref, qn_ref, kn_ref, cosa_ref, sina_ref, cosi_ref, sini_ref,
                 q_ref, k_ref, v_ref, iq_ref, ik_ref, iw_ref, pb_ref, pc_ref, pd_ref):
    x = x_ref[...]
    ms = jnp.mean(x * x, axis=-1, keepdims=True)
    h = (x * lax.rsqrt(ms + EPS)) * n1_ref[...]
    h = h * (1.0 + mod_ref[0, 1:2, :]) + mod_ref[0, 0:1, :]
    proj = jnp.dot(h.astype(BF16), w_ref[...], preferred_element_type=F32)
    G = GROUP_W
    cos_a, sin_a = cosa_ref[...], sina_ref[...]
    cos_i, sin_i = cosi_ref[...], sini_ref[...]

    def qk(t, g_ref):
        tn = (t * lax.rsqrt(_seg_mean(t * t, ATT_HEAD_DIM) + EPS)) * g_ref[...]
        return _rope(tn, cos_a, sin_a, ATT_HEAD_DIM, ROPE_DIM // 2)

    q_ref[...] = (qk(proj[:, 0:G], qn_ref) * (ATT_HEAD_DIM ** -0.5)).T.astype(BF16)
    k_ref[...] = qk(proj[:, G:2 * G], kn_ref).astype(BF16)
    v_ref[0] = proj[:, 2 * G:3 * G].T.astype(BF16)
    o = 3 * G
    iq_ref[...] = _rope(proj[:, o:o + LANES], cos_i, sin_i, IDX_DIM, IDX_ROPE_DIM // 2).T.astype(BF16)
    ik_ref[...] = _rope(proj[:, o + LANES:o + 2 * LANES], cos_i, sin_i, IDX_DIM, IDX_ROPE_DIM // 2).astype(BF16)
    iw_ref[...] = (proj[:, o + 2 * LANES:o + 3 * LANES] * (IDX_HEADS ** -0.5)).T[0:SUBLANES, :]
    o += 3 * LANES
    pb_ref[...] = proj[:, o:o + 2 * G]
    pc_ref[...] = proj[:, o + 2 * G:o + 3 * G]
    pd_ref[...] = proj[:, o + 3 * G:o + 5 * G]


def _proj_call(x2, mod, n1, w_in_p, qn_t, kn_t, cos_a, sin_a, cos_i, sin_i, seq):
    N, D = x2.shape
    tm = TM_PROJ
    tpb = seq // tm
    G = GROUP_W
    assert tm == TK_ATT
    row = lambda i: (i, 0)
    col = lambda i: (0, i)
    fixed = lambda i: (0, 0)
    sds = jax.ShapeDtypeStruct
    out_specs = [
        pl.BlockSpec((G, tm), col),
        pl.BlockSpec((tm, G), row),
        pl.BlockSpec((1, G, tm), lambda i: (i, 0, 0)),
        pl.BlockSpec((LANES, tm), col),
        pl.BlockSpec((tm, LANES), row),
        pl.BlockSpec((SUBLANES, tm), col),
        pl.BlockSpec((tm, 2 * G), row),
        pl.BlockSpec((tm, G), row),
        pl.BlockSpec((tm, 2 * G), row),
    ]
    out_shape = [sds((G, N), BF16), sds((N, G), BF16), sds((N // tm, G, tm), BF16), sds((LANES, N), BF16),
                 sds((N, LANES), BF16), sds((SUBLANES, N), F32), sds((N, 2 * G), F32), sds((N, G), F32),
                 sds((N, 2 * G), F32)]
    return pl.pallas_call(
        _proj_kernel,
        grid=(N // tm,),
        in_specs=[
            pl.BlockSpec((tm, D), row),
            pl.BlockSpec((1, 6, D), lambda i: (i // tpb, 0, 0)),
            pl.BlockSpec((1, D), fixed),
            pl.BlockSpec((D, W_IN_COLS), fixed),
            pl.BlockSpec((1, G), fixed),
            pl.BlockSpec((1, G), fixed),
            pl.BlockSpec((tm, G), row),
            pl.BlockSpec((tm, G), row),
            pl.BlockSpec((tm, LANES), row),
            pl.BlockSpec((tm, LANES), row),
        ],
        out_specs=out_specs,
        out_shape=out_shape,
        compiler_params=_cparams(("parallel",)),
        name="norm_in_proj",
    )(x2, mod, n1, w_in_p, qn_t, kn_t, cos_a, sin_a, cos_i, sin_i)


def _pair_rhs(xt, head_rows, h0):
    head = lax.broadcasted_iota(jnp.int32, xt.shape, 0) >> int(np.log2(head_rows))
    zero = jnp.zeros_like(xt)
    return jnp.concatenate([jnp.where(head == h0, xt, zero), jnp.where(head == h0 + 1, xt, zero)], axis=1)


def _dsa_kernel(q_ref, k_ref, v_ref, iq_ref, ik_ref, iw_ref, ltri_ref, o_ref, sc_ref, lge_ref, lgo_ref, *, topk):
    tq, tk = TQ_ATT, TK_ATT
    i = pl.program_id(1)
    q0 = i * tq
    n_kv = (q0 + tq + tk - 1) // tk
    kf = float(topk)

    q_pos = _lane_iota((1, tq)) + q0
    chunk_bits = int(np.log2(CHUNK))
    key_end = ((q_pos >> chunk_bits) + 1) << chunk_bits
    key_i = lax.broadcasted_iota(jnp.int32, (tk, tq), 0)

    def fold8(x, op, fn=None):
        n_acc = 4
        rows = lambda r: slice(r * SUBLANES, (r + 1) * SUBLANES)
        piece = (lambda r: x[rows(r), :]) if fn is None else (lambda r: fn(x[rows(r), :]))
        accs = [piece(r) for r in range(n_acc)]
        for r in range(n_acc, tk // SUBLANES):
            accs[r % n_acc] = op(accs[r % n_acc], piece(r))
        return op(op(accs[0], accs[1]), op(accs[2], accs[3]))

    iqt = iq_ref[...]
    iq_pairs = [_pair_rhs(iqt, IDX_DIM, h0) for h0 in range(0, IDX_HEADS, 2)]
    iw_h = [iw_ref[h:h + 1, :] for h in range(IDX_HEADS)]

    def score_body(kc, carry):
        hi8, lo8 = carry
        ikc = ik_ref[pl.ds(pl.multiple_of(kc * tk, tk), tk), :]
        s = jnp.zeros((tk, tq), F32)
        for pi, rhs in enumerate(iq_pairs):
            d2 = jnp.dot(ikc, rhs, preferred_element_type=F32)
            for j in range(2):
                d = d2[:, j * tq:(j + 1) * tq]
                s = s + jnp.maximum(d * (IDX_DIM ** -0.5), 0.0) * iw_h[2 * pi + j]
        adm = key_i + kc * tk < key_end
        s_top = jnp.where(adm, s, NEG_INF)
        sc_ref[kc] = s_top
        return (jnp.maximum(hi8, fold8(s_top, jnp.maximum)),
                jnp.minimum(lo8, fold8(jnp.where(adm, s, jnp.inf), jnp.minimum)))

    hi8, lo8 = lax.fori_loop(0, n_kv, score_body,
                             (jnp.full((SUBLANES, tq), NEG_INF, F32), jnp.full((SUBLANES, tq), jnp.inf, F32)))
    col_max = jnp.max(hi8, axis=0, keepdims=True)
    col_min = jnp.min(lo8, axis=0, keepdims=True)

    def reduce_chunks(fns, init, combine, fold):
        def body(kc, parts):
            blk = sc_ref.at[kc]
            return tuple(combine(p, fold8(blk, combine, fn)) for p, fn in zip(parts, fns))
        parts = lax.fori_loop(0, n_kv, body, tuple(jnp.full((SUBLANES, tq), init, F32) for _ in fns))
        return [fold(p, axis=0, keepdims=True) for p in parts]

    def count(*inds):
        return reduce_chunks(inds, 0.0, jnp.add, jnp.sum)

    def col_maximum(val):
        return reduce_chunks((val,), NEG_INF, jnp.maximum, jnp.max)[0]

    small = key_end <= topk

    @pl.when(q0 + tq > topk)
    def _select():
        def bis_body(_, c):
            lo, hi = c
            mid = lo + (hi - lo) * 0.5
            ge = count(lambda b: jnp.where(b >= mid, 1.0, 0.0))[0] >= kf
            return jnp.where(ge, mid, lo), jnp.where(ge, hi, mid)

        hi0 = col_max + jnp.maximum(jnp.abs(col_max), 1e-30) * (2.0 ** -20)
        lo, hi = lax.fori_loop(0, BISECT_ITERS, bis_body, (col_min, hi0))

        def sd_cond(c):
            return c[0] > 0.0

        def sd_body(c):
            _, hi, thr, done = c
            cand = col_maximum(lambda b: jnp.where(b < hi, b, NEG_INF))
            ok = count(lambda b: jnp.where(b >= cand, 1.0, 0.0))[0] >= kf
            thr = jnp.where(done > 0.0, thr, cand)
            hi = jnp.where(done > 0.0, hi, cand)
            done = jnp.where(ok, 1.0, done)
            return jnp.sum(1.0 - done), hi, thr, done

        done0 = jnp.where(small, 1.0, 0.0)
        n0 = jnp.sum(1.0 - done0)
        _, _, thr, _ = lax.while_loop(sd_cond, sd_body, (n0, hi, jnp.full((1, tq), NEG_INF, F32), done0))
        thr = jnp.where(small, NEG_INF, thr)

        n_above, n_tied = count(lambda b: jnp.where(b > thr, 1.0, 0.0), lambda b: jnp.where(b == thr, 1.0, 0.0))
        need = kf - n_above
        excess = jnp.sum(jnp.where(jnp.where(small, 0.0, n_tied) > need, 1.0, 0.0))

        @pl.when(excess <= 0.0)
        def _keep_all_ties():
            def bias_body(kc, carry):
                blk = sc_ref[kc]
                sc_ref[kc] = jnp.where(blk == NEG_INF, NEG_INF, jnp.where(blk >= thr, 0.0, NEG_INF))
                return carry
            lax.fori_loop(0, n_kv, bias_body, 0)

        @pl.when(excess > 0.0)
        def _rank_ties():
            half = tk // 2
            ltri_top = ltri_ref[0:half, 0:half]
            ltri_bot = ltri_ref[half:tk, :]

            def bias_body(kc, seen):
                blk = sc_ref[kc]
                tied = jnp.where(blk == thr, 1.0, 0.0)
                tied16 = tied.astype(BF16)
                rank = jnp.concatenate([jnp.dot(ltri_top, tied16[0:half, :], preferred_element_type=F32),
                                        jnp.dot(ltri_bot, tied16, preferred_element_type=F32)], axis=0) + seen
                tie = jnp.where(blk == thr, jnp.where(rank <= need, 0.0, NEG_INF), NEG_INF)
                bias = jnp.where(blk > thr, 0.0, tie)
                sc_ref[kc] = jnp.where(blk == NEG_INF, NEG_INF, bias)
                return seen + jnp.sum(fold8(tied, jnp.add), axis=0, keepdims=True)

            lax.fori_loop(0, n_kv, bias_body, jnp.zeros((1, tq), F32))

    @pl.when(q0 + tq <= topk)
    def _all():
        def bias_body(kc, carry):
            sc_ref[kc] = jnp.where(sc_ref[kc] == NEG_INF, NEG_INF, 0.0)
            return carry
        lax.fori_loop(0, n_kv, bias_body, 0)

    qt = q_ref[...]
    q_pairs = [_pair_rhs(qt, ATT_HEAD_DIM, h0) for h0 in range(0, ATT_HEADS, 2)]
    dh = ATT_HEAD_DIM

    def store_logits(buf, kc):
        kblk = k_ref[pl.ds(pl.multiple_of(kc * tk, tk), tk), :]
        bias = sc_ref[kc]
        for pi, rhs in enumerate(q_pairs):
            s2 = jnp.dot(kblk, rhs, preferred_element_type=F32)
            for j in range(2):
                buf[2 * pi + j] = s2[:, j * tq:(j + 1) * tq] + bias

    def absorb(buf, kc, state):
        ms, ls, accs = state
        vt = v_ref[kc]
        ms_n, ls_n, accs_n = [], [], []
        for h in range(ATT_HEADS):
            s = buf[h]
            m_new = jnp.maximum(ms[h], jnp.max(fold8(s, jnp.maximum), axis=0, keepdims=True))
            m_safe = jnp.where(m_new == NEG_INF, 0.0, m_new)
            alpha = jnp.exp(ms[h] - m_safe)
            p = jnp.exp(s - m_safe)
            ls_n.append(alpha * ls[h] + fold8(p, jnp.add))
            pv = jnp.dot(vt[h * dh:(h + 1) * dh, :], p.astype(BF16), preferred_element_type=F32)
            accs_n.append(alpha * accs[h] + pv)
            ms_n.append(m_new)
        return tuple(ms_n), tuple(ls_n), tuple(accs_n)

    def pair_body(jj, state):
        s = 2 * jj + 1
        store_logits(lgo_ref, s)
        state = absorb(lge_ref, s - 1, state)
        store_logits(lge_ref, s + 1)
        return absorb(lgo_ref, s, state)

    state = (tuple(jnp.full((1, tq), NEG_INF, F32) for _ in range(ATT_HEADS)),
             tuple(jnp.zeros((SUBLANES, tq), F32) for _ in range(ATT_HEADS)),
             tuple(jnp.zeros((dh, tq), F32) for _ in range(ATT_HEADS)))
    store_logits(lge_ref, 0)
    n_pairs = (n_kv - 1) // 2
    state = lax.fori_loop(0, n_pairs, pair_body, state)
    last_even = 2 * n_pairs

    def tail_two(state):
        store_logits(lgo_ref, last_even + 1)
        return absorb(lgo_ref, last_even + 1, absorb(lge_ref, last_even, state))

    def tail_one(state):
        return absorb(lge_ref, last_even, state)

    _, ls, accs = lax.cond(n_kv - 1 - last_even > 0, tail_two, tail_one, state)
    out_t = jnp.concatenate([accs[h] / jnp.sum(ls[h], axis=0, keepdims=True) for h in range(ATT_HEADS)], axis=0)
    o_ref[...] = out_t.T


def _dsa_call(qt, k, vt, iqt, ik, iwt, batch, seq):
    N, G = k.shape
    tq = TQ_ATT
    nq = seq // tq
    nkc = seq // TK_ATT
    topk = min(TOPK_MAX, seq // 4)
    qcol = lambda b, i: (0, b * nq + i)
    brow = lambda b, i: (b, 0)
    return pl.pallas_call(
        functools.partial(_dsa_kernel, topk=topk),
        grid=(batch, nq),
        in_specs=[
            pl.BlockSpec((G, tq), qcol),
            pl.BlockSpec((seq, G), brow, pipeline_mode=pl.Buffered(1)),
            pl.BlockSpec((nkc, G, TK_ATT), lambda b, i: (b, 0, 0), pipeline_mode=pl.Buffered(1)),
            pl.BlockSpec((LANES, tq), qcol),
            pl.BlockSpec((seq, LANES), brow, pipeline_mode=pl.Buffered(1)),
            pl.BlockSpec((SUBLANES, tq), qcol),
            pl.BlockSpec((TK_ATT, TK_ATT), lambda b, i: (0, 0), pipeline_mode=pl.Buffered(1)),
        ],
        out_specs=pl.BlockSpec((tq, G), lambda b, i: (b * nq + i, 0)),
        out_shape=jax.ShapeDtypeStruct((N, G), F32),
        scratch_shapes=[pltpu.VMEM((nkc, TK_ATT, tq), F32), pltpu.VMEM((ATT_HEADS, TK_ATT, tq), F32),
                        pltpu.VMEM((ATT_HEADS, TK_ATT, tq), F32)],
        compiler_params=_cparams(("parallel", "arbitrary")),
        name="dsa_attention",
    )(qt, k, vt, iqt, ik, iwt, jnp.tril(jnp.ones((TK_ATT, TK_ATT), BF16)))


def _mix_kernel(x_ref, oa_ref, pb_ref, pbh_ref, pc_ref, pch_ref, pd_ref, mod_ref,
                cw_ref, cb_ref, clg_ref, clb_ref, cpw_ref, cpb_ref,
                pw_ref, pbias_ref, ps_ref, slg_ref, slb_ref, sw_ref, sb_ref,
                on_ref, wo_ref, n2_ref, wr_ref, br_ref,
                xo_ref, h2_ref, route_ref, cnt_ref,
                ypad_ref, ppad_ref, s2_ref, s4_ref, s8_ref, run_ref, cph_ref, *, tiles_per_batch):
    tm = TM_MIX
    G = GROUP_W
    i = pl.program_id(0)
    t_in_b = i % tiles_per_batch
    first = t_in_b == 0
    lane_g = _lane_iota((tm, G))

    def glu(pb):
        return pb[:, 0:G] * jax.nn.sigmoid(pb[:, G:2 * G])

    ypad_ref[0:HALO, :] = jnp.where(first, 0.0, glu(pbh_ref[...]))
    ypad_ref[HALO:HALO + tm, :] = glu(pb_ref[...])
    acc = jnp.zeros((tm, G), F32)
    first_off = HALO - (CONV_WIDTH - 1)
    for phase in range(SUBLANES):
        offs = [o for o in range(first_off, HALO + 1) if o % SUBLANES == phase]
        if not offs:
            continue
        span = offs[-1] - offs[0] + tm
        cph_ref[phase, 0:span, :] = ypad_ref[offs[0]:offs[0] + span, :]
        for o in offs:
            acc = acc + cw_ref[o - first_off:o - first_off + 1, :] * cph_ref[phase, o - offs[0]:o - offs[0] + tm, :]
    y = acc + cb_ref[...]
    gw = G // CONV_GROUPS
    mu = _seg_mean(y, gw)
    yc = y - mu
    var = _seg_mean(yc * yc, gw)
    y = (yc * lax.rsqrt(var + EPS)) * clg_ref[...] + clb_ref[...]
    o_b = jnp.dot(_silu(y).astype(BF16), cpw_ref[...], preferred_element_type=F32) + cpb_ref[...]

    p = pc_ref[...]
    ppad_ref[0:HALO, :] = jnp.where(first, 0.0, pch_ref[...])
    ppad_ref[HALO:HALO + tm, :] = p
    n8 = tm + HALO - 8
    s2_ref[8:8 + n8, :] = ppad_ref[8:8 + n8, :] + ppad_ref[7:7 + n8, :]
    n16 = tm + HALO - 16
    s4_ref[16:16 + n16, :] = s2_ref[16:16 + n16, :] + s2_ref[14:14 + n16, :]
    n24 = tm + HALO - 24
    s8_ref[24:24 + n24, :] = s4_ref[24:24 + n24, :] + s4_ref[20:20 + n24, :]
    s2 = s2_ref[HALO:HALO + tm, :]
    s4 = s4_ref[HALO:HALO + tm, :]
    s8 = s8_ref[HALO:HALO + tm, :]
    s16 = s8 + s8_ref[HALO - 8:HALO - 8 + tm, :]
    pgrp = lane_g >> int(np.log2(POOL_CH))
    wsum = jnp.where(pgrp == 0, s2, jnp.where(pgrp == 1, s4, jnp.where(pgrp == 2, s8, s16)))
    wlen = jnp.where(pgrp == 0, 2.0, jnp.where(pgrp == 1, 4.0, jnp.where(pgrp == 2, 8.0, 16.0)))
    tpos = (lax.broadcasted_iota(jnp.int32, (tm, G), 0) + t_in_b * tm + 1).astype(F32)
    pooled = wsum / jnp.minimum(tpos, wlen) - p
    o_c = (jnp.dot(pooled.astype(BF16), pw_ref[...], preferred_element_type=F32) + pbias_ref[...]) * ps_ref[...]

    pd = pd_ref[...]
    u, v = pd[:, 0:G], pd[:, G:2 * G]
    mu = jnp.mean(v, axis=-1, keepdims=True)
    vc = v - mu
    var = jnp.mean(vc * vc, axis=-1, keepdims=True)
    vn = ((vc * lax.rsqrt(var + EPS)) * slg_ref[...] + slb_ref[...]).astype(BF16)
    r_i = lax.broadcasted_iota(jnp.int32, (SGU_CHUNK, SGU_CHUNK), 0)
    c_i = lax.broadcasted_iota(jnp.int32, (SGU_CHUNK, SGU_CHUNK), 1)
    w_heads = [jnp.where(r_i >= c_i, sw_ref[h], 0.0).astype(BF16) for h in range(SGU_HEADS)]
    lane_c = _lane_iota((SGU_CHUNK, G)) >> int(np.log2(G // SGU_HEADS))
    mixed = []
    for n in range(tm // SGU_CHUNK):
        vch = vn[n * SGU_CHUNK:(n + 1) * SGU_CHUNK, :]
        mx = jnp.zeros((SGU_CHUNK, G), F32)
        for h in range(SGU_HEADS):
            mx = jnp.where(lane_c == h, jnp.dot(w_heads[h], vch, preferred_element_type=F32), mx)
        mixed.append(mx + sb_ref[...])
    o_d = u * jnp.concatenate(mixed, axis=0)

    proj = jnp.zeros((tm, D_MODEL), F32)
    for g, piece in enumerate((oa_ref[...], o_b, o_c, o_d)):
        ms = jnp.mean(piece * piece, axis=-1, keepdims=True)
        pn = (piece * lax.rsqrt(ms + EPS)) * on_ref[:, g * G:(g + 1) * G]
        proj = proj + jnp.dot(pn.astype(BF16), wo_ref[g * G:(g + 1) * G, :], preferred_element_type=F32)
    x_new = x_ref[...] + mod_ref[0, 2:3, :] * proj
    xo_ref[...] = x_new

    ms = jnp.mean(x_new * x_new, axis=-1, keepdims=True)
    h2 = (x_new * lax.rsqrt(ms + EPS)) * n2_ref[...]
    h2 = h2 * (1.0 + mod_ref[0, 4:5, :]) + mod_ref[0, 3:4, :]
    _to_token_major(h2_ref, h2)
    logits = jnp.dot(h2.astype(BF16), wr_ref[...], preferred_element_type=F32) + br_ref[...]
    lane = _lane_iota((tm, LANES))
    lane_f = lane.astype(F32)
    big = float(LANES)
    glog = jnp.where(lane < N_EXP_GROUPS, logits, NEG_INF)
    gmax = jnp.max(glog, axis=-1, keepdims=True)
    p_sel = 1.0 / jnp.sum(jnp.exp(glog - gmax), axis=-1, keepdims=True)
    g_idx = jnp.min(jnp.where(glog == gmax, lane_f, big), axis=-1, keepdims=True)
    e_lane = lane - N_EXP_GROUPS
    elog = jnp.where((e_lane >> 3).astype(F32) == g_idx, logits, NEG_INF)
    top1 = jnp.max(elog, axis=-1, keepdims=True)
    j1 = jnp.min(jnp.where(elog == top1, lane_f, big), axis=-1, keepdims=True)
    elog2 = jnp.where(lane_f == j1, NEG_INF, elog)
    top2 = jnp.max(elog2, axis=-1, keepdims=True)
    j2 = jnp.min(jnp.where(elog2 == top2, lane_f, big), axis=-1, keepdims=True)
    e2w = jnp.exp(top2 - top1)
    gate1 = p_sel * (1.0 / (1.0 + e2w))
    gate2 = p_sel * (e2w / (1.0 + e2w))
    e1 = j1 - float(N_EXP_GROUPS)
    e2 = j2 - float(N_EXP_GROUPS)

    @pl.when(i == 0)
    def _init():
        run_ref[...] = jnp.zeros_like(run_ref)

    onehot = jnp.where(jnp.logical_or(lane_f == e1, lane_f == e2), 1.0, 0.0)
    rr = lax.broadcasted_iota(jnp.int32, (tm, tm), 0)
    cc = lax.broadcasted_iota(jnp.int32, (tm, tm), 1)
    before = jnp.where(rr > cc, 1.0, 0.0).astype(BF16)
    prior = jnp.dot(before, onehot.astype(BF16), preferred_element_type=F32) + run_ref[0:1, :]
    rank1 = jnp.sum(jnp.where(lane_f == e1, prior, 0.0), axis=-1, keepdims=True)
    rank2 = jnp.sum(jnp.where(lane_f == e2, prior, 0.0), axis=-1, keepdims=True)
    run_new = run_ref[0:1, :] + jnp.sum(onehot, axis=0, keepdims=True)
    run_ref[...] = jnp.broadcast_to(run_new, run_ref.shape)
    cnt_ref[...] = jnp.broadcast_to(run_new, cnt_ref.shape)
    route = jnp.where(lane == 0, e1, jnp.where(lane == 1, e2, jnp.where(lane == 2, rank1, jnp.where(
        lane == 3, rank2, jnp.where(lane == 4, gate1, jnp.where(lane == 5, gate2, 0.0))))))
    route_ref[...] = route


def _mix_call(x2, oa, pb, pc, pd, mod, lw, seq):
    N, D = x2.shape
    tm = TM_MIX
    tpb = seq // tm
    G = GROUP_W
    row = lambda i: (i, 0)
    halo = lambda i: (jnp.maximum(i * (tm // HALO) - 1, 0), 0)
    fixed2 = lambda i: (0, 0)
    fixed3 = lambda i: (0, 0, 0)
    params = [lw["conv_w"], lw["conv_b"], lw["conv_ln_g"], lw["conv_ln_b"], lw["conv_pw_w"], lw["conv_pw_b"],
              lw["pool_w"], lw["pool_b"], lw["pool_scale"], lw["sgu_ln_g"], lw["sgu_ln_b"], lw["sgu_w"], lw["sgu_b"],
              lw["out_norm"], lw["w_out"], lw["norm2"], lw["w_router"], lw["b_router"]]
    param_specs = [pl.BlockSpec(p.shape, fixed3 if p.ndim == 3 else fixed2, pipeline_mode=pl.Buffered(1))
                   for p in params]
    return pl.pallas_call(
        functools.partial(_mix_kernel, tiles_per_batch=tpb),
        grid=(N // tm,),
        in_specs=[
            pl.BlockSpec((tm, D), row),
            pl.BlockSpec((tm, G), row),
            pl.BlockSpec((tm, 2 * G), row),
            pl.BlockSpec((HALO, 2 * G), halo),
            pl.BlockSpec((tm, G), row),
            pl.BlockSpec((HALO, G), halo),
            pl.BlockSpec((tm, 2 * G), row),
            pl.BlockSpec((1, 6, D), lambda i: (i // tpb, 0, 0)),
        ] + param_specs,
        out_specs=[
            pl.BlockSpec((tm, D), row),
            pl.BlockSpec((tm * TOK_SUB, LANES), row),
            pl.BlockSpec((tm, LANES), row),
            pl.BlockSpec((SUBLANES, LANES), fixed2),
        ],
        out_shape=[
            jax.ShapeDtypeStruct((N, D), F32),
            jax.ShapeDtypeStruct((N * TOK_SUB, LANES), F32),
            jax.ShapeDtypeStruct((N, LANES), F32),
            jax.ShapeDtypeStruct((SUBLANES, LANES), F32),
        ],
        scratch_shapes=[pltpu.VMEM((tm + HALO, G), F32) for _ in range(5)] + [
            pltpu.VMEM((SUBLANES, LANES), F32), pltpu.VMEM((SUBLANES, tm + HALO, G), F32)],
        compiler_params=_cparams(("arbitrary",)),
        name="mixers_out_router",
    )(x2, oa, pb, pb, pc, pc, pd, mod, *params)


TOK_SUB = D_MODEL // LANES
ROW_DMA_UNROLL = 8


def _to_token_major(ref, x):
    n = x.shape[0]
    for j in range(TOK_SUB):
        ref[pl.ds(j, n, stride=TOK_SUB), :] = x[:, j * LANES:(j + 1) * LANES]


def _from_token_major(ref, tok0, n):
    return jnp.concatenate([ref[pl.ds(tok0 * TOK_SUB + j, n, stride=TOK_SUB), :] for j in range(TOK_SUB)], axis=1)


def _token_copy(src_hbm, row8, dst_ref, slot, r, sem):
    src = src_hbm.at[pl.ds(pl.multiple_of(row8, TOK_SUB), TOK_SUB), :]
    dst = dst_ref.at[slot, pl.ds(pl.multiple_of(r * TOK_SUB, TOK_SUB), TOK_SUB), :]
    return pltpu.make_async_copy(src, dst, sem.at[slot])


def _start_tokens(idx_ref, src_hbm, dst_ref, slot, sem, n):
    def body(g, carry):
        for j in range(ROW_DMA_UNROLL):
            r = g * ROW_DMA_UNROLL + j
            _token_copy(src_hbm, idx_ref[0, 0, r], dst_ref, slot, r, sem).start(priority=j % 2)
        return carry
    lax.fori_loop(0, n // ROW_DMA_UNROLL, body, 0)


def _wait_tokens(src_hbm, dst_ref, slot, sem, n):
    pltpu.make_async_copy(src_hbm.at[pl.ds(0, n * TOK_SUB), :], dst_ref.at[slot], sem.at[slot]).wait()


def _dispatch_kernel(tail_ref, nu_ref, pos_ref, h_ref, xs_hbm, zbuf, zsem, sem):
    i = pl.program_id(0)
    tm = TM_COMB
    blk = MOE_ROWS * TOK_SUB
    n_blocks = xs_hbm.shape[0] // blk

    def zero_block(row0):
        return pltpu.make_async_copy(zbuf, xs_hbm.at[pl.ds(pl.multiple_of(row0, TOK_SUB), blk), :], zsem.at[0])

    @pl.when(i == 0)
    def _zero_fill():
        zbuf[...] = jnp.zeros_like(zbuf)
        n_used = nu_ref[0]
        for e in range(N_EXPERTS):
            @pl.when(tail_ref[e] >= 0)
            def _():
                zero_block(tail_ref[e]).start()
        lax.fori_loop(n_used, n_blocks, lambda b, c: (zero_block(b * blk).start(), c)[1], 0)
        for e in range(N_EXPERTS):
            @pl.when(tail_ref[e] >= 0)
            def _():
                zero_block(tail_ref[e]).wait()
        lax.fori_loop(n_used, n_blocks, lambda b, c: (zero_block(b * blk).wait(), c)[1], 0)

    def body(g, carry):
        for j in range(ROW_DMA_UNROLL):
            r = g * ROW_DMA_UNROLL + j
            t = jnp.where(r < tm, r, r - tm)
            src = h_ref.at[pl.ds(pl.multiple_of(t * TOK_SUB, TOK_SUB), TOK_SUB), :]
            dst = xs_hbm.at[pl.ds(pl.multiple_of(pos_ref[0, 0, r], TOK_SUB), TOK_SUB), :]
            pltpu.make_async_copy(src, dst, sem.at[0]).start(priority=j % 2)
        return carry
    lax.fori_loop(0, 2 * tm // ROW_DMA_UNROLL, body, 0)
    for _ in range(MOE_TOPK):
        pltpu.make_async_copy(h_ref, xs_hbm.at[pl.ds(0, tm * TOK_SUB), :], sem.at[0]).wait()


def _dispatch_call(tail, n_used, pos2, h2, n_rows):
    nt = pos2.shape[0]
    tm = TM_COMB
    grid_spec = pltpu.PrefetchScalarGridSpec(
        num_scalar_prefetch=2,
        grid=(nt,),
        in_specs=[
            pl.BlockSpec((1, 1, 2 * tm), lambda i, *_: (i, 0, 0), memory_space=pltpu.SMEM),
            pl.BlockSpec((tm * TOK_SUB, LANES), lambda i, *_: (i, 0)),
        ],
        out_specs=pl.BlockSpec(memory_space=pl.ANY),
        scratch_shapes=[
            pltpu.VMEM((MOE_ROWS * TOK_SUB, LANES), F32),
            pltpu.SemaphoreType.DMA((1,)),
            pltpu.SemaphoreType.DMA((1,)),
        ],
    )
    return pl.pallas_call(
        _dispatch_kernel,
        grid_spec=grid_spec,
        out_shape=jax.ShapeDtypeStruct((n_rows * TOK_SUB, LANES), F32),
        compiler_params=_cparams(("arbitrary",)),
        name="moe_dispatch",
    )(tail, n_used, pos2, h2)


def _expert_kernel(be_ref, nu_ref, run_ref, x_ref, wg_hbm, wu_hbm, wd_hbm, y_ref,
                   wbuf_g, wbuf_u, wbuf_d, wsem, wg_bf, wu_bf, wd_bf, *, layer):
    i = pl.program_id(0)
    n_used = nu_ref[0]
    rows = MOE_ROWS

    def weight_copies(e, s):
        return [pltpu.make_async_copy(w_hbm.at[layer, e], buf.at[s], wsem.at[s])
                for w_hbm, buf in ((wg_hbm, wbuf_g), (wu_hbm, wbuf_u), (wd_hbm, wbuf_d))]

    @pl.when(jnp.logical_and(i == 0, n_used > 0))
    def _first():
        for cp in weight_copies(be_ref[0], 0):
            cp.start()

    @pl.when(jnp.logical_and(run_ref[0, i] == 1, i < n_used))
    def _new_expert():
        ws = run_ref[1, i]
        for cp in weight_copies(be_ref[i], ws):
            cp.wait()
        wg_bf[...] = wbuf_g[ws].astype(BF16)
        wu_bf[...] = wbuf_u[ws].astype(BF16)
        wd_bf[...] = wbuf_d[ws].astype(BF16)

        @pl.when(run_ref[2, i] >= 0)
        def _next_weights():
            for cp in weight_copies(run_ref[2, i], 1 - ws):
                cp.start()

    @pl.when(i < n_used)
    def _compute():
        xb = _from_token_major(x_ref, 0, rows).astype(BF16)
        g = jnp.dot(xb, wg_bf[...], preferred_element_type=F32)
        u = jnp.dot(xb, wu_bf[...], preferred_element_type=F32)
        hid = (_silu(g) * u).astype(BF16)
        _to_token_major(y_ref, jnp.dot(hid, wd_bf[...], preferred_element_type=F32))

    @pl.when(i >= n_used)
    def _skip():
        y_ref[...] = jnp.zeros_like(y_ref)


def _expert_call(block_expert, n_used, xs, w_gate, w_up, w_down, layer):
    rows = MOE_ROWS
    n_blocks = xs.shape[0] // (rows * TOK_SUB)
    D = D_MODEL
    FF = w_gate.shape[-1]
    idx = jnp.arange(n_blocks, dtype=jnp.int32)
    first = jnp.logical_and(idx < n_used[0], jnp.concatenate(
        [jnp.ones((1,), jnp.bool_), block_expert[1:] != block_expert[:-1]]))
    run_slot = (jnp.cumsum(first.astype(jnp.int32)) - 1) % 2
    big = jnp.int32(n_blocks)
    first_at = jnp.where(first, idx, big)
    next_first = jnp.concatenate([lax.cummin(first_at[::-1])[::-1][1:], big[None]])
    next_expert = jnp.where(next_first < big, block_expert[jnp.minimum(next_first, n_blocks - 1)], -1)
    runs = jnp.stack([first.astype(jnp.int32), run_slot, next_expert]).astype(jnp.int32)
    grid_spec = pltpu.PrefetchScalarGridSpec(
        num_scalar_prefetch=3,
        grid=(n_blocks,),
        in_specs=[
            pl.BlockSpec((rows * TOK_SUB, LANES),
                         lambda i, be, nu, rn: (jnp.minimum(i, jnp.maximum(nu[0] - 1, 0)), 0)),
            pl.BlockSpec(memory_space=pl.ANY),
            pl.BlockSpec(memory_space=pl.ANY),
            pl.BlockSpec(memory_space=pl.ANY),
        ],
        out_specs=pl.BlockSpec((rows * TOK_SUB, LANES), lambda i, be, nu, rn: (i, 0)),
        scratch_shapes=[
            pltpu.VMEM((2, D, FF), F32),
            pltpu.VMEM((2, D, FF), F32),
            pltpu.VMEM((2, FF, D), F32),
            pltpu.SemaphoreType.DMA((2,)),
            pltpu.VMEM((D, FF), BF16),
            pltpu.VMEM((D, FF), BF16),
            pltpu.VMEM((FF, D), BF16),
        ],
    )
    return pl.pallas_call(
        functools.partial(_expert_kernel, layer=layer),
        grid_spec=grid_spec,
        out_shape=jax.ShapeDtypeStruct((n_blocks * rows * TOK_SUB, LANES), F32),
        compiler_params=_cparams(("arbitrary",)),
        name="expert_mlp",
    )(block_expert, n_used, runs, xs, w_gate, w_up, w_down)


def _combine_kernel(pos_ref, posn_ref, y_hbm, x_ref, route_ref, mod_ref, o_ref, ybuf, sem):
    i = pl.program_id(0)
    n = pl.num_programs(0)
    slot = i % 2
    tm = TM_COMB

    @pl.when(i == 0)
    def _first():
        _start_tokens(pos_ref, y_hbm, ybuf, 0, sem, 2 * tm)

    @pl.when(i + 1 < n)
    def _prefetch():
        _start_tokens(posn_ref, y_hbm, ybuf, 1 - slot, sem, 2 * tm)

    _wait_tokens(y_hbm, ybuf, slot, sem, 2 * tm)
    route = route_ref[...]
    yb = ybuf.at[slot]
    y = route[:, 4:5] * _from_token_major(yb, 0, tm) + route[:, 5:6] * _from_token_major(yb, tm, tm)
    o_ref[...] = x_ref[...] + mod_ref[0, 5:6, :] * y


def _combine_call(pos2, y_rows, x2, route, mod, seq):
    N, D = x2.shape
    tm = TM_COMB
    tpb = seq // tm
    nt = N // tm
    row = lambda i: (i, 0)
    return pl.pallas_call(
        _combine_kernel,
        grid=(nt,),
        in_specs=[
            pl.BlockSpec((1, 1, 2 * tm), lambda i: (i, 0, 0), memory_space=pltpu.SMEM),
            pl.BlockSpec((1, 1, 2 * tm), lambda i: (jnp.minimum(i + 1, nt - 1), 0, 0), memory_space=pltpu.SMEM),
            pl.BlockSpec(memory_space=pl.ANY),
            pl.BlockSpec((tm, D), row),
            pl.BlockSpec((tm, LANES), row),
            pl.BlockSpec((1, 6, D), lambda i: (i // tpb, 0, 0)),
        ],
        out_specs=pl.BlockSpec((tm, D), row),
        out_shape=jax.ShapeDtypeStruct((N, D), F32),
        scratch_shapes=[pltpu.VMEM((2, 2 * tm * TOK_SUB, LANES), F32), pltpu.SemaphoreType.DMA((2,))],
        compiler_params=_cparams(("arbitrary",)),
        name="moe_combine",
    )(pos2, pos2, y_rows, x2, route, mod)


def _rope_lane_tables(positions, rot_dim, head_w, n_rep):
    half = rot_dim // 2
    inv = jnp.power(jnp.float32(ROPE_THETA), -2.0 * jnp.arange(half, dtype=jnp.float32) / rot_dim)
    ang = positions.astype(jnp.float32)[..., None] * inv
    n = positions.shape[0] * positions.shape[1]
    cos, sin = jnp.cos(ang).reshape(n, half), jnp.sin(ang).reshape(n, half)
    lane = np.arange(head_w * n_rep) % head_w
    sel = (lane[None, :] % half == np.arange(half)[:, None]) & (lane[None, :] < rot_dim)
    sign = np.where(lane < half, -1.0, 1.0)[None, :]
    hi = lax.Precision.HIGHEST
    cos_f = jnp.dot(cos, jnp.asarray(sel, F32), precision=hi) + jnp.asarray(lane >= rot_dim, F32)[None, :]
    sin_s = jnp.dot(sin, jnp.asarray(sel * sign, F32), precision=hi)
    return cos_f, sin_s


def _layer_weights(l, w_in, q_norm, k_norm, conv_w, conv_b, conv_ln_g, conv_ln_b, conv_pw_w, conv_pw_b,
                   pool_w, pool_b, pool_scale, sgu_ln_g, sgu_ln_b, sgu_w, sgu_b, out_norm, w_out, norm2,
                   w_rg, b_rg, w_re, b_re):
    G = GROUP_W
    D = D_MODEL
    pts = np.cumsum(IN_SPLITS)[:-1].tolist()
    wq, wk, wv, wiq, wik, wiw, wb, wc, wd = jnp.split(w_in[l], pts, axis=-1)
    wiw_p = jnp.pad(wiw, ((0, 0), (0, LANES - IDX_HEADS)))
    w_in_p = jnp.concatenate([wq, wk, wv, wiq, jnp.tile(wik, (1, IDX_HEADS)), wiw_p, wb, wc, wd], axis=-1).astype(BF16)
    npool = len(POOL_WINDOWS)
    pool_bd = jnp.zeros((G, G), F32)
    for g in range(npool):
        pool_bd = lax.dynamic_update_slice(pool_bd, pool_w[l, g], (g * POOL_CH, g * POOL_CH))
    sgu_bias = jnp.repeat(sgu_b[l].T, G // SGU_HEADS, axis=1)
    w_router = jnp.concatenate([w_rg[l], w_re[l].reshape(D, N_EXPERTS),
                                jnp.zeros((D, LANES - N_EXP_GROUPS - N_EXPERTS), F32)], axis=-1).astype(BF16)
    b_router = jnp.concatenate([b_rg[l], b_re[l].reshape(N_EXPERTS),
                                jnp.zeros((LANES - N_EXP_GROUPS - N_EXPERTS,), F32)]).reshape(1, LANES)
    r1 = lambda a: a.reshape(1, -1)
    return dict(
        w_in=w_in_p,
        q_norm=jnp.tile(q_norm[l], ATT_HEADS).reshape(1, G), k_norm=jnp.tile(k_norm[l], ATT_HEADS).reshape(1, G),
        conv_w=conv_w[l], conv_b=r1(conv_b[l]), conv_ln_g=r1(conv_ln_g[l]), conv_ln_b=r1(conv_ln_b[l]),
        conv_pw_w=conv_pw_w[l].astype(BF16), conv_pw_b=r1(conv_pw_b[l]),
        pool_w=pool_bd.astype(BF16), pool_b=r1(pool_b[l]), pool_scale=r1(pool_scale[l]),
        sgu_ln_g=r1(sgu_ln_g[l]), sgu_ln_b=r1(sgu_ln_b[l]), sgu_w=sgu_w[l], sgu_b=sgu_bias,
        out_norm=r1(out_norm[l]), w_out=w_out[l].astype(BF16), norm2=r1(norm2[l]),
        w_router=w_router, b_router=b_router,
    )


def _dispatch_tables(route, cnt, n_tokens):
    rows_blk = MOE_ROWS
    e = route[:, 0:2].astype(jnp.int32)
    rank = route[:, 2:4].astype(jnp.int32)
    counts = cnt[0, :N_EXPERTS].astype(jnp.int32)
    padded = (counts + rows_blk - 1) // rows_blk * rows_blk
    pad_end = jnp.cumsum(padded)
    pad_start = pad_end - padded
    seg0 = jnp.sum(jnp.where(e[..., None] == jnp.arange(N_EXPERTS, dtype=jnp.int32), pad_start, 0), axis=-1)
    pos = seg0 + rank
    m = n_tokens * MOE_TOPK
    n_blocks = (m + N_EXPERTS * (rows_blk - 1) + rows_blk - 1) // rows_blk
    n_used = (pad_end[-1] // rows_blk).astype(jnp.int32).reshape(1)
    blk_row0 = jnp.arange(n_blocks, dtype=jnp.int32) * rows_blk
    block_expert = jnp.minimum(jnp.sum((pad_end[None, :] <= blk_row0[:, None]).astype(jnp.int32), axis=1),
                               N_EXPERTS - 1)
    pos2 = (pos * TOK_SUB).reshape(n_tokens // TM_COMB, TM_COMB, MOE_TOPK).transpose(0, 2, 1).reshape(
        -1, 1, MOE_TOPK * TM_COMB)
    tail = jnp.where(counts > 0, (pad_end - rows_blk) * TOK_SUB, -1).astype(jnp.int32)
    return block_expert, n_used, tail, pos2, n_blocks * rows_blk


def kernel(x, c, positions, w_ada, b_ada, norm1, w_in, q_norm, k_norm, conv_w, conv_b, conv_ln_g, conv_ln_b, conv_pw_w, conv_pw_b, pool_w, pool_b, pool_scale, sgu_ln_g, sgu_ln_b, sgu_w, sgu_b, out_norm, w_out, norm2, w_rg, b_rg, w_re, b_re, w_gate, w_up, w_down):
    B, S, D = x.shape
    N = B * S
    assert D == D_MODEL and S % TM_PROJ == 0 and S % TK_ATT == 0 and N % TM_COMB == 0
    depth = w_ada.shape[0]
    cos_a, sin_a = _rope_lane_tables(positions, ROPE_DIM, ATT_HEAD_DIM, ATT_HEADS)
    cos_i, sin_i = _rope_lane_tables(positions, IDX_ROPE_DIM, IDX_DIM, IDX_HEADS)
    c_pad = jnp.pad(c, ((0, (-B) % SUBLANES), (0, 0)))
    mod_all = _ada_call(c_pad, w_ada, b_ada)
    x2 = x.reshape(N, D)
    for l in range(depth):
        lw = _layer_weights(l, w_in, q_norm, k_norm, conv_w, conv_b, conv_ln_g, conv_ln_b, conv_pw_w, conv_pw_b,
                            pool_w, pool_b, pool_scale, sgu_ln_g, sgu_ln_b, sgu_w, sgu_b, out_norm, w_out, norm2,
                            w_rg, b_rg, w_re, b_re)
        mod = mod_all[l, :B].reshape(B, 6, D)
        q, k, v, iq, ik, iw, pb, pc, pd = _proj_call(
            x2, mod, norm1[l].reshape(1, D), lw["w_in"], lw["q_norm"], lw["k_norm"], cos_a, sin_a, cos_i, sin_i, S)
        oa = _dsa_call(q, k, v, iq, ik, iw, B, S)
        x_mid, h2, route, cnt = _mix_call(x2, oa, pb, pc, pd, mod, lw, S)
        block_expert, n_used, tail, pos2, n_rows = _dispatch_tables(route, cnt, N)
        xs = _dispatch_call(tail, n_used, pos2, h2, n_rows)
        y_rows = _expert_call(block_expert, n_used, xs, w_gate, w_up, w_down, l)
        x2 = _combine_call(pos2, y_rows, x_mid, route, mod, S)
    return x2.reshape(B, S, D)
```
